```python
import jax
import jax.numpy as jnp
from jax import lax

D_MODEL = 1024
BATCH = 16
SEQ = 2048
DEPTH = 1

GRID_W = 64
CTX_LEN = 256
EPS = 1e-6
NEG_INF = -1e30

NA_HEADS = 8
NA_HEAD_DIM = 64
NA_WIDTH = NA_HEADS * NA_HEAD_DIM
NA_WIN_ROWS = 8
NA_WIN_COLS = 16
ROPE_THETA = 10000.0

LRU_WIDTH = D_MODEL
LRU_BLOCKS = 8
LRU_BLOCK = LRU_WIDTH // LRU_BLOCKS
LRU_CONV = 4
LRU_C = 8.0

N_GROUPS = 4
EXPERTS_PER_GROUP = 8
N_EXPERTS = N_GROUPS * EXPERTS_PER_GROUP
TOP_K = 2
D_EXPERT = 512
MOE_BLOCK = 256

K_OFF = 0
V_OFF = K_OFF + NA_WIDTH
LX_OFF = V_OFF + NA_WIDTH
CTX_COLS = LX_OFF + LRU_WIDTH
Q_OFF = CTX_COLS
LG_OFF = Q_OFF + NA_WIDTH
GA_OFF = LG_OFF + LRU_WIDTH
GB_OFF = GA_OFF + D_MODEL
PROJ_COLS = GB_OFF + D_MODEL

kernel_name = 'hybrid_natten_rglru_hmoe_dit_block'


def _rms(x):
    x32 = x.astype(jnp.float32)
    return x32 * lax.rsqrt(jnp.mean(x32 * x32, axis=-1, keepdims=True) + EPS)


def ada_rmsnorm(x, gain, shift, scale):
    n = _rms(x) * gain.astype(jnp.float32)
    return (n * (1.0 + scale.astype(jnp.float32)) + shift.astype(jnp.float32)).astype(x.dtype)


def axial_rope(x, row_pos, col_pos):
    hd = x.shape[-1]
    half = hd // 2
    nf = half // 2
    inv_freq = ROPE_THETA ** (-jnp.arange(nf, dtype=jnp.float32) / nf)

    def rot(xp, pos):
        ang = pos[:, None] * inv_freq
        cos = jnp.cos(ang)[:, None, :]
        sin = jnp.sin(ang)[:, None, :]
        x1, x2 = xp[..., :nf], xp[..., nf:]
        return jnp.concatenate([x1 * cos - x2 * sin, x1 * sin + x2 * cos], axis=-1)

    x32 = x.astype(jnp.float32)
    out = jnp.concatenate([rot(x32[..., :half], row_pos), rot(x32[..., half:], col_pos)], axis=-1)
    return out.astype(x.dtype)


def neighbourhood_attention(q, k, v, k_ctx, v_ctx, rpb):
    bsz, length, n_heads, hd = q.shape
    rows = length // GRID_W
    kr = min(NA_WIN_ROWS, rows)
    t = jnp.arange(length)
    row_pos = (t // GRID_W).astype(jnp.float32)
    col_pos = (t % GRID_W).astype(jnp.float32)
    scale = hd ** -0.5
    grid = (bsz, rows, GRID_W, n_heads, hd)
    q_plain = q.reshape(grid)
    q_rot = axial_rope(q, row_pos, col_pos).reshape(grid)
    k_rot = axial_rope(k, row_pos, col_pos).reshape(grid)
    v_g = v.reshape(grid)
    r = jnp.arange(rows)
    r_start = jnp.clip(r - kr // 2, 0, rows - kr)
    r_win = r_start[:, None] + jnp.arange(kr)[None, :]
    n_win = kr * GRID_W
    k_win = k_rot[:, r_win].reshape(bsz, rows, n_win, n_heads, hd)
    v_win = v_g[:, r_win].reshape(bsz, rows, n_win, n_heads, hd)
    cq = jnp.arange(GRID_W)
    c_start = jnp.clip(cq - NA_WIN_COLS // 2, 0, GRID_W - NA_WIN_COLS)
    band = (cq[None, :] >= c_start[:, None]) & (cq[None, :] < c_start[:, None] + NA_WIN_COLS)
    mask = jnp.broadcast_to(band[:, None, :], (GRID_W, kr, GRID_W)).reshape(GRID_W, n_win)
    dr = r_win - r[:, None] + (NA_WIN_ROWS - 1)
    dc = jnp.clip(cq[None, :] - cq[:, None], 1 - NA_WIN_COLS, NA_WIN_COLS - 1) + (NA_WIN_COLS - 1)
    bias = rpb.astype(jnp.float32)[:, dr][..., dc]
    bias = bias.transpose(0, 1, 3, 2, 4).reshape(n_heads, rows, GRID_W, n_win)
    s_lat = jnp.einsum('brqhd,brjhd->bhrqj', q_rot, k_win).astype(jnp.float32) * scale + bias[None]
    s_lat = jnp.where(mask, s_lat, NEG_INF)
    s_ctx = jnp.einsum('brqhd,bchd->bhrqc', q_plain, k_ctx).astype(jnp.float32) * scale
    p = jax.nn.softmax(jnp.concatenate([s_lat, s_ctx], axis=-1), axis=-1).astype(v.dtype)
    o = (jnp.einsum('bhrqj,brjhd->brqhd', p[..., :n_win], v_win)
         + jnp.einsum('bhrqc,bchd->brqhd', p[..., n_win:], v_ctx))
    return o.reshape(bsz, length, n_heads * hd)


def context_attention(q, k, v):
    bsz, n_ctx, n_heads, hd = q.shape
    s = jnp.einsum('bqhd,bkhd->bhqk', q, k).astype(jnp.float32) * (hd ** -0.5)
    p = jax.nn.softmax(s, axis=-1).astype(v.dtype)
    return jnp.einsum('bhqk,bkhd->bqhd', p, v).reshape(bsz, n_ctx, n_heads * hd)


def short_conv(x, w, b):
    n_ch = x.shape[-1]
    pad_l = LRU_CONV // 2
    y = lax.conv_general_dilated(x, w[:, None, :], window_strides=(1,),
                                 padding=[(pad_l, LRU_CONV - 1 - pad_l)],
                                 dimension_numbers=('NWC', 'WIO', 'NWC'),
                                 feature_group_count=n_ch)
    return y + b


def _scan_combine(left, right):
    a1, b1 = left
    a2, b2 = right
    return a1 * a2, a2 * b1 + b2


def lru_direction(xc, h0, wa, ba, wx, bx, lam, reverse):
    bsz, length, n_ch = xc.shape
    xb = xc.reshape(bsz, length, LRU_BLOCKS, LRU_BLOCK)
    r = jax.nn.sigmoid(jnp.einsum('blnc,ncd->blnd', xb, wa).reshape(bsz, length, n_ch) + ba)
    i = jax.nn.sigmoid(jnp.einsum('blnc,ncd->blnd', xb, wx).reshape(bsz, length, n_ch) + bx)
    log_a = -LRU_C * r.astype(jnp.float32) * jax.nn.softplus(-lam.astype(jnp.float32))
    a = jnp.exp(log_a)
    u = jnp.sqrt(-jnp.expm1(2.0 * log_a)) * (i * xc).astype(jnp.float32)
    a_cum, h = lax.associative_scan(_scan_combine, (a, u), axis=1, reverse=reverse)
    if h0 is not None:
        h = h + a_cum * h0[:, None, :]
    return h


def branch_merge(proj, o_att, o_lru, w_up_attn, w_up_lru, w_out):
    g_a = jax.nn.sigmoid(proj[..., GA_OFF:GA_OFF + D_MODEL])
    g_b = jax.nn.sigmoid(proj[..., GB_OFF:GB_OFF + D_MODEL])
    y = g_a * (o_att @ w_up_attn) + g_b * (o_lru @ w_up_lru)
    return y @ w_out


def hier_moe(h, wg, bg, we, be, w1, w3, w2):
    n_tok, d = h.shape
    h32 = h.astype(jnp.float32)
    p_grp = jax.nn.softmax(h32 @ wg.astype(jnp.float32) + bg.astype(jnp.float32), axis=-1)
    p_top, g_idx = lax.top_k(p_grp, 1)
    le = (h32 @ we.astype(jnp.float32) + be.astype(jnp.float32)).reshape(n_tok, N_GROUPS, EXPERTS_PER_GROUP)
    le_sel = le[jnp.arange(n_tok), g_idx[:, 0]]
    ev, e_idx = lax.top_k(le_sel, TOP_K)
    gate = p_top * jax.nn.softmax(ev, axis=-1)
    eid = g_idx * EXPERTS_PER_GROUP + e_idx
    n_asg = n_tok * TOP_K
    flat_e = eid.reshape(n_asg)
    flat_tok = jnp.repeat(jnp.arange(n_tok), TOP_K)
    flat_w = gate.reshape(n_asg)
    order = jnp.argsort(flat_e)
    e_s, tok_s, w_s = flat_e[order], flat_tok[order], flat_w[order]
    counts = jnp.bincount(flat_e, length=N_EXPERTS)
    starts = jnp.cumsum(counts) - counts
    padded = (counts + MOE_BLOCK - 1) // MOE_BLOCK * MOE_BLOCK
    p_ends = jnp.cumsum(padded)
    p_starts = p_ends - padded
    dest = p_starts[e_s] + (jnp.arange(n_asg) - starts[e_s])
    n_blk = -(-(n_asg + N_EXPERTS * (MOE_BLOCK - 1)) // MOE_BLOCK)
    buf_tok = jnp.full((n_blk * MOE_BLOCK,), n_tok, dtype=jnp.int32).at[dest].set(tok_s)
    buf_w = jnp.zeros((n_blk * MOE_BLOCK,), jnp.float32).at[dest].set(w_s)
    blk_e = jnp.minimum(jnp.searchsorted(p_ends, jnp.arange(n_blk) * MOE_BLOCK, side='right'), N_EXPERTS - 1)
    h_pad = jnp.concatenate([h, jnp.zeros((1, d), h.dtype)], axis=0)
    xb = h_pad[buf_tok].reshape(n_blk, MOE_BLOCK, d)

    def expert_block(args):
        xblk, e = args
        return (jax.nn.silu(xblk @ w1[e]) * (xblk @ w3[e])) @ w2[e]

    yb = lax.map(expert_block, (xb, blk_e)).reshape(n_blk * MOE_BLOCK, d)
    out = jnp.zeros((n_tok + 1, d), jnp.float32).at[buf_tok].add(yb.astype(jnp.float32) * buf_w[:, None])
    return out[:n_tok].astype(h.dtype)


def hybrid_layer(x, ctx, c, c_ctx, w_mod, b_mod, g_mix, g_ffn, w_in, rpb, conv_w, conv_b,
                 lru_wa, lru_ba, lru_wx, lru_bx, lru_lambda, w_up_attn, w_up_lru, w_out,
                 wg, bg, we, be, w1, w3, w2, last):
    bsz, length, d = x.shape
    n_ctx = ctx.shape[1]
    mod = (jax.nn.silu(c) @ w_mod + b_mod)[:, None, :]
    sh1, sc1, ga1, sh2, sc2, ga2 = jnp.split(mod, 6, axis=-1)
    n_cm = 2 if last else 6
    mod_c = jnp.split(jax.nn.silu(c_ctx) @ w_mod[:, :n_cm * d] + b_mod[:n_cm * d], n_cm)

    h = ada_rmsnorm(x, g_mix, sh1, sc1)
    hc = ada_rmsnorm(ctx, g_mix, mod_c[0], mod_c[1])
    proj = h @ w_in
    proj_c = hc @ (w_in[:, :CTX_COLS] if last else w_in)

    def heads(t):
        return t.reshape(t.shape[0], t.shape[1], NA_HEADS, NA_HEAD_DIM)

    q = heads(proj[..., Q_OFF:Q_OFF + NA_WIDTH])
    k = heads(proj[..., K_OFF:K_OFF + NA_WIDTH])
    v = heads(proj[..., V_OFF:V_OFF + NA_WIDTH])
    k_c = heads(proj_c[..., K_OFF:K_OFF + NA_WIDTH])
    v_c = heads(proj_c[..., V_OFF:V_OFF + NA_WIDTH])
    o_att = neighbourhood_attention(q, k, v, k_c, v_c, rpb)

    x_l = short_conv(proj[..., LX_OFF:LX_OFF + LRU_WIDTH], conv_w, conv_b)
    x_c = short_conv(proj_c[..., LX_OFF:LX_OFF + LRU_WIDTH], conv_w, conv_b)
    fwd = (lru_wa[0], lru_ba[0], lru_wx[0], lru_bx[0], lru_lambda[0])
    bwd = (lru_wa[1], lru_ba[1], lru_wx[1], lru_bx[1], lru_lambda[1])
    hc_f = lru_direction(x_c, None, *fwd, reverse=False)
    hc_b = lru_direction(x_c, None, *bwd, reverse=True)
    h_f = lru_direction(x_l, hc_f[:, -1], *fwd, reverse=False)
    h_b = lru_direction(x_l, hc_b[:, 0], *bwd, reverse=True)
    o_lru = ((h_f + h_b) * jax.nn.gelu(proj[..., LG_OFF:LG_OFF + LRU_WIDTH].astype(jnp.float32))).astype(x.dtype)
    x = x + ga1 * branch_merge(proj, o_att, o_lru, w_up_attn, w_up_lru, w_out)
    if not last:
        q_c = heads(proj_c[..., Q_OFF:Q_OFF + NA_WIDTH])
        oc_att = context_attention(q_c, k_c, v_c)
        oc_lru = ((hc_f + hc_b) * jax.nn.gelu(proj_c[..., LG_OFF:LG_OFF + LRU_WIDTH].astype(jnp.float32))).astype(ctx.dtype)
        ctx = ctx + mod_c[2] * branch_merge(proj_c, oc_att, oc_lru, w_up_attn, w_up_lru, w_out)

    h2 = ada_rmsnorm(x, g_ffn, sh2, sc2)
    x = x + ga2 * hier_moe(h2.reshape(bsz * length, d), wg, bg, we, be, w1, w3, w2).reshape(bsz, length, d)
    if not last:
        hc2 = ada_rmsnorm(ctx, g_ffn, mod_c[3], mod_c[4])
        ctx = ctx + mod_c[5] * hier_moe(hc2.reshape(bsz * n_ctx, d), wg, bg, we, be, w1, w3, w2).reshape(bsz, n_ctx, d)
    return x, ctx


def setup_inputs(seed: int = 0) -> dict:
    key = jax.random.key(seed)
    ks = jax.random.split(key, 32)
    f32 = jnp.float32
    d = D_MODEL

    def nrm(k, shape, scale):
        return jax.random.normal(k, shape, f32) * scale

    u = jax.random.uniform(ks[16], (DEPTH, 2, LRU_WIDTH), f32, 0.9, 0.999)
    a0 = u ** (1.0 / LRU_C)
    return {
        'x': nrm(ks[0], (BATCH, SEQ, d), 1.0),
        'c': nrm(ks[1], (BATCH, d), 1.0),
        'ctx': nrm(ks[2], (BATCH, CTX_LEN, d), 1.0),
        'c_ctx': nrm(ks[3], (d,), 1.0),
        'w_mod': nrm(ks[4], (DEPTH, d, 6 * d), 0.5 * d ** -0.5),
        'b_mod': nrm(ks[5], (DEPTH, 6 * d), 0.02),
        'g_mix': 1.0 + nrm(ks[6], (DEPTH, d), 0.02),
        'g_ffn': 1.0 + nrm(ks[7], (DEPTH, d), 0.02),
        'w_in': nrm(ks[8], (DEPTH, d, PROJ_COLS), d ** -0.5),
        'rpb': nrm(ks[9], (DEPTH, NA_HEADS, 2 * NA_WIN_ROWS - 1, 2 * NA_WIN_COLS - 1), 0.1),
        'conv_w': nrm(ks[10], (DEPTH, LRU_CONV, LRU_WIDTH), LRU_CONV ** -0.5),
        'conv_b': nrm(ks[11], (DEPTH, LRU_WIDTH), 0.02),
        'lru_wa': nrm(ks[12], (DEPTH, 2, LRU_BLOCKS, LRU_BLOCK, LRU_BLOCK), LRU_BLOCK ** -0.5),
        'lru_ba': nrm(ks[13], (DEPTH, 2, LRU_WIDTH), 0.02),
        'lru_wx': nrm(ks[14], (DEPTH, 2, LRU_BLOCKS, LRU_BLOCK, LRU_BLOCK), LRU_BLOCK ** -0.5),
        'lru_bx': nrm(ks[15], (DEPTH, 2, LRU_WIDTH), 0.02),
        'lru_lambda': jnp.log(a0) - jnp.log1p(-a0),
        'w_up_attn': nrm(ks[17], (DEPTH, NA_WIDTH, d), NA_WIDTH ** -0.5),
        'w_up_lru': nrm(ks[18], (DEPTH, LRU_WIDTH, d), LRU_WIDTH ** -0.5),
        'w_out': nrm(ks[19], (DEPTH, d, d), d ** -0.5),
        'router_group_w': nrm(ks[20], (DEPTH, d, N_GROUPS), d ** -0.5),
        'router_group_b': nrm(ks[21], (DEPTH, N_GROUPS), 0.01),
        'router_expert_w': nrm(ks[22], (DEPTH, d, N_EXPERTS), d ** -0.5),
        'router_expert_b': nrm(ks[23], (DEPTH, N_EXPERTS), 0.01),
        'expert_w_gate': nrm(ks[24], (DEPTH, N_EXPERTS, d, D_EXPERT), d ** -0.5),
        'expert_w_up': nrm(ks[25], (DEPTH, N_EXPERTS, d, D_EXPERT), d ** -0.5),
        'expert_w_down': nrm(ks[26], (DEPTH, N_EXPERTS, D_EXPERT, d), D_EXPERT ** -0.5),
        'g_final': 1.0 + nrm(ks[27], (d,), 0.02),
    }


def reference(x, c, ctx, c_ctx, w_mod, b_mod, g_mix, g_ffn, w_in, rpb, conv_w, conv_b,
              lru_wa, lru_ba, lru_wx, lru_bx, lru_lambda, w_up_attn, w_up_lru, w_out,
              router_group_w, router_group_b, router_expert_w, router_expert_b,
              expert_w_gate, expert_w_up, expert_w_down, g_final):
    for l in range(DEPTH):
        x, ctx = hybrid_layer(x, ctx, c, c_ctx, w_mod[l], b_mod[l], g_mix[l], g_ffn[l], w_in[l], rpb[l],
                              conv_w[l], conv_b[l], lru_wa[l], lru_ba[l], lru_wx[l], lru_bx[l], lru_lambda[l],
                              w_up_attn[l], w_up_lru[l], w_out[l],
                              router_group_w[l], router_group_b[l], router_expert_w[l], router_expert_b[l],
                              expert_w_gate[l], expert_w_up[l], expert_w_down[l], last=(l == DEPTH - 1))
    return (_rms(x) * g_final.astype(jnp.float32)).astype(x.dtype)
```

```python
import functools

import numpy as np
import jax
import jax.numpy as jnp
from jax import lax
from jax.experimental import pallas as pl
from jax.experimental.pallas import tpu as pltpu

F32 = jnp.float32
BF16 = jnp.bfloat16
I32 = jnp.int32
U32 = jnp.uint32

D_MODEL = 1024
GRID_W = 64
EPS = 1e-6
NEG_INF = -1e30

NA_HEADS = 8
NA_HEAD_DIM = 64
NA_WIDTH = NA_HEADS * NA_HEAD_DIM
NA_WIN_ROWS = 8
NA_WIN_COLS = 16
ROPE_THETA = 10000.0

LRU_WIDTH = D_MODEL
LRU_BLOCKS = 8
LRU_BLOCK = LRU_WIDTH // LRU_BLOCKS
LRU_CONV = 4
LRU_C = 8.0

N_GROUPS = 4
EXPERTS_PER_GROUP = 8
N_EXPERTS = N_GROUPS * EXPERTS_PER_GROUP
TOP_K = 2
D_EXPERT = 512

K_OFF = 0
V_OFF = K_OFF + NA_WIDTH
LX_OFF = V_OFF + NA_WIDTH
CTX_COLS = LX_OFF + LRU_WIDTH
Q_OFF = CTX_COLS
LG_OFF = Q_OFF + NA_WIDTH
GA_OFF = LG_OFF + LRU_WIDTH
GB_OFF = GA_OFF + D_MODEL
PROJ_COLS = GB_OFF + D_MODEL

LANES = 128
SUBLANES = 8

Q_ROWS = 4
K_ROW_BLOCKS = 3
ATT_TQ = Q_ROWS * GRID_W
ATT_TK = K_ROW_BLOCKS * ATT_TQ

MOE_BLK_LOG2 = 8
MOE_BLK = 1 << MOE_BLK_LOG2
MOD_ROWS = 24
ROUTE_ROWS = 64

LRU_SPLIT = SUBLANES

PROJ_TM = 512
MERGE_TM = 512
ROUTE_TB = 2048
DISPATCH_TM = 512
COMBINE_TM = 256

VMEM_LIMIT = 56 * 1024 * 1024


def _cparams(sem, vmem=VMEM_LIMIT):
    return pltpu.CompilerParams(dimension_semantics=sem, vmem_limit_bytes=vmem)


def _dot(a, b):
    return jnp.dot(a, b, preferred_element_type=F32)


def _dot_nt(a, b):
    return lax.dot_general(a, b, (((1,), (1,)), ((), ())), preferred_element_type=F32)


def _split2(a):
    hi = a.astype(BF16)
    lo = (a - hi.astype(F32)).astype(BF16)
    return hi, lo


def _dot3(a, b):
    ah, al = _split2(a)
    bh, bl = _split2(b)
    return _dot(ah, bh) + (_dot(ah, bl) + _dot(al, bh))


def _dot3_nt(a, b):
    ah, al = _split2(a)
    bh, bl = _split2(b)
    return _dot_nt(ah, bh) + (_dot_nt(ah, bl) + _dot_nt(al, bh))


def _sigmoid(x):
    return 1.0 / (1.0 + jnp.exp(-x))


def _ada_norm(x, g, sc, sh):
    ms = jnp.mean(x * x, axis=-1, keepdims=True)
    return (x * lax.rsqrt(ms + EPS) * g) * (1.0 + sc) + sh


def _mod_body(cc_ref, w_ref, b_ref, o_ref):
    cc = cc_ref[...]
    o_ref[...] = _dot3(cc * _sigmoid(cc), w_ref[...]) + b_ref[...]


def _mod_call(cc, w_mod, b_mod):
    n = w_mod.shape[1]
    bn = 1024
    return pl.pallas_call(
        _mod_body,
        grid=(n // bn,),
        in_specs=[pl.BlockSpec((MOD_ROWS, D_MODEL), lambda j: (0, 0)),
                  pl.BlockSpec((D_MODEL, bn), lambda j: (0, j)),
                  pl.BlockSpec((1, bn), lambda j: (0, j))],
        out_specs=pl.BlockSpec((MOD_ROWS, bn), lambda j: (0, j)),
        out_shape=jax.ShapeDtypeStruct((MOD_ROWS, n), F32),
        compiler_params=_cparams(("arbitrary",)),
        name="mod",
    )(cc, w_mod, b_mod.reshape(1, n))


def _rope(t, cos, sin):
    lane = lax.broadcasted_iota(I32, (t.shape[0], LANES), 1)
    first = (lane & 16) == 0
    parts = []
    for c in range(t.shape[1] // LANES):
        tc = t[:, c * LANES:(c + 1) * LANES]
        parts.append(jnp.where(first, pltpu.roll(tc, LANES - 16, 1), pltpu.roll(tc, 16, 1)))
    partner = jnp.concatenate(parts, axis=1)
    return t * cos + partner * sin


def _gelu_tanh(x):
    return 0.5 * x * (1.0 + jnp.tanh(0.7978845608028654 * (x + 0.044715 * (x * x * x))))


def _store_split(ref, rows):
    n_j = ref.shape[1]
    for s in range(rows.shape[0] // n_j):
        ref[0, :, s * LRU_WIDTH:(s + 1) * LRU_WIDTH] = rows[s * n_j:(s + 1) * n_j, :]


def _load_split(ref):
    n_s = ref.shape[2] // LRU_WIDTH
    return jnp.concatenate([ref[0, :, s * LRU_WIDTH:(s + 1) * LRU_WIDTH] for s in range(n_s)], axis=0)


def _inproj_body(x_ref, mod_ref, g_ref, w_ref, cos_ref, sin_ref,
                 qre_ref, qro_ref, qpe_ref, qpo_ref, k_ref, v_ref, lx_ref, glu_ref, ga_ref, gb_ref):
    sh = mod_ref[0, :, 0:D_MODEL]
    sc = mod_ref[0, :, D_MODEL:2 * D_MODEL]
    h = _ada_norm(x_ref[...], g_ref[...], sc, sh).astype(BF16)
    cos = cos_ref[...]
    sin = sin_ref[...]
    scale = NA_HEAD_DIM ** -0.5

    k_ref[...] = _rope(_dot(h, w_ref[:, K_OFF:K_OFF + NA_WIDTH]), cos, sin).astype(BF16)
    v_ref[...] = _dot(h, w_ref[:, V_OFF:V_OFF + NA_WIDTH]).astype(BF16)
    _store_split(lx_ref, _dot(h, w_ref[:, LX_OFF:LX_OFF + LRU_WIDTH]))

    q = _dot(h, w_ref[:, Q_OFF:Q_OFF + NA_WIDTH]) * scale
    qr = _rope(q, cos, sin)
    lane = lax.broadcasted_iota(I32, q.shape, 1)
    even = (lane & NA_HEAD_DIM) == 0
    qre_ref[...] = jnp.where(even, qr, 0.0).astype(BF16)
    qro_ref[...] = jnp.where(even, 0.0, qr).astype(BF16)
    qpe_ref[...] = jnp.where(even, q, 0.0).astype(BF16)
    qpo_ref[...] = jnp.where(even, 0.0, q).astype(BF16)

    glu_ref[...] = _gelu_tanh(_dot(h, w_ref[:, LG_OFF:LG_OFF + LRU_WIDTH])).astype(BF16)
    ga_ref[...] = _sigmoid(_dot(h, w_ref[:, GA_OFF:GA_OFF + D_MODEL])).astype(BF16)
    gb_ref[...] = _sigmoid(_dot(h, w_ref[:, GB_OFF:GB_OFF + D_MODEL])).astype(BF16)


def _inproj_call(x2, mod3, g_mix, w_in_bf, cos_t, sin_t, seq):
    t = x2.shape[0]
    tm = PROJ_TM
    per_b = seq // tm
    n_j = seq // LRU_SPLIT
    split_w = (tm // n_j) * LRU_WIDTH
    row = lambda i: (i, 0)
    wide = lambda n, dt: jax.ShapeDtypeStruct((t, n), dt)
    return pl.pallas_call(
        _inproj_body,
        grid=(t // tm,),
        in_specs=[pl.BlockSpec((tm, D_MODEL), row),
                  pl.BlockSpec((1, 1, 6 * D_MODEL), lambda i: (i // per_b, 0, 0)),
                  pl.BlockSpec((1, D_MODEL), lambda i: (0, 0)),
                  pl.BlockSpec((D_MODEL, PROJ_COLS), lambda i: (0, 0), pipeline_mode=pl.Buffered(1)),
                  pl.BlockSpec((tm, NA_WIDTH), lambda i: (i % per_b, 0)),
                  pl.BlockSpec((tm, NA_WIDTH), lambda i: (i % per_b, 0))],
        out_specs=[pl.BlockSpec((tm, NA_WIDTH), row)] * 6
                  + [pl.BlockSpec((1, n_j, split_w), lambda i: (i // per_b, 0, i % per_b))]
                  + [pl.BlockSpec((tm, LRU_WIDTH), row)] * 3,
        out_shape=[wide(NA_WIDTH, BF16)] * 6
                  + [jax.ShapeDtypeStruct((t // seq, n_j, LRU_SPLIT * LRU_WIDTH), F32),
                     wide(LRU_WIDTH, BF16), wide(D_MODEL, BF16), wide(D_MODEL, BF16)],
        compiler_params=_cparams(("parallel",)),
        name="inproj",
    )(x2, mod3, g_mix, w_in_bf, cos_t, sin_t)


def _ctxproj_body(x_ref, mod_ref, g_ref, w_ref, k_ref, v_ref, lx_ref):
    sh = mod_ref[:, 0:D_MODEL]
    sc = mod_ref[:, D_MODEL:2 * D_MODEL]
    h = _ada_norm(x_ref[...], g_ref[...], sc, sh).astype(BF16)
    k_ref[...] = _dot(h, w_ref[:, K_OFF:K_OFF + NA_WIDTH]).astype(BF16)
    v_ref[...] = _dot(h, w_ref[:, V_OFF:V_OFF + NA_WIDTH]).astype(BF16)
    _store_split(lx_ref, _dot(h, w_ref[:, LX_OFF:LX_OFF + LRU_WIDTH]))


def _ctxproj_call(c2, mod_c, g_mix, w_ctx_bf, n_ctx):
    t = c2.shape[0]
    tm = n_ctx
    n_j = n_ctx // LRU_SPLIT
    row = lambda i: (i, 0)
    return pl.pallas_call(
        _ctxproj_body,
        grid=(t // tm,),
        in_specs=[pl.BlockSpec((tm, D_MODEL), row),
                  pl.BlockSpec((1, 6 * D_MODEL), lambda i: (0, 0)),
                  pl.BlockSpec((1, D_MODEL), lambda i: (0, 0)),
                  pl.BlockSpec((D_MODEL, CTX_COLS), lambda i: (0, 0))],
        out_specs=[pl.BlockSpec((tm, NA_WIDTH), row), pl.BlockSpec((tm, NA_WIDTH), row),
                   pl.BlockSpec((1, n_j, LRU_SPLIT * LRU_WIDTH), lambda i: (i, 0, 0))],
        out_shape=[jax.ShapeDtypeStruct((t, NA_WIDTH), BF16), jax.ShapeDtypeStruct((t, NA_WIDTH), BF16),
                   jax.ShapeDtypeStruct((t // tm, n_j, LRU_SPLIT * LRU_WIDTH), F32)],
        compiler_params=_cparams(("parallel",)),
        name="ctxproj",
    )(c2, mod_c, g_mix, w_ctx_bf)


N_DR = 2 * NA_WIN_ROWS - 1
N_DC = 2 * NA_WIN_COLS - 1


def _rpbcol_body(rpb_ref, o_ref):
    n = GRID_W * GRID_W
    flat = lax.broadcasted_iota(I32, (32, n), 1)
    qc = flat >> 6
    kc = flat & (GRID_W - 1)
    dc = jnp.clip(kc - qc, 1 - NA_WIN_COLS, NA_WIN_COLS - 1) + (NA_WIN_COLS - 1)
    d_iota = lax.broadcasted_iota(I32, (32, n), 0)
    onehot = jnp.where(dc == d_iota, 1.0, 0.0).astype(BF16)
    r = rpb_ref[...]
    r1 = r.astype(BF16)
    rem = r - r1.astype(F32)
    r2 = rem.astype(BF16)
    r3 = (rem - r2.astype(F32)).astype(BF16)
    val = _dot(r1, onehot) + (_dot(r2, onehot) + _dot(r3, onehot))
    qc1 = qc[0:1, :]
    kc1 = kc[0:1, :]
    c_start = jnp.clip(qc1 - NA_WIN_COLS // 2, 0, GRID_W - NA_WIN_COLS)
    band = (kc1 >= c_start) & (kc1 < c_start + NA_WIN_COLS)
    o_ref[...] = jnp.where(band, val, NEG_INF)


def _rpbcol_call(rpb):
    rows = NA_HEADS * N_DR
    r2 = jnp.pad(rpb.reshape(rows, N_DC), ((0, 0), (0, 32 - N_DC)))
    n = GRID_W * GRID_W
    return pl.pallas_call(
        _rpbcol_body,
        in_specs=[pl.BlockSpec((rows, 32), lambda: (0, 0))],
        out_specs=pl.BlockSpec((rows, n), lambda: (0, 0)),
        out_shape=jax.ShapeDtypeStruct((rows, n), F32),
        name="rpbcol",
    )(r2)


def _bias_tables(rpbcol):
    t = rpbcol.reshape(NA_HEADS, N_DR, GRID_W, GRID_W)
    neg = jnp.full((NA_HEADS, GRID_W, GRID_W), NEG_INF, F32)
    n_kj = K_ROW_BLOCKS * Q_ROWS
    classes = []
    for lo_fn, dr_off in ((lambda ri: 0, 7), (lambda ri: ri, 3), (lambda ri: 4, -1)):
        rows = []
        for ri in range(Q_ROWS):
            lo = lo_fn(ri)
            blocks = []
            for kj in range(n_kj):
                inside = lo <= kj < lo + NA_WIN_ROWS
                blocks.append(t[:, kj - ri + dr_off] if inside else neg)
            rows.append(jnp.concatenate(blocks, axis=2))
        classes.append(jnp.concatenate(rows, axis=1))
    return jnp.stack(classes, axis=0)


def _attn_body(qre_ref, qro_ref, qpe_ref, qpo_ref, k0_ref, k1_ref, k2_ref, v0_ref, v1_ref, v2_ref,
               kc_ref, vc_ref, bias_ref, o_ref):
    lane = lax.broadcasted_iota(I32, (ATT_TQ, LANES), 1)
    k_refs = (k0_ref, k1_ref, k2_ref)
    v_refs = (v0_ref, v1_ref, v2_ref)
    for p in range(NA_HEADS // 2):
        sl = slice(p * LANES, (p + 1) * LANES)
        kk = [r[:, sl] for r in k_refs]
        vv = [r[:, sl] for r in v_refs]
        kc = kc_ref[:, sl]
        vc = vc_ref[:, sl]
        outs = []
        for hh, (qr_ref, qp_ref) in enumerate(((qre_ref, qpe_ref), (qro_ref, qpo_ref))):
            h = 2 * p + hh
            qr = qr_ref[:, sl]
            qp = qp_ref[:, sl]
            s = [_dot_nt(qr, kk[j]) + bias_ref[0, h, :, j * ATT_TQ:(j + 1) * ATT_TQ]
                 for j in range(K_ROW_BLOCKS)]
            s.append(_dot_nt(qp, kc))
            m = s[0].max(axis=-1, keepdims=True)
            for sj in s[1:]:
                m = jnp.maximum(m, sj.max(axis=-1, keepdims=True))
            e = [jnp.exp(sj - m) for sj in s]
            den = e[0].sum(axis=-1, keepdims=True)
            for ej in e[1:]:
                den = den + ej.sum(axis=-1, keepdims=True)
            acc = _dot(e[K_ROW_BLOCKS].astype(BF16), vc)
            for j in range(K_ROW_BLOCKS):
                acc = acc + _dot(e[j].astype(BF16), vv[j])
            outs.append(acc / den)
        o_ref[:, sl] = jnp.where(lane < NA_HEAD_DIM, outs[0], outs[1]).astype(BF16)


def _attn_call(qre, qro, qpe, qpo, k, v, kc, vc, bias, bsz, seq, n_ctx):
    t = bsz * seq
    n_grp = seq // ATT_TQ
    per_b = n_grp
    max_kb = n_grp - K_ROW_BLOCKS

    def qmap(g, b):
        return (b * per_b + g, 0)

    def kmap(j):
        return lambda g, b: (b * per_b + jnp.clip(g - 1, 0, max_kb) + j, 0)

    def cls(g, b):
        return (jnp.where(g == 0, 0, jnp.where(g == n_grp - 1, 2, 1)), 0, 0, 0)

    qspec = pl.BlockSpec((ATT_TQ, NA_WIDTH), qmap)
    cspec = pl.BlockSpec((n_ctx, NA_WIDTH), lambda g, b: (b, 0))
    return pl.pallas_call(
        _attn_body,
        grid=(n_grp, bsz),
        in_specs=[qspec] * 4
                 + [pl.BlockSpec((ATT_TQ, NA_WIDTH), kmap(j)) for j in range(K_ROW_BLOCKS)] * 2
                 + [cspec, cspec, pl.BlockSpec((1, NA_HEADS, ATT_TQ, ATT_TK), cls)],
        out_specs=pl.BlockSpec((ATT_TQ, NA_WIDTH), qmap),
        out_shape=jax.ShapeDtypeStruct((t, NA_WIDTH), BF16),
        compiler_params=_cparams(("arbitrary", "arbitrary")),
        name="attn",
    )(qre, qro, qpe, qpo, k, k, k, v, v, v, kc, vc, bias)


def _shift_down(v, row):
    return jnp.where(row >= 1, pltpu.roll(v, 1, 0), 0.0)


def _shift_up(v, row):
    return jnp.where(row < SUBLANES - 1, pltpu.roll(v, SUBLANES - 1, 0), 0.0)


def _conv4(x, w_ref, b_ref):
    n = x.shape[0]
    s = SUBLANES
    row = lax.broadcasted_iota(I32, (s, LANES), 0)
    last = _shift_down(x[n - s:n], row)
    last2 = _shift_down(x[n - 2 * s:n - s], row)
    first = _shift_up(x[0:s], row)
    xm1 = jnp.concatenate([last, x[0:n - s]], axis=0)
    xm2 = jnp.concatenate([last2, last, x[0:n - 2 * s]], axis=0)
    xp1 = jnp.concatenate([x[s:n], first], axis=0)
    return (w_ref[0:1, :] * xm2 + w_ref[1:2, :] * xm1 + w_ref[2:3, :] * x + w_ref[3:4, :] * xp1) + b_ref[...]


def _softplus(z):
    return jnp.maximum(z, 0.0) + jnp.log1p(jnp.exp(-jnp.abs(z)))


def _gates(xc, d, wa_ref, wx_ref, ba_ref, bx_ref, lam_ref, a_ref, u_ref):
    n = xc.shape[0]
    xb = xc.astype(BF16)
    tr = jnp.tanh(_dot(xb, (0.5 * wa_ref[d, 0]).astype(BF16)) + 0.5 * ba_ref[d:d + 1, :])
    ti = jnp.tanh(_dot(xb, (0.5 * wx_ref[d, 0]).astype(BF16)) + 0.5 * bx_ref[d:d + 1, :])
    half_c = (0.5 * LRU_C) * _softplus(-lam_ref[d:d + 1, :])
    neg_log_a = half_c * tr + half_c
    a = jnp.exp(-neg_log_a)
    a_ref[0:n, :] = a
    s2 = jnp.tanh(neg_log_a) * (a * a + 1.0)
    root = jnp.where(s2 > 0.0, s2 * lax.rsqrt(s2), 0.0)
    xh = 0.5 * xc
    u_ref[0:n, :] = root * (xh * ti + xh)


def _scan4(a, u, h, p):
    a01 = a[1] * a[0]
    u01 = a[1] * u[0] + u[1]
    a23 = a[3] * a[2]
    u23 = a[3] * u[2] + u[3]
    a012 = a[2] * a01
    u012 = a[2] * u01 + u[2]
    a0123 = a23 * a01
    u0123 = a23 * u01 + u23
    hs = [a[0] * h + u[0], a01 * h + u01, a012 * h + u012, a0123 * h + u0123]
    ps = [a[0] * p, a01 * p, a012 * p, a0123 * p]
    return hs, ps


SCAN_STEPS = 4


def _scan_local(af_ref, uf_ref, ab_ref, ub_ref, n_vreg):
    s = SUBLANES
    zero = jnp.zeros((s, LANES), F32)
    one = jnp.ones((s, LANES), F32)
    span = SCAN_STEPS * s

    def body(q, carry):
        hf, pf, hb, pb = carry
        base = pl.multiple_of(q * span, span)
        rows = [pl.ds(base + i * s, s) for i in range(SCAN_STEPS)]
        hs, ps = _scan4([af_ref[r, :] for r in rows], [uf_ref[r, :] for r in rows], hf, pf)
        for r, h, p in zip(rows, hs, ps):
            uf_ref[r, :] = h
            af_ref[r, :] = p
        hf, pf = hs[-1], ps[-1]
        base = pl.multiple_of((n_vreg - SCAN_STEPS) * s - q * span, span)
        rows = [pl.ds(base + (SCAN_STEPS - 1 - i) * s, s) for i in range(SCAN_STEPS)]
        hs, ps = _scan4([ab_ref[r, :] for r in rows], [ub_ref[r, :] for r in rows], hb, pb)
        for r, h, p in zip(rows, hs, ps):
            ub_ref[r, :] = h
            ab_ref[r, :] = p
        return hf, pf, hs[-1], ps[-1]

    return lax.fori_loop(0, n_vreg // SCAN_STEPS, body, (zero, one, zero, one), unroll=2)


def _link_states(hf, pf, hb, pb, h0f, h0b):
    s = SUBLANES
    row = lax.broadcasted_iota(I32, (s, LANES), 0)
    a, u = pf, hf
    for k in (1, 2, 4):
        keep = row >= k
        u = u + a * jnp.where(keep, pltpu.roll(u, k, 0), 0.0)
        a = a * jnp.where(keep, pltpu.roll(a, k, 0), 1.0)
    end_f = u + a * h0f
    in_f = jnp.where(row >= 1, pltpu.roll(end_f, 1, 0), h0f)
    a, u = pb, hb
    for k in (1, 2, 4):
        keep = row < s - k
        u = u + a * jnp.where(keep, pltpu.roll(u, s - k, 0), 0.0)
        a = a * jnp.where(keep, pltpu.roll(a, s - k, 0), 1.0)
    end_b = u + a * h0b
    in_b = jnp.where(row < s - 1, pltpu.roll(end_b, s - 1, 0), h0b)
    return in_f, in_b, end_f[s - 1:s, :], end_b[0:1, :]


def _lru_body(lx_ref, lxc_ref, cw_ref, cb_ref, wa_ref, wx_ref, ba_ref, bx_ref, lam_ref,
              o_ref, af_ref, uf_ref, ab_ref, ub_ref):
    n_j = lx_ref.shape[1]
    n_jc = lxc_ref.shape[1]
    s = SUBLANES
    gate_args = (wa_ref, wx_ref, ba_ref, bx_ref, lam_ref)
    zero = jnp.zeros((1, LANES), F32)

    xc = _conv4(lxc_ref[0].reshape(n_jc * s, LANES), cw_ref, cb_ref)
    _gates(xc, 0, *gate_args, af_ref, uf_ref)
    _gates(xc, 1, *gate_args, ab_ref, ub_ref)
    ends = _scan_local(af_ref, uf_ref, ab_ref, ub_ref, n_jc)
    _, _, cf, cb = _link_states(*ends, zero, zero)

    xl = _conv4(lx_ref[0].reshape(n_j * s, LANES), cw_ref, cb_ref)
    _gates(xl, 0, *gate_args, af_ref, uf_ref)
    _gates(xl, 1, *gate_args, ab_ref, ub_ref)
    ends = _scan_local(af_ref, uf_ref, ab_ref, ub_ref, n_j)
    in_f, in_b, _, _ = _link_states(*ends, cf, cb)
    split = lambda ref: ref[...].reshape(n_j, s, LANES)
    o_ref[0] = (split(uf_ref) + split(af_ref) * in_f[None]) + (split(ub_ref) + split(ab_ref) * in_b[None])


def _lru_call(lx4, lxc4, conv_w, conv_b, wa, wx, ba, bx, lam):
    bsz, n_j, n_s, _ = lx4.shape
    n_jc = lxc4.shape[1]
    col = lambda b, n: (b, 0, 0, n)
    par = lambda b, n: (0, n)
    wspec = pl.BlockSpec((2, 1, LRU_BLOCK, LRU_BLOCK), lambda b, n: (0, n, 0, 0))
    return pl.pallas_call(
        _lru_body,
        grid=(bsz, LRU_BLOCKS),
        in_specs=[pl.BlockSpec((1, n_j, n_s, LRU_BLOCK), col),
                  pl.BlockSpec((1, n_jc, n_s, LRU_BLOCK), col),
                  pl.BlockSpec((LRU_CONV, LRU_BLOCK), par),
                  pl.BlockSpec((1, LRU_BLOCK), par),
                  wspec, wspec,
                  pl.BlockSpec((2, LRU_BLOCK), par),
                  pl.BlockSpec((2, LRU_BLOCK), par),
                  pl.BlockSpec((2, LRU_BLOCK), par)],
        out_specs=pl.BlockSpec((1, n_j, n_s, LRU_BLOCK), col),
        out_shape=jax.ShapeDtypeStruct((bsz, n_j, n_s, LRU_WIDTH), F32),
        scratch_shapes=[pltpu.VMEM((n_j * n_s, LRU_BLOCK), F32)] * 4,
        compiler_params=_cparams(("parallel", "arbitrary")),
        name="lru",
    )(lx4, lxc4, conv_w, conv_b, wa, wx, ba, bx, lam)


def _merge_body(x_ref, oa_ref, hs_ref, glu_ref, ga_ref, gb_ref, mod_ref, g_ref, wua_ref, wul_ref, wo_ref,
                wr_ref, br_ref, x1_ref, h2_ref, lt_ref):
    ga1 = mod_ref[0, :, 2 * D_MODEL:3 * D_MODEL]
    sh2 = mod_ref[0, :, 3 * D_MODEL:4 * D_MODEL]
    sc2 = mod_ref[0, :, 4 * D_MODEL:5 * D_MODEL]
    o_lru = (_load_split(hs_ref) * glu_ref[...].astype(F32)).astype(BF16)
    y = (ga_ref[...].astype(F32) * _dot(oa_ref[...], wua_ref[...])
         + gb_ref[...].astype(F32) * _dot(o_lru, wul_ref[...]))
    x1 = x_ref[...] + ga1 * _dot(y.astype(BF16), wo_ref[...])
    x1_ref[...] = x1
    h2 = _ada_norm(x1, g_ref[...], sc2, sh2)
    lt_ref[...] = _dot3_nt(wr_ref[...], h2) + br_ref[:, 0:1]
    h2_ref[...] = h2


def _merge_call(x2, o_att, hs3, glu, ga, gb, mod3, g_ffn, wua, wul, wo, wr_t, br, seq):
    t = x2.shape[0]
    tm = MERGE_TM
    per_b = seq // tm
    n_j = hs3.shape[1]
    split_w = (tm // n_j) * LRU_WIDTH
    row = lambda i: (i, 0)
    full = lambda i: (0, 0)
    resident = lambda shape: pl.BlockSpec(shape, full, pipeline_mode=pl.Buffered(1))
    return pl.pallas_call(
        _merge_body,
        grid=(t // tm,),
        in_specs=[pl.BlockSpec((tm, D_MODEL), row),
                  pl.BlockSpec((tm, NA_WIDTH), row),
                  pl.BlockSpec((1, n_j, split_w), lambda i: (i // per_b, 0, i % per_b)),
                  pl.BlockSpec((tm, LRU_WIDTH), row),
                  pl.BlockSpec((tm, D_MODEL), row),
                  pl.BlockSpec((tm, D_MODEL), row),
                  pl.BlockSpec((1, 1, 6 * D_MODEL), lambda i: (i // per_b, 0, 0)),
                  pl.BlockSpec((1, D_MODEL), full),
                  resident((NA_WIDTH, D_MODEL)),
                  resident((LRU_WIDTH, D_MODEL)),
                  resident((D_MODEL, D_MODEL)),
                  pl.BlockSpec((ROUTE_ROWS, D_MODEL), full),
                  pl.BlockSpec((ROUTE_ROWS, LANES), full)],
        out_specs=[pl.BlockSpec((tm, D_MODEL), row),
                   pl.BlockSpec((tm, D_MODEL), row),
                   pl.BlockSpec((ROUTE_ROWS, tm), lambda i: (0, i))],
        out_shape=[jax.ShapeDtypeStruct((t, D_MODEL), F32),
                   jax.ShapeDtypeStruct((t, D_MODEL), F32),
                   jax.ShapeDtypeStruct((ROUTE_ROWS, t), F32)],
        compiler_params=_cparams(("parallel",)),
        name="merge",
    )(x2, o_att, hs3, glu, ga, gb, mod3, g_ffn, wua, wul, wo, wr_t, br)


def _route_body(lt_ref, eid_ref, gate_ref, rank_ref, cnt_ref, carry_ref):
    step = pl.program_id(0)

    @pl.when(step == 0)
    def _():
        carry_ref[...] = jnp.zeros_like(carry_ref)

    tb = lt_ref.shape[1]
    lg = [lt_ref[r:r + 1, :] for r in range(N_GROUPS)]
    best = lg[0]
    gidx = jnp.zeros((1, tb), I32)
    for r in range(1, N_GROUPS):
        better = lg[r] > best
        gidx = jnp.where(better, r, gidx)
        best = jnp.maximum(best, lg[r])
    den = jnp.exp(lg[0] - best)
    for r in range(1, N_GROUPS):
        den = den + jnp.exp(lg[r] - best)
    p_top = 1.0 / den

    ev = []
    for j in range(EXPERTS_PER_GROUP):
        sel = lt_ref[N_GROUPS + j:N_GROUPS + j + 1, :]
        for g in range(1, N_GROUPS):
            row = N_GROUPS + g * EXPERTS_PER_GROUP + j
            sel = jnp.where(gidx == g, lt_ref[row:row + 1, :], sel)
        ev.append(sel)
    v0 = ev[0]
    i0 = jnp.zeros((1, tb), I32)
    for j in range(1, EXPERTS_PER_GROUP):
        better = ev[j] > v0
        i0 = jnp.where(better, j, i0)
        v0 = jnp.maximum(v0, ev[j])
    v1 = jnp.full((1, tb), -jnp.inf, F32)
    i1 = jnp.zeros((1, tb), I32)
    for j in range(EXPERTS_PER_GROUP):
        better = (ev[j] > v1) & (i0 != j)
        i1 = jnp.where(better, j, i1)
        v1 = jnp.where(better, ev[j], v1)
    e1 = jnp.exp(v1 - v0)
    inv = 1.0 / (1.0 + e1)
    eid0 = gidx * EXPERTS_PER_GROUP + i0
    eid1 = gidx * EXPERTS_PER_GROUP + i1
    eid_ref[0:1, :] = eid0
    eid_ref[1:2, :] = eid1
    gate_ref[...] = jnp.zeros_like(gate_ref)
    gate_ref[0:1, :] = p_top * inv
    gate_ref[1:2, :] = p_top * (e1 * inv)

    sub = 256
    e_iota = lax.broadcasted_iota(I32, (N_EXPERTS, sub), 0)
    tri = jnp.where(lax.broadcasted_iota(I32, (sub, sub), 0) <= lax.broadcasted_iota(I32, (sub, sub), 1),
                    1.0, 0.0).astype(BF16)
    carry = carry_ref[...]
    for c in range(tb // sub):
        sl = slice(c * sub, (c + 1) * sub)
        m0 = eid0[:, sl] == e_iota
        m1 = eid1[:, sl] == e_iota
        oh = jnp.where(m0 | m1, 1.0, 0.0)
        incl = _dot(oh.astype(BF16), tri)
        excl = incl - oh + carry[:, 0:1]
        rank_ref[0:1, sl] = jnp.sum(jnp.where(m0, excl, 0.0), axis=0, keepdims=True).astype(I32)
        rank_ref[1:2, sl] = jnp.sum(jnp.where(m1, excl, 0.0), axis=0, keepdims=True).astype(I32)
        carry = carry + incl[:, sub - 1:sub]
    carry_ref[...] = carry
    cnt_ref[...] = carry


def _route_call(logits_t):
    t = logits_t.shape[1]
    tb = ROUTE_TB
    col = lambda i: (0, i)
    return pl.pallas_call(
        _route_body,
        grid=(t // tb,),
        in_specs=[pl.BlockSpec((ROUTE_ROWS, tb), col)],
        out_specs=[pl.BlockSpec((TOP_K, tb), col), pl.BlockSpec((SUBLANES, tb), col),
                   pl.BlockSpec((TOP_K, tb), col), pl.BlockSpec((N_EXPERTS, LANES), lambda i: (0, 0))],
        out_shape=[jax.ShapeDtypeStruct((TOP_K, t), I32), jax.ShapeDtypeStruct((SUBLANES, t), F32),
                   jax.ShapeDtypeStruct((TOP_K, t), I32), jax.ShapeDtypeStruct((N_EXPERTS, LANES), F32)],
        scratch_shapes=[pltpu.VMEM((N_EXPERTS, LANES), F32)],
        compiler_params=_cparams(("arbitrary",)),
        name="route",
    )(logits_t)


def _dest_body(cnt_ref, eid_ref, rank_ref, dest_ref, blk_ref):
    cnt = cnt_ref[...].astype(I32)
    padded = ((cnt + (MOE_BLK - 1)) >> MOE_BLK_LOG2) << MOE_BLK_LOG2
    e_iota = lax.broadcasted_iota(I32, (N_EXPERTS, LANES), 0)
    p_end = jnp.zeros((N_EXPERTS, LANES), I32)
    for e in range(N_EXPERTS):
        tot = jnp.sum(jnp.where(e_iota <= e, padded, 0), axis=0, keepdims=True)
        p_end = jnp.where(e_iota == e, tot, p_end)
    p_start = p_end - padded
    tb = eid_ref.shape[1]
    ps = jnp.concatenate([p_start] * (tb // LANES), axis=1)
    e_wide = lax.broadcasted_iota(I32, (N_EXPERTS, tb), 0)
    for k in range(TOP_K):
        start = jnp.sum(jnp.where(eid_ref[k:k + 1, :] == e_wide, ps, 0), axis=0, keepdims=True)
        dest_ref[k:k + 1, :] = start + rank_ref[k:k + 1, :]
    nb = blk_ref.shape[1]
    pe = jnp.concatenate([p_end] * (nb // LANES), axis=1)
    first_row = lax.broadcasted_iota(I32, (N_EXPERTS, nb), 1) * MOE_BLK
    n_before = jnp.sum(jnp.where(pe <= first_row, 1, 0), axis=0, keepdims=True)
    blk = jnp.minimum(n_before, N_EXPERTS - 1)
    blk_ref[...] = jnp.broadcast_to(blk, blk_ref.shape)
    blk_ref[1:2, :] = jnp.broadcast_to(p_end[N_EXPERTS - 1:N_EXPERTS, 0:1] >> MOE_BLK_LOG2, (1, nb))
    on_diag = e_iota == lax.broadcasted_iota(I32, (N_EXPERTS, LANES), 1)
    blk_ref[2:3, 0:LANES] = jnp.sum(jnp.where(on_diag, p_end, 0), axis=0, keepdims=True)
    blk_ref[3:4, 0:LANES] = jnp.sum(jnp.where(on_diag, padded, 0), axis=0, keepdims=True)


def _dest_call(cnt, eid, rank, nb_pad):
    t = eid.shape[1]
    tb = ROUTE_TB
    col = lambda i: (0, i)
    return pl.pallas_call(
        _dest_body,
        grid=(t // tb,),
        in_specs=[pl.BlockSpec((N_EXPERTS, LANES), lambda i: (0, 0)),
                  pl.BlockSpec((TOP_K, tb), col), pl.BlockSpec((TOP_K, tb), col)],
        out_specs=[pl.BlockSpec((TOP_K, tb), col), pl.BlockSpec((SUBLANES, nb_pad), lambda i: (0, 0))],
        out_shape=[jax.ShapeDtypeStruct((TOP_K, t), I32), jax.ShapeDtypeStruct((SUBLANES, nb_pad), I32)],
        compiler_params=_cparams(("arbitrary",)),
        name="dest",
    )(cnt, eid, rank)


def _dispatch_body(dest_ref, pend_ref, plen_ref, h_ref, xs_ref, zero_ref, sem, zsem):
    tm = h_ref.shape[0]
    n_tok = pl.num_programs(0) * tm
    base = pl.program_id(0) * tm

    @pl.when(pl.program_id(0) == 0)
    def _():
        zero_ref[...] = jnp.zeros_like(zero_ref)

        def zero_block(start):
            return pltpu.make_async_copy(zero_ref, xs_ref.at[pl.ds(pl.multiple_of(start, MOE_BLK), MOE_BLK)], zsem)

        def fill(e, c):
            @pl.when(plen_ref[e] > 0)
            def _():
                zero_block(pend_ref[e] - MOE_BLK).start()
            return c

        def drain(e, c):
            @pl.when(plen_ref[e] > 0)
            def _():
                zero_block(pend_ref[e] - MOE_BLK).wait()
            return c

        lax.fori_loop(0, N_EXPERTS, fill, 0)
        n_used = pend_ref[N_EXPERTS - 1] >> MOE_BLK_LOG2
        n_blk = xs_ref.shape[0] // MOE_BLK
        lax.fori_loop(n_used, n_blk, lambda j, c: (zero_block(j * MOE_BLK).start(), c)[1], 0)
        lax.fori_loop(0, N_EXPERTS, drain, 0)
        lax.fori_loop(n_used, n_blk, lambda j, c: (zero_block(j * MOE_BLK).wait(), c)[1], 0)

    def issue(r, c):
        for k in range(TOP_K):
            d = dest_ref[k * n_tok + base + r]
            pltpu.make_async_copy(h_ref.at[pl.ds(r, 1)], xs_ref.at[pl.ds(d, 1)], sem).start()
        return c

    lax.fori_loop(0, tm, issue, 0, unroll=8)
    for k in range(TOP_K):
        pltpu.make_async_copy(h_ref, xs_ref.at[pl.ds(0, tm)], sem).wait()


def _dispatch_call(dest_flat, p_end, p_len, h2, n_slots):
    t, w = h2.shape
    tm = DISPATCH_TM
    return pl.pallas_call(
        _dispatch_body,
        grid_spec=pltpu.PrefetchScalarGridSpec(
            num_scalar_prefetch=3,
            grid=(t // tm,),
            in_specs=[pl.BlockSpec((tm, w), lambda i, d, pe, pn: (i, 0))],
            out_specs=pl.BlockSpec(memory_space=pl.ANY),
            scratch_shapes=[pltpu.VMEM((MOE_BLK, w), F32),
                            pltpu.SemaphoreType.DMA(()), pltpu.SemaphoreType.DMA(())]),
        out_shape=jax.ShapeDtypeStruct((n_slots, w), F32),
        compiler_params=_cparams(("arbitrary",)),
        name="dispatch",
    )(dest_flat, p_end, p_len, h2)


def _experts_body(blk_ref, used_ref, xs_ref, w1_ref, w3_ref, w2_ref, y_ref):
    j = pl.program_id(0)

    @pl.when(j < used_ref[0])
    def _():
        x = xs_ref[...].astype(BF16)
        g = _dot(x, w1_ref[0].astype(BF16))
        u = _dot(x, w3_ref[0].astype(BF16))
        mid = (g * _sigmoid(g)) * u
        y_ref[...] = _dot(mid.astype(BF16), w2_ref[0].astype(BF16))

    @pl.when(j >= used_ref[0])
    def _():
        y_ref[...] = jnp.zeros_like(y_ref)


def _experts_call(blk_e, n_used, xs, w1, w3, w2):
    n_slots, w = xs.shape
    nb = n_slots // MOE_BLK
    wmap = lambda j, blk, used: (blk[j], 0, 0)
    xmap = lambda j, blk, used: (j, 0)
    return pl.pallas_call(
        _experts_body,
        grid_spec=pltpu.PrefetchScalarGridSpec(
            num_scalar_prefetch=2,
            grid=(nb,),
            in_specs=[pl.BlockSpec((MOE_BLK, w), xmap),
                      pl.BlockSpec((1, D_MODEL, D_EXPERT), wmap),
                      pl.BlockSpec((1, D_MODEL, D_EXPERT), wmap),
                      pl.BlockSpec((1, D_EXPERT, D_MODEL), wmap)],
            out_specs=pl.BlockSpec((MOE_BLK, D_MODEL), lambda j, blk, used: (j, 0))),
        out_shape=jax.ShapeDtypeStruct((n_slots, D_MODEL), F32),
        compiler_params=_cparams(("arbitrary",)),
        name="experts",
    )(blk_e, n_used, xs, w1, w3, w2)


def _combine_body(dest_ref, x1_ref, gate_ref, mod_ref, gf_ref, y_ref, o_ref, buf_ref, sem):
    tm = x1_ref.shape[0]
    n_tok = pl.num_programs(0) * tm
    base = pl.program_id(0) * tm

    def issue(r, c):
        for k in range(TOP_K):
            d = dest_ref[k * n_tok + base + r]
            pltpu.make_async_copy(y_ref.at[pl.ds(d, 1)], buf_ref.at[k, pl.ds(r, 1)], sem).start()
        return c

    lax.fori_loop(0, tm, issue, 0, unroll=8)

    eye = jnp.where(lax.broadcasted_iota(I32, (tm, tm), 0) == lax.broadcasted_iota(I32, (tm, tm), 1),
                    1.0, 0.0).astype(BF16)
    g = gate_ref[...]
    g1 = g.astype(BF16)
    rem = g - g1.astype(F32)
    g2 = rem.astype(BF16)
    g3 = (rem - g2.astype(F32)).astype(BF16)
    gt = _dot_nt(eye, g1) + (_dot_nt(eye, g2) + _dot_nt(eye, g3))

    for k in range(TOP_K):
        pltpu.make_async_copy(y_ref.at[pl.ds(0, tm)], buf_ref.at[k], sem).wait()

    ga2 = mod_ref[0, :, 5 * D_MODEL:6 * D_MODEL]
    moe = gt[:, 0:1] * buf_ref[0] + gt[:, 1:2] * buf_ref[1]
    x2 = x1_ref[...] + ga2 * moe
    ms = jnp.mean(x2 * x2, axis=-1, keepdims=True)
    o_ref[...] = x2 * lax.rsqrt(ms + EPS) * gf_ref[...]


def _combine_call(dest_flat, x1, gate, mod3, g_final, y, seq):
    t = x1.shape[0]
    tm = COMBINE_TM
    per_b = seq // tm
    return pl.pallas_call(
        _combine_body,
        grid_spec=pltpu.PrefetchScalarGridSpec(
            num_scalar_prefetch=1,
            grid=(t // tm,),
            in_specs=[pl.BlockSpec((tm, D_MODEL), lambda i, d: (i, 0)),
                      pl.BlockSpec((SUBLANES, tm), lambda i, d: (0, i)),
                      pl.BlockSpec((1, 1, 6 * D_MODEL), lambda i, d: (i // per_b, 0, 0)),
                      pl.BlockSpec((1, D_MODEL), lambda i, d: (0, 0)),
                      pl.BlockSpec(memory_space=pl.ANY)],
            out_specs=pl.BlockSpec((tm, D_MODEL), lambda i, d: (i, 0)),
            scratch_shapes=[pltpu.VMEM((TOP_K, tm, D_MODEL), F32), pltpu.SemaphoreType.DMA(())]),
        out_shape=jax.ShapeDtypeStruct((t, D_MODEL), F32),
        compiler_params=_cparams(("arbitrary",)),
        name="combine",
    )(dest_flat, x1, gate, mod3, g_final, y)


def _rope_tables(seq):
    half = NA_HEAD_DIM // 2
    nf = half // 2
    inv_freq = ROPE_THETA ** (-jnp.arange(nf, dtype=F32) / nf)
    t = jnp.arange(seq)
    row_pos = (t // GRID_W).astype(F32)
    col_pos = (t % GRID_W).astype(F32)
    ang_r = row_pos[:, None] * inv_freq
    ang_c = col_pos[:, None] * inv_freq
    cos = jnp.concatenate([jnp.cos(ang_r), jnp.cos(ang_r), jnp.cos(ang_c), jnp.cos(ang_c)], axis=-1)
    sin = jnp.concatenate([-jnp.sin(ang_r), jnp.sin(ang_r), -jnp.sin(ang_c), jnp.sin(ang_c)], axis=-1)
    return jnp.tile(cos, (1, NA_HEADS)), jnp.tile(sin, (1, NA_HEADS))


def _layer(x, c, ctx, c_ctx, w_mod, b_mod, g_mix, g_ffn, w_in, rpb, conv_w, conv_b, lru_wa, lru_ba,
           lru_wx, lru_bx, lru_lambda, w_up_attn, w_up_lru, w_out, wg, bg, we, be, w1, w3, w2, g_final):
    bsz, seq, d = x.shape
    n_ctx = ctx.shape[1]
    t = bsz * seq
    assert d == D_MODEL and seq % ATT_TQ == 0 and seq // ATT_TQ > K_ROW_BLOCKS
    assert bsz + 1 <= MOD_ROWS and seq % PROJ_TM == 0 and seq % MERGE_TM == 0
    assert t % ROUTE_TB == 0 and t % DISPATCH_TM == 0
    n_j, n_jc = seq // LRU_SPLIT, n_ctx // LRU_SPLIT
    assert PROJ_TM % n_j == 0 and MERGE_TM % n_j == 0 and n_j % 16 == 0 and n_jc % 16 == 0

    cc = jnp.concatenate([c, c_ctx[None, :], jnp.zeros((MOD_ROWS - bsz - 1, d), F32)], axis=0)
    mod = _mod_call(cc, w_mod, b_mod)
    mod3 = mod[:bsz].reshape(bsz, 1, 6 * d)
    mod_c = mod[bsz:bsz + 1]

    x2 = x.reshape(t, d)
    g_mix2 = g_mix.reshape(1, d)
    w_in_bf = w_in.astype(BF16)
    kc, vc, lxc = _ctxproj_call(ctx.reshape(bsz * n_ctx, d), mod_c, g_mix2, w_in_bf[:, :CTX_COLS], n_ctx)
    cos_t, sin_t = _rope_tables(seq)
    qre, qro, qpe, qpo, k, v, lx, glu, ga, gb = _inproj_call(x2, mod3, g_mix2, w_in_bf, cos_t, sin_t, seq)

    bias = _bias_tables(_rpbcol_call(rpb))
    o_att = _attn_call(qre, qro, qpe, qpo, k, v, kc, vc, bias, bsz, seq, n_ctx)

    hs = _lru_call(lx.reshape(bsz, n_j, LRU_SPLIT, LRU_WIDTH), lxc.reshape(bsz, n_jc, LRU_SPLIT, LRU_WIDTH),
                   conv_w, conv_b.reshape(1, LRU_WIDTH), lru_wa, lru_wx, lru_ba, lru_bx, lru_lambda)

    wr_t = jnp.concatenate([wg.T, we.T, jnp.zeros((ROUTE_ROWS - N_GROUPS - N_EXPERTS, d), F32)], axis=0)
    br = jnp.concatenate([bg, be, jnp.zeros((ROUTE_ROWS - N_GROUPS - N_EXPERTS,), F32)])
    br = jnp.broadcast_to(br[:, None], (ROUTE_ROWS, LANES))
    x1, h2, logits_t = _merge_call(x2, o_att, hs.reshape(bsz, n_j, LRU_SPLIT * LRU_WIDTH), glu, ga, gb, mod3,
                                   g_ffn.reshape(1, d), w_up_attn.astype(BF16), w_up_lru.astype(BF16),
                                   w_out.astype(BF16), wr_t, br, seq)

    eid, gate, rank, cnt = _route_call(logits_t)
    n_blk = -(-(t * TOP_K + N_EXPERTS * (MOE_BLK - 1)) // MOE_BLK)
    nb_pad = -(-n_blk // LANES) * LANES
    dest, blk = _dest_call(cnt, eid, rank, nb_pad)
    dest_flat = dest.reshape(TOP_K * t)
    xs = _dispatch_call(dest_flat, blk[2, :N_EXPERTS], blk[3, :N_EXPERTS], h2, n_blk * MOE_BLK)
    y = _experts_call(blk[0, :n_blk], blk[1, :1], xs, w1, w3, w2)
    return _combine_call(dest_flat, x1, gate, mod3, g_final.reshape(1, d), y, seq).reshape(bsz, seq, d)


def kernel(x, c, ctx, c_ctx, w_mod, b_mod, g_mix, g_ffn, w_in, rpb, conv_w, conv_b, lru_wa, lru_ba, lru_wx,
           lru_bx, lru_lambda, w_up_attn, w_up_lru, w_out, router_group_w, router_group_b, router_expert_w,
           router_expert_b, expert_w_gate, expert_w_up, expert_w_down, g_final):
    assert w_mod.shape[0] == 1, "single-layer block"
    return _layer(x, c, ctx, c_ctx, w_mod[0], b_mod[0], g_mix[0], g_ffn[0], w_in[0], rpb[0], conv_w[0],
                  conv_b[0], lru_wa[0], lru_ba[0], lru_wx[0], lru_bx[0], lru_lambda[0], w_up_attn[0],
                  w_up_lru[0], w_out[0], router_group_w[0], router_group_b[0], router_expert_w[0],
                  router_expert_b[0], expert_w_gate[0], expert_w_up[0], expert_w_down[0], g_final)
```

```python
import functools

import numpy as np
import jax
import jax.numpy as jnp
from jax import lax
from jax.experimental import pallas as pl
from jax.experimental.pallas import tpu as pltpu

F32 = jnp.float32
BF16 = jnp.bfloat16
I32 = jnp.int32
U32 = jnp.uint32

D_MODEL = 1024
GRID_W = 64
EPS = 1e-6
NEG_INF = -1e30

NA_HEADS = 8
NA_HEAD_DIM = 64
NA_WIDTH = NA_HEADS * NA_HEAD_DIM
NA_WIN_ROWS = 8
NA_WIN_COLS = 16
ROPE_THETA = 10000.0

LRU_WIDTH = D_MODEL
LRU_BLOCKS = 8
LRU_BLOCK = LRU_WIDTH // LRU_BLOCKS
LRU_CONV = 4
LRU_C = 8.0

N_GROUPS = 4
EXPERTS_PER_GROUP = 8
N_EXPERTS = N_GROUPS * EXPERTS_PER_GROUP
TOP_K = 2
D_EXPERT = 512

K_OFF = 0
V_OFF = K_OFF + NA_WIDTH
LX_OFF = V_OFF + NA_WIDTH
CTX_COLS = LX_OFF + LRU_WIDTH
Q_OFF = CTX_COLS
LG_OFF = Q_OFF + NA_WIDTH
GA_OFF = LG_OFF + LRU_WIDTH
GB_OFF = GA_OFF + D_MODEL
PROJ_COLS = GB_OFF + D_MODEL

LANES = 128
SUBLANES = 8

Q_ROWS = 4
K_ROW_BLOCKS = 3
ATT_TQ = Q_ROWS * GRID_W
ATT_TK = K_ROW_BLOCKS * ATT_TQ

MOE_BLK_LOG2 = 8
MOE_BLK = 1 << MOE_BLK_LOG2
MOD_ROWS = 24
ROUTE_ROWS = 64

PROJ_TM = 512
MERGE_TM = 512
ROUTE_TB = 2048
COMBINE_TM = 256

VMEM_LIMIT = 56 * 1024 * 1024


def _cparams(sem, vmem=VMEM_LIMIT):
    return pltpu.CompilerParams(dimension_semantics=sem, vmem_limit_bytes=vmem)


def _dot(a, b):
    return jnp.dot(a, b, preferred_element_type=F32)


def _dot_nt(a, b):
    return lax.dot_general(a, b, (((1,), (1,)), ((), ())), preferred_element_type=F32)


def _split2(a):
    hi = a.astype(BF16)
    lo = (a - hi.astype(F32)).astype(BF16)
    return hi, lo


def _dot3(a, b):
    ah, al = _split2(a)
    bh, bl = _split2(b)
    return _dot(ah, bh) + (_dot(ah, bl) + _dot(al, bh))


def _dot3_nt(a, b):
    ah, al = _split2(a)
    bh, bl = _split2(b)
    return _dot_nt(ah, bh) + (_dot_nt(ah, bl) + _dot_nt(al, bh))


def _sigmoid(x):
    return 1.0 / (1.0 + jnp.exp(-x))


def _ada_norm(x, g, sc, sh):
    ms = jnp.mean(x * x, axis=-1, keepdims=True)
    return (x * lax.rsqrt(ms + EPS) * g) * (1.0 + sc) + sh


def _mod_body(cc_ref, w_ref, b_ref, o_ref):
    cc = cc_ref[...]
    o_ref[...] = _dot3(cc * _sigmoid(cc), w_ref[...]) + b_ref[...]


def _mod_call(cc, w_mod, b_mod):
    n = w_mod.shape[1]
    bn = 1024
    return pl.pallas_call(
        _mod_body,
        grid=(n // bn,),
        in_specs=[pl.BlockSpec((MOD_ROWS, D_MODEL), lambda j: (0, 0)),
                  pl.BlockSpec((D_MODEL, bn), lambda j: (0, j)),
                  pl.BlockSpec((1, bn), lambda j: (0, j))],
        out_specs=pl.BlockSpec((MOD_ROWS, bn), lambda j: (0, j)),
        out_shape=jax.ShapeDtypeStruct((MOD_ROWS, n), F32),
        compiler_params=_cparams(("arbitrary",)),
        name="mod",
    )(cc, w_mod, b_mod.reshape(1, n))


def _rope(t, cos, sin):
    lane = lax.broadcasted_iota(I32, (t.shape[0], LANES), 1)
    first = (lane & 16) == 0
    parts = []
    for c in range(t.shape[1] // LANES):
        tc = t[:, c * LANES:(c + 1) * LANES]
        parts.append(jnp.where(first, pltpu.roll(tc, LANES - 16, 1), pltpu.roll(tc, 16, 1)))
    partner = jnp.concatenate(parts, axis=1)
    return t * cos + partner * sin


def _gelu_tanh(x):
    return 0.5 * x * (1.0 + jnp.tanh(0.7978845608028654 * (x + 0.044715 * (x * x * x))))


def _inproj_body(x_ref, mod_ref, g_ref, w_ref, cos_ref, sin_ref,
                 qre_ref, qro_ref, qpe_ref, qpo_ref, k_ref, v_ref, lx_ref, glu_ref, ga_ref, gb_ref):
    sh = mod_ref[0, :, 0:D_MODEL]
    sc = mod_ref[0, :, D_MODEL:2 * D_MODEL]
    h = _ada_norm(x_ref[...], g_ref[...], sc, sh).astype(BF16)
    cos = cos_ref[...]
    sin = sin_ref[...]
    scale = NA_HEAD_DIM ** -0.5

    k_ref[...] = _rope(_dot(h, w_ref[:, K_OFF:K_OFF + NA_WIDTH]), cos, sin).astype(BF16)
    v_ref[...] = _dot(h, w_ref[:, V_OFF:V_OFF + NA_WIDTH]).astype(BF16)
    lx_ref[...] = _dot(h, w_ref[:, LX_OFF:LX_OFF + LRU_WIDTH])

    q = _dot(h, w_ref[:, Q_OFF:Q_OFF + NA_WIDTH]) * scale
    qr = _rope(q, cos, sin)
    lane = lax.broadcasted_iota(I32, q.shape, 1)
    even = (lane & NA_HEAD_DIM) == 0
    qre_ref[...] = jnp.where(even, qr, 0.0).astype(BF16)
    qro_ref[...] = jnp.where(even, 0.0, qr).astype(BF16)
    qpe_ref[...] = jnp.where(even, q, 0.0).astype(BF16)
    qpo_ref[...] = jnp.where(even, 0.0, q).astype(BF16)

    glu_ref[...] = _gelu_tanh(_dot(h, w_ref[:, LG_OFF:LG_OFF + LRU_WIDTH])).astype(BF16)
    ga_ref[...] = _sigmoid(_dot(h, w_ref[:, GA_OFF:GA_OFF + D_MODEL])).astype(BF16)
    gb_ref[...] = _sigmoid(_dot(h, w_ref[:, GB_OFF:GB_OFF + D_MODEL])).astype(BF16)


def _inproj_call(x2, mod3, g_mix, w_in_bf, cos_t, sin_t, seq):
    t = x2.shape[0]
    tm = PROJ_TM
    per_b = seq // tm
    row = lambda i: (i, 0)
    wide = lambda n, dt: jax.ShapeDtypeStruct((t, n), dt)
    return pl.pallas_call(
        _inproj_body,
        grid=(t // tm,),
        in_specs=[pl.BlockSpec((tm, D_MODEL), row),
                  pl.BlockSpec((1, 1, 6 * D_MODEL), lambda i: (i // per_b, 0, 0)),
                  pl.BlockSpec((1, D_MODEL), lambda i: (0, 0)),
                  pl.BlockSpec((D_MODEL, PROJ_COLS), lambda i: (0, 0), pipeline_mode=pl.Buffered(1)),
                  pl.BlockSpec((tm, NA_WIDTH), lambda i: (i % per_b, 0)),
                  pl.BlockSpec((tm, NA_WIDTH), lambda i: (i % per_b, 0))],
        out_specs=[pl.BlockSpec((tm, NA_WIDTH), row)] * 6
                  + [pl.BlockSpec((tm, LRU_WIDTH), row)] * 4,
        out_shape=[wide(NA_WIDTH, BF16)] * 6
                  + [wide(LRU_WIDTH, F32), wide(LRU_WIDTH, BF16), wide(D_MODEL, BF16), wide(D_MODEL, BF16)],
        compiler_params=_cparams(("parallel",)),
        name="inproj",
    )(x2, mod3, g_mix, w_in_bf, cos_t, sin_t)


def _ctxproj_body(x_ref, mod_ref, g_ref, w_ref, k_ref, v_ref, lx_ref):
    sh = mod_ref[:, 0:D_MODEL]
    sc = mod_ref[:, D_MODEL:2 * D_MODEL]
    h = _ada_norm(x_ref[...], g_ref[...], sc, sh).astype(BF16)
    k_ref[...] = _dot(h, w_ref[:, K_OFF:K_OFF + NA_WIDTH]).astype(BF16)
    v_ref[...] = _dot(h, w_ref[:, V_OFF:V_OFF + NA_WIDTH]).astype(BF16)
    lx_ref[...] = _dot(h, w_ref[:, LX_OFF:LX_OFF + LRU_WIDTH])


def _ctxproj_call(c2, mod_c, g_mix, w_ctx_bf):
    t = c2.shape[0]
    tm = PROJ_TM
    row = lambda i: (i, 0)
    return pl.pallas_call(
        _ctxproj_body,
        grid=(t // tm,),
        in_specs=[pl.BlockSpec((tm, D_MODEL), row),
                  pl.BlockSpec((1, 6 * D_MODEL), lambda i: (0, 0)),
                  pl.BlockSpec((1, D_MODEL), lambda i: (0, 0)),
                  pl.BlockSpec((D_MODEL, CTX_COLS), lambda i: (0, 0))],
        out_specs=[pl.BlockSpec((tm, NA_WIDTH), row), pl.BlockSpec((tm, NA_WIDTH), row),
                   pl.BlockSpec((tm, LRU_WIDTH), row)],
        out_shape=[jax.ShapeDtypeStruct((t, NA_WIDTH), BF16), jax.ShapeDtypeStruct((t, NA_WIDTH), BF16),
                   jax.ShapeDtypeStruct((t, LRU_WIDTH), F32)],
        compiler_params=_cparams(("parallel",)),
        name="ctxproj",
    )(c2, mod_c, g_mix, w_ctx_bf)


N_DR = 2 * NA_WIN_ROWS - 1
N_DC = 2 * NA_WIN_COLS - 1


def _rpbcol_body(rpb_ref, o_ref):
    n = GRID_W * GRID_W
    flat = lax.broadcasted_iota(I32, (32, n), 1)
    qc = flat >> 6
    kc = flat & (GRID_W - 1)
    dc = jnp.clip(kc - qc, 1 - NA_WIN_COLS, NA_WIN_COLS - 1) + (NA_WIN_COLS - 1)
    d_iota = lax.broadcasted_iota(I32, (32, n), 0)
    onehot = jnp.where(dc == d_iota, 1.0, 0.0).astype(BF16)
    r = rpb_ref[...]
    r1 = r.astype(BF16)
    rem = r - r1.astype(F32)
    r2 = rem.astype(BF16)
    r3 = (rem - r2.astype(F32)).astype(BF16)
    val = _dot(r1, onehot) + (_dot(r2, onehot) + _dot(r3, onehot))
    qc1 = qc[0:1, :]
    kc1 = kc[0:1, :]
    c_start = jnp.clip(qc1 - NA_WIN_COLS // 2, 0, GRID_W - NA_WIN_COLS)
    band = (kc1 >= c_start) & (kc1 < c_start + NA_WIN_COLS)
    o_ref[...] = jnp.where(band, val, NEG_INF)


def _rpbcol_call(rpb):
    rows = NA_HEADS * N_DR
    r2 = jnp.pad(rpb.reshape(rows, N_DC), ((0, 0), (0, 32 - N_DC)))
    n = GRID_W * GRID_W
    return pl.pallas_call(
        _rpbcol_body,
        in_specs=[pl.BlockSpec((rows, 32), lambda: (0, 0))],
        out_specs=pl.BlockSpec((rows, n), lambda: (0, 0)),
        out_shape=jax.ShapeDtypeStruct((rows, n), F32),
        name="rpbcol",
    )(r2)


def _bias_tables(rpbcol):
    t = rpbcol.reshape(NA_HEADS, N_DR, GRID_W, GRID_W)
    neg = jnp.full((NA_HEADS, GRID_W, GRID_W), NEG_INF, F32)
    n_kj = K_ROW_BLOCKS * Q_ROWS
    classes = []
    for lo_fn, dr_off in ((lambda ri: 0, 7), (lambda ri: ri, 3), (lambda ri: 4, -1)):
        rows = []
        for ri in range(Q_ROWS):
            lo = lo_fn(ri)
            blocks = []
            for kj in range(n_kj):
                inside = lo <= kj < lo + NA_WIN_ROWS
                blocks.append(t[:, kj - ri + dr_off] if inside else neg)
            rows.append(jnp.concatenate(blocks, axis=2))
        classes.append(jnp.concatenate(rows, axis=1))
    return jnp.stack(classes, axis=0)


def _attn_body(qre_ref, qro_ref, qpe_ref, qpo_ref, k0_ref, k1_ref, k2_ref, v0_ref, v1_ref, v2_ref,
               kc_ref, vc_ref, bias_ref, o_ref):
    lane = lax.broadcasted_iota(I32, (ATT_TQ, LANES), 1)
    k_refs = (k0_ref, k1_ref, k2_ref)
    v_refs = (v0_ref, v1_ref, v2_ref)
    for p in range(NA_HEADS // 2):
        sl = slice(p * LANES, (p + 1) * LANES)
        kk = [r[:, sl] for r in k_refs]
        vv = [r[:, sl] for r in v_refs]
        kc = kc_ref[:, sl]
        vc = vc_ref[:, sl]
        outs = []
        for hh, (qr_ref, qp_ref) in enumerate(((qre_ref, qpe_ref), (qro_ref, qpo_ref))):
            h = 2 * p + hh
            qr = qr_ref[:, sl]
            qp = qp_ref[:, sl]
            s = [_dot_nt(qr, kk[j]) + bias_ref[0, h, :, j * ATT_TQ:(j + 1) * ATT_TQ]
                 for j in range(K_ROW_BLOCKS)]
            s.append(_dot_nt(qp, kc))
            m = s[0].max(axis=-1, keepdims=True)
            for sj in s[1:]:
                m = jnp.maximum(m, sj.max(axis=-1, keepdims=True))
            e = [jnp.exp(sj - m) for sj in s]
            den = e[0].sum(axis=-1, keepdims=True)
            for ej in e[1:]:
                den = den + ej.sum(axis=-1, keepdims=True)
            acc = _dot(e[K_ROW_BLOCKS].astype(BF16), vc)
            for j in range(K_ROW_BLOCKS):
                acc = acc + _dot(e[j].astype(BF16), vv[j])
            outs.append(acc / den)
        o_ref[:, sl] = jnp.where(lane < NA_HEAD_DIM, outs[0], outs[1]).astype(BF16)


def _attn_call(qre, qro, qpe, qpo, k, v, kc, vc, bias, bsz, seq, n_ctx):
    t = bsz * seq
    n_grp = seq // ATT_TQ
    per_b = n_grp
    max_kb = n_grp - K_ROW_BLOCKS

    def qmap(g, b):
        return (b * per_b + g, 0)

    def kmap(j):
        return lambda g, b: (b * per_b + jnp.clip(g - 1, 0, max_kb) + j, 0)

    def cls(g, b):
        return (jnp.where(g == 0, 0, jnp.where(g == n_grp - 1, 2, 1)), 0, 0, 0)

    qspec = pl.BlockSpec((ATT_TQ, NA_WIDTH), qmap)
    cspec = pl.BlockSpec((n_ctx, NA_WIDTH), lambda g, b: (b, 0))
    return pl.pallas_call(
        _attn_body,
        grid=(n_grp, bsz),
        in_specs=[qspec] * 4
                 + [pl.BlockSpec((ATT_TQ, NA_WIDTH), kmap(j)) for j in range(K_ROW_BLOCKS)] * 2
                 + [cspec, cspec, pl.BlockSpec((1, NA_HEADS, ATT_TQ, ATT_TK), cls)],
        out_specs=pl.BlockSpec((ATT_TQ, NA_WIDTH), qmap),
        out_shape=jax.ShapeDtypeStruct((t, NA_WIDTH), BF16),
        compiler_params=_cparams(("arbitrary", "arbitrary")),
        name="attn",
    )(qre, qro, qpe, qpo, k, k, k, v, v, v, kc, vc, bias)


def _shift_down(v, row):
    return jnp.where(row >= 1, pltpu.roll(v, 1, 0), 0.0)


def _shift_up(v, row):
    return jnp.where(row < SUBLANES - 1, pltpu.roll(v, SUBLANES - 1, 0), 0.0)


def _conv4(x, w_ref, b_ref):
    n = x.shape[0]
    s = SUBLANES
    row = lax.broadcasted_iota(I32, (s, LANES), 0)
    last = _shift_down(x[n - s:n], row)
    last2 = _shift_down(x[n - 2 * s:n - s], row)
    first = _shift_up(x[0:s], row)
    xm1 = jnp.concatenate([last, x[0:n - s]], axis=0)
    xm2 = jnp.concatenate([last2, last, x[0:n - 2 * s]], axis=0)
    xp1 = jnp.concatenate([x[s:n], first], axis=0)
    return (w_ref[0:1, :] * xm2 + w_ref[1:2, :] * xm1 + w_ref[2:3, :] * x + w_ref[3:4, :] * xp1) + b_ref[...]


def _softplus(z):
    return jnp.maximum(z, 0.0) + jnp.log1p(jnp.exp(-jnp.abs(z)))


def _gates(xc, d, wa_ref, wx_ref, ba_ref, bx_ref, lam_ref, a_ref, u_ref):
    n = xc.shape[0]
    xb = xc.astype(BF16)
    tr = jnp.tanh(_dot(xb, (0.5 * wa_ref[d, 0]).astype(BF16)) + 0.5 * ba_ref[d:d + 1, :])
    ti = jnp.tanh(_dot(xb, (0.5 * wx_ref[d, 0]).astype(BF16)) + 0.5 * bx_ref[d:d + 1, :])
    half_c = (0.5 * LRU_C) * _softplus(-lam_ref[d:d + 1, :])
    neg_log_a = half_c * tr + half_c
    a = jnp.exp(-neg_log_a)
    a_ref[0:n, :] = a
    s2 = jnp.tanh(neg_log_a) * (a * a + 1.0)
    root = jnp.where(s2 > 0.0, s2 * lax.rsqrt(s2), 0.0)
    xh = 0.5 * xc
    u_ref[0:n, :] = root * (xh * ti + xh)


def _scan4(a, u, h, p):
    a01 = a[1] * a[0]
    u01 = a[1] * u[0] + u[1]
    a23 = a[3] * a[2]
    u23 = a[3] * u[2] + u[3]
    a012 = a[2] * a01
    u012 = a[2] * u01 + u[2]
    a0123 = a23 * a01
    u0123 = a23 * u01 + u23
    hs = [a[0] * h + u[0], a01 * h + u01, a012 * h + u012, a0123 * h + u0123]
    ps = [a[0] * p, a01 * p, a012 * p, a0123 * p]
    return hs, ps


SCAN_STEPS = 4


def _scan_local(af_ref, uf_ref, ab_ref, ub_ref, n_vreg):
    s = SUBLANES
    zero = jnp.zeros((s, LANES), F32)
    one = jnp.ones((s, LANES), F32)
    span = SCAN_STEPS * s

    def body(q, carry):
        hf, pf, hb, pb = carry
        base = pl.multiple_of(q * span, span)
        rows = [pl.ds(base + i * s, s) for i in range(SCAN_STEPS)]
        hs, ps = _scan4([af_ref[r, :] for r in rows], [uf_ref[r, :] for r in rows], hf, pf)
        for r, h, p in zip(rows, hs, ps):
            uf_ref[r, :] = h
            af_ref[r, :] = p
        hf, pf = hs[-1], ps[-1]
        base = pl.multiple_of((n_vreg - SCAN_STEPS) * s - q * span, span)
        rows = [pl.ds(base + (SCAN_STEPS - 1 - i) * s, s) for i in range(SCAN_STEPS)]
        hs, ps = _scan4([ab_ref[r, :] for r in rows], [ub_ref[r, :] for r in rows], hb, pb)
        for r, h, p in zip(rows, hs, ps):
            ub_ref[r, :] = h
            ab_ref[r, :] = p
        return hf, pf, hs[-1], ps[-1]

    return lax.fori_loop(0, n_vreg // SCAN_STEPS, body, (zero, one, zero, one), unroll=2)


def _link_states(hf, pf, hb, pb, h0f, h0b):
    s = SUBLANES
    row = lax.broadcasted_iota(I32, (s, LANES), 0)
    a, u = pf, hf
    for k in (1, 2, 4):
        keep = row >= k
        u = u + a * jnp.where(keep, pltpu.roll(u, k, 0), 0.0)
        a = a * jnp.where(keep, pltpu.roll(a, k, 0), 1.0)
    end_f = u + a * h0f
    in_f = jnp.where(row >= 1, pltpu.roll(end_f, 1, 0), h0f)
    a, u = pb, hb
    for k in (1, 2, 4):
        keep = row < s - k
        u = u + a * jnp.where(keep, pltpu.roll(u, s - k, 0), 0.0)
        a = a * jnp.where(keep, pltpu.roll(a, s - k, 0), 1.0)
    end_b = u + a * h0b
    in_b = jnp.where(row < s - 1, pltpu.roll(end_b, s - 1, 0), h0b)
    return in_f, in_b, end_f[s - 1:s, :], end_b[0:1, :]


PITCH_PAD = 4


def _to_split(x_ref, pad_ref, dst_ref, n):
    s = SUBLANES
    n_j = n // s
    pitch = n_j + PITCH_PAD
    for q in range(s):
        pad_ref[pl.ds(q * pitch, n_j), :] = x_ref[0, pl.ds(q * n_j, n_j), :]

    def body(j, c):
        dst_ref[pl.ds(pl.multiple_of(j * s, s), s), :] = pad_ref[pl.ds(j, s, stride=pitch), :]
        return c

    lax.fori_loop(0, n_j, body, 0, unroll=8)


def _lru_body(lx_ref, lxc_ref, cw_ref, cb_ref, wa_ref, wx_ref, ba_ref, bx_ref, lam_ref,
              o_ref, af_ref, uf_ref, ab_ref, ub_ref, pad_ref):
    n = lx_ref.shape[1]
    n_c = lxc_ref.shape[1]
    s = SUBLANES
    n_j = n // s
    pitch = n_j + PITCH_PAD
    gate_args = (wa_ref, wx_ref, ba_ref, bx_ref, lam_ref)
    zero = jnp.zeros((1, LANES), F32)

    _to_split(lxc_ref, pad_ref, uf_ref, n_c)
    xc = _conv4(uf_ref[0:n_c, :], cw_ref, cb_ref)
    _gates(xc, 0, *gate_args, af_ref, uf_ref)
    _gates(xc, 1, *gate_args, ab_ref, ub_ref)
    ends = _scan_local(af_ref, uf_ref, ab_ref, ub_ref, n_c // s)
    _, _, cf, cb = _link_states(*ends, zero, zero)

    _to_split(lx_ref, pad_ref, uf_ref, n)
    xl = _conv4(uf_ref[...], cw_ref, cb_ref)
    _gates(xl, 0, *gate_args, af_ref, uf_ref)
    _gates(xl, 1, *gate_args, ab_ref, ub_ref)
    ends = _scan_local(af_ref, uf_ref, ab_ref, ub_ref, n_j)
    in_f, in_b, _, _ = _link_states(*ends, cf, cb)

    def finish(j, c):
        rows = pl.ds(pl.multiple_of(j * s, s), s)
        h = (uf_ref[rows, :] + af_ref[rows, :] * in_f) + (ub_ref[rows, :] + ab_ref[rows, :] * in_b)
        pad_ref[pl.ds(j, s, stride=pitch), :] = h
        return c

    lax.fori_loop(0, n_j, finish, 0, unroll=8)
    for q in range(s):
        o_ref[0, pl.ds(q * n_j, n_j), :] = pad_ref[pl.ds(q * pitch, n_j), :]


def _lru_call(lx3, lxc3, conv_w, conv_b, wa, wx, ba, bx, lam):
    bsz, seq, _ = lx3.shape
    n_ctx = lxc3.shape[1]
    col = lambda b, n: (b, 0, n)
    par = lambda b, n: (0, n)
    wspec = pl.BlockSpec((2, 1, LRU_BLOCK, LRU_BLOCK), lambda b, n: (0, n, 0, 0))
    return pl.pallas_call(
        _lru_body,
        grid=(bsz, LRU_BLOCKS),
        in_specs=[pl.BlockSpec((1, seq, LRU_BLOCK), col),
                  pl.BlockSpec((1, n_ctx, LRU_BLOCK), col),
                  pl.BlockSpec((LRU_CONV, LRU_BLOCK), par),
                  pl.BlockSpec((1, LRU_BLOCK), par),
                  wspec, wspec,
                  pl.BlockSpec((2, LRU_BLOCK), par),
                  pl.BlockSpec((2, LRU_BLOCK), par),
                  pl.BlockSpec((2, LRU_BLOCK), par)],
        out_specs=pl.BlockSpec((1, seq, LRU_BLOCK), col),
        out_shape=jax.ShapeDtypeStruct((bsz, seq, LRU_WIDTH), F32),
        scratch_shapes=[pltpu.VMEM((seq, LRU_BLOCK), F32)] * 4
                       + [pltpu.VMEM((seq + SUBLANES * PITCH_PAD, LRU_BLOCK), F32)],
        compiler_params=_cparams(("parallel", "arbitrary")),
        name="lru",
    )(lx3, lxc3, conv_w, conv_b, wa, wx, ba, bx, lam)


def _merge_body(x_ref, oa_ref, hs_ref, glu_ref, ga_ref, gb_ref, mod_ref, g_ref, wua_ref, wul_ref, wo_ref,
                wr_ref, br_ref, x1_ref, h2_ref, lt_ref):
    ga1 = mod_ref[0, :, 2 * D_MODEL:3 * D_MODEL]
    sh2 = mod_ref[0, :, 3 * D_MODEL:4 * D_MODEL]
    sc2 = mod_ref[0, :, 4 * D_MODEL:5 * D_MODEL]
    o_lru = (hs_ref[...] * glu_ref[...].astype(F32)).astype(BF16)
    y = (ga_ref[...].astype(F32) * _dot(oa_ref[...], wua_ref[...])
         + gb_ref[...].astype(F32) * _dot(o_lru, wul_ref[...]))
    x1 = x_ref[...] + ga1 * _dot(y.astype(BF16), wo_ref[...])
    x1_ref[...] = x1
    h2 = _ada_norm(x1, g_ref[...], sc2, sh2)
    lt_ref[...] = _dot3_nt(wr_ref[...], h2) + br_ref[:, 0:1]
    h2_ref[...] = h2


def _merge_call(x2, o_att, hs, glu, ga, gb, mod3, g_ffn, wua, wul, wo, wr_t, br, seq):
    t = x2.shape[0]
    tm = MERGE_TM
    per_b = seq // tm
    row = lambda i: (i, 0)
    full = lambda i: (0, 0)
    resident = lambda shape: pl.BlockSpec(shape, full, pipeline_mode=pl.Buffered(1))
    return pl.pallas_call(
        _merge_body,
        grid=(t // tm,),
        in_specs=[pl.BlockSpec((tm, D_MODEL), row),
                  pl.BlockSpec((tm, NA_WIDTH), row),
                  pl.BlockSpec((tm, LRU_WIDTH), row),
                  pl.BlockSpec((tm, LRU_WIDTH), row),
                  pl.BlockSpec((tm, D_MODEL), row),
                  pl.BlockSpec((tm, D_MODEL), row),
                  pl.BlockSpec((1, 1, 6 * D_MODEL), lambda i: (i // per_b, 0, 0)),
                  pl.BlockSpec((1, D_MODEL), full),
                  resident((NA_WIDTH, D_MODEL)),
                  resident((LRU_WIDTH, D_MODEL)),
                  resident((D_MODEL, D_MODEL)),
                  pl.BlockSpec((ROUTE_ROWS, D_MODEL), full),
                  pl.BlockSpec((ROUTE_ROWS, LANES), full)],
        out_specs=[pl.BlockSpec((tm, D_MODEL), row),
                   pl.BlockSpec((tm, D_MODEL), row),
                   pl.BlockSpec((ROUTE_ROWS, tm), lambda i: (0, i))],
        out_shape=[jax.ShapeDtypeStruct((t, D_MODEL), F32),
                   jax.ShapeDtypeStruct((t, D_MODEL), F32),
                   jax.ShapeDtypeStruct((ROUTE_ROWS, t), F32)],
        compiler_params=_cparams(("parallel",)),
        name="merge",
    )(x2, o_att, hs, glu, ga, gb, mod3, g_ffn, wua, wul, wo, wr_t, br)


def _route_body(lt_ref, eid_ref, gate_ref, rank_ref, cnt_ref, carry_ref):
    step = pl.program_id(0)

    @pl.when(step == 0)
    def _():
        carry_ref[...] = jnp.zeros_like(carry_ref)

    tb = lt_ref.shape[1]
    lg = [lt_ref[r:r + 1, :] for r in range(N_GROUPS)]
    best = lg[0]
    gidx = jnp.zeros((1, tb), I32)
    for r in range(1, N_GROUPS):
        better = lg[r] > best
        gidx = jnp.where(better, r, gidx)
        best = jnp.maximum(best, lg[r])
    den = jnp.exp(lg[0] - best)
    for r in range(1, N_GROUPS):
        den = den + jnp.exp(lg[r] - best)
    p_top = 1.0 / den

    ev = []
    for j in range(EXPERTS_PER_GROUP):
        sel = lt_ref[N_GROUPS + j:N_GROUPS + j + 1, :]
        for g in range(1, N_GROUPS):
            row = N_GROUPS + g * EXPERTS_PER_GROUP + j
            sel = jnp.where(gidx == g, lt_ref[row:row + 1, :], sel)
        ev.append(sel)
    v0 = ev[0]
    i0 = jnp.zeros((1, tb), I32)
    for j in range(1, EXPERTS_PER_GROUP):
        better = ev[j] > v0
        i0 = jnp.where(better, j, i0)
        v0 = jnp.maximum(v0, ev[j])
    v1 = jnp.full((1, tb), -jnp.inf, F32)
    i1 = jnp.zeros((1, tb), I32)
    for j in range(EXPERTS_PER_GROUP):
        better = (ev[j] > v1) & (i0 != j)
        i1 = jnp.where(better, j, i1)
        v1 = jnp.where(better, ev[j], v1)
    e1 = jnp.exp(v1 - v0)
    inv = 1.0 / (1.0 + e1)
    eid0 = gidx * EXPERTS_PER_GROUP + i0
    eid1 = gidx * EXPERTS_PER_GROUP + i1
    eid_ref[0:1, :] = eid0
    eid_ref[1:2, :] = eid1
    gate_ref[...] = jnp.zeros_like(gate_ref)
    gate_ref[0:1, :] = p_top * inv
    gate_ref[1:2, :] = p_top * (e1 * inv)

    sub = 256
    e_iota = lax.broadcasted_iota(I32, (N_EXPERTS, sub), 0)
    tri = jnp.where(lax.broadcasted_iota(I32, (sub, sub), 0) <= lax.broadcasted_iota(I32, (sub, sub), 1),
                    1.0, 0.0).astype(BF16)
    carry = carry_ref[...]
    for c in range(tb // sub):
        sl = slice(c * sub, (c + 1) * sub)
        m0 = eid0[:, sl] == e_iota
        m1 = eid1[:, sl] == e_iota
        oh = jnp.where(m0 | m1, 1.0, 0.0)
        incl = _dot(oh.astype(BF16), tri)
        excl = incl - oh + carry[:, 0:1]
        rank_ref[0:1, sl] = jnp.sum(jnp.where(m0, excl, 0.0), axis=0, keepdims=True).astype(I32)
        rank_ref[1:2, sl] = jnp.sum(jnp.where(m1, excl, 0.0), axis=0, keepdims=True).astype(I32)
        carry = carry + incl[:, sub - 1:sub]
    carry_ref[...] = carry
    cnt_ref[...] = carry


def _route_call(logits_t):
    t = logits_t.shape[1]
    tb = ROUTE_TB
    col = lambda i: (0, i)
    return pl.pallas_call(
        _route_body,
        grid=(t // tb,),
        in_specs=[pl.BlockSpec((ROUTE_ROWS, tb), col)],
        out_specs=[pl.BlockSpec((TOP_K, tb), col), pl.BlockSpec((SUBLANES, tb), col),
                   pl.BlockSpec((TOP_K, tb), col), pl.BlockSpec((N_EXPERTS, LANES), lambda i: (0, 0))],
        out_shape=[jax.ShapeDtypeStruct((TOP_K, t), I32), jax.ShapeDtypeStruct((SUBLANES, t), F32),
                   jax.ShapeDtypeStruct((TOP_K, t), I32), jax.ShapeDtypeStruct((N_EXPERTS, LANES), F32)],
        scratch_shapes=[pltpu.VMEM((N_EXPERTS, LANES), F32)],
        compiler_params=_cparams(("arbitrary",)),
        name="route",
    )(logits_t)


def _dest_body(cnt_ref, eid_ref, rank_ref, dest_ref, blk_ref):
    cnt = cnt_ref[...].astype(I32)
    padded = ((cnt + (MOE_BLK - 1)) >> MOE_BLK_LOG2) << MOE_BLK_LOG2
    e_iota = lax.broadcasted_iota(I32, (N_EXPERTS, LANES), 0)
    p_end = jnp.zeros((N_EXPERTS, LANES), I32)
    for e in range(N_EXPERTS):
        tot = jnp.sum(jnp.where(e_iota <= e, padded, 0), axis=0, keepdims=True)
        p_end = jnp.where(e_iota == e, tot, p_end)
    p_start = p_end - padded
    tb = eid_ref.shape[1]
    ps = jnp.concatenate([p_start] * (tb // LANES), axis=1)
    e_wide = lax.broadcasted_iota(I32, (N_EXPERTS, tb), 0)
    for k in range(TOP_K):
        start = jnp.sum(jnp.where(eid_ref[k:k + 1, :] == e_wide, ps, 0), axis=0, keepdims=True)
        dest_ref[k:k + 1, :] = start + rank_ref[k:k + 1, :]
    nb = blk_ref.shape[1]
    pe = jnp.concatenate([p_end] * (nb // LANES), axis=1)
    first_row = lax.broadcasted_iota(I32, (N_EXPERTS, nb), 1) * MOE_BLK
    n_before = jnp.sum(jnp.where(pe <= first_row, 1, 0), axis=0, keepdims=True)
    blk = jnp.minimum(n_before, N_EXPERTS - 1)
    blk_ref[...] = jnp.broadcast_to(blk, blk_ref.shape)
    blk_ref[1:2, :] = jnp.broadcast_to(p_end[N_EXPERTS - 1:N_EXPERTS, 0:1] >> MOE_BLK_LOG2, (1, nb))


def _dest_call(cnt, eid, rank, nb_pad):
    t = eid.shape[1]
    tb = ROUTE_TB
    col = lambda i: (0, i)
    return pl.pallas_call(
        _dest_body,
        grid=(t // tb,),
        in_specs=[pl.BlockSpec((N_EXPERTS, LANES), lambda i: (0, 0)),
                  pl.BlockSpec((TOP_K, tb), col), pl.BlockSpec((TOP_K, tb), col)],
        out_specs=[pl.BlockSpec((TOP_K, tb), col), pl.BlockSpec((SUBLANES, nb_pad), lambda i: (0, 0))],
        out_shape=[jax.ShapeDtypeStruct((TOP_K, t), I32), jax.ShapeDtypeStruct((SUBLANES, nb_pad), I32)],
        compiler_params=_cparams(("arbitrary",)),
        name="dest",
    )(cnt, eid, rank)


def _slot_tokens_body(dest_ref, tok_ref):
    n_tok = dest_ref.shape[0] // TOP_K
    n_chunk = pl.num_programs(0) // 2
    step = pl.program_id(0)
    clear_len = tok_ref.shape[0] // n_chunk
    place_len = n_tok // n_chunk

    @pl.when(step < n_chunk)
    def _():
        def clear(i, c):
            tok_ref[step * clear_len + i] = 0
            return c

        lax.fori_loop(0, clear_len, clear, 0, unroll=8)

    @pl.when(step >= n_chunk)
    def _():
        def place(i, c):
            t = (step - n_chunk) * place_len + i
            for k in range(TOP_K):
                tok_ref[dest_ref[k * n_tok + t]] = t
            return c

        lax.fori_loop(0, place_len, place, 0, unroll=8)


SLOT_CHUNKS = 32


def _slot_tokens_call(dest_flat, n_slots):
    assert n_slots % SLOT_CHUNKS == 0 and (dest_flat.shape[0] // TOP_K) % SLOT_CHUNKS == 0
    return pl.pallas_call(
        _slot_tokens_body,
        grid_spec=pltpu.PrefetchScalarGridSpec(
            num_scalar_prefetch=1,
            grid=(2 * SLOT_CHUNKS,),
            in_specs=[],
            out_specs=pl.BlockSpec(memory_space=pltpu.SMEM)),
        out_shape=jax.ShapeDtypeStruct((n_slots,), I32),
        compiler_params=_cparams(("arbitrary",)),
        name="slot_tokens",
    )(dest_flat)


def _experts_body(blk_ref, used_ref, tok_ref, h_ref, w1a_ref, w3a_ref, w2a_ref, w1b_ref, w3b_ref, w2b_ref,
                  y_ref, xbuf_ref, sem):
    del blk_ref
    i = pl.program_id(0)
    used = used_ref[0]
    n_blk = 2 * pl.num_programs(0)

    def start_gather(blk, slot):
        base = blk * MOE_BLK
        for r in range(MOE_BLK):
            pltpu.make_async_copy(h_ref.at[pl.ds(tok_ref[base + r], 1)], xbuf_ref.at[slot, pl.ds(r, 1)],
                                  sem.at[slot]).start()

    def wait_gather(slot):
        pltpu.make_async_copy(h_ref.at[pl.ds(0, MOE_BLK)], xbuf_ref.at[slot], sem.at[slot]).wait()

    @pl.when(i == 0)
    def _():
        start_gather(0, 0)

    for half, (w1_ref, w3_ref, w2_ref) in enumerate(((w1a_ref, w3a_ref, w2a_ref), (w1b_ref, w3b_ref, w2b_ref))):
        blk = 2 * i + half
        rows = pl.ds(half * MOE_BLK, MOE_BLK)

        @pl.when(blk < used)
        def _():
            wait_gather(half)
            start_gather(blk + 1, 1 - half)
            x = xbuf_ref[half].astype(BF16)
            g = _dot(x, w1_ref[0].astype(BF16))
            u = _dot(x, w3_ref[0].astype(BF16))
            mid = (g * _sigmoid(g)) * u
            y_ref[rows, :] = _dot(mid.astype(BF16), w2_ref[0].astype(BF16))

            @pl.when(blk + 1 == n_blk)
            def _():
                wait_gather(1 - half)

        @pl.when(blk >= used)
        def _():
            @pl.when(blk == used)
            def _():
                wait_gather(half)
            y_ref[rows, :] = jnp.zeros((MOE_BLK, D_MODEL), F32)


def _experts_call(blk_e, n_used, slot_tok, h2, w1, w3, w2):
    nb = blk_e.shape[0]
    assert nb % 2 == 0 and slot_tok.shape[0] == (nb + 1) * MOE_BLK
    wmap = lambda half: (lambda i, blk, used, tok: (blk[2 * i + half], 0, 0))
    wspecs = lambda half: [pl.BlockSpec((1, D_MODEL, D_EXPERT), wmap(half)),
                           pl.BlockSpec((1, D_MODEL, D_EXPERT), wmap(half)),
                           pl.BlockSpec((1, D_EXPERT, D_MODEL), wmap(half))]
    return pl.pallas_call(
        _experts_body,
        grid_spec=pltpu.PrefetchScalarGridSpec(
            num_scalar_prefetch=3,
            grid=(nb // 2,),
            in_specs=[pl.BlockSpec(memory_space=pl.ANY)] + wspecs(0) + wspecs(1),
            out_specs=pl.BlockSpec((2 * MOE_BLK, D_MODEL), lambda i, blk, used, tok: (i, 0)),
            scratch_shapes=[pltpu.VMEM((2, MOE_BLK, D_MODEL), F32), pltpu.SemaphoreType.DMA((2,))]),
        out_shape=jax.ShapeDtypeStruct((nb * MOE_BLK, D_MODEL), F32),
        compiler_params=_cparams(("arbitrary",)),
        name="experts",
    )(blk_e, n_used, slot_tok, h2, w1, w3, w2, w1, w3, w2)


def _combine_body(dest_ref, x1_ref, gate_ref, mod_ref, gf_ref, y_ref, o_ref, buf_ref, sem):
    tm = x1_ref.shape[0]
    step = pl.program_id(0)
    n_step = pl.num_programs(0)
    n_tok = n_step * tm
    slot = step % 2

    def start_gather(for_step):
        def issue(r, c):
            for k in range(TOP_K):
                d = dest_ref[k * n_tok + for_step * tm + r]
                pltpu.make_async_copy(y_ref.at[pl.ds(d, 1)], buf_ref.at[for_step % 2, k, pl.ds(r, 1)],
                                      sem.at[for_step % 2]).start()
            return c

        lax.fori_loop(0, tm, issue, 0, unroll=8)

    @pl.when(step == 0)
    def _():
        start_gather(step)

    @pl.when(step + 1 < n_step)
    def _():
        start_gather(step + 1)

    eye = jnp.where(lax.broadcasted_iota(I32, (tm, tm), 0) == lax.broadcasted_iota(I32, (tm, tm), 1),
                    1.0, 0.0).astype(BF16)
    g = gate_ref[...]
    g1 = g.astype(BF16)
    rem = g - g1.astype(F32)
    g2 = rem.astype(BF16)
    g3 = (rem - g2.astype(F32)).astype(BF16)
    gt = _dot_nt(eye, g1) + (_dot_nt(eye, g2) + _dot_nt(eye, g3))

    for k in range(TOP_K):
        pltpu.make_async_copy(y_ref.at[pl.ds(0, tm)], buf_ref.at[slot, k], sem.at[slot]).wait()

    ga2 = mod_ref[0, :, 5 * D_MODEL:6 * D_MODEL]
    moe = gt[:, 0:1] * buf_ref[slot, 0] + gt[:, 1:2] * buf_ref[slot, 1]
    x2 = x1_ref[...] + ga2 * moe
    ms = jnp.mean(x2 * x2, axis=-1, keepdims=True)
    o_ref[...] = x2 * lax.rsqrt(ms + EPS) * gf_ref[...]


def _combine_call(dest_flat, x1, gate, mod3, g_final, y, seq):
    t = x1.shape[0]
    tm = COMBINE_TM
    per_b = seq // tm
    return pl.pallas_call(
        _combine_body,
        grid_spec=pltpu.PrefetchScalarGridSpec(
            num_scalar_prefetch=1,
            grid=(t // tm,),
            in_specs=[pl.BlockSpec((tm, D_MODEL), lambda i, d: (i, 0)),
                      pl.BlockSpec((SUBLANES, tm), lambda i, d: (0, i)),
                      pl.BlockSpec((1, 1, 6 * D_MODEL), lambda i, d: (i // per_b, 0, 0)),
                      pl.BlockSpec((1, D_MODEL), lambda i, d: (0, 0)),
                      pl.BlockSpec(memory_space=pl.ANY)],
            out_specs=pl.BlockSpec((tm, D_MODEL), lambda i, d: (i, 0)),
            scratch_shapes=[pltpu.VMEM((2, TOP_K, tm, D_MODEL), F32), pltpu.SemaphoreType.DMA((2,))]),
        out_shape=jax.ShapeDtypeStruct((t, D_MODEL), F32),
        compiler_params=_cparams(("arbitrary",)),
        name="combine",
    )(dest_flat, x1, gate, mod3, g_final, y)


def _rope_tables(seq):
    half = NA_HEAD_DIM // 2
    nf = half // 2
    inv_freq = ROPE_THETA ** (-jnp.arange(nf, dtype=F32) / nf)
    t = jnp.arange(seq)
    row_pos = (t // GRID_W).astype(F32)
    col_pos = (t % GRID_W).astype(F32)
    ang_r = row_pos[:, None] * inv_freq
    ang_c = col_pos[:, None] * inv_freq
    cos = jnp.concatenate([jnp.cos(ang_r), jnp.cos(ang_r), jnp.cos(ang_c), jnp.cos(ang_c)], axis=-1)
    sin = jnp.concatenate([-jnp.sin(ang_r), jnp.sin(ang_r), -jnp.sin(ang_c), jnp.sin(ang_c)], axis=-1)
    return jnp.tile(cos, (1, NA_HEADS)), jnp.tile(sin, (1, NA_HEADS))


def _layer(x, c, ctx, c_ctx, w_mod, b_mod, g_mix, g_ffn, w_in, rpb, conv_w, conv_b, lru_wa, lru_ba,
           lru_wx, lru_bx, lru_lambda, w_up_attn, w_up_lru, w_out, wg, bg, we, be, w1, w3, w2, g_final):
    bsz, seq, d = x.shape
    n_ctx = ctx.shape[1]
    t = bsz * seq
    assert d == D_MODEL and seq % ATT_TQ == 0 and seq // ATT_TQ > K_ROW_BLOCKS
    assert bsz + 1 <= MOD_ROWS and seq % PROJ_TM == 0 and seq % MERGE_TM == 0
    assert t % ROUTE_TB == 0 and t % COMBINE_TM == 0
    assert (bsz * n_ctx) % PROJ_TM == 0 and n_ctx <= seq
    assert seq % (SUBLANES * SUBLANES) == 0 and n_ctx % (SUBLANES * SUBLANES) == 0
    assert (seq // SUBLANES) % SCAN_STEPS == 0 and (n_ctx // SUBLANES) % SCAN_STEPS == 0

    cc = jnp.concatenate([c, c_ctx[None, :], jnp.zeros((MOD_ROWS - bsz - 1, d), F32)], axis=0)
    mod = _mod_call(cc, w_mod, b_mod)
    mod3 = mod[:bsz].reshape(bsz, 1, 6 * d)
    mod_c = mod[bsz:bsz + 1]

    x2 = x.reshape(t, d)
    g_mix2 = g_mix.reshape(1, d)
    w_in_bf = w_in.astype(BF16)
    kc, vc, lxc = _ctxproj_call(ctx.reshape(bsz * n_ctx, d), mod_c, g_mix2, w_in_bf[:, :CTX_COLS])
    cos_t, sin_t = _rope_tables(seq)
    qre, qro, qpe, qpo, k, v, lx, glu, ga, gb = _inproj_call(x2, mod3, g_mix2, w_in_bf, cos_t, sin_t, seq)

    bias = _bias_tables(_rpbcol_call(rpb))
    o_att = _attn_call(qre, qro, qpe, qpo, k, v, kc, vc, bias, bsz, seq, n_ctx)

    hs = _lru_call(lx.reshape(bsz, seq, LRU_WIDTH), lxc.reshape(bsz, n_ctx, LRU_WIDTH),
                   conv_w, conv_b.reshape(1, LRU_WIDTH), lru_wa, lru_wx, lru_ba, lru_bx, lru_lambda)

    wr_t = jnp.concatenate([wg.T, we.T, jnp.zeros((ROUTE_ROWS - N_GROUPS - N_EXPERTS, d), F32)], axis=0)
    br = jnp.concatenate([bg, be, jnp.zeros((ROUTE_ROWS - N_GROUPS - N_EXPERTS,), F32)])
    br = jnp.broadcast_to(br[:, None], (ROUTE_ROWS, LANES))
    x1, h2, logits_t = _merge_call(x2, o_att, hs.reshape(t, LRU_WIDTH), glu, ga, gb, mod3,
                                   g_ffn.reshape(1, d), w_up_attn.astype(BF16), w_up_lru.astype(BF16),
                                   w_out.astype(BF16), wr_t, br, seq)

    eid, gate, rank, cnt = _route_call(logits_t)
    n_blk = -(-(t * TOP_K + N_EXPERTS * (MOE_BLK - 1)) // MOE_BLK)
    n_blk += n_blk % 2
    nb_pad = -(-n_blk // LANES) * LANES
    dest, blk = _dest_call(cnt, eid, rank, nb_pad)
    dest_flat = dest.reshape(TOP_K * t)
    slot_tok = _slot_tokens_call(dest_flat, (n_blk + 1) * MOE_BLK)
    y = _experts_call(blk[0, :n_blk], blk[1, :1], slot_tok, h2, w1, w3, w2)
    return _combine_call(dest_flat, x1, gate, mod3, g_final.reshape(1, d), y, seq).reshape(bsz, seq, d)


def kernel(x, c, ctx, c_ctx, w_mod, b_mod, g_mix, g_ffn, w_in, rpb, conv_w, conv_b, lru_wa, lru_ba, lru_wx,
           lru_bx, lru_lambda, w_up_attn, w_up_lru, w_out, router_group_w, router_group_b, router_expert_w,
           router_expert_b, expert_w_gate, expert_w_up, expert_w_down, g_final):
    assert w_mod.shape[0] == 1, "single-layer block"
    return _layer(x, c, ctx, c_ctx, w_mod[0], b_mod[0], g_mix[0], g_ffn[0], w_in[0], rpb[0], conv_w[0],
                  conv_b[0], lru_wa[0], lru_ba[0], lru_wx[0], lru_bx[0], lru_lambda[0], w_up_attn[0],
                  w_up_lru[0], w_out[0], router_group_w[0], router_group_b[0], router_expert_w[0],
                  router_expert_b[0], expert_w_gate[0], expert_w_up[0], expert_w_down[0], g_final)
```

```python
import functools

import numpy as np
import jax
import jax.numpy as jnp
from jax import lax
from jax.experimental import pallas as pl
from jax.experimental.pallas import tpu as pltpu

F32 = jnp.float32
BF16 = jnp.bfloat16
I32 = jnp.int32
U32 = jnp.uint32

D_MODEL = 1024
GRID_W = 64
EPS = 1e-6
NEG_INF = -1e30

NA_HEADS = 8
NA_HEAD_DIM = 64
NA_WIDTH = NA_HEADS * NA_HEAD_DIM
NA_WIN_ROWS = 8
NA_WIN_COLS = 16
ROPE_THETA = 10000.0

LRU_WIDTH = D_MODEL
LRU_BLOCKS = 8
LRU_BLOCK = LRU_WIDTH // LRU_BLOCKS
LRU_CONV = 4
LRU_C = 8.0

N_GROUPS = 4
EXPERTS_PER_GROUP = 8
N_EXPERTS = N_GROUPS * EXPERTS_PER_GROUP
TOP_K = 2
D_EXPERT = 512

K_OFF = 0
V_OFF = K_OFF + NA_WIDTH
LX_OFF = V_OFF + NA_WIDTH
CTX_COLS = LX_OFF + LRU_WIDTH
Q_OFF = CTX_COLS
LG_OFF = Q_OFF + NA_WIDTH
GA_OFF = LG_OFF + LRU_WIDTH
GB_OFF = GA_OFF + D_MODEL
PROJ_COLS = GB_OFF + D_MODEL

LANES = 128
SUBLANES = 8

Q_ROWS = 4
K_ROW_BLOCKS = 3
ATT_TQ = Q_ROWS * GRID_W
ATT_TK = K_ROW_BLOCKS * ATT_TQ

MOE_BLK_LOG2 = 8
MOE_BLK = 1 << MOE_BLK_LOG2
MOD_ROWS = 24
ROUTE_ROWS = 64

PROJ_TM = 512
MERGE_TM = 512
ROUTE_TB = 2048
DISPATCH_TM = 512
COMBINE_TM = 256

VMEM_LIMIT = 56 * 1024 * 1024


def _cparams(sem, vmem=VMEM_LIMIT):
    return pltpu.CompilerParams(dimension_semantics=sem, vmem_limit_bytes=vmem)


def _dot(a, b):
    return jnp.dot(a, b, preferred_element_type=F32)


def _dot_nt(a, b):
    return lax.dot_general(a, b, (((1,), (1,)), ((), ())), preferred_element_type=F32)


def _split2(a):
    hi = a.astype(BF16)
    lo = (a - hi.astype(F32)).astype(BF16)
    return hi, lo


def _dot3(a, b):
    ah, al = _split2(a)
    bh, bl = _split2(b)
    return _dot(ah, bh) + (_dot(ah, bl) + _dot(al, bh))


def _dot3_nt(a, b):
    ah, al = _split2(a)
    bh, bl = _split2(b)
    return _dot_nt(ah, bh) + (_dot_nt(ah, bl) + _dot_nt(al, bh))


def _sigmoid(x):
    return 1.0 / (1.0 + jnp.exp(-x))


def _ada_norm(x, g, sc, sh):
    ms = jnp.mean(x * x, axis=-1, keepdims=True)
    return (x * lax.rsqrt(ms + EPS) * g) * (1.0 + sc) + sh


def _mod_body(cc_ref, w_ref, b_ref, o_ref):
    cc = cc_ref[...]
    o_ref[...] = _dot3(cc * _sigmoid(cc), w_ref[...]) + b_ref[...]


def _mod_call(cc, w_mod, b_mod):
    n = w_mod.shape[1]
    bn = 1024
    return pl.pallas_call(
        _mod_body,
        grid=(n // bn,),
        in_specs=[pl.BlockSpec((MOD_ROWS, D_MODEL), lambda j: (0, 0)),
                  pl.BlockSpec((D_MODEL, bn), lambda j: (0, j)),
                  pl.BlockSpec((1, bn), lambda j: (0, j))],
        out_specs=pl.BlockSpec((MOD_ROWS, bn), lambda j: (0, j)),
        out_shape=jax.ShapeDtypeStruct((MOD_ROWS, n), F32),
        compiler_params=_cparams(("arbitrary",)),
        name="mod",
    )(cc, w_mod, b_mod.reshape(1, n))


def _rope(t, cos, sin):
    lane = lax.broadcasted_iota(I32, (t.shape[0], LANES), 1)
    first = (lane & 16) == 0
    parts = []
    for c in range(t.shape[1] // LANES):
        tc = t[:, c * LANES:(c + 1) * LANES]
        parts.append(jnp.where(first, pltpu.roll(tc, LANES - 16, 1), pltpu.roll(tc, 16, 1)))
    partner = jnp.concatenate(parts, axis=1)
    return t * cos + partner * sin


def _gelu_tanh(x):
    return 0.5 * x * (1.0 + jnp.tanh(0.7978845608028654 * (x + 0.044715 * (x * x * x))))


def _inproj_body(x_ref, mod_ref, g_ref, w_ref, cos_ref, sin_ref,
                 qre_ref, qro_ref, qpe_ref, qpo_ref, k_ref, v_ref, lx_ref, glu_ref, ga_ref, gb_ref):
    sh = mod_ref[0, :, 0:D_MODEL]
    sc = mod_ref[0, :, D_MODEL:2 * D_MODEL]
    h = _ada_norm(x_ref[...], g_ref[...], sc, sh).astype(BF16)
    cos = cos_ref[...]
    sin = sin_ref[...]
    scale = NA_HEAD_DIM ** -0.5

    k_ref[...] = _rope(_dot(h, w_ref[:, K_OFF:K_OFF + NA_WIDTH]), cos, sin).astype(BF16)
    v_ref[...] = _dot(h, w_ref[:, V_OFF:V_OFF + NA_WIDTH]).astype(BF16)
    lx_ref[...] = _dot(h, w_ref[:, LX_OFF:LX_OFF + LRU_WIDTH])

    q = _dot(h, w_ref[:, Q_OFF:Q_OFF + NA_WIDTH]) * scale
    qr = _rope(q, cos, sin)
    lane = lax.broadcasted_iota(I32, q.shape, 1)
    even = (lane & NA_HEAD_DIM) == 0
    qre_ref[...] = jnp.where(even, qr, 0.0).astype(BF16)
    qro_ref[...] = jnp.where(even, 0.0, qr).astype(BF16)
    qpe_ref[...] = jnp.where(even, q, 0.0).astype(BF16)
    qpo_ref[...] = jnp.where(even, 0.0, q).astype(BF16)

    glu_ref[...] = _gelu_tanh(_dot(h, w_ref[:, LG_OFF:LG_OFF + LRU_WIDTH])).astype(BF16)
    ga_ref[...] = _sigmoid(_dot(h, w_ref[:, GA_OFF:GA_OFF + D_MODEL])).astype(BF16)
    gb_ref[...] = _sigmoid(_dot(h, w_ref[:, GB_OFF:GB_OFF + D_MODEL])).astype(BF16)


def _inproj_call(x2, mod3, g_mix, w_in_bf, cos_t, sin_t, seq):
    t = x2.shape[0]
    tm = PROJ_TM
    per_b = seq // tm
    row = lambda i: (i, 0)
    wide = lambda n, dt: jax.ShapeDtypeStruct((t, n), dt)
    return pl.pallas_call(
        _inproj_body,
        grid=(t // tm,),
        in_specs=[pl.BlockSpec((tm, D_MODEL), row),
                  pl.BlockSpec((1, 1, 6 * D_MODEL), lambda i: (i // per_b, 0, 0)),
                  pl.BlockSpec((1, D_MODEL), lambda i: (0, 0)),
                  pl.BlockSpec((D_MODEL, PROJ_COLS), lambda i: (0, 0), pipeline_mode=pl.Buffered(1)),
                  pl.BlockSpec((tm, NA_WIDTH), lambda i: (i % per_b, 0)),
                  pl.BlockSpec((tm, NA_WIDTH), lambda i: (i % per_b, 0))],
        out_specs=[pl.BlockSpec((tm, NA_WIDTH), row)] * 6
                  + [pl.BlockSpec((tm, LRU_WIDTH), row)] * 4,
        out_shape=[wide(NA_WIDTH, BF16)] * 6
                  + [wide(LRU_WIDTH, F32), wide(LRU_WIDTH, BF16), wide(D_MODEL, BF16), wide(D_MODEL, BF16)],
        compiler_params=_cparams(("parallel",)),
        name="inproj",
    )(x2, mod3, g_mix, w_in_bf, cos_t, sin_t)


def _ctxproj_body(x_ref, mod_ref, g_ref, w_ref, k_ref, v_ref, lx_ref):
    sh = mod_ref[:, 0:D_MODEL]
    sc = mod_ref[:, D_MODEL:2 * D_MODEL]
    h = _ada_norm(x_ref[...], g_ref[...], sc, sh).astype(BF16)
    k_ref[...] = _dot(h, w_ref[:, K_OFF:K_OFF + NA_WIDTH]).astype(BF16)
    v_ref[...] = _dot(h, w_ref[:, V_OFF:V_OFF + NA_WIDTH]).astype(BF16)
    lx_ref[...] = _dot(h, w_ref[:, LX_OFF:LX_OFF + LRU_WIDTH])


def _ctxproj_call(c2, mod_c, g_mix, w_ctx_bf):
    t = c2.shape[0]
    tm = PROJ_TM
    row = lambda i: (i, 0)
    return pl.pallas_call(
        _ctxproj_body,
        grid=(t // tm,),
        in_specs=[pl.BlockSpec((tm, D_MODEL), row),
                  pl.BlockSpec((1, 6 * D_MODEL), lambda i: (0, 0)),
                  pl.BlockSpec((1, D_MODEL), lambda i: (0, 0)),
                  pl.BlockSpec((D_MODEL, CTX_COLS), lambda i: (0, 0))],
        out_specs=[pl.BlockSpec((tm, NA_WIDTH), row), pl.BlockSpec((tm, NA_WIDTH), row),
                   pl.BlockSpec((tm, LRU_WIDTH), row)],
        out_shape=[jax.ShapeDtypeStruct((t, NA_WIDTH), BF16), jax.ShapeDtypeStruct((t, NA_WIDTH), BF16),
                   jax.ShapeDtypeStruct((t, LRU_WIDTH), F32)],
        compiler_params=_cparams(("parallel",)),
        name="ctxproj",
    )(c2, mod_c, g_mix, w_ctx_bf)


N_DR = 2 * NA_WIN_ROWS - 1
N_DC = 2 * NA_WIN_COLS - 1


def _rpbcol_body(rpb_ref, o_ref):
    n = GRID_W * GRID_W
    flat = lax.broadcasted_iota(I32, (32, n), 1)
    qc = flat >> 6
    kc = flat & (GRID_W - 1)
    dc = jnp.clip(kc - qc, 1 - NA_WIN_COLS, NA_WIN_COLS - 1) + (NA_WIN_COLS - 1)
    d_iota = lax.broadcasted_iota(I32, (32, n), 0)
    onehot = jnp.where(dc == d_iota, 1.0, 0.0).astype(BF16)
    r = rpb_ref[...]
    r1 = r.astype(BF16)
    rem = r - r1.astype(F32)
    r2 = rem.astype(BF16)
    r3 = (rem - r2.astype(F32)).astype(BF16)
    val = _dot(r1, onehot) + (_dot(r2, onehot) + _dot(r3, onehot))
    qc1 = qc[0:1, :]
    kc1 = kc[0:1, :]
    c_start = jnp.clip(qc1 - NA_WIN_COLS // 2, 0, GRID_W - NA_WIN_COLS)
    band = (kc1 >= c_start) & (kc1 < c_start + NA_WIN_COLS)
    o_ref[...] = jnp.where(band, val, NEG_INF)


def _rpbcol_call(rpb):
    rows = NA_HEADS * N_DR
    r2 = jnp.pad(rpb.reshape(rows, N_DC), ((0, 0), (0, 32 - N_DC)))
    n = GRID_W * GRID_W
    return pl.pallas_call(
        _rpbcol_body,
        in_specs=[pl.BlockSpec((rows, 32), lambda: (0, 0))],
        out_specs=pl.BlockSpec((rows, n), lambda: (0, 0)),
        out_shape=jax.ShapeDtypeStruct((rows, n), F32),
        name="rpbcol",
    )(r2)


def _bias_tables(rpbcol):
    t = rpbcol.reshape(NA_HEADS, N_DR, GRID_W, GRID_W)
    neg = jnp.full((NA_HEADS, GRID_W, GRID_W), NEG_INF, F32)
    n_kj = K_ROW_BLOCKS * Q_ROWS
    classes = []
    for lo_fn, dr_off in ((lambda ri: 0, 7), (lambda ri: ri, 3), (lambda ri: 4, -1)):
        rows = []
        for ri in range(Q_ROWS):
            lo = lo_fn(ri)
            blocks = []
            for kj in range(n_kj):
                inside = lo <= kj < lo + NA_WIN_ROWS
                blocks.append(t[:, kj - ri + dr_off] if inside else neg)
            rows.append(jnp.concatenate(blocks, axis=2))
        classes.append(jnp.concatenate(rows, axis=1))
    return jnp.stack(classes, axis=0)


def _attn_body(qre_ref, qro_ref, qpe_ref, qpo_ref, k0_ref, k1_ref, k2_ref, v0_ref, v1_ref, v2_ref,
               kc_ref, vc_ref, bias_ref, o_ref):
    lane = lax.broadcasted_iota(I32, (ATT_TQ, LANES), 1)
    k_refs = (k0_ref, k1_ref, k2_ref)
    v_refs = (v0_ref, v1_ref, v2_ref)
    for p in range(NA_HEADS // 2):
        sl = slice(p * LANES, (p + 1) * LANES)
        kk = [r[:, sl] for r in k_refs]
        vv = [r[:, sl] for r in v_refs]
        kc = kc_ref[:, sl]
        vc = vc_ref[:, sl]
        outs = []
        for hh, (qr_ref, qp_ref) in enumerate(((qre_ref, qpe_ref), (qro_ref, qpo_ref))):
            h = 2 * p + hh
            qr = qr_ref[:, sl]
            qp = qp_ref[:, sl]
            s = [_dot_nt(qr, kk[j]) + bias_ref[0, h, :, j * ATT_TQ:(j + 1) * ATT_TQ]
                 for j in range(K_ROW_BLOCKS)]
            s.append(_dot_nt(qp, kc))
            m = s[0].max(axis=-1, keepdims=True)
            for sj in s[1:]:
                m = jnp.maximum(m, sj.max(axis=-1, keepdims=True))
            e = [jnp.exp(sj - m) for sj in s]
            den = e[0].sum(axis=-1, keepdims=True)
            for ej in e[1:]:
                den = den + ej.sum(axis=-1, keepdims=True)
            acc = _dot(e[K_ROW_BLOCKS].astype(BF16), vc)
            for j in range(K_ROW_BLOCKS):
                acc = acc + _dot(e[j].astype(BF16), vv[j])
            outs.append(acc / den)
        o_ref[:, sl] = jnp.where(lane < NA_HEAD_DIM, outs[0], outs[1]).astype(BF16)


def _attn_call(qre, qro, qpe, qpo, k, v, kc, vc, bias, bsz, seq, n_ctx):
    t = bsz * seq
    n_grp = seq // ATT_TQ
    per_b = n_grp
    max_kb = n_grp - K_ROW_BLOCKS

    def qmap(g, b):
        return (b * per_b + g, 0)

    def kmap(j):
        return lambda g, b: (b * per_b + jnp.clip(g - 1, 0, max_kb) + j, 0)

    def cls(g, b):
        return (jnp.where(g == 0, 0, jnp.where(g == n_grp - 1, 2, 1)), 0, 0, 0)

    qspec = pl.BlockSpec((ATT_TQ, NA_WIDTH), qmap)
    cspec = pl.BlockSpec((n_ctx, NA_WIDTH), lambda g, b: (b, 0))
    return pl.pallas_call(
        _attn_body,
        grid=(n_grp, bsz),
        in_specs=[qspec] * 4
                 + [pl.BlockSpec((ATT_TQ, NA_WIDTH), kmap(j)) for j in range(K_ROW_BLOCKS)] * 2
                 + [cspec, cspec, pl.BlockSpec((1, NA_HEADS, ATT_TQ, ATT_TK), cls)],
        out_specs=pl.BlockSpec((ATT_TQ, NA_WIDTH), qmap),
        out_shape=jax.ShapeDtypeStruct((t, NA_WIDTH), BF16),
        compiler_params=_cparams(("arbitrary", "arbitrary")),
        name="attn",
    )(qre, qro, qpe, qpo, k, k, k, v, v, v, kc, vc, bias)


def _shift_down(v, row):
    return jnp.where(row >= 1, pltpu.roll(v, 1, 0), 0.0)


def _shift_up(v, row):
    return jnp.where(row < SUBLANES - 1, pltpu.roll(v, SUBLANES - 1, 0), 0.0)


def _conv4(x, w_ref, b_ref):
    n = x.shape[0]
    s = SUBLANES
    row = lax.broadcasted_iota(I32, (s, LANES), 0)
    last = _shift_down(x[n - s:n], row)
    last2 = _shift_down(x[n - 2 * s:n - s], row)
    first = _shift_up(x[0:s], row)
    xm1 = jnp.concatenate([last, x[0:n - s]], axis=0)
    xm2 = jnp.concatenate([last2, last, x[0:n - 2 * s]], axis=0)
    xp1 = jnp.concatenate([x[s:n], first], axis=0)
    return (w_ref[0:1, :] * xm2 + w_ref[1:2, :] * xm1 + w_ref[2:3, :] * x + w_ref[3:4, :] * xp1) + b_ref[...]


def _softplus(z):
    return jnp.maximum(z, 0.0) + jnp.log1p(jnp.exp(-jnp.abs(z)))


def _gates(xc, d, wa_ref, wx_ref, ba_ref, bx_ref, lam_ref, a_ref, u_ref):
    n = xc.shape[0]
    xb = xc.astype(BF16)
    tr = jnp.tanh(_dot(xb, (0.5 * wa_ref[d, 0]).astype(BF16)) + 0.5 * ba_ref[d:d + 1, :])
    ti = jnp.tanh(_dot(xb, (0.5 * wx_ref[d, 0]).astype(BF16)) + 0.5 * bx_ref[d:d + 1, :])
    half_c = (0.5 * LRU_C) * _softplus(-lam_ref[d:d + 1, :])
    neg_log_a = half_c * tr + half_c
    a = jnp.exp(-neg_log_a)
    a_ref[0:n, :] = a
    s2 = jnp.tanh(neg_log_a) * (a * a + 1.0)
    root = jnp.where(s2 > 0.0, s2 * lax.rsqrt(s2), 0.0)
    xh = 0.5 * xc
    u_ref[0:n, :] = root * (xh * ti + xh)


def _scan4(a, u, h, p):
    a01 = a[1] * a[0]
    u01 = a[1] * u[0] + u[1]
    a23 = a[3] * a[2]
    u23 = a[3] * u[2] + u[3]
    a012 = a[2] * a01
    u012 = a[2] * u01 + u[2]
    a0123 = a23 * a01
    u0123 = a23 * u01 + u23
    hs = [a[0] * h + u[0], a01 * h + u01, a012 * h + u012, a0123 * h + u0123]
    ps = [a[0] * p, a01 * p, a012 * p, a0123 * p]
    return hs, ps


SCAN_STEPS = 4


def _scan_local(af_ref, uf_ref, ab_ref, ub_ref, n_vreg):
    s = SUBLANES
    zero = jnp.zeros((s, LANES), F32)
    one = jnp.ones((s, LANES), F32)
    span = SCAN_STEPS * s

    def body(q, carry):
        hf, pf, hb, pb = carry
        base = pl.multiple_of(q * span, span)
        rows = [pl.ds(base + i * s, s) for i in range(SCAN_STEPS)]
        hs, ps = _scan4([af_ref[r, :] for r in rows], [uf_ref[r, :] for r in rows], hf, pf)
        for r, h, p in zip(rows, hs, ps):
            uf_ref[r, :] = h
            af_ref[r, :] = p
        hf, pf = hs[-1], ps[-1]
        base = pl.multiple_of((n_vreg - SCAN_STEPS) * s - q * span, span)
        rows = [pl.ds(base + (SCAN_STEPS - 1 - i) * s, s) for i in range(SCAN_STEPS)]
        hs, ps = _scan4([ab_ref[r, :] for r in rows], [ub_ref[r, :] for r in rows], hb, pb)
        for r, h, p in zip(rows, hs, ps):
            ub_ref[r, :] = h
            ab_ref[r, :] = p
        return hf, pf, hs[-1], ps[-1]

    return lax.fori_loop(0, n_vreg // SCAN_STEPS, body, (zero, one, zero, one), unroll=2)


def _link_states(hf, pf, hb, pb, h0f, h0b):
    s = SUBLANES
    row = lax.broadcasted_iota(I32, (s, LANES), 0)
    a, u = pf, hf
    for k in (1, 2, 4):
        keep = row >= k
        u = u + a * jnp.where(keep, pltpu.roll(u, k, 0), 0.0)
        a = a * jnp.where(keep, pltpu.roll(a, k, 0), 1.0)
    end_f = u + a * h0f
    in_f = jnp.where(row >= 1, pltpu.roll(end_f, 1, 0), h0f)
    a, u = pb, hb
    for k in (1, 2, 4):
        keep = row < s - k
        u = u + a * jnp.where(keep, pltpu.roll(u, s - k, 0), 0.0)
        a = a * jnp.where(keep, pltpu.roll(a, s - k, 0), 1.0)
    end_b = u + a * h0b
    in_b = jnp.where(row < s - 1, pltpu.roll(end_b, s - 1, 0), h0b)
    return in_f, in_b, end_f[s - 1:s, :], end_b[0:1, :]


PITCH_PAD = 4


def _to_split(x_ref, pad_ref, dst_ref, n):
    s = SUBLANES
    n_j = n // s
    pitch = n_j + PITCH_PAD
    for q in range(s):
        pad_ref[pl.ds(q * pitch, n_j), :] = x_ref[0, pl.ds(q * n_j, n_j), :]

    def body(j, c):
        dst_ref[pl.ds(pl.multiple_of(j * s, s), s), :] = pad_ref[pl.ds(j, s, stride=pitch), :]
        return c

    lax.fori_loop(0, n_j, body, 0, unroll=8)


def _lru_body(lx_ref, lxc_ref, cw_ref, cb_ref, wa_ref, wx_ref, ba_ref, bx_ref, lam_ref,
              o_ref, af_ref, uf_ref, ab_ref, ub_ref, pad_ref):
    n = lx_ref.shape[1]
    n_c = lxc_ref.shape[1]
    s = SUBLANES
    n_j = n // s
    pitch = n_j + PITCH_PAD
    gate_args = (wa_ref, wx_ref, ba_ref, bx_ref, lam_ref)
    zero = jnp.zeros((1, LANES), F32)

    _to_split(lxc_ref, pad_ref, uf_ref, n_c)
    xc = _conv4(uf_ref[0:n_c, :], cw_ref, cb_ref)
    _gates(xc, 0, *gate_args, af_ref, uf_ref)
    _gates(xc, 1, *gate_args, ab_ref, ub_ref)
    ends = _scan_local(af_ref, uf_ref, ab_ref, ub_ref, n_c // s)
    _, _, cf, cb = _link_states(*ends, zero, zero)

    _to_split(lx_ref, pad_ref, uf_ref, n)
    xl = _conv4(uf_ref[...], cw_ref, cb_ref)
    _gates(xl, 0, *gate_args, af_ref, uf_ref)
    _gates(xl, 1, *gate_args, ab_ref, ub_ref)
    ends = _scan_local(af_ref, uf_ref, ab_ref, ub_ref, n_j)
    in_f, in_b, _, _ = _link_states(*ends, cf, cb)

    def finish(j, c):
        rows = pl.ds(pl.multiple_of(j * s, s), s)
        h = (uf_ref[rows, :] + af_ref[rows, :] * in_f) + (ub_ref[rows, :] + ab_ref[rows, :] * in_b)
        pad_ref[pl.ds(j, s, stride=pitch), :] = h
        return c

    lax.fori_loop(0, n_j, finish, 0, unroll=8)
    for q in range(s):
        o_ref[0, pl.ds(q * n_j, n_j), :] = pad_ref[pl.ds(q * pitch, n_j), :]


def _lru_call(lx3, lxc3, conv_w, conv_b, wa, wx, ba, bx, lam):
    bsz, seq, _ = lx3.shape
    n_ctx = lxc3.shape[1]
    col = lambda b, n: (b, 0, n)
    par = lambda b, n: (0, n)
    wspec = pl.BlockSpec((2, 1, LRU_BLOCK, LRU_BLOCK), lambda b, n: (0, n, 0, 0))
    return pl.pallas_call(
        _lru_body,
        grid=(bsz, LRU_BLOCKS),
        in_specs=[pl.BlockSpec((1, seq, LRU_BLOCK), col),
                  pl.BlockSpec((1, n_ctx, LRU_BLOCK), col),
                  pl.BlockSpec((LRU_CONV, LRU_BLOCK), par),
                  pl.BlockSpec((1, LRU_BLOCK), par),
                  wspec, wspec,
                  pl.BlockSpec((2, LRU_BLOCK), par),
                  pl.BlockSpec((2, LRU_BLOCK), par),
                  pl.BlockSpec((2, LRU_BLOCK), par)],
        out_specs=pl.BlockSpec((1, seq, LRU_BLOCK), col),
        out_shape=jax.ShapeDtypeStruct((bsz, seq, LRU_WIDTH), F32),
        scratch_shapes=[pltpu.VMEM((seq, LRU_BLOCK), F32)] * 4
                       + [pltpu.VMEM((seq + SUBLANES * PITCH_PAD, LRU_BLOCK), F32)],
        compiler_params=_cparams(("parallel", "arbitrary")),
        name="lru",
    )(lx3, lxc3, conv_w, conv_b, wa, wx, ba, bx, lam)


def _merge_body(x_ref, oa_ref, hs_ref, glu_ref, ga_ref, gb_ref, mod_ref, g_ref, wua_ref, wul_ref, wo_ref,
                wr_ref, br_ref, x1_ref, h2_ref, lt_ref):
    ga1 = mod_ref[0, :, 2 * D_MODEL:3 * D_MODEL]
    sh2 = mod_ref[0, :, 3 * D_MODEL:4 * D_MODEL]
    sc2 = mod_ref[0, :, 4 * D_MODEL:5 * D_MODEL]
    o_lru = (hs_ref[...] * glu_ref[...].astype(F32)).astype(BF16)
    y = (ga_ref[...].astype(F32) * _dot(oa_ref[...], wua_ref[...])
         + gb_ref[...].astype(F32) * _dot(o_lru, wul_ref[...]))
    x1 = x_ref[...] + ga1 * _dot(y.astype(BF16), wo_ref[...])
    x1_ref[...] = x1
    h2 = _ada_norm(x1, g_ref[...], sc2, sh2)
    lt_ref[...] = _dot3_nt(wr_ref[...], h2) + br_ref[:, 0:1]
    h2_ref[...] = h2


def _merge_call(x2, o_att, hs, glu, ga, gb, mod3, g_ffn, wua, wul, wo, wr_t, br, seq):
    t = x2.shape[0]
    tm = MERGE_TM
    per_b = seq // tm
    row = lambda i: (i, 0)
    full = lambda i: (0, 0)
    resident = lambda shape: pl.BlockSpec(shape, full, pipeline_mode=pl.Buffered(1))
    return pl.pallas_call(
        _merge_body,
        grid=(t // tm,),
        in_specs=[pl.BlockSpec((tm, D_MODEL), row),
                  pl.BlockSpec((tm, NA_WIDTH), row),
                  pl.BlockSpec((tm, LRU_WIDTH), row),
                  pl.BlockSpec((tm, LRU_WIDTH), row),
                  pl.BlockSpec((tm, D_MODEL), row),
                  pl.BlockSpec((tm, D_MODEL), row),
                  pl.BlockSpec((1, 1, 6 * D_MODEL), lambda i: (i // per_b, 0, 0)),
                  pl.BlockSpec((1, D_MODEL), full),
                  resident((NA_WIDTH, D_MODEL)),
                  resident((LRU_WIDTH, D_MODEL)),
                  resident((D_MODEL, D_MODEL)),
                  pl.BlockSpec((ROUTE_ROWS, D_MODEL), full),
                  pl.BlockSpec((ROUTE_ROWS, LANES), full)],
        out_specs=[pl.BlockSpec((tm, D_MODEL), row),
                   pl.BlockSpec((tm, D_MODEL), row),
                   pl.BlockSpec((ROUTE_ROWS, tm), lambda i: (0, i))],
        out_shape=[jax.ShapeDtypeStruct((t, D_MODEL), F32),
                   jax.ShapeDtypeStruct((t, D_MODEL), F32),
                   jax.ShapeDtypeStruct((ROUTE_ROWS, t), F32)],
        compiler_params=_cparams(("parallel",)),
        name="merge",
    )(x2, o_att, hs, glu, ga, gb, mod3, g_ffn, wua, wul, wo, wr_t, br)


def _route_body(lt_ref, eid_ref, gate_ref, rank_ref, cnt_ref, carry_ref):
    step = pl.program_id(0)

    @pl.when(step == 0)
    def _():
        carry_ref[...] = jnp.zeros_like(carry_ref)

    tb = lt_ref.shape[1]
    lg = [lt_ref[r:r + 1, :] for r in range(N_GROUPS)]
    best = lg[0]
    gidx = jnp.zeros((1, tb), I32)
    for r in range(1, N_GROUPS):
        better = lg[r] > best
        gidx = jnp.where(better, r, gidx)
        best = jnp.maximum(best, lg[r])
    den = jnp.exp(lg[0] - best)
    for r in range(1, N_GROUPS):
        den = den + jnp.exp(lg[r] - best)
    p_top = 1.0 / den

    ev = []
    for j in range(EXPERTS_PER_GROUP):
        sel = lt_ref[N_GROUPS + j:N_GROUPS + j + 1, :]
        for g in range(1, N_GROUPS):
            row = N_GROUPS + g * EXPERTS_PER_GROUP + j
            sel = jnp.where(gidx == g, lt_ref[row:row + 1, :], sel)
        ev.append(sel)
    v0 = ev[0]
    i0 = jnp.zeros((1, tb), I32)
    for j in range(1, EXPERTS_PER_GROUP):
        better = ev[j] > v0
        i0 = jnp.where(better, j, i0)
        v0 = jnp.maximum(v0, ev[j])
    v1 = jnp.full((1, tb), -jnp.inf, F32)
    i1 = jnp.zeros((1, tb), I32)
    for j in range(EXPERTS_PER_GROUP):
        better = (ev[j] > v1) & (i0 != j)
        i1 = jnp.where(better, j, i1)
        v1 = jnp.where(better, ev[j], v1)
    e1 = jnp.exp(v1 - v0)
    inv = 1.0 / (1.0 + e1)
    eid0 = gidx * EXPERTS_PER_GROUP + i0
    eid1 = gidx * EXPERTS_PER_GROUP + i1
    eid_ref[0:1, :] = eid0
    eid_ref[1:2, :] = eid1
    gate_ref[...] = jnp.zeros_like(gate_ref)
    gate_ref[0:1, :] = p_top * inv
    gate_ref[1:2, :] = p_top * (e1 * inv)

    sub = 256
    e_iota = lax.broadcasted_iota(I32, (N_EXPERTS, sub), 0)
    tri = jnp.where(lax.broadcasted_iota(I32, (sub, sub), 0) <= lax.broadcasted_iota(I32, (sub, sub), 1),
                    1.0, 0.0).astype(BF16)
    carry = carry_ref[...]
    for c in range(tb // sub):
        sl = slice(c * sub, (c + 1) * sub)
        m0 = eid0[:, sl] == e_iota
        m1 = eid1[:, sl] == e_iota
        oh = jnp.where(m0 | m1, 1.0, 0.0)
        incl = _dot(oh.astype(BF16), tri)
        excl = incl - oh + carry[:, 0:1]
        rank_ref[0:1, sl] = jnp.sum(jnp.where(m0, excl, 0.0), axis=0, keepdims=True).astype(I32)
        rank_ref[1:2, sl] = jnp.sum(jnp.where(m1, excl, 0.0), axis=0, keepdims=True).astype(I32)
        carry = carry + incl[:, sub - 1:sub]
    carry_ref[...] = carry
    cnt_ref[...] = carry


def _route_call(logits_t):
    t = logits_t.shape[1]
    tb = ROUTE_TB
    col = lambda i: (0, i)
    return pl.pallas_call(
        _route_body,
        grid=(t // tb,),
        in_specs=[pl.BlockSpec((ROUTE_ROWS, tb), col)],
        out_specs=[pl.BlockSpec((TOP_K, tb), col), pl.BlockSpec((SUBLANES, tb), col),
                   pl.BlockSpec((TOP_K, tb), col), pl.BlockSpec((N_EXPERTS, LANES), lambda i: (0, 0))],
        out_shape=[jax.ShapeDtypeStruct((TOP_K, t), I32), jax.ShapeDtypeStruct((SUBLANES, t), F32),
                   jax.ShapeDtypeStruct((TOP_K, t), I32), jax.ShapeDtypeStruct((N_EXPERTS, LANES), F32)],
        scratch_shapes=[pltpu.VMEM((N_EXPERTS, LANES), F32)],
        compiler_params=_cparams(("arbitrary",)),
        name="route",
    )(logits_t)


def _dest_body(cnt_ref, eid_ref, rank_ref, dest_ref, blk_ref):
    cnt = cnt_ref[...].astype(I32)
    padded = ((cnt + (MOE_BLK - 1)) >> MOE_BLK_LOG2) << MOE_BLK_LOG2
    e_iota = lax.broadcasted_iota(I32, (N_EXPERTS, LANES), 0)
    p_end = jnp.zeros((N_EXPERTS, LANES), I32)
    for e in range(N_EXPERTS):
        tot = jnp.sum(jnp.where(e_iota <= e, padded, 0), axis=0, keepdims=True)
        p_end = jnp.where(e_iota == e, tot, p_end)
    p_start = p_end - padded
    tb = eid_ref.shape[1]
    ps = jnp.concatenate([p_start] * (tb // LANES), axis=1)
    e_wide = lax.broadcasted_iota(I32, (N_EXPERTS, tb), 0)
    for k in range(TOP_K):
        start = jnp.sum(jnp.where(eid_ref[k:k + 1, :] == e_wide, ps, 0), axis=0, keepdims=True)
        dest_ref[k:k + 1, :] = start + rank_ref[k:k + 1, :]
    nb = blk_ref.shape[1]
    pe = jnp.concatenate([p_end] * (nb // LANES), axis=1)
    first_row = lax.broadcasted_iota(I32, (N_EXPERTS, nb), 1) * MOE_BLK
    n_before = jnp.sum(jnp.where(pe <= first_row, 1, 0), axis=0, keepdims=True)
    blk = jnp.minimum(n_before, N_EXPERTS - 1)
    blk_ref[...] = jnp.broadcast_to(blk, blk_ref.shape)
    blk_ref[1:2, :] = jnp.broadcast_to(p_end[N_EXPERTS - 1:N_EXPERTS, 0:1] >> MOE_BLK_LOG2, (1, nb))
    on_diag = e_iota == lax.broadcasted_iota(I32, (N_EXPERTS, LANES), 1)
    blk_ref[2:3, 0:LANES] = jnp.sum(jnp.where(on_diag, p_end, 0), axis=0, keepdims=True)
    blk_ref[3:4, 0:LANES] = jnp.sum(jnp.where(on_diag, padded, 0), axis=0, keepdims=True)


def _dest_call(cnt, eid, rank, nb_pad):
    t = eid.shape[1]
    tb = ROUTE_TB
    col = lambda i: (0, i)
    return pl.pallas_call(
        _dest_body,
        grid=(t // tb,),
        in_specs=[pl.BlockSpec((N_EXPERTS, LANES), lambda i: (0, 0)),
                  pl.BlockSpec((TOP_K, tb), col), pl.BlockSpec((TOP_K, tb), col)],
        out_specs=[pl.BlockSpec((TOP_K, tb), col), pl.BlockSpec((SUBLANES, nb_pad), lambda i: (0, 0))],
        out_shape=[jax.ShapeDtypeStruct((TOP_K, t), I32), jax.ShapeDtypeStruct((SUBLANES, nb_pad), I32)],
        compiler_params=_cparams(("arbitrary",)),
        name="dest",
    )(cnt, eid, rank)


def _dispatch_body(dest_ref, pend_ref, plen_ref, h_ref, xs_ref, zero_ref, sem, zsem):
    tm = h_ref.shape[0]
    n_tok = pl.num_programs(0) * tm
    base = pl.program_id(0) * tm

    @pl.when(pl.program_id(0) == 0)
    def _():
        zero_ref[...] = jnp.zeros_like(zero_ref)

        def zero_block(start):
            return pltpu.make_async_copy(zero_ref, xs_ref.at[pl.ds(pl.multiple_of(start, MOE_BLK), MOE_BLK)], zsem)

        def fill(e, c):
            @pl.when(plen_ref[e] > 0)
            def _():
                zero_block(pend_ref[e] - MOE_BLK).start()
            return c

        def drain(e, c):
            @pl.when(plen_ref[e] > 0)
            def _():
                zero_block(pend_ref[e] - MOE_BLK).wait()
            return c

        lax.fori_loop(0, N_EXPERTS, fill, 0)
        n_used = pend_ref[N_EXPERTS - 1] >> MOE_BLK_LOG2
        n_blk = xs_ref.shape[0] // MOE_BLK
        lax.fori_loop(n_used, n_blk, lambda j, c: (zero_block(j * MOE_BLK).start(), c)[1], 0)
        lax.fori_loop(0, N_EXPERTS, drain, 0)
        lax.fori_loop(n_used, n_blk, lambda j, c: (zero_block(j * MOE_BLK).wait(), c)[1], 0)

    def issue(r, c):
        for k in range(TOP_K):
            d = dest_ref[k * n_tok + base + r]
            pltpu.make_async_copy(h_ref.at[pl.ds(r, 1)], xs_ref.at[pl.ds(d, 1)], sem).start(priority=k)
        return c

    lax.fori_loop(0, tm, issue, 0, unroll=8)
    for k in range(TOP_K):
        pltpu.make_async_copy(h_ref, xs_ref.at[pl.ds(0, tm)], sem).wait()


def _dispatch_call(dest_flat, p_end, p_len, h2, n_slots):
    t, w = h2.shape
    tm = DISPATCH_TM
    return pl.pallas_call(
        _dispatch_body,
        grid_spec=pltpu.PrefetchScalarGridSpec(
            num_scalar_prefetch=3,
            grid=(t // tm,),
            in_specs=[pl.BlockSpec((tm, w), lambda i, d, pe, pn: (i, 0))],
            out_specs=pl.BlockSpec(memory_space=pl.ANY),
            scratch_shapes=[pltpu.VMEM((MOE_BLK, w), F32),
                            pltpu.SemaphoreType.DMA(()), pltpu.SemaphoreType.DMA(())]),
        out_shape=jax.ShapeDtypeStruct((n_slots, w), F32),
        compiler_params=_cparams(("arbitrary",)),
        name="dispatch",
    )(dest_flat, p_end, p_len, h2)


EXPERT_BLKS_PER_STEP = 2


def _experts_body(blk_ref, used_ref, xs_ref, w1a_ref, w3a_ref, w2a_ref, w1b_ref, w3b_ref, w2b_ref, y_ref):
    del blk_ref
    i = pl.program_id(0)
    for half, (w1_ref, w3_ref, w2_ref) in enumerate(((w1a_ref, w3a_ref, w2a_ref), (w1b_ref, w3b_ref, w2b_ref))):
        blk = EXPERT_BLKS_PER_STEP * i + half
        rows = pl.ds(half * MOE_BLK, MOE_BLK)

        @pl.when(blk < used_ref[0])
        def _():
            x = xs_ref[rows, :].astype(BF16)
            g = _dot(x, w1_ref[0].astype(BF16))
            u = _dot(x, w3_ref[0].astype(BF16))
            mid = (g * _sigmoid(g)) * u
            y_ref[rows, :] = _dot(mid.astype(BF16), w2_ref[0].astype(BF16))

        @pl.when(blk >= used_ref[0])
        def _():
            y_ref[rows, :] = jnp.zeros((MOE_BLK, D_MODEL), F32)


def _experts_call(blk_e, n_used, xs, w1, w3, w2):
    n_slots, w = xs.shape
    per = EXPERT_BLKS_PER_STEP
    nb = n_slots // MOE_BLK
    assert nb % per == 0 and blk_e.shape[0] == nb
    wmap = lambda half: (lambda i, blk, used: (blk[per * i + half], 0, 0))
    wspecs = lambda half: [pl.BlockSpec((1, D_MODEL, D_EXPERT), wmap(half)),
                           pl.BlockSpec((1, D_MODEL, D_EXPERT), wmap(half)),
                           pl.BlockSpec((1, D_EXPERT, D_MODEL), wmap(half))]
    return pl.pallas_call(
        _experts_body,
        grid_spec=pltpu.PrefetchScalarGridSpec(
            num_scalar_prefetch=2,
            grid=(nb // per,),
            in_specs=[pl.BlockSpec((per * MOE_BLK, w), lambda i, blk, used: (i, 0))] + wspecs(0) + wspecs(1),
            out_specs=pl.BlockSpec((per * MOE_BLK, D_MODEL), lambda i, blk, used: (i, 0))),
        out_shape=jax.ShapeDtypeStruct((n_slots, D_MODEL), F32),
        compiler_params=_cparams(("arbitrary",)),
        name="experts",
    )(blk_e, n_used, xs, w1, w3, w2, w1, w3, w2)


def _combine_body(dest_ref, x1_ref, gate_ref, mod_ref, gf_ref, y_ref, o_ref, buf_ref, sem):
    tm = x1_ref.shape[0]
    step = pl.program_id(0)
    n_step = pl.num_programs(0)
    n_tok = n_step * tm
    slot = step % 2

    def start_gather(for_step):
        def issue(r, c):
            for k in range(TOP_K):
                d = dest_ref[k * n_tok + for_step * tm + r]
                pltpu.make_async_copy(y_ref.at[pl.ds(d, 1)], buf_ref.at[for_step % 2, k, pl.ds(r, 1)],
                                      sem.at[for_step % 2]).start(priority=k)
            return c

        lax.fori_loop(0, tm, issue, 0, unroll=8)

    @pl.when(step == 0)
    def _():
        start_gather(step)

    @pl.when(step + 1 < n_step)
    def _():
        start_gather(step + 1)

    eye = jnp.where(lax.broadcasted_iota(I32, (tm, tm), 0) == lax.broadcasted_iota(I32, (tm, tm), 1),
                    1.0, 0.0).astype(BF16)
    g = gate_ref[...]
    g1 = g.astype(BF16)
    rem = g - g1.astype(F32)
    g2 = rem.astype(BF16)
    g3 = (rem - g2.astype(F32)).astype(BF16)
    gt = _dot_nt(eye, g1) + (_dot_nt(eye, g2) + _dot_nt(eye, g3))

    for k in range(TOP_K):
        pltpu.make_async_copy(y_ref.at[pl.ds(0, tm)], buf_ref.at[slot, k], sem.at[slot]).wait()

    ga2 = mod_ref[0, :, 5 * D_MODEL:6 * D_MODEL]
    moe = gt[:, 0:1] * buf_ref[slot, 0] + gt[:, 1:2] * buf_ref[slot, 1]
    x2 = x1_ref[...] + ga2 * moe
    ms = jnp.mean(x2 * x2, axis=-1, keepdims=True)
    o_ref[...] = x2 * lax.rsqrt(ms + EPS) * gf_ref[...]


def _combine_call(dest_flat, x1, gate, mod3, g_final, y, seq):
    t = x1.shape[0]
    tm = COMBINE_TM
    per_b = seq // tm
    return pl.pallas_call(
        _combine_body,
        grid_spec=pltpu.PrefetchScalarGridSpec(
            num_scalar_prefetch=1,
            grid=(t // tm,),
            in_specs=[pl.BlockSpec((tm, D_MODEL), lambda i, d: (i, 0)),
                      pl.BlockSpec((SUBLANES, tm), lambda i, d: (0, i)),
                      pl.BlockSpec((1, 1, 6 * D_MODEL), lambda i, d: (i // per_b, 0, 0)),
                      pl.BlockSpec((1, D_MODEL), lambda i, d: (0, 0)),
                      pl.BlockSpec(memory_space=pl.ANY)],
            out_specs=pl.BlockSpec((tm, D_MODEL), lambda i, d: (i, 0)),
            scratch_shapes=[pltpu.VMEM((2, TOP_K, tm, D_MODEL), F32), pltpu.SemaphoreType.DMA((2,))]),
        out_shape=jax.ShapeDtypeStruct((t, D_MODEL), F32),
        compiler_params=_cparams(("arbitrary",)),
        name="combine",
    )(dest_flat, x1, gate, mod3, g_final, y)


def _rope_tables(seq):
    half = NA_HEAD_DIM // 2
    nf = half // 2
    inv_freq = ROPE_THETA ** (-jnp.arange(nf, dtype=F32) / nf)
    t = jnp.arange(seq)
    row_pos = (t // GRID_W).astype(F32)
    col_pos = (t % GRID_W).astype(F32)
    ang_r = row_pos[:, None] * inv_freq
    ang_c = col_pos[:, None] * inv_freq
    cos = jnp.concatenate([jnp.cos(ang_r), jnp.cos(ang_r), jnp.cos(ang_c), jnp.cos(ang_c)], axis=-1)
    sin = jnp.concatenate([-jnp.sin(ang_r), jnp.sin(ang_r), -jnp.sin(ang_c), jnp.sin(ang_c)], axis=-1)
    return jnp.tile(cos, (1, NA_HEADS)), jnp.tile(sin, (1, NA_HEADS))


def _layer(x, c, ctx, c_ctx, w_mod, b_mod, g_mix, g_ffn, w_in, rpb, conv_w, conv_b, lru_wa, lru_ba,
           lru_wx, lru_bx, lru_lambda, w_up_attn, w_up_lru, w_out, wg, bg, we, be, w1, w3, w2, g_final):
    bsz, seq, d = x.shape
    n_ctx = ctx.shape[1]
    t = bsz * seq
    assert d == D_MODEL and seq % ATT_TQ == 0 and seq // ATT_TQ > K_ROW_BLOCKS
    assert bsz + 1 <= MOD_ROWS and seq % PROJ_TM == 0 and seq % MERGE_TM == 0
    assert t % ROUTE_TB == 0 and t % COMBINE_TM == 0 and t % DISPATCH_TM == 0
    assert (bsz * n_ctx) % PROJ_TM == 0 and n_ctx <= seq
    assert seq % (SUBLANES * SUBLANES) == 0 and n_ctx % (SUBLANES * SUBLANES) == 0
    assert (seq // SUBLANES) % SCAN_STEPS == 0 and (n_ctx // SUBLANES) % SCAN_STEPS == 0

    cc = jnp.concatenate([c, c_ctx[None, :], jnp.zeros((MOD_ROWS - bsz - 1, d), F32)], axis=0)
    mod = _mod_call(cc, w_mod, b_mod)
    mod3 = mod[:bsz].reshape(bsz, 1, 6 * d)
    mod_c = mod[bsz:bsz + 1]

    x2 = x.reshape(t, d)
    g_mix2 = g_mix.reshape(1, d)
    w_in_bf = w_in.astype(BF16)
    kc, vc, lxc = _ctxproj_call(ctx.reshape(bsz * n_ctx, d), mod_c, g_mix2, w_in_bf[:, :CTX_COLS])
    cos_t, sin_t = _rope_tables(seq)
    qre, qro, qpe, qpo, k, v, lx, glu, ga, gb = _inproj_call(x2, mod3, g_mix2, w_in_bf, cos_t, sin_t, seq)

    bias = _bias_tables(_rpbcol_call(rpb))
    o_att = _attn_call(qre, qro, qpe, qpo, k, v, kc, vc, bias, bsz, seq, n_ctx)

    hs = _lru_call(lx.reshape(bsz, seq, LRU_WIDTH), lxc.reshape(bsz, n_ctx, LRU_WIDTH),
                   conv_w, conv_b.reshape(1, LRU_WIDTH), lru_wa, lru_wx, lru_ba, lru_bx, lru_lambda)

    wr_t = jnp.concatenate([wg.T, we.T, jnp.zeros((ROUTE_ROWS - N_GROUPS - N_EXPERTS, d), F32)], axis=0)
    br = jnp.concatenate([bg, be, jnp.zeros((ROUTE_ROWS - N_GROUPS - N_EXPERTS,), F32)])
    br = jnp.broadcast_to(br[:, None], (ROUTE_ROWS, LANES))
    x1, h2, logits_t = _merge_call(x2, o_att, hs.reshape(t, LRU_WIDTH), glu, ga, gb, mod3,
                                   g_ffn.reshape(1, d), w_up_attn.astype(BF16), w_up_lru.astype(BF16),
                                   w_out.astype(BF16), wr_t, br, seq)

    eid, gate, rank, cnt = _route_call(logits_t)
    n_blk = -(-(t * TOP_K + N_EXPERTS * (MOE_BLK - 1)) // MOE_BLK)
    n_blk = -(-n_blk // EXPERT_BLKS_PER_STEP) * EXPERT_BLKS_PER_STEP
    nb_pad = -(-n_blk // LANES) * LANES
    dest, blk = _dest_call(cnt, eid, rank, nb_pad)
    dest_flat = dest.reshape(TOP_K * t)
    xs = _dispatch_call(dest_flat, blk[2, :N_EXPERTS], blk[3, :N_EXPERTS], h2, n_blk * MOE_BLK)
    y = _experts_call(blk[0, :n_blk], blk[1, :1], xs, w1, w3, w2)
    return _combine_call(dest_flat, x1, gate, mod3, g_final.reshape(1, d), y, seq).reshape(bsz, seq, d)


def kernel(x, c, ctx, c_ctx, w_mod, b_mod, g_mix, g_ffn, w_in, rpb, conv_w, conv_b, lru_wa, lru_ba, lru_wx,
           lru_bx, lru_lambda, w_up_attn, w_up_lru, w_out, router_group_w, router_group_b, router_expert_w,
           router_expert_b, expert_w_gate, expert_w_up, expert_w_down, g_final):
    assert w_mod.shape[0] == 1, "single-layer block"
    return _layer(x, c, ctx, c_ctx, w_mod[0], b_mod[0], g_mix[0], g_ffn[0], w_in[0], rpb[0], conv_w[0],
                  conv_b[0], lru_wa[0], lru_ba[0], lru_wx[0], lru_bx[0], lru_lambda[0], w_up_attn[0],
                  w_up_lru[0], w_out[0], router_group_w[0], router_group_b[0], router_expert_w[0],
                  router_expert_b[0], expert_w_gate[0], expert_w_up[0], expert_w_down[0], g_final)
```

```python
import functools

import numpy as np
import jax
import jax.numpy as jnp
from jax import lax
from jax.experimental import pallas as pl
from jax.experimental.pallas import tpu as pltpu

F32 = jnp.float32
BF16 = jnp.bfloat16
I32 = jnp.int32
U32 = jnp.uint32

D_MODEL = 1024
GRID_W = 64
EPS = 1e-6
NEG_INF = -1e30

NA_HEADS = 8
NA_HEAD_DIM = 64
NA_WIDTH = NA_HEADS * NA_HEAD_DIM
NA_WIN_ROWS = 8
NA_WIN_COLS = 16
ROPE_THETA = 10000.0

LRU_WIDTH = D_MODEL
LRU_BLOCKS = 8
LRU_BLOCK = LRU_WIDTH // LRU_BLOCKS
LRU_CONV = 4
LRU_C = 8.0

N_GROUPS = 4
EXPERTS_PER_GROUP = 8
N_EXPERTS = N_GROUPS * EXPERTS_PER_GROUP
TOP_K = 2
D_EXPERT = 512

K_OFF = 0
V_OFF = K_OFF + NA_WIDTH
LX_OFF = V_OFF + NA_WIDTH
CTX_COLS = LX_OFF + LRU_WIDTH
Q_OFF = CTX_COLS
LG_OFF = Q_OFF + NA_WIDTH
GA_OFF = LG_OFF + LRU_WIDTH
GB_OFF = GA_OFF + D_MODEL
PROJ_COLS = GB_OFF + D_MODEL

LANES = 128
SUBLANES = 8

Q_ROWS = 4
K_ROW_BLOCKS = 3
ATT_TQ = Q_ROWS * GRID_W
ATT_TK = K_ROW_BLOCKS * ATT_TQ

MOE_BLK_LOG2 = 8
MOE_BLK = 1 << MOE_BLK_LOG2
MOD_ROWS = 24
ROUTE_ROWS = 64

PROJ_TM = 512
MERGE_TM = 512
ROUTE_TB = 2048
DISPATCH_TM = 512
COMBINE_TM = 256

VMEM_LIMIT = 56 * 1024 * 1024


def _cparams(sem, vmem=VMEM_LIMIT):
    return pltpu.CompilerParams(dimension_semantics=sem, vmem_limit_bytes=vmem)


def _dot(a, b):
    return jnp.dot(a, b, preferred_element_type=F32)


def _dot_nt(a, b):
    return lax.dot_general(a, b, (((1,), (1,)), ((), ())), preferred_element_type=F32)


def _split2(a):
    hi = a.astype(BF16)
    lo = (a - hi.astype(F32)).astype(BF16)
    return hi, lo


def _dot3(a, b):
    ah, al = _split2(a)
    bh, bl = _split2(b)
    return _dot(ah, bh) + (_dot(ah, bl) + _dot(al, bh))


def _dot3_nt(a, b):
    ah, al = _split2(a)
    bh, bl = _split2(b)
    return _dot_nt(ah, bh) + (_dot_nt(ah, bl) + _dot_nt(al, bh))


def _sigmoid(x):
    return 1.0 / (1.0 + jnp.exp(-x))


def _ada_norm(x, g, sc, sh):
    ms = jnp.mean(x * x, axis=-1, keepdims=True)
    return (x * lax.rsqrt(ms + EPS) * g) * (1.0 + sc) + sh


def _mod_body(cc_ref, w_ref, b_ref, o_ref):
    cc = cc_ref[...]
    o_ref[...] = _dot3(cc * _sigmoid(cc), w_ref[...]) + b_ref[...]


def _mod_call(cc, w_mod, b_mod):
    n = w_mod.shape[1]
    bn = 1024
    return pl.pallas_call(
        _mod_body,
        grid=(n // bn,),
        in_specs=[pl.BlockSpec((MOD_ROWS, D_MODEL), lambda j: (0, 0)),
                  pl.BlockSpec((D_MODEL, bn), lambda j: (0, j)),
                  pl.BlockSpec((1, bn), lambda j: (0, j))],
        out_specs=pl.BlockSpec((MOD_ROWS, bn), lambda j: (0, j)),
        out_shape=jax.ShapeDtypeStruct((MOD_ROWS, n), F32),
        compiler_params=_cparams(("arbitrary",)),
        name="mod",
    )(cc, w_mod, b_mod.reshape(1, n))


def _rope(t, cos, sin):
    lane = lax.broadcasted_iota(I32, (t.shape[0], LANES), 1)
    first = (lane & 16) == 0
    parts = []
    for c in range(t.shape[1] // LANES):
        tc = t[:, c * LANES:(c + 1) * LANES]
        parts.append(jnp.where(first, pltpu.roll(tc, LANES - 16, 1), pltpu.roll(tc, 16, 1)))
    partner = jnp.concatenate(parts, axis=1)
    return t * cos + partner * sin


def _gelu_tanh(x):
    return 0.5 * x * (1.0 + jnp.tanh(0.7978845608028654 * (x + 0.044715 * (x * x * x))))


def _inproj_body(x_ref, mod_ref, g_ref, w_ref, cos_ref, sin_ref,
                 qre_ref, qro_ref, qpe_ref, qpo_ref, k_ref, v_ref, lx_ref, glu_ref, ga_ref, gb_ref):
    sh = mod_ref[0, :, 0:D_MODEL]
    sc = mod_ref[0, :, D_MODEL:2 * D_MODEL]
    h = _ada_norm(x_ref[...], g_ref[...], sc, sh).astype(BF16)
    cos = cos_ref[...]
    sin = sin_ref[...]
    scale = NA_HEAD_DIM ** -0.5

    k_ref[...] = _rope(_dot(h, w_ref[:, K_OFF:K_OFF + NA_WIDTH]), cos, sin).astype(BF16)
    v_ref[...] = _dot(h, w_ref[:, V_OFF:V_OFF + NA_WIDTH]).astype(BF16)
    lx_ref[...] = _dot(h, w_ref[:, LX_OFF:LX_OFF + LRU_WIDTH])

    q = _dot(h, w_ref[:, Q_OFF:Q_OFF + NA_WIDTH]) * scale
    qr = _rope(q, cos, sin)
    lane = lax.broadcasted_iota(I32, q.shape, 1)
    even = (lane & NA_HEAD_DIM) == 0
    qre_ref[...] = jnp.where(even, qr, 0.0).astype(BF16)
    qro_ref[...] = jnp.where(even, 0.0, qr).astype(BF16)
    qpe_ref[...] = jnp.where(even, q, 0.0).astype(BF16)
    qpo_ref[...] = jnp.where(even, 0.0, q).astype(BF16)

    glu_ref[...] = _gelu_tanh(_dot(h, w_ref[:, LG_OFF:LG_OFF + LRU_WIDTH])).astype(BF16)
    ga_ref[...] = _sigmoid(_dot(h, w_ref[:, GA_OFF:GA_OFF + D_MODEL])).astype(BF16)
    gb_ref[...] = _sigmoid(_dot(h, w_ref[:, GB_OFF:GB_OFF + D_MODEL])).astype(BF16)


def _inproj_call(x2, mod3, g_mix, w_in_bf, cos_t, sin_t, seq):
    t = x2.shape[0]
    tm = PROJ_TM
    per_b = seq // tm
    row = lambda i: (i, 0)
    wide = lambda n, dt: jax.ShapeDtypeStruct((t, n), dt)
    return pl.pallas_call(
        _inproj_body,
        grid=(t // tm,),
        in_specs=[pl.BlockSpec((tm, D_MODEL), row),
                  pl.BlockSpec((1, 1, 6 * D_MODEL), lambda i: (i // per_b, 0, 0)),
                  pl.BlockSpec((1, D_MODEL), lambda i: (0, 0)),
                  pl.BlockSpec((D_MODEL, PROJ_COLS), lambda i: (0, 0), pipeline_mode=pl.Buffered(1)),
                  pl.BlockSpec((tm, NA_WIDTH), lambda i: (i % per_b, 0)),
                  pl.BlockSpec((tm, NA_WIDTH), lambda i: (i % per_b, 0))],
        out_specs=[pl.BlockSpec((tm, NA_WIDTH), row)] * 6
                  + [pl.BlockSpec((tm, LRU_WIDTH), row)] * 4,
        out_shape=[wide(NA_WIDTH, BF16)] * 6
                  + [wide(LRU_WIDTH, F32), wide(LRU_WIDTH, BF16), wide(D_MODEL, BF16), wide(D_MODEL, BF16)],
        compiler_params=_cparams(("parallel",)),
        name="inproj",
    )(x2, mod3, g_mix, w_in_bf, cos_t, sin_t)


def _ctxproj_body(x_ref, mod_ref, g_ref, w_ref, k_ref, v_ref, lx_ref):
    sh = mod_ref[:, 0:D_MODEL]
    sc = mod_ref[:, D_MODEL:2 * D_MODEL]
    h = _ada_norm(x_ref[...], g_ref[...], sc, sh).astype(BF16)
    k_ref[...] = _dot(h, w_ref[:, K_OFF:K_OFF + NA_WIDTH]).astype(BF16)
    v_ref[...] = _dot(h, w_ref[:, V_OFF:V_OFF + NA_WIDTH]).astype(BF16)
    lx_ref[...] = _dot(h, w_ref[:, LX_OFF:LX_OFF + LRU_WIDTH])


def _ctxproj_call(c2, mod_c, g_mix, w_ctx_bf):
    t = c2.shape[0]
    tm = PROJ_TM
    row = lambda i: (i, 0)
    return pl.pallas_call(
        _ctxproj_body,
        grid=(t // tm,),
        in_specs=[pl.BlockSpec((tm, D_MODEL), row),
                  pl.BlockSpec((1, 6 * D_MODEL), lambda i: (0, 0)),
                  pl.BlockSpec((1, D_MODEL), lambda i: (0, 0)),
                  pl.BlockSpec((D_MODEL, CTX_COLS), lambda i: (0, 0))],
        out_specs=[pl.BlockSpec((tm, NA_WIDTH), row), pl.BlockSpec((tm, NA_WIDTH), row),
                   pl.BlockSpec((tm, LRU_WIDTH), row)],
        out_shape=[jax.ShapeDtypeStruct((t, NA_WIDTH), BF16), jax.ShapeDtypeStruct((t, NA_WIDTH), BF16),
                   jax.ShapeDtypeStruct((t, LRU_WIDTH), F32)],
        compiler_params=_cparams(("parallel",)),
        name="ctxproj",
    )(c2, mod_c, g_mix, w_ctx_bf)


N_DR = 2 * NA_WIN_ROWS - 1
N_DC = 2 * NA_WIN_COLS - 1


def _rpbcol_body(rpb_ref, o_ref):
    n = GRID_W * GRID_W
    flat = lax.broadcasted_iota(I32, (32, n), 1)
    qc = flat >> 6
    kc = flat & (GRID_W - 1)
    dc = jnp.clip(kc - qc, 1 - NA_WIN_COLS, NA_WIN_COLS - 1) + (NA_WIN_COLS - 1)
    d_iota = lax.broadcasted_iota(I32, (32, n), 0)
    onehot = jnp.where(dc == d_iota, 1.0, 0.0).astype(BF16)
    r = rpb_ref[...]
    r1 = r.astype(BF16)
    rem = r - r1.astype(F32)
    r2 = rem.astype(BF16)
    r3 = (rem - r2.astype(F32)).astype(BF16)
    val = _dot(r1, onehot) + (_dot(r2, onehot) + _dot(r3, onehot))
    qc1 = qc[0:1, :]
    kc1 = kc[0:1, :]
    c_start = jnp.clip(qc1 - NA_WIN_COLS // 2, 0, GRID_W - NA_WIN_COLS)
    band = (kc1 >= c_start) & (kc1 < c_start + NA_WIN_COLS)
    o_ref[...] = jnp.where(band, val, NEG_INF)


def _rpbcol_call(rpb):
    rows = NA_HEADS * N_DR
    r2 = jnp.pad(rpb.reshape(rows, N_DC), ((0, 0), (0, 32 - N_DC)))
    n = GRID_W * GRID_W
    return pl.pallas_call(
        _rpbcol_body,
        in_specs=[pl.BlockSpec((rows, 32), lambda: (0, 0))],
        out_specs=pl.BlockSpec((rows, n), lambda: (0, 0)),
        out_shape=jax.ShapeDtypeStruct((rows, n), F32),
        name="rpbcol",
    )(r2)


def _bias_tables(rpbcol):
    t = rpbcol.reshape(NA_HEADS, N_DR, GRID_W, GRID_W)
    neg = jnp.full((NA_HEADS, GRID_W, GRID_W), NEG_INF, F32)
    n_kj = K_ROW_BLOCKS * Q_ROWS
    classes = []
    for lo_fn, dr_off in ((lambda ri: 0, 7), (lambda ri: ri, 3), (lambda ri: 4, -1)):
        rows = []
        for ri in range(Q_ROWS):
            lo = lo_fn(ri)
            blocks = []
            for kj in range(n_kj):
                inside = lo <= kj < lo + NA_WIN_ROWS
                blocks.append(t[:, kj - ri + dr_off] if inside else neg)
            rows.append(jnp.concatenate(blocks, axis=2))
        classes.append(jnp.concatenate(rows, axis=1))
    return jnp.stack(classes, axis=0)


def _attn_body(qre_ref, qro_ref, qpe_ref, qpo_ref, k0_ref, k1_ref, k2_ref, v0_ref, v1_ref, v2_ref,
               kc_ref, vc_ref, bias_ref, o_ref):
    lane = lax.broadcasted_iota(I32, (ATT_TQ, LANES), 1)
    k_refs = (k0_ref, k1_ref, k2_ref)
    v_refs = (v0_ref, v1_ref, v2_ref)
    for p in range(NA_HEADS // 2):
        sl = slice(p * LANES, (p + 1) * LANES)
        kk = [r[:, sl] for r in k_refs]
        vv = [r[:, sl] for r in v_refs]
        kc = kc_ref[:, sl]
        vc = vc_ref[:, sl]
        outs = []
        for hh, (qr_ref, qp_ref) in enumerate(((qre_ref, qpe_ref), (qro_ref, qpo_ref))):
            h = 2 * p + hh
            qr = qr_ref[:, sl]
            qp = qp_ref[:, sl]
            s = [_dot_nt(qr, kk[j]) + bias_ref[0, h, :, j * ATT_TQ:(j + 1) * ATT_TQ]
                 for j in range(K_ROW_BLOCKS)]
            s.append(_dot_nt(qp, kc))
            m = s[0].max(axis=-1, keepdims=True)
            for sj in s[1:]:
                m = jnp.maximum(m, sj.max(axis=-1, keepdims=True))
            e = [jnp.exp(sj - m) for sj in s]
            den = e[0].sum(axis=-1, keepdims=True)
            for ej in e[1:]:
                den = den + ej.sum(axis=-1, keepdims=True)
            acc = _dot(e[K_ROW_BLOCKS].astype(BF16), vc)
            for j in range(K_ROW_BLOCKS):
                acc = acc + _dot(e[j].astype(BF16), vv[j])
            outs.append(acc / den)
        o_ref[:, sl] = jnp.where(lane < NA_HEAD_DIM, outs[0], outs[1]).astype(BF16)


def _attn_call(qre, qro, qpe, qpo, k, v, kc, vc, bias, bsz, seq, n_ctx):
    t = bsz * seq
    n_grp = seq // ATT_TQ
    per_b = n_grp
    max_kb = n_grp - K_ROW_BLOCKS

    def qmap(g, b):
        return (b * per_b + g, 0)

    def kmap(j):
        return lambda g, b: (b * per_b + jnp.clip(g - 1, 0, max_kb) + j, 0)

    def cls(g, b):
        return (jnp.where(g == 0, 0, jnp.where(g == n_grp - 1, 2, 1)), 0, 0, 0)

    qspec = pl.BlockSpec((ATT_TQ, NA_WIDTH), qmap)
    cspec = pl.BlockSpec((n_ctx, NA_WIDTH), lambda g, b: (b, 0))
    return pl.pallas_call(
        _attn_body,
        grid=(n_grp, bsz),
        in_specs=[qspec] * 4
                 + [pl.BlockSpec((ATT_TQ, NA_WIDTH), kmap(j)) for j in range(K_ROW_BLOCKS)] * 2
                 + [cspec, cspec, pl.BlockSpec((1, NA_HEADS, ATT_TQ, ATT_TK), cls)],
        out_specs=pl.BlockSpec((ATT_TQ, NA_WIDTH), qmap),
        out_shape=jax.ShapeDtypeStruct((t, NA_WIDTH), BF16),
        compiler_params=_cparams(("arbitrary", "arbitrary")),
        name="attn",
    )(qre, qro, qpe, qpo, k, k, k, v, v, v, kc, vc, bias)


def _shift_down(v, row):
    return jnp.where(row >= 1, pltpu.roll(v, 1, 0), 0.0)


def _shift_up(v, row):
    return jnp.where(row < SUBLANES - 1, pltpu.roll(v, SUBLANES - 1, 0), 0.0)


def _conv4(x, w_ref, b_ref):
    n = x.shape[0]
    s = SUBLANES
    row = lax.broadcasted_iota(I32, (s, LANES), 0)
    last = _shift_down(x[n - s:n], row)
    last2 = _shift_down(x[n - 2 * s:n - s], row)
    first = _shift_up(x[0:s], row)
    xm1 = jnp.concatenate([last, x[0:n - s]], axis=0)
    xm2 = jnp.concatenate([last2, last, x[0:n - 2 * s]], axis=0)
    xp1 = jnp.concatenate([x[s:n], first], axis=0)
    return (w_ref[0:1, :] * xm2 + w_ref[1:2, :] * xm1 + w_ref[2:3, :] * x + w_ref[3:4, :] * xp1) + b_ref[...]


def _softplus(z):
    return jnp.maximum(z, 0.0) + jnp.log1p(jnp.exp(-jnp.abs(z)))


def _gates(xc, d, wa_ref, wx_ref, ba_ref, bx_ref, lam_ref, a_ref, u_ref):
    n = xc.shape[0]
    xb = xc.astype(BF16)
    tr = jnp.tanh(_dot(xb, (0.5 * wa_ref[d, 0]).astype(BF16)) + 0.5 * ba_ref[d:d + 1, :])
    ti = jnp.tanh(_dot(xb, (0.5 * wx_ref[d, 0]).astype(BF16)) + 0.5 * bx_ref[d:d + 1, :])
    half_c = (0.5 * LRU_C) * _softplus(-lam_ref[d:d + 1, :])
    neg_log_a = half_c * tr + half_c
    a = jnp.exp(-neg_log_a)
    a_ref[0:n, :] = a
    s2 = jnp.tanh(neg_log_a) * (a * a + 1.0)
    root = jnp.where(s2 > 0.0, s2 * lax.rsqrt(s2), 0.0)
    xh = 0.5 * xc
    u_ref[0:n, :] = root * (xh * ti + xh)


def _scan4(a, u, h, p):
    a01 = a[1] * a[0]
    u01 = a[1] * u[0] + u[1]
    a23 = a[3] * a[2]
    u23 = a[3] * u[2] + u[3]
    a012 = a[2] * a01
    u012 = a[2] * u01 + u[2]
    a0123 = a23 * a01
    u0123 = a23 * u01 + u23
    hs = [a[0] * h + u[0], a01 * h + u01, a012 * h + u012, a0123 * h + u0123]
    ps = [a[0] * p, a01 * p, a012 * p, a0123 * p]
    return hs, ps


SCAN_STEPS = 4


def _scan_local(af_ref, uf_ref, ab_ref, ub_ref, n_vreg):
    s = SUBLANES
    zero = jnp.zeros((s, LANES), F32)
    one = jnp.ones((s, LANES), F32)
    span = SCAN_STEPS * s

    def body(q, carry):
        hf, pf, hb, pb = carry
        base = pl.multiple_of(q * span, span)
        rows = [pl.ds(base + i * s, s) for i in range(SCAN_STEPS)]
        hs, ps = _scan4([af_ref[r, :] for r in rows], [uf_ref[r, :] for r in rows], hf, pf)
        for r, h, p in zip(rows, hs, ps):
            uf_ref[r, :] = h
            af_ref[r, :] = p
        hf, pf = hs[-1], ps[-1]
        base = pl.multiple_of((n_vreg - SCAN_STEPS) * s - q * span, span)
        rows = [pl.ds(base + (SCAN_STEPS - 1 - i) * s, s) for i in range(SCAN_STEPS)]
        hs, ps = _scan4([ab_ref[r, :] for r in rows], [ub_ref[r, :] for r in rows], hb, pb)
        for r, h, p in zip(rows, hs, ps):
            ub_ref[r, :] = h
            ab_ref[r, :] = p
        return hf, pf, hs[-1], ps[-1]

    return lax.fori_loop(0, n_vreg // SCAN_STEPS, body, (zero, one, zero, one), unroll=2)


def _link_states(hf, pf, hb, pb, h0f, h0b):
    s = SUBLANES
    row = lax.broadcasted_iota(I32, (s, LANES), 0)
    a, u = pf, hf
    for k in (1, 2, 4):
        keep = row >= k
        u = u + a * jnp.where(keep, pltpu.roll(u, k, 0), 0.0)
        a = a * jnp.where(keep, pltpu.roll(a, k, 0), 1.0)
    end_f = u + a * h0f
    in_f = jnp.where(row >= 1, pltpu.roll(end_f, 1, 0), h0f)
    a, u = pb, hb
    for k in (1, 2, 4):
        keep = row < s - k
        u = u + a * jnp.where(keep, pltpu.roll(u, s - k, 0), 0.0)
        a = a * jnp.where(keep, pltpu.roll(a, s - k, 0), 1.0)
    end_b = u + a * h0b
    in_b = jnp.where(row < s - 1, pltpu.roll(end_b, s - 1, 0), h0b)
    return in_f, in_b, end_f[s - 1:s, :], end_b[0:1, :]


PITCH_PAD = 4


def _to_split(x_ref, pad_ref, dst_ref, n):
    s = SUBLANES
    n_j = n // s
    pitch = n_j + PITCH_PAD
    for q in range(s):
        pad_ref[pl.ds(q * pitch, n_j), :] = x_ref[0, pl.ds(q * n_j, n_j), :]

    def body(j, c):
        dst_ref[pl.ds(pl.multiple_of(j * s, s), s), :] = pad_ref[pl.ds(j, s, stride=pitch), :]
        return c

    lax.fori_loop(0, n_j, body, 0, unroll=8)


def _lru_body(lx_ref, lxc_ref, cw_ref, cb_ref, wa_ref, wx_ref, ba_ref, bx_ref, lam_ref,
              o_ref, af_ref, uf_ref, ab_ref, ub_ref, pad_ref):
    n = lx_ref.shape[1]
    n_c = lxc_ref.shape[1]
    s = SUBLANES
    n_j = n // s
    pitch = n_j + PITCH_PAD
    gate_args = (wa_ref, wx_ref, ba_ref, bx_ref, lam_ref)
    zero = jnp.zeros((1, LANES), F32)

    _to_split(lxc_ref, pad_ref, uf_ref, n_c)
    xc = _conv4(uf_ref[0:n_c, :], cw_ref, cb_ref)
    _gates(xc, 0, *gate_args, af_ref, uf_ref)
    _gates(xc, 1, *gate_args, ab_ref, ub_ref)
    ends = _scan_local(af_ref, uf_ref, ab_ref, ub_ref, n_c // s)
    _, _, cf, cb = _link_states(*ends, zero, zero)

    _to_split(lx_ref, pad_ref, uf_ref, n)
    xl = _conv4(uf_ref[...], cw_ref, cb_ref)
    _gates(xl, 0, *gate_args, af_ref, uf_ref)
    _gates(xl, 1, *gate_args, ab_ref, ub_ref)
    ends = _scan_local(af_ref, uf_ref, ab_ref, ub_ref, n_j)
    in_f, in_b, _, _ = _link_states(*ends, cf, cb)

    def finish(j, c):
        rows = pl.ds(pl.multiple_of(j * s, s), s)
        h = (uf_ref[rows, :] + af_ref[rows, :] * in_f) + (ub_ref[rows, :] + ab_ref[rows, :] * in_b)
        pad_ref[pl.ds(j, s, stride=pitch), :] = h
        return c

    lax.fori_loop(0, n_j, finish, 0, unroll=8)
    for q in range(s):
        o_ref[0, pl.ds(q * n_j, n_j), :] = pad_ref[pl.ds(q * pitch, n_j), :]


def _lru_call(lx3, lxc3, conv_w, conv_b, wa, wx, ba, bx, lam):
    bsz, seq, _ = lx3.shape
    n_ctx = lxc3.shape[1]
    col = lambda b, n: (b, 0, n)
    par = lambda b, n: (0, n)
    wspec = pl.BlockSpec((2, 1, LRU_BLOCK, LRU_BLOCK), lambda b, n: (0, n, 0, 0))
    return pl.pallas_call(
        _lru_body,
        grid=(bsz, LRU_BLOCKS),
        in_specs=[pl.BlockSpec((1, seq, LRU_BLOCK), col),
                  pl.BlockSpec((1, n_ctx, LRU_BLOCK), col),
                  pl.BlockSpec((LRU_CONV, LRU_BLOCK), par),
                  pl.BlockSpec((1, LRU_BLOCK), par),
                  wspec, wspec,
                  pl.BlockSpec((2, LRU_BLOCK), par),
                  pl.BlockSpec((2, LRU_BLOCK), par),
                  pl.BlockSpec((2, LRU_BLOCK), par)],
        out_specs=pl.BlockSpec((1, seq, LRU_BLOCK), col),
        out_shape=jax.ShapeDtypeStruct((bsz, seq, LRU_WIDTH), F32),
        scratch_shapes=[pltpu.VMEM((seq, LRU_BLOCK), F32)] * 4
                       + [pltpu.VMEM((seq + SUBLANES * PITCH_PAD, LRU_BLOCK), F32)],
        compiler_params=_cparams(("parallel", "arbitrary")),
        name="lru",
    )(lx3, lxc3, conv_w, conv_b, wa, wx, ba, bx, lam)


TOKEN_TILE = D_MODEL // LANES


def _store_token_tiles(ref, row0, x):
    m = x.shape[0]
    for c in range(TOKEN_TILE):
        ref[pl.ds(row0 + c, m, stride=TOKEN_TILE), :] = x[:, c * LANES:(c + 1) * LANES]


def _load_token_tiles(ref, row0, m):
    return jnp.concatenate([ref[pl.ds(row0 + c, m, stride=TOKEN_TILE), :] for c in range(TOKEN_TILE)], axis=1)


def _merge_body(x_ref, oa_ref, hs_ref, glu_ref, ga_ref, gb_ref, mod_ref, g_ref, wua_ref, wul_ref, wo_ref,
                wr_ref, br_ref, x1_ref, h2_ref, lt_ref):
    ga1 = mod_ref[0, :, 2 * D_MODEL:3 * D_MODEL]
    sh2 = mod_ref[0, :, 3 * D_MODEL:4 * D_MODEL]
    sc2 = mod_ref[0, :, 4 * D_MODEL:5 * D_MODEL]
    o_lru = (hs_ref[...] * glu_ref[...].astype(F32)).astype(BF16)
    y = (ga_ref[...].astype(F32) * _dot(oa_ref[...], wua_ref[...])
         + gb_ref[...].astype(F32) * _dot(o_lru, wul_ref[...]))
    x1 = x_ref[...] + ga1 * _dot(y.astype(BF16), wo_ref[...])
    x1_ref[...] = x1
    h2 = _ada_norm(x1, g_ref[...], sc2, sh2)
    lt_ref[...] = _dot3_nt(wr_ref[...], h2) + br_ref[:, 0:1]
    _store_token_tiles(h2_ref, 0, h2)


def _merge_call(x2, o_att, hs, glu, ga, gb, mod3, g_ffn, wua, wul, wo, wr_t, br, seq):
    t = x2.shape[0]
    tm = MERGE_TM
    per_b = seq // tm
    row = lambda i: (i, 0)
    full = lambda i: (0, 0)
    resident = lambda shape: pl.BlockSpec(shape, full, pipeline_mode=pl.Buffered(1))
    return pl.pallas_call(
        _merge_body,
        grid=(t // tm,),
        in_specs=[pl.BlockSpec((tm, D_MODEL), row),
                  pl.BlockSpec((tm, NA_WIDTH), row),
                  pl.BlockSpec((tm, LRU_WIDTH), row),
                  pl.BlockSpec((tm, LRU_WIDTH), row),
                  pl.BlockSpec((tm, D_MODEL), row),
                  pl.BlockSpec((tm, D_MODEL), row),
                  pl.BlockSpec((1, 1, 6 * D_MODEL), lambda i: (i // per_b, 0, 0)),
                  pl.BlockSpec((1, D_MODEL), full),
                  resident((NA_WIDTH, D_MODEL)),
                  resident((LRU_WIDTH, D_MODEL)),
                  resident((D_MODEL, D_MODEL)),
                  pl.BlockSpec((ROUTE_ROWS, D_MODEL), full),
                  pl.BlockSpec((ROUTE_ROWS, LANES), full)],
        out_specs=[pl.BlockSpec((tm, D_MODEL), row),
                   pl.BlockSpec((tm * TOKEN_TILE, LANES), row),
                   pl.BlockSpec((ROUTE_ROWS, tm), lambda i: (0, i))],
        out_shape=[jax.ShapeDtypeStruct((t, D_MODEL), F32),
                   jax.ShapeDtypeStruct((t * TOKEN_TILE, LANES), F32),
                   jax.ShapeDtypeStruct((ROUTE_ROWS, t), F32)],
        compiler_params=_cparams(("parallel",)),
        name="merge",
    )(x2, o_att, hs, glu, ga, gb, mod3, g_ffn, wua, wul, wo, wr_t, br)


def _route_body(lt_ref, eid_ref, gate_ref, rank_ref, cnt_ref, carry_ref):
    step = pl.program_id(0)

    @pl.when(step == 0)
    def _():
        carry_ref[...] = jnp.zeros_like(carry_ref)

    tb = lt_ref.shape[1]
    lg = [lt_ref[r:r + 1, :] for r in range(N_GROUPS)]
    best = lg[0]
    gidx = jnp.zeros((1, tb), I32)
    for r in range(1, N_GROUPS):
        better = lg[r] > best
        gidx = jnp.where(better, r, gidx)
        best = jnp.maximum(best, lg[r])
    den = jnp.exp(lg[0] - best)
    for r in range(1, N_GROUPS):
        den = den + jnp.exp(lg[r] - best)
    p_top = 1.0 / den

    ev = []
    for j in range(EXPERTS_PER_GROUP):
        sel = lt_ref[N_GROUPS + j:N_GROUPS + j + 1, :]
        for g in range(1, N_GROUPS):
            row = N_GROUPS + g * EXPERTS_PER_GROUP + j
            sel = jnp.where(gidx == g, lt_ref[row:row + 1, :], sel)
        ev.append(sel)
    v0 = ev[0]
    i0 = jnp.zeros((1, tb), I32)
    for j in range(1, EXPERTS_PER_GROUP):
        better = ev[j] > v0
        i0 = jnp.where(better, j, i0)
        v0 = jnp.maximum(v0, ev[j])
    v1 = jnp.full((1, tb), -jnp.inf, F32)
    i1 = jnp.zeros((1, tb), I32)
    for j in range(EXPERTS_PER_GROUP):
        better = (ev[j] > v1) & (i0 != j)
        i1 = jnp.where(better, j, i1)
        v1 = jnp.where(better, ev[j], v1)
    e1 = jnp.exp(v1 - v0)
    inv = 1.0 / (1.0 + e1)
    eid0 = gidx * EXPERTS_PER_GROUP + i0
    eid1 = gidx * EXPERTS_PER_GROUP + i1
    eid_ref[0:1, :] = eid0
    eid_ref[1:2, :] = eid1
    gate_ref[...] = jnp.zeros_like(gate_ref)
    gate_ref[0:1, :] = p_top * inv
    gate_ref[1:2, :] = p_top * (e1 * inv)

    sub = 256
    e_iota = lax.broadcasted_iota(I32, (N_EXPERTS, sub), 0)
    tri = jnp.where(lax.broadcasted_iota(I32, (sub, sub), 0) <= lax.broadcasted_iota(I32, (sub, sub), 1),
                    1.0, 0.0).astype(BF16)
    carry = carry_ref[...]
    for c in range(tb // sub):
        sl = slice(c * sub, (c + 1) * sub)
        m0 = eid0[:, sl] == e_iota
        m1 = eid1[:, sl] == e_iota
        oh = jnp.where(m0 | m1, 1.0, 0.0)
        incl = _dot(oh.astype(BF16), tri)
        excl = incl - oh + carry[:, 0:1]
        rank_ref[0:1, sl] = jnp.sum(jnp.where(m0, excl, 0.0), axis=0, keepdims=True).astype(I32)
        rank_ref[1:2, sl] = jnp.sum(jnp.where(m1, excl, 0.0), axis=0, keepdims=True).astype(I32)
        carry = carry + incl[:, sub - 1:sub]
    carry_ref[...] = carry
    cnt_ref[...] = carry


def _route_call(logits_t):
    t = logits_t.shape[1]
    tb = ROUTE_TB
    col = lambda i: (0, i)
    return pl.pallas_call(
        _route_body,
        grid=(t // tb,),
        in_specs=[pl.BlockSpec((ROUTE_ROWS, tb), col)],
        out_specs=[pl.BlockSpec((TOP_K, tb), col), pl.BlockSpec((SUBLANES, tb), col),
                   pl.BlockSpec((TOP_K, tb), col), pl.BlockSpec((N_EXPERTS, LANES), lambda i: (0, 0))],
        out_shape=[jax.ShapeDtypeStruct((TOP_K, t), I32), jax.ShapeDtypeStruct((SUBLANES, t), F32),
                   jax.ShapeDtypeStruct((TOP_K, t), I32), jax.ShapeDtypeStruct((N_EXPERTS, LANES), F32)],
        scratch_shapes=[pltpu.VMEM((N_EXPERTS, LANES), F32)],
        compiler_params=_cparams(("arbitrary",)),
        name="route",
    )(logits_t)


def _dest_body(cnt_ref, eid_ref, rank_ref, dest_ref, blk_ref):
    cnt = cnt_ref[...].astype(I32)
    padded = ((cnt + (MOE_BLK - 1)) >> MOE_BLK_LOG2) << MOE_BLK_LOG2
    e_iota = lax.broadcasted_iota(I32, (N_EXPERTS, LANES), 0)
    p_end = jnp.zeros((N_EXPERTS, LANES), I32)
    for e in range(N_EXPERTS):
        tot = jnp.sum(jnp.where(e_iota <= e, padded, 0), axis=0, keepdims=True)
        p_end = jnp.where(e_iota == e, tot, p_end)
    p_start = p_end - padded
    tb = eid_ref.shape[1]
    ps = jnp.concatenate([p_start] * (tb // LANES), axis=1)
    e_wide = lax.broadcasted_iota(I32, (N_EXPERTS, tb), 0)
    for k in range(TOP_K):
        start = jnp.sum(jnp.where(eid_ref[k:k + 1, :] == e_wide, ps, 0), axis=0, keepdims=True)
        dest_ref[k:k + 1, :] = start + rank_ref[k:k + 1, :]
    nb = blk_ref.shape[1]
    pe = jnp.concatenate([p_end] * (nb // LANES), axis=1)
    first_row = lax.broadcasted_iota(I32, (N_EXPERTS, nb), 1) * MOE_BLK
    n_before = jnp.sum(jnp.where(pe <= first_row, 1, 0), axis=0, keepdims=True)
    blk = jnp.minimum(n_before, N_EXPERTS - 1)
    blk_ref[...] = jnp.broadcast_to(blk, blk_ref.shape)
    blk_ref[1:2, :] = jnp.broadcast_to(p_end[N_EXPERTS - 1:N_EXPERTS, 0:1] >> MOE_BLK_LOG2, (1, nb))
    on_diag = e_iota == lax.broadcasted_iota(I32, (N_EXPERTS, LANES), 1)
    blk_ref[2:3, 0:LANES] = jnp.sum(jnp.where(on_diag, p_end, 0), axis=0, keepdims=True)
    blk_ref[3:4, 0:LANES] = jnp.sum(jnp.where(on_diag, padded, 0), axis=0, keepdims=True)


def _dest_call(cnt, eid, rank, nb_pad):
    t = eid.shape[1]
    tb = ROUTE_TB
    col = lambda i: (0, i)
    return pl.pallas_call(
        _dest_body,
        grid=(t // tb,),
        in_specs=[pl.BlockSpec((N_EXPERTS, LANES), lambda i: (0, 0)),
                  pl.BlockSpec((TOP_K, tb), col), pl.BlockSpec((TOP_K, tb), col)],
        out_specs=[pl.BlockSpec((TOP_K, tb), col), pl.BlockSpec((SUBLANES, nb_pad), lambda i: (0, 0))],
        out_shape=[jax.ShapeDtypeStruct((TOP_K, t), I32), jax.ShapeDtypeStruct((SUBLANES, nb_pad), I32)],
        compiler_params=_cparams(("arbitrary",)),
        name="dest",
    )(cnt, eid, rank)


def _token_tile(ref, t):
    return ref.at[pl.ds(pl.multiple_of(t * TOKEN_TILE, TOKEN_TILE), TOKEN_TILE)]


def _dispatch_body(dest_ref, pend_ref, plen_ref, h_ref, xs_ref, zero_ref, sem, zsem):
    tm = h_ref.shape[0] // TOKEN_TILE
    blk_rows = MOE_BLK * TOKEN_TILE
    n_tok = pl.num_programs(0) * tm
    base = pl.program_id(0) * tm

    @pl.when(pl.program_id(0) == 0)
    def _():
        zero_ref[...] = jnp.zeros_like(zero_ref)

        def zero_block(start):
            rows = pl.ds(pl.multiple_of(start * TOKEN_TILE, blk_rows), blk_rows)
            return pltpu.make_async_copy(zero_ref, xs_ref.at[rows], zsem)

        def fill(e, c):
            @pl.when(plen_ref[e] > 0)
            def _():
                zero_block(pend_ref[e] - MOE_BLK).start()
            return c

        def drain(e, c):
            @pl.when(plen_ref[e] > 0)
            def _():
                zero_block(pend_ref[e] - MOE_BLK).wait()
            return c

        lax.fori_loop(0, N_EXPERTS, fill, 0)
        n_used = pend_ref[N_EXPERTS - 1] >> MOE_BLK_LOG2
        n_blk = xs_ref.shape[0] // blk_rows
        lax.fori_loop(n_used, n_blk, lambda j, c: (zero_block(j * MOE_BLK).start(), c)[1], 0)
        lax.fori_loop(0, N_EXPERTS, drain, 0)
        lax.fori_loop(n_used, n_blk, lambda j, c: (zero_block(j * MOE_BLK).wait(), c)[1], 0)

    def issue(r, c):
        for k in range(TOP_K):
            d = dest_ref[k * n_tok + base + r]
            pltpu.make_async_copy(_token_tile(h_ref, r), _token_tile(xs_ref, d), sem).start(priority=k)
        return c

    lax.fori_loop(0, tm, issue, 0, unroll=8)
    for k in range(TOP_K):
        pltpu.make_async_copy(h_ref, xs_ref.at[pl.ds(0, tm * TOKEN_TILE)], sem).wait()


def _dispatch_call(dest_flat, p_end, p_len, h2t, n_slots):
    rows = DISPATCH_TM * TOKEN_TILE
    return pl.pallas_call(
        _dispatch_body,
        grid_spec=pltpu.PrefetchScalarGridSpec(
            num_scalar_prefetch=3,
            grid=(h2t.shape[0] // rows,),
            in_specs=[pl.BlockSpec((rows, LANES), lambda i, d, pe, pn: (i, 0))],
            out_specs=pl.BlockSpec(memory_space=pl.ANY),
            scratch_shapes=[pltpu.VMEM((MOE_BLK * TOKEN_TILE, LANES), F32),
                            pltpu.SemaphoreType.DMA(()), pltpu.SemaphoreType.DMA(())]),
        out_shape=jax.ShapeDtypeStruct((n_slots * TOKEN_TILE, LANES), F32),
        compiler_params=_cparams(("arbitrary",)),
        name="dispatch",
    )(dest_flat, p_end, p_len, h2t)


EXPERT_BLKS_PER_STEP = 2


def _experts_body(blk_ref, used_ref, xs_ref, w1a_ref, w3a_ref, w2a_ref, w1b_ref, w3b_ref, w2b_ref, y_ref):
    del blk_ref
    i = pl.program_id(0)
    for half, (w1_ref, w3_ref, w2_ref) in enumerate(((w1a_ref, w3a_ref, w2a_ref), (w1b_ref, w3b_ref, w2b_ref))):
        blk = EXPERT_BLKS_PER_STEP * i + half
        row0 = half * MOE_BLK * TOKEN_TILE

        @pl.when(blk < used_ref[0])
        def _():
            x = _load_token_tiles(xs_ref, row0, MOE_BLK).astype(BF16)
            g = _dot(x, w1_ref[0].astype(BF16))
            u = _dot(x, w3_ref[0].astype(BF16))
            mid = (g * _sigmoid(g)) * u
            _store_token_tiles(y_ref, row0, _dot(mid.astype(BF16), w2_ref[0].astype(BF16)))

        @pl.when(blk >= used_ref[0])
        def _():
            y_ref[pl.ds(row0, MOE_BLK * TOKEN_TILE), :] = jnp.zeros((MOE_BLK * TOKEN_TILE, LANES), F32)


def _experts_call(blk_e, n_used, xs, w1, w3, w2):
    per = EXPERT_BLKS_PER_STEP
    rows = per * MOE_BLK * TOKEN_TILE
    nb = xs.shape[0] // (MOE_BLK * TOKEN_TILE)
    assert nb % per == 0 and blk_e.shape[0] == nb
    wmap = lambda half: (lambda i, blk, used: (blk[per * i + half], 0, 0))
    wspecs = lambda half: [pl.BlockSpec((1, D_MODEL, D_EXPERT), wmap(half)),
                           pl.BlockSpec((1, D_MODEL, D_EXPERT), wmap(half)),
                           pl.BlockSpec((1, D_EXPERT, D_MODEL), wmap(half))]
    return pl.pallas_call(
        _experts_body,
        grid_spec=pltpu.PrefetchScalarGridSpec(
            num_scalar_prefetch=2,
            grid=(nb // per,),
            in_specs=[pl.BlockSpec((rows, LANES), lambda i, blk, used: (i, 0))] + wspecs(0) + wspecs(1),
            out_specs=pl.BlockSpec((rows, LANES), lambda i, blk, used: (i, 0))),
        out_shape=jax.ShapeDtypeStruct(xs.shape, F32),
        compiler_params=_cparams(("arbitrary",)),
        name="experts",
    )(blk_e, n_used, xs, w1, w3, w2, w1, w3, w2)


def _combine_body(dest_ref, x1_ref, gate_ref, mod_ref, gf_ref, y_ref, o_ref, buf_ref, sem):
    tm = x1_ref.shape[0]
    step = pl.program_id(0)
    n_step = pl.num_programs(0)
    n_tok = n_step * tm
    slot = step % 2
    region = tm * TOKEN_TILE

    def region_row0(buf, k):
        return pl.multiple_of((buf * TOP_K + k) * region, region)

    def start_gather(for_step):
        def issue(r, c):
            for k in range(TOP_K):
                d = dest_ref[k * n_tok + for_step * tm + r]
                dst = buf_ref.at[pl.ds(pl.multiple_of(region_row0(for_step % 2, k) + r * TOKEN_TILE, TOKEN_TILE),
                                       TOKEN_TILE)]
                pltpu.make_async_copy(_token_tile(y_ref, d), dst, sem.at[for_step % 2]).start(priority=k)
            return c

        lax.fori_loop(0, tm, issue, 0, unroll=8)

    @pl.when(step == 0)
    def _():
        start_gather(step)

    @pl.when(step + 1 < n_step)
    def _():
        start_gather(step + 1)

    eye = jnp.where(lax.broadcasted_iota(I32, (tm, tm), 0) == lax.broadcasted_iota(I32, (tm, tm), 1),
                    1.0, 0.0).astype(BF16)
    g = gate_ref[...]
    g1 = g.astype(BF16)
    rem = g - g1.astype(F32)
    g2 = rem.astype(BF16)
    g3 = (rem - g2.astype(F32)).astype(BF16)
    gt = _dot_nt(eye, g1) + (_dot_nt(eye, g2) + _dot_nt(eye, g3))

    for k in range(TOP_K):
        pltpu.make_async_copy(y_ref.at[pl.ds(0, region)], buf_ref.at[pl.ds(region_row0(slot, k), region)],
                              sem.at[slot]).wait()

    ga2 = mod_ref[0, :, 5 * D_MODEL:6 * D_MODEL]
    moe = (gt[:, 0:1] * _load_token_tiles(buf_ref, region_row0(slot, 0), tm)
           + gt[:, 1:2] * _load_token_tiles(buf_ref, region_row0(slot, 1), tm))
    x2 = x1_ref[...] + ga2 * moe
    ms = jnp.mean(x2 * x2, axis=-1, keepdims=True)
    o_ref[...] = x2 * lax.rsqrt(ms + EPS) * gf_ref[...]


def _combine_call(dest_flat, x1, gate, mod3, g_final, y, seq):
    t = x1.shape[0]
    tm = COMBINE_TM
    per_b = seq // tm
    return pl.pallas_call(
        _combine_body,
        grid_spec=pltpu.PrefetchScalarGridSpec(
            num_scalar_prefetch=1,
            grid=(t // tm,),
            in_specs=[pl.BlockSpec((tm, D_MODEL), lambda i, d: (i, 0)),
                      pl.BlockSpec((SUBLANES, tm), lambda i, d: (0, i)),
                      pl.BlockSpec((1, 1, 6 * D_MODEL), lambda i, d: (i // per_b, 0, 0)),
                      pl.BlockSpec((1, D_MODEL), lambda i, d: (0, 0)),
                      pl.BlockSpec(memory_space=pl.ANY)],
            out_specs=pl.BlockSpec((tm, D_MODEL), lambda i, d: (i, 0)),
            scratch_shapes=[pltpu.VMEM((2 * TOP_K * tm * TOKEN_TILE, LANES), F32),
                            pltpu.SemaphoreType.DMA((2,))]),
        out_shape=jax.ShapeDtypeStruct((t, D_MODEL), F32),
        compiler_params=_cparams(("arbitrary",)),
        name="combine",
    )(dest_flat, x1, gate, mod3, g_final, y)


def _rope_tables(seq):
    half = NA_HEAD_DIM // 2
    nf = half // 2
    inv_freq = ROPE_THETA ** (-jnp.arange(nf, dtype=F32) / nf)
    t = jnp.arange(seq)
    row_pos = (t // GRID_W).astype(F32)
    col_pos = (t % GRID_W).astype(F32)
    ang_r = row_pos[:, None] * inv_freq
    ang_c = col_pos[:, None] * inv_freq
    cos = jnp.concatenate([jnp.cos(ang_r), jnp.cos(ang_r), jnp.cos(ang_c), jnp.cos(ang_c)], axis=-1)
    sin = jnp.concatenate([-jnp.sin(ang_r), jnp.sin(ang_r), -jnp.sin(ang_c), jnp.sin(ang_c)], axis=-1)
    return jnp.tile(cos, (1, NA_HEADS)), jnp.tile(sin, (1, NA_HEADS))


def _layer(x, c, ctx, c_ctx, w_mod, b_mod, g_mix, g_ffn, w_in, rpb, conv_w, conv_b, lru_wa, lru_ba,
           lru_wx, lru_bx, lru_lambda, w_up_attn, w_up_lru, w_out, wg, bg, we, be, w1, w3, w2, g_final):
    bsz, seq, d = x.shape
    n_ctx = ctx.shape[1]
    t = bsz * seq
    assert d == D_MODEL and seq % ATT_TQ == 0 and seq // ATT_TQ > K_ROW_BLOCKS
    assert bsz + 1 <= MOD_ROWS and seq % PROJ_TM == 0 and seq % MERGE_TM == 0
    assert t % ROUTE_TB == 0 and t % COMBINE_TM == 0 and t % DISPATCH_TM == 0
    assert (bsz * n_ctx) % PROJ_TM == 0 and n_ctx <= seq
    assert seq % (SUBLANES * SUBLANES) == 0 and n_ctx % (SUBLANES * SUBLANES) == 0
    assert (seq // SUBLANES) % SCAN_STEPS == 0 and (n_ctx // SUBLANES) % SCAN_STEPS == 0

    cc = jnp.concatenate([c, c_ctx[None, :], jnp.zeros((MOD_ROWS - bsz - 1, d), F32)], axis=0)
    mod = _mod_call(cc, w_mod, b_mod)
    mod3 = mod[:bsz].reshape(bsz, 1, 6 * d)
    mod_c = mod[bsz:bsz + 1]

    x2 = x.reshape(t, d)
    g_mix2 = g_mix.reshape(1, d)
    w_in_bf = w_in.astype(BF16)
    kc, vc, lxc = _ctxproj_call(ctx.reshape(bsz * n_ctx, d), mod_c, g_mix2, w_in_bf[:, :CTX_COLS])
    cos_t, sin_t = _rope_tables(seq)
    qre, qro, qpe, qpo, k, v, lx, glu, ga, gb = _inproj_call(x2, mod3, g_mix2, w_in_bf, cos_t, sin_t, seq)

    bias = _bias_tables(_rpbcol_call(rpb))
    o_att = _attn_call(qre, qro, qpe, qpo, k, v, kc, vc, bias, bsz, seq, n_ctx)

    hs = _lru_call(lx.reshape(bsz, seq, LRU_WIDTH), lxc.reshape(bsz, n_ctx, LRU_WIDTH),
                   conv_w, conv_b.reshape(1, LRU_WIDTH), lru_wa, lru_wx, lru_ba, lru_bx, lru_lambda)

    wr_t = jnp.concatenate([wg.T, we.T, jnp.zeros((ROUTE_ROWS - N_GROUPS - N_EXPERTS, d), F32)], axis=0)
    br = jnp.concatenate([bg, be, jnp.zeros((ROUTE_ROWS - N_GROUPS - N_EXPERTS,), F32)])
    br = jnp.broadcast_to(br[:, None], (ROUTE_ROWS, LANES))
    x1, h2, logits_t = _merge_call(x2, o_att, hs.reshape(t, LRU_WIDTH), glu, ga, gb, mod3,
                                   g_ffn.reshape(1, d), w_up_attn.astype(BF16), w_up_lru.astype(BF16),
                                   w_out.astype(BF16), wr_t, br, seq)

    eid, gate, rank, cnt = _route_call(logits_t)
    n_blk = -(-(t * TOP_K + N_EXPERTS * (MOE_BLK - 1)) // MOE_BLK)
    n_blk = -(-n_blk // EXPERT_BLKS_PER_STEP) * EXPERT_BLKS_PER_STEP
    nb_pad = -(-n_blk // LANES) * LANES
    dest, blk = _dest_call(cnt, eid, rank, nb_pad)
    dest_flat = dest.reshape(TOP_K * t)
    xs = _dispatch_call(dest_flat, blk[2, :N_EXPERTS], blk[3, :N_EXPERTS], h2, n_blk * MOE_BLK)
    y = _experts_call(blk[0, :n_blk], blk[1, :1], xs, w1, w3, w2)
    return _combine_call(dest_flat, x1, gate, mod3, g_final.reshape(1, d), y, seq).reshape(bsz, seq, d)


def kernel(x, c, ctx, c_ctx, w_mod, b_mod, g_mix, g_ffn, w_in, rpb, conv_w, conv_b, lru_wa, lru_ba, lru_wx,
           lru_bx, lru_lambda, w_up_attn, w_up_lru, w_out, router_group_w, router_group_b, router_expert_w,
           router_expert_b, expert_w_gate, expert_w_up, expert_w_down, g_final):
    assert w_mod.shape[0] == 1, "single-layer block"
    return _layer(x, c, ctx, c_ctx, w_mod[0], b_mod[0], g_mix[0], g_ffn[0], w_in[0], rpb[0], conv_w[0],
                  conv_b[0], lru_wa[0], lru_ba[0], lru_wx[0], lru_bx[0], lru_lambda[0], w_up_attn[0],
                  w_up_lru[0], w_out[0], router_group_w[0], router_group_b[0], router_expert_w[0],
                  router_expert_b[0], expert_w_gate[0], expert_w_up[0], expert_w_down[0], g_final)
```

```python
import functools

import numpy as np
import jax
import jax.numpy as jnp
from jax import lax
from jax.experimental import pallas as pl
from jax.experimental.pallas import tpu as pltpu

F32 = jnp.float32
BF16 = jnp.bfloat16
I32 = jnp.int32
U32 = jnp.uint32

D_MODEL = 1024
GRID_W = 64
EPS = 1e-6
NEG_INF = -1e30

NA_HEADS = 8
NA_HEAD_DIM = 64
NA_WIDTH = NA_HEADS * NA_HEAD_DIM
NA_WIN_ROWS = 8
NA_WIN_COLS = 16
ROPE_THETA = 10000.0

LRU_WIDTH = D_MODEL
LRU_BLOCKS = 8
LRU_BLOCK = LRU_WIDTH // LRU_BLOCKS
LRU_CONV = 4
LRU_C = 8.0

N_GROUPS = 4
EXPERTS_PER_GROUP = 8
N_EXPERTS = N_GROUPS * EXPERTS_PER_GROUP
TOP_K = 2
D_EXPERT = 512

K_OFF = 0
V_OFF = K_OFF + NA_WIDTH
LX_OFF = V_OFF + NA_WIDTH
CTX_COLS = LX_OFF + LRU_WIDTH
Q_OFF = CTX_COLS
LG_OFF = Q_OFF + NA_WIDTH
GA_OFF = LG_OFF + LRU_WIDTH
GB_OFF = GA_OFF + D_MODEL
PROJ_COLS = GB_OFF + D_MODEL

LANES = 128
SUBLANES = 8

Q_ROWS = 4
K_ROW_BLOCKS = 3
ATT_TQ = Q_ROWS * GRID_W
ATT_TK = K_ROW_BLOCKS * ATT_TQ

MOE_BLK_LOG2 = 8
MOE_BLK = 1 << MOE_BLK_LOG2
MOD_ROWS = 24
ROUTE_ROWS = 64

PROJ_TM = 512
MERGE_TM = 512
ROUTE_TB = 2048
DISPATCH_TM = 512
COMBINE_TM = 256

VMEM_LIMIT = 56 * 1024 * 1024


def _cparams(sem, vmem=VMEM_LIMIT):
    return pltpu.CompilerParams(dimension_semantics=sem, vmem_limit_bytes=vmem)


def _dot(a, b):
    return jnp.dot(a, b, preferred_element_type=F32)


def _dot_nt(a, b):
    return lax.dot_general(a, b, (((1,), (1,)), ((), ())), preferred_element_type=F32)


def _split2(a):
    hi = a.astype(BF16)
    lo = (a - hi.astype(F32)).astype(BF16)
    return hi, lo


def _dot3(a, b):
    ah, al = _split2(a)
    bh, bl = _split2(b)
    return _dot(ah, bh) + (_dot(ah, bl) + _dot(al, bh))


def _dot3_nt(a, b):
    ah, al = _split2(a)
    bh, bl = _split2(b)
    return _dot_nt(ah, bh) + (_dot_nt(ah, bl) + _dot_nt(al, bh))


def _sigmoid(x):
    return 1.0 / (1.0 + jnp.exp(-x))


def _ada_norm(x, g, sc, sh):
    ms = jnp.mean(x * x, axis=-1, keepdims=True)
    return (x * lax.rsqrt(ms + EPS) * g) * (1.0 + sc) + sh


def _mod_body(cc_ref, w_ref, b_ref, o_ref):
    cc = cc_ref[...]
    o_ref[...] = _dot3(cc * _sigmoid(cc), w_ref[...]) + b_ref[...]


def _mod_call(cc, w_mod, b_mod):
    n = w_mod.shape[1]
    bn = 1024
    return pl.pallas_call(
        _mod_body,
        grid=(n // bn,),
        in_specs=[pl.BlockSpec((MOD_ROWS, D_MODEL), lambda j: (0, 0)),
                  pl.BlockSpec((D_MODEL, bn), lambda j: (0, j)),
                  pl.BlockSpec((1, bn), lambda j: (0, j))],
        out_specs=pl.BlockSpec((MOD_ROWS, bn), lambda j: (0, j)),
        out_shape=jax.ShapeDtypeStruct((MOD_ROWS, n), F32),
        compiler_params=_cparams(("arbitrary",)),
        name="mod",
    )(cc, w_mod, b_mod.reshape(1, n))


def _rope(t, cos, sin):
    lane = lax.broadcasted_iota(I32, (t.shape[0], LANES), 1)
    first = (lane & 16) == 0
    parts = []
    for c in range(t.shape[1] // LANES):
        tc = t[:, c * LANES:(c + 1) * LANES]
        parts.append(jnp.where(first, pltpu.roll(tc, LANES - 16, 1), pltpu.roll(tc, 16, 1)))
    partner = jnp.concatenate(parts, axis=1)
    return t * cos + partner * sin


def _gelu_tanh(x):
    return 0.5 * x * (1.0 + jnp.tanh(0.7978845608028654 * (x + 0.044715 * (x * x * x))))


def _inproj_body(x_ref, mod_ref, g_ref, w_ref, cos_ref, sin_ref,
                 qre_ref, qro_ref, qpe_ref, qpo_ref, k_ref, v_ref, lx_ref, glu_ref, ga_ref, gb_ref):
    sh = mod_ref[0, :, 0:D_MODEL]
    sc = mod_ref[0, :, D_MODEL:2 * D_MODEL]
    h = _ada_norm(x_ref[...], g_ref[...], sc, sh).astype(BF16)
    cos = cos_ref[...]
    sin = sin_ref[...]
    scale = NA_HEAD_DIM ** -0.5

    k_ref[...] = _rope(_dot(h, w_ref[:, K_OFF:K_OFF + NA_WIDTH]), cos, sin).astype(BF16)
    v_ref[...] = _dot(h, w_ref[:, V_OFF:V_OFF + NA_WIDTH]).astype(BF16)
    lx_ref[...] = _dot(h, w_ref[:, LX_OFF:LX_OFF + LRU_WIDTH])

    q = _dot(h, w_ref[:, Q_OFF:Q_OFF + NA_WIDTH]) * scale
    qr = _rope(q, cos, sin)
    lane = lax.broadcasted_iota(I32, q.shape, 1)
    even = (lane & NA_HEAD_DIM) == 0
    qre_ref[...] = jnp.where(even, qr, 0.0).astype(BF16)
    qro_ref[...] = jnp.where(even, 0.0, qr).astype(BF16)
    qpe_ref[...] = jnp.where(even, q, 0.0).astype(BF16)
    qpo_ref[...] = jnp.where(even, 0.0, q).astype(BF16)

    glu_ref[...] = _gelu_tanh(_dot(h, w_ref[:, LG_OFF:LG_OFF + LRU_WIDTH])).astype(BF16)
    ga_ref[...] = _sigmoid(_dot(h, w_ref[:, GA_OFF:GA_OFF + D_MODEL])).astype(BF16)
    gb_ref[...] = _sigmoid(_dot(h, w_ref[:, GB_OFF:GB_OFF + D_MODEL])).astype(BF16)


def _inproj_call(x2, mod3, g_mix, w_in_bf, cos_t, sin_t, seq):
    t = x2.shape[0]
    tm = PROJ_TM
    per_b = seq // tm
    row = lambda i: (i, 0)
    wide = lambda n, dt: jax.ShapeDtypeStruct((t, n), dt)
    return pl.pallas_call(
        _inproj_body,
        grid=(t // tm,),
        in_specs=[pl.BlockSpec((tm, D_MODEL), row),
                  pl.BlockSpec((1, 1, 6 * D_MODEL), lambda i: (i // per_b, 0, 0)),
                  pl.BlockSpec((1, D_MODEL), lambda i: (0, 0)),
                  pl.BlockSpec((D_MODEL, PROJ_COLS), lambda i: (0, 0), pipeline_mode=pl.Buffered(1)),
                  pl.BlockSpec((tm, NA_WIDTH), lambda i: (i % per_b, 0)),
                  pl.BlockSpec((tm, NA_WIDTH), lambda i: (i % per_b, 0))],
        out_specs=[pl.BlockSpec((tm, NA_WIDTH), row)] * 6
                  + [pl.BlockSpec((tm, LRU_WIDTH), row)] * 4,
        out_shape=[wide(NA_WIDTH, BF16)] * 6
                  + [wide(LRU_WIDTH, F32), wide(LRU_WIDTH, BF16), wide(D_MODEL, BF16), wide(D_MODEL, BF16)],
        compiler_params=_cparams(("parallel",)),
        name="inproj",
    )(x2, mod3, g_mix, w_in_bf, cos_t, sin_t)


def _ctxproj_body(x_ref, mod_ref, g_ref, w_ref, k_ref, v_ref, lx_ref):
    sh = mod_ref[:, 0:D_MODEL]
    sc = mod_ref[:, D_MODEL:2 * D_MODEL]
    h = _ada_norm(x_ref[...], g_ref[...], sc, sh).astype(BF16)
    k_ref[...] = _dot(h, w_ref[:, K_OFF:K_OFF + NA_WIDTH]).astype(BF16)
    v_ref[...] = _dot(h, w_ref[:, V_OFF:V_OFF + NA_WIDTH]).astype(BF16)
    lx_ref[...] = _dot(h, w_ref[:, LX_OFF:LX_OFF + LRU_WIDTH])


def _ctxproj_call(c2, mod_c, g_mix, w_ctx_bf):
    t = c2.shape[0]
    tm = PROJ_TM
    row = lambda i: (i, 0)
    return pl.pallas_call(
        _ctxproj_body,
        grid=(t // tm,),
        in_specs=[pl.BlockSpec((tm, D_MODEL), row),
                  pl.BlockSpec((1, 6 * D_MODEL), lambda i: (0, 0)),
                  pl.BlockSpec((1, D_MODEL), lambda i: (0, 0)),
                  pl.BlockSpec((D_MODEL, CTX_COLS), lambda i: (0, 0))],
        out_specs=[pl.BlockSpec((tm, NA_WIDTH), row), pl.BlockSpec((tm, NA_WIDTH), row),
                   pl.BlockSpec((tm, LRU_WIDTH), row)],
        out_shape=[jax.ShapeDtypeStruct((t, NA_WIDTH), BF16), jax.ShapeDtypeStruct((t, NA_WIDTH), BF16),
                   jax.ShapeDtypeStruct((t, LRU_WIDTH), F32)],
        compiler_params=_cparams(("parallel",)),
        name="ctxproj",
    )(c2, mod_c, g_mix, w_ctx_bf)


N_DR = 2 * NA_WIN_ROWS - 1
N_DC = 2 * NA_WIN_COLS - 1


def _rpbcol_body(rpb_ref, o_ref):
    n = GRID_W * GRID_W
    flat = lax.broadcasted_iota(I32, (32, n), 1)
    qc = flat >> 6
    kc = flat & (GRID_W - 1)
    dc = jnp.clip(kc - qc, 1 - NA_WIN_COLS, NA_WIN_COLS - 1) + (NA_WIN_COLS - 1)
    d_iota = lax.broadcasted_iota(I32, (32, n), 0)
    onehot = jnp.where(dc == d_iota, 1.0, 0.0).astype(BF16)
    r = rpb_ref[...]
    r1 = r.astype(BF16)
    rem = r - r1.astype(F32)
    r2 = rem.astype(BF16)
    r3 = (rem - r2.astype(F32)).astype(BF16)
    val = _dot(r1, onehot) + (_dot(r2, onehot) + _dot(r3, onehot))
    qc1 = qc[0:1, :]
    kc1 = kc[0:1, :]
    c_start = jnp.clip(qc1 - NA_WIN_COLS // 2, 0, GRID_W - NA_WIN_COLS)
    band = (kc1 >= c_start) & (kc1 < c_start + NA_WIN_COLS)
    o_ref[...] = jnp.where(band, val, NEG_INF)


def _rpbcol_call(rpb):
    rows = NA_HEADS * N_DR
    r2 = jnp.pad(rpb.reshape(rows, N_DC), ((0, 0), (0, 32 - N_DC)))
    n = GRID_W * GRID_W
    return pl.pallas_call(
        _rpbcol_body,
        in_specs=[pl.BlockSpec((rows, 32), lambda: (0, 0))],
        out_specs=pl.BlockSpec((rows, n), lambda: (0, 0)),
        out_shape=jax.ShapeDtypeStruct((rows, n), F32),
        name="rpbcol",
    )(r2)


def _bias_tables(rpbcol):
    t = rpbcol.reshape(NA_HEADS, N_DR, GRID_W, GRID_W)
    neg = jnp.full((NA_HEADS, GRID_W, GRID_W), NEG_INF, F32)
    n_kj = K_ROW_BLOCKS * Q_ROWS
    classes = []
    for lo_fn, dr_off in ((lambda ri: 0, 7), (lambda ri: ri, 3), (lambda ri: 4, -1)):
        rows = []
        for ri in range(Q_ROWS):
            lo = lo_fn(ri)
            blocks = []
            for kj in range(n_kj):
                inside = lo <= kj < lo + NA_WIN_ROWS
                blocks.append(t[:, kj - ri + dr_off] if inside else neg)
            rows.append(jnp.concatenate(blocks, axis=2))
        classes.append(jnp.concatenate(rows, axis=1))
    return jnp.stack(classes, axis=0)


def _attn_body(qre_ref, qro_ref, qpe_ref, qpo_ref, k0_ref, k1_ref, k2_ref, v0_ref, v1_ref, v2_ref,
               kc_ref, vc_ref, bias_ref, o_ref):
    lane = lax.broadcasted_iota(I32, (ATT_TQ, LANES), 1)
    k_refs = (k0_ref, k1_ref, k2_ref)
    v_refs = (v0_ref, v1_ref, v2_ref)
    for p in range(NA_HEADS // 2):
        sl = slice(p * LANES, (p + 1) * LANES)
        k_lat = jnp.concatenate([r[:, sl] for r in k_refs], axis=0)
        kc = kc_ref[:, sl]
        v_all = jnp.concatenate([r[:, sl] for r in v_refs] + [vc_ref[:, sl]], axis=0)
        outs = []
        for hh, (qr_ref, qp_ref) in enumerate(((qre_ref, qpe_ref), (qro_ref, qpo_ref))):
            h = 2 * p + hh
            s_lat = _dot_nt(qr_ref[:, sl], k_lat) + bias_ref[0, h]
            s_ctx = _dot_nt(qp_ref[:, sl], kc)
            tiles = ([s_lat[:, j * ATT_TQ:(j + 1) * ATT_TQ] for j in range(K_ROW_BLOCKS)]
                     + [s_ctx[:, j * ATT_TQ:(j + 1) * ATT_TQ] for j in range(s_ctx.shape[1] // ATT_TQ)])
            m = functools.reduce(jnp.maximum, tiles).max(axis=-1, keepdims=True)
            e = [jnp.exp(t - m) for t in tiles]
            den = functools.reduce(jnp.add, e).sum(axis=-1, keepdims=True)
            acc = _dot(jnp.concatenate(e, axis=1).astype(BF16), v_all)
            outs.append(acc / den)
        o_ref[:, sl] = jnp.where(lane < NA_HEAD_DIM, outs[0], outs[1]).astype(BF16)


def _attn_call(qre, qro, qpe, qpo, k, v, kc, vc, bias, bsz, seq, n_ctx):
    t = bsz * seq
    n_grp = seq // ATT_TQ
    per_b = n_grp
    max_kb = n_grp - K_ROW_BLOCKS

    def qmap(g, b):
        return (b * per_b + g, 0)

    def kmap(j):
        return lambda g, b: (b * per_b + jnp.clip(g - 1, 0, max_kb) + j, 0)

    def cls(g, b):
        return (jnp.where(g == 0, 0, jnp.where(g == n_grp - 1, 2, 1)), 0, 0, 0)

    qspec = pl.BlockSpec((ATT_TQ, NA_WIDTH), qmap)
    cspec = pl.BlockSpec((n_ctx, NA_WIDTH), lambda g, b: (b, 0))
    return pl.pallas_call(
        _attn_body,
        grid=(n_grp, bsz),
        in_specs=[qspec] * 4
                 + [pl.BlockSpec((ATT_TQ, NA_WIDTH), kmap(j)) for j in range(K_ROW_BLOCKS)] * 2
                 + [cspec, cspec, pl.BlockSpec((1, NA_HEADS, ATT_TQ, ATT_TK), cls)],
        out_specs=pl.BlockSpec((ATT_TQ, NA_WIDTH), qmap),
        out_shape=jax.ShapeDtypeStruct((t, NA_WIDTH), BF16),
        compiler_params=_cparams(("arbitrary", "arbitrary")),
        name="attn",
    )(qre, qro, qpe, qpo, k, k, k, v, v, v, kc, vc, bias)


def _shift_down(v, row):
    return jnp.where(row >= 1, pltpu.roll(v, 1, 0), 0.0)


def _shift_up(v, row):
    return jnp.where(row < SUBLANES - 1, pltpu.roll(v, SUBLANES - 1, 0), 0.0)


def _conv4(x, w_ref, b_ref):
    n = x.shape[0]
    s = SUBLANES
    row = lax.broadcasted_iota(I32, (s, LANES), 0)
    last = _shift_down(x[n - s:n], row)
    last2 = _shift_down(x[n - 2 * s:n - s], row)
    first = _shift_up(x[0:s], row)
    xm1 = jnp.concatenate([last, x[0:n - s]], axis=0)
    xm2 = jnp.concatenate([last2, last, x[0:n - 2 * s]], axis=0)
    xp1 = jnp.concatenate([x[s:n], first], axis=0)
    return (w_ref[0:1, :] * xm2 + w_ref[1:2, :] * xm1 + w_ref[2:3, :] * x + w_ref[3:4, :] * xp1) + b_ref[...]


def _softplus(z):
    return jnp.maximum(z, 0.0) + jnp.log1p(jnp.exp(-jnp.abs(z)))


def _gates(xc, d, wa_ref, wx_ref, ba_ref, bx_ref, lam_ref, a_ref, u_ref):
    n = xc.shape[0]
    xb = xc.astype(BF16)
    tr = jnp.tanh(_dot(xb, (0.5 * wa_ref[d, 0]).astype(BF16)) + 0.5 * ba_ref[d:d + 1, :])
    ti = jnp.tanh(_dot(xb, (0.5 * wx_ref[d, 0]).astype(BF16)) + 0.5 * bx_ref[d:d + 1, :])
    half_c = (0.5 * LRU_C) * _softplus(-lam_ref[d:d + 1, :])
    neg_log_a = half_c * tr + half_c
    a = jnp.exp(-neg_log_a)
    a_ref[0:n, :] = a
    s2 = jnp.tanh(neg_log_a) * (a * a + 1.0)
    root = jnp.where(s2 > 0.0, s2 * lax.rsqrt(s2), 0.0)
    xh = 0.5 * xc
    u_ref[0:n, :] = root * (xh * ti + xh)


def _scan4(a, u, h, p):
    a01 = a[1] * a[0]
    u01 = a[1] * u[0] + u[1]
    a23 = a[3] * a[2]
    u23 = a[3] * u[2] + u[3]
    a012 = a[2] * a01
    u012 = a[2] * u01 + u[2]
    a0123 = a23 * a01
    u0123 = a23 * u01 + u23
    hs = [a[0] * h + u[0], a01 * h + u01, a012 * h + u012, a0123 * h + u0123]
    ps = [a[0] * p, a01 * p, a012 * p, a0123 * p]
    return hs, ps


SCAN_STEPS = 4


def _scan_local(af_ref, uf_ref, ab_ref, ub_ref, n_vreg):
    s = SUBLANES
    zero = jnp.zeros((s, LANES), F32)
    one = jnp.ones((s, LANES), F32)
    span = SCAN_STEPS * s

    def body(q, carry):
        hf, pf, hb, pb = carry
        base = pl.multiple_of(q * span, span)
        rows = [pl.ds(base + i * s, s) for i in range(SCAN_STEPS)]
        hs, ps = _scan4([af_ref[r, :] for r in rows], [uf_ref[r, :] for r in rows], hf, pf)
        for r, h, p in zip(rows, hs, ps):
            uf_ref[r, :] = h
            af_ref[r, :] = p
        hf, pf = hs[-1], ps[-1]
        base = pl.multiple_of((n_vreg - SCAN_STEPS) * s - q * span, span)
        rows = [pl.ds(base + (SCAN_STEPS - 1 - i) * s, s) for i in range(SCAN_STEPS)]
        hs, ps = _scan4([ab_ref[r, :] for r in rows], [ub_ref[r, :] for r in rows], hb, pb)
        for r, h, p in zip(rows, hs, ps):
            ub_ref[r, :] = h
            ab_ref[r, :] = p
        return hf, pf, hs[-1], ps[-1]

    return lax.fori_loop(0, n_vreg // SCAN_STEPS, body, (zero, one, zero, one), unroll=2)


def _link_states(hf, pf, hb, pb, h0f, h0b):
    s = SUBLANES
    row = lax.broadcasted_iota(I32, (s, LANES), 0)
    a, u = pf, hf
    for k in (1, 2, 4):
        keep = row >= k
        u = u + a * jnp.where(keep, pltpu.roll(u, k, 0), 0.0)
        a = a * jnp.where(keep, pltpu.roll(a, k, 0), 1.0)
    end_f = u + a * h0f
    in_f = jnp.where(row >= 1, pltpu.roll(end_f, 1, 0), h0f)
    a, u = pb, hb
    for k in (1, 2, 4):
        keep = row < s - k
        u = u + a * jnp.where(keep, pltpu.roll(u, s - k, 0), 0.0)
        a = a * jnp.where(keep, pltpu.roll(a, s - k, 0), 1.0)
    end_b = u + a * h0b
    in_b = jnp.where(row < s - 1, pltpu.roll(end_b, s - 1, 0), h0b)
    return in_f, in_b, end_f[s - 1:s, :], end_b[0:1, :]


PITCH_PAD = 4


def _to_split(x_ref, pad_ref, dst_ref, n):
    s = SUBLANES
    n_j = n // s
    pitch = n_j + PITCH_PAD
    for q in range(s):
        pad_ref[pl.ds(q * pitch, n_j), :] = x_ref[0, pl.ds(q * n_j, n_j), :]

    def body(j, c):
        dst_ref[pl.ds(pl.multiple_of(j * s, s), s), :] = pad_ref[pl.ds(j, s, stride=pitch), :]
        return c

    lax.fori_loop(0, n_j, body, 0, unroll=8)


def _lru_body(lx_ref, lxc_ref, cw_ref, cb_ref, wa_ref, wx_ref, ba_ref, bx_ref, lam_ref,
              o_ref, af_ref, uf_ref, ab_ref, ub_ref, pad_ref):
    n = lx_ref.shape[1]
    n_c = lxc_ref.shape[1]
    s = SUBLANES
    n_j = n // s
    pitch = n_j + PITCH_PAD
    gate_args = (wa_ref, wx_ref, ba_ref, bx_ref, lam_ref)
    zero = jnp.zeros((1, LANES), F32)

    _to_split(lxc_ref, pad_ref, uf_ref, n_c)
    xc = _conv4(uf_ref[0:n_c, :], cw_ref, cb_ref)
    _gates(xc, 0, *gate_args, af_ref, uf_ref)
    _gates(xc, 1, *gate_args, ab_ref, ub_ref)
    ends = _scan_local(af_ref, uf_ref, ab_ref, ub_ref, n_c // s)
    _, _, cf, cb = _link_states(*ends, zero, zero)

    _to_split(lx_ref, pad_ref, uf_ref, n)
    xl = _conv4(uf_ref[...], cw_ref, cb_ref)
    _gates(xl, 0, *gate_args, af_ref, uf_ref)
    _gates(xl, 1, *gate_args, ab_ref, ub_ref)
    ends = _scan_local(af_ref, uf_ref, ab_ref, ub_ref, n_j)
    in_f, in_b, _, _ = _link_states(*ends, cf, cb)

    def finish(j, c):
        rows = pl.ds(pl.multiple_of(j * s, s), s)
        h = (uf_ref[rows, :] + af_ref[rows, :] * in_f) + (ub_ref[rows, :] + ab_ref[rows, :] * in_b)
        pad_ref[pl.ds(j, s, stride=pitch), :] = h
        return c

    lax.fori_loop(0, n_j, finish, 0, unroll=8)
    for q in range(s):
        o_ref[0, pl.ds(q * n_j, n_j), :] = pad_ref[pl.ds(q * pitch, n_j), :].astype(o_ref.dtype)


def _lru_call(lx3, lxc3, conv_w, conv_b, wa, wx, ba, bx, lam):
    bsz, seq, _ = lx3.shape
    n_ctx = lxc3.shape[1]
    col = lambda b, n: (b, 0, n)
    par = lambda b, n: (0, n)
    wspec = pl.BlockSpec((2, 1, LRU_BLOCK, LRU_BLOCK), lambda b, n: (0, n, 0, 0))
    return pl.pallas_call(
        _lru_body,
        grid=(bsz, LRU_BLOCKS),
        in_specs=[pl.BlockSpec((1, seq, LRU_BLOCK), col),
                  pl.BlockSpec((1, n_ctx, LRU_BLOCK), col),
                  pl.BlockSpec((LRU_CONV, LRU_BLOCK), par),
                  pl.BlockSpec((1, LRU_BLOCK), par),
                  wspec, wspec,
                  pl.BlockSpec((2, LRU_BLOCK), par),
                  pl.BlockSpec((2, LRU_BLOCK), par),
                  pl.BlockSpec((2, LRU_BLOCK), par)],
        out_specs=pl.BlockSpec((1, seq, LRU_BLOCK), col),
        out_shape=jax.ShapeDtypeStruct((bsz, seq, LRU_WIDTH), BF16),
        scratch_shapes=[pltpu.VMEM((seq, LRU_BLOCK), F32)] * 4
                       + [pltpu.VMEM((seq + SUBLANES * PITCH_PAD, LRU_BLOCK), F32)],
        compiler_params=_cparams(("parallel", "arbitrary")),
        name="lru",
    )(lx3, lxc3, conv_w, conv_b, wa, wx, ba, bx, lam)


TOKEN_TILE = D_MODEL // LANES


def _store_token_tiles(ref, row0, x):
    m = x.shape[0]
    for c in range(TOKEN_TILE):
        ref[pl.ds(row0 + c, m, stride=TOKEN_TILE), :] = x[:, c * LANES:(c + 1) * LANES]


def _load_token_tiles(ref, row0, m):
    return jnp.concatenate([ref[pl.ds(row0 + c, m, stride=TOKEN_TILE), :] for c in range(TOKEN_TILE)], axis=1)


def _merge_body(x_ref, oa_ref, hs_ref, glu_ref, ga_ref, gb_ref, mod_ref, g_ref, wua_ref, wul_ref, wo_ref,
                wr_ref, br_ref, x1_ref, h2_ref, lt_ref):
    ga1 = mod_ref[0, :, 2 * D_MODEL:3 * D_MODEL]
    sh2 = mod_ref[0, :, 3 * D_MODEL:4 * D_MODEL]
    sc2 = mod_ref[0, :, 4 * D_MODEL:5 * D_MODEL]
    o_lru = hs_ref[...] * glu_ref[...]
    y = (ga_ref[...].astype(F32) * _dot(oa_ref[...], wua_ref[...])
         + gb_ref[...].astype(F32) * _dot(o_lru, wul_ref[...]))
    x1 = x_ref[...] + ga1 * _dot(y.astype(BF16), wo_ref[...])
    x1_ref[...] = x1
    h2 = _ada_norm(x1, g_ref[...], sc2, sh2)
    lt_ref[...] = _dot3_nt(wr_ref[...], h2) + br_ref[:, 0:1]
    _store_token_tiles(h2_ref, 0, h2)


def _merge_call(x2, o_att, hs, glu, ga, gb, mod3, g_ffn, wua, wul, wo, wr_t, br, seq):
    t = x2.shape[0]
    tm = MERGE_TM
    per_b = seq // tm
    row = lambda i: (i, 0)
    full = lambda i: (0, 0)
    resident = lambda shape: pl.BlockSpec(shape, full, pipeline_mode=pl.Buffered(1))
    return pl.pallas_call(
        _merge_body,
        grid=(t // tm,),
        in_specs=[pl.BlockSpec((tm, D_MODEL), row),
                  pl.BlockSpec((tm, NA_WIDTH), row),
                  pl.BlockSpec((tm, LRU_WIDTH), row),
                  pl.BlockSpec((tm, LRU_WIDTH), row),
                  pl.BlockSpec((tm, D_MODEL), row),
                  pl.BlockSpec((tm, D_MODEL), row),
                  pl.BlockSpec((1, 1, 6 * D_MODEL), lambda i: (i // per_b, 0, 0)),
                  pl.BlockSpec((1, D_MODEL), full),
                  resident((NA_WIDTH, D_MODEL)),
                  resident((LRU_WIDTH, D_MODEL)),
                  resident((D_MODEL, D_MODEL)),
                  pl.BlockSpec((ROUTE_ROWS, D_MODEL), full),
                  pl.BlockSpec((ROUTE_ROWS, LANES), full)],
        out_specs=[pl.BlockSpec((tm, D_MODEL), row),
                   pl.BlockSpec((tm * TOKEN_TILE, LANES), row),
                   pl.BlockSpec((ROUTE_ROWS, tm), lambda i: (0, i))],
        out_shape=[jax.ShapeDtypeStruct((t, D_MODEL), F32),
                   jax.ShapeDtypeStruct((t * TOKEN_TILE, LANES), F32),
                   jax.ShapeDtypeStruct((ROUTE_ROWS, t), F32)],
        compiler_params=_cparams(("parallel",)),
        name="merge",
    )(x2, o_att, hs, glu, ga, gb, mod3, g_ffn, wua, wul, wo, wr_t, br)


def _route_body(lt_ref, eid_ref, gate_ref, rank_ref, cnt_ref, carry_ref):
    step = pl.program_id(0)

    @pl.when(step == 0)
    def _():
        carry_ref[...] = jnp.zeros_like(carry_ref)

    tb = lt_ref.shape[1]
    lg = [lt_ref[r:r + 1, :] for r in range(N_GROUPS)]
    best = lg[0]
    gidx = jnp.zeros((1, tb), I32)
    for r in range(1, N_GROUPS):
        better = lg[r] > best
        gidx = jnp.where(better, r, gidx)
        best = jnp.maximum(best, lg[r])
    den = jnp.exp(lg[0] - best)
    for r in range(1, N_GROUPS):
        den = den + jnp.exp(lg[r] - best)
    p_top = 1.0 / den

    ev = []
    for j in range(EXPERTS_PER_GROUP):
        sel = lt_ref[N_GROUPS + j:N_GROUPS + j + 1, :]
        for g in range(1, N_GROUPS):
            row = N_GROUPS + g * EXPERTS_PER_GROUP + j
            sel = jnp.where(gidx == g, lt_ref[row:row + 1, :], sel)
        ev.append(sel)
    v0 = ev[0]
    i0 = jnp.zeros((1, tb), I32)
    for j in range(1, EXPERTS_PER_GROUP):
        better = ev[j] > v0
        i0 = jnp.where(better, j, i0)
        v0 = jnp.maximum(v0, ev[j])
    v1 = jnp.full((1, tb), -jnp.inf, F32)
    i1 = jnp.zeros((1, tb), I32)
    for j in range(EXPERTS_PER_GROUP):
        better = (ev[j] > v1) & (i0 != j)
        i1 = jnp.where(better, j, i1)
        v1 = jnp.where(better, ev[j], v1)
    e1 = jnp.exp(v1 - v0)
    inv = 1.0 / (1.0 + e1)
    eid0 = gidx * EXPERTS_PER_GROUP + i0
    eid1 = gidx * EXPERTS_PER_GROUP + i1
    eid_ref[0:1, :] = eid0
    eid_ref[1:2, :] = eid1
    gate_ref[...] = jnp.zeros_like(gate_ref)
    gate_ref[0:1, :] = p_top * inv
    gate_ref[1:2, :] = p_top * (e1 * inv)

    sub = 256
    e_iota = lax.broadcasted_iota(I32, (N_EXPERTS, sub), 0)
    tri = jnp.where(lax.broadcasted_iota(I32, (sub, sub), 0) <= lax.broadcasted_iota(I32, (sub, sub), 1),
                    1.0, 0.0).astype(BF16)
    carry = carry_ref[...]
    for c in range(tb // sub):
        sl = slice(c * sub, (c + 1) * sub)
        m0 = eid0[:, sl] == e_iota
        m1 = eid1[:, sl] == e_iota
        oh = jnp.where(m0 | m1, 1.0, 0.0)
        incl = _dot(oh.astype(BF16), tri)
        excl = incl - oh + carry[:, 0:1]
        rank_ref[0:1, sl] = jnp.sum(jnp.where(m0, excl, 0.0), axis=0, keepdims=True).astype(I32)
        rank_ref[1:2, sl] = jnp.sum(jnp.where(m1, excl, 0.0), axis=0, keepdims=True).astype(I32)
        carry = carry + incl[:, sub - 1:sub]
    carry_ref[...] = carry
    cnt_ref[...] = carry


def _route_call(logits_t):
    t = logits_t.shape[1]
    tb = ROUTE_TB
    col = lambda i: (0, i)
    return pl.pallas_call(
        _route_body,
        grid=(t // tb,),
        in_specs=[pl.BlockSpec((ROUTE_ROWS, tb), col)],
        out_specs=[pl.BlockSpec((TOP_K, tb), col), pl.BlockSpec((SUBLANES, tb), col),
                   pl.BlockSpec((TOP_K, tb), col), pl.BlockSpec((N_EXPERTS, LANES), lambda i: (0, 0))],
        out_shape=[jax.ShapeDtypeStruct((TOP_K, t), I32), jax.ShapeDtypeStruct((SUBLANES, t), F32),
                   jax.ShapeDtypeStruct((TOP_K, t), I32), jax.ShapeDtypeStruct((N_EXPERTS, LANES), F32)],
        scratch_shapes=[pltpu.VMEM((N_EXPERTS, LANES), F32)],
        compiler_params=_cparams(("arbitrary",)),
        name="route",
    )(logits_t)


def _dest_body(cnt_ref, eid_ref, rank_ref, dest_ref, blk_ref):
    cnt = cnt_ref[...].astype(I32)
    padded = ((cnt + (MOE_BLK - 1)) >> MOE_BLK_LOG2) << MOE_BLK_LOG2
    e_iota = lax.broadcasted_iota(I32, (N_EXPERTS, LANES), 0)
    p_end = jnp.zeros((N_EXPERTS, LANES), I32)
    for e in range(N_EXPERTS):
        tot = jnp.sum(jnp.where(e_iota <= e, padded, 0), axis=0, keepdims=True)
        p_end = jnp.where(e_iota == e, tot, p_end)
    p_start = p_end - padded
    tb = eid_ref.shape[1]
    ps = jnp.concatenate([p_start] * (tb // LANES), axis=1)
    e_wide = lax.broadcasted_iota(I32, (N_EXPERTS, tb), 0)
    for k in range(TOP_K):
        start = jnp.sum(jnp.where(eid_ref[k:k + 1, :] == e_wide, ps, 0), axis=0, keepdims=True)
        dest_ref[k:k + 1, :] = start + rank_ref[k:k + 1, :]
    nb = blk_ref.shape[1]
    pe = jnp.concatenate([p_end] * (nb // LANES), axis=1)
    first_row = lax.broadcasted_iota(I32, (N_EXPERTS, nb), 1) * MOE_BLK
    n_before = jnp.sum(jnp.where(pe <= first_row, 1, 0), axis=0, keepdims=True)
    blk = jnp.minimum(n_before, N_EXPERTS - 1)
    blk_ref[...] = jnp.broadcast_to(blk, blk_ref.shape)
    blk_ref[1:2, :] = jnp.broadcast_to(p_end[N_EXPERTS - 1:N_EXPERTS, 0:1] >> MOE_BLK_LOG2, (1, nb))
    on_diag = e_iota == lax.broadcasted_iota(I32, (N_EXPERTS, LANES), 1)
    blk_ref[2:3, 0:LANES] = jnp.sum(jnp.where(on_diag, p_end, 0), axis=0, keepdims=True)
    blk_ref[3:4, 0:LANES] = jnp.sum(jnp.where(on_diag, padded, 0), axis=0, keepdims=True)


def _dest_call(cnt, eid, rank, nb_pad):
    t = eid.shape[1]
    tb = ROUTE_TB
    col = lambda i: (0, i)
    return pl.pallas_call(
        _dest_body,
        grid=(t // tb,),
        in_specs=[pl.BlockSpec((N_EXPERTS, LANES), lambda i: (0, 0)),
                  pl.BlockSpec((TOP_K, tb), col), pl.BlockSpec((TOP_K, tb), col)],
        out_specs=[pl.BlockSpec((TOP_K, tb), col), pl.BlockSpec((SUBLANES, nb_pad), lambda i: (0, 0))],
        out_shape=[jax.ShapeDtypeStruct((TOP_K, t), I32), jax.ShapeDtypeStruct((SUBLANES, nb_pad), I32)],
        compiler_params=_cparams(("arbitrary",)),
        name="dest",
    )(cnt, eid, rank)


def _token_tile(ref, t):
    return ref.at[pl.ds(pl.multiple_of(t * TOKEN_TILE, TOKEN_TILE), TOKEN_TILE)]


def _dispatch_body(dest_ref, pend_ref, plen_ref, h_ref, xs_ref, zero_ref, sem, zsem):
    tm = h_ref.shape[0] // TOKEN_TILE
    blk_rows = MOE_BLK * TOKEN_TILE
    n_tok = pl.num_programs(0) * tm
    base = pl.program_id(0) * tm

    @pl.when(pl.program_id(0) == 0)
    def _():
        zero_ref[...] = jnp.zeros_like(zero_ref)

        def zero_block(start):
            rows = pl.ds(pl.multiple_of(start * TOKEN_TILE, blk_rows), blk_rows)
            return pltpu.make_async_copy(zero_ref, xs_ref.at[rows], zsem)

        def fill(e, c):
            @pl.when(plen_ref[e] > 0)
            def _():
                zero_block(pend_ref[e] - MOE_BLK).start()
            return c

        def drain(e, c):
            @pl.when(plen_ref[e] > 0)
            def _():
                zero_block(pend_ref[e] - MOE_BLK).wait()
            return c

        lax.fori_loop(0, N_EXPERTS, fill, 0)
        n_used = pend_ref[N_EXPERTS - 1] >> MOE_BLK_LOG2
        n_blk = xs_ref.shape[0] // blk_rows
        lax.fori_loop(n_used, n_blk, lambda j, c: (zero_block(j * MOE_BLK).start(), c)[1], 0)
        lax.fori_loop(0, N_EXPERTS, drain, 0)
        lax.fori_loop(n_used, n_blk, lambda j, c: (zero_block(j * MOE_BLK).wait(), c)[1], 0)

    def issue(r, c):
        for k in range(TOP_K):
            d = dest_ref[k * n_tok + base + r]
            pltpu.make_async_copy(_token_tile(h_ref, r), _token_tile(xs_ref, d), sem).start(priority=k)
        return c

    lax.fori_loop(0, tm, issue, 0, unroll=8)
    for k in range(TOP_K):
        pltpu.make_async_copy(h_ref, xs_ref.at[pl.ds(0, tm * TOKEN_TILE)], sem).wait()


def _dispatch_call(dest_flat, p_end, p_len, h2t, n_slots):
    rows = DISPATCH_TM * TOKEN_TILE
    return pl.pallas_call(
        _dispatch_body,
        grid_spec=pltpu.PrefetchScalarGridSpec(
            num_scalar_prefetch=3,
            grid=(h2t.shape[0] // rows,),
            in_specs=[pl.BlockSpec((rows, LANES), lambda i, d, pe, pn: (i, 0))],
            out_specs=pl.BlockSpec(memory_space=pl.ANY),
            scratch_shapes=[pltpu.VMEM((MOE_BLK * TOKEN_TILE, LANES), F32),
                            pltpu.SemaphoreType.DMA(()), pltpu.SemaphoreType.DMA(())]),
        out_shape=jax.ShapeDtypeStruct((n_slots * TOKEN_TILE, LANES), F32),
        compiler_params=_cparams(("arbitrary",)),
        name="dispatch",
    )(dest_flat, p_end, p_len, h2t)


EXPERT_BLKS_PER_STEP = 2


def _experts_body(blk_ref, used_ref, xs_ref, w1a_ref, w3a_ref, w2a_ref, w1b_ref, w3b_ref, w2b_ref, y_ref,
                  b1_ref, b3_ref, b2_ref):
    i = pl.program_id(0)
    per = EXPERT_BLKS_PER_STEP
    for half, (w1_ref, w3_ref, w2_ref) in enumerate(((w1a_ref, w3a_ref, w2a_ref), (w1b_ref, w3b_ref, w2b_ref))):
        blk = per * i + half
        row0 = half * MOE_BLK * TOKEN_TILE
        stale = (i == 0) | (blk_ref[blk] != blk_ref[jnp.maximum(blk - per, 0)])

        @pl.when((blk < used_ref[0]) & stale)
        def _():
            b1_ref[half] = w1_ref[0].astype(BF16)
            b3_ref[half] = w3_ref[0].astype(BF16)
            b2_ref[half] = w2_ref[0].astype(BF16)

        @pl.when(blk < used_ref[0])
        def _():
            x = _load_token_tiles(xs_ref, row0, MOE_BLK).astype(BF16)
            g = _dot(x, b1_ref[half])
            u = _dot(x, b3_ref[half])
            mid = (g * _sigmoid(g)) * u
            _store_token_tiles(y_ref, row0, _dot(mid.astype(BF16), b2_ref[half]))

        @pl.when(blk >= used_ref[0])
        def _():
            y_ref[pl.ds(row0, MOE_BLK * TOKEN_TILE), :] = jnp.zeros((MOE_BLK * TOKEN_TILE, LANES), F32)


def _experts_call(blk_e, n_used, xs, w1, w3, w2):
    per = EXPERT_BLKS_PER_STEP
    rows = per * MOE_BLK * TOKEN_TILE
    nb = xs.shape[0] // (MOE_BLK * TOKEN_TILE)
    assert nb % per == 0 and blk_e.shape[0] == nb
    wmap = lambda half: (lambda i, blk, used: (blk[per * i + half], 0, 0))
    wspecs = lambda half: [pl.BlockSpec((1, D_MODEL, D_EXPERT), wmap(half)),
                           pl.BlockSpec((1, D_MODEL, D_EXPERT), wmap(half)),
                           pl.BlockSpec((1, D_EXPERT, D_MODEL), wmap(half))]
    return pl.pallas_call(
        _experts_body,
        grid_spec=pltpu.PrefetchScalarGridSpec(
            num_scalar_prefetch=2,
            grid=(nb // per,),
            in_specs=[pl.BlockSpec((rows, LANES), lambda i, blk, used: (i, 0))] + wspecs(0) + wspecs(1),
            out_specs=pl.BlockSpec((rows, LANES), lambda i, blk, used: (i, 0)),
            scratch_shapes=[pltpu.VMEM((per, D_MODEL, D_EXPERT), BF16), pltpu.VMEM((per, D_MODEL, D_EXPERT), BF16),
                            pltpu.VMEM((per, D_EXPERT, D_MODEL), BF16)]),
        out_shape=jax.ShapeDtypeStruct(xs.shape, F32),
        compiler_params=_cparams(("arbitrary",)),
        name="experts",
    )(blk_e, n_used, xs, w1, w3, w2, w1, w3, w2)


def _combine_body(dest_ref, x1_ref, gate_ref, mod_ref, gf_ref, y_ref, o_ref, buf_ref, sem):
    tm = x1_ref.shape[0]
    step = pl.program_id(0)
    n_step = pl.num_programs(0)
    n_tok = n_step * tm
    slot = step % 2
    region = tm * TOKEN_TILE

    def region_row0(buf, k):
        return pl.multiple_of((buf * TOP_K + k) * region, region)

    def start_gather(for_step):
        def issue(r, c):
            for k in range(TOP_K):
                d = dest_ref[k * n_tok + for_step * tm + r]
                dst = buf_ref.at[pl.ds(pl.multiple_of(region_row0(for_step % 2, k) + r * TOKEN_TILE, TOKEN_TILE),
                                       TOKEN_TILE)]
                pltpu.make_async_copy(_token_tile(y_ref, d), dst, sem.at[for_step % 2]).start(priority=k)
            return c

        lax.fori_loop(0, tm, issue, 0, unroll=8)

    @pl.when(step == 0)
    def _():
        start_gather(step)

    @pl.when(step + 1 < n_step)
    def _():
        start_gather(step + 1)

    eye = jnp.where(lax.broadcasted_iota(I32, (tm, tm), 0) == lax.broadcasted_iota(I32, (tm, tm), 1),
                    1.0, 0.0).astype(BF16)
    g = gate_ref[...]
    g1 = g.astype(BF16)
    rem = g - g1.astype(F32)
    g2 = rem.astype(BF16)
    g3 = (rem - g2.astype(F32)).astype(BF16)
    gt = _dot_nt(eye, g1) + (_dot_nt(eye, g2) + _dot_nt(eye, g3))

    for k in range(TOP_K):
        pltpu.make_async_copy(y_ref.at[pl.ds(0, region)], buf_ref.at[pl.ds(region_row0(slot, k), region)],
                              sem.at[slot]).wait()

    ga2 = mod_ref[0, :, 5 * D_MODEL:6 * D_MODEL]
    moe = (gt[:, 0:1] * _load_token_tiles(buf_ref, region_row0(slot, 0), tm)
           + gt[:, 1:2] * _load_token_tiles(buf_ref, region_row0(slot, 1), tm))
    x2 = x1_ref[...] + ga2 * moe
    ms = jnp.mean(x2 * x2, axis=-1, keepdims=True)
    o_ref[...] = x2 * lax.rsqrt(ms + EPS) * gf_ref[...]


def _combine_call(dest_flat, x1, gate, mod3, g_final, y, seq):
    t = x1.shape[0]
    tm = COMBINE_TM
    per_b = seq // tm
    return pl.pallas_call(
        _combine_body,
        grid_spec=pltpu.PrefetchScalarGridSpec(
            num_scalar_prefetch=1,
            grid=(t // tm,),
            in_specs=[pl.BlockSpec((tm, D_MODEL), lambda i, d: (i, 0)),
                      pl.BlockSpec((SUBLANES, tm), lambda i, d: (0, i)),
                      pl.BlockSpec((1, 1, 6 * D_MODEL), lambda i, d: (i // per_b, 0, 0)),
                      pl.BlockSpec((1, D_MODEL), lambda i, d: (0, 0)),
                      pl.BlockSpec(memory_space=pl.ANY)],
            out_specs=pl.BlockSpec((tm, D_MODEL), lambda i, d: (i, 0)),
            scratch_shapes=[pltpu.VMEM((2 * TOP_K * tm * TOKEN_TILE, LANES), F32),
                            pltpu.SemaphoreType.DMA((2,))]),
        out_shape=jax.ShapeDtypeStruct((t, D_MODEL), F32),
        compiler_params=_cparams(("arbitrary",)),
        name="combine",
    )(dest_flat, x1, gate, mod3, g_final, y)


def _rope_tables(seq):
    half = NA_HEAD_DIM // 2
    nf = half // 2
    inv_freq = ROPE_THETA ** (-jnp.arange(nf, dtype=F32) / nf)
    t = jnp.arange(seq)
    row_pos = (t // GRID_W).astype(F32)
    col_pos = (t % GRID_W).astype(F32)
    ang_r = row_pos[:, None] * inv_freq
    ang_c = col_pos[:, None] * inv_freq
    cos = jnp.concatenate([jnp.cos(ang_r), jnp.cos(ang_r), jnp.cos(ang_c), jnp.cos(ang_c)], axis=-1)
    sin = jnp.concatenate([-jnp.sin(ang_r), jnp.sin(ang_r), -jnp.sin(ang_c), jnp.sin(ang_c)], axis=-1)
    return jnp.tile(cos, (1, NA_HEADS)), jnp.tile(sin, (1, NA_HEADS))


def _layer(x, c, ctx, c_ctx, w_mod, b_mod, g_mix, g_ffn, w_in, rpb, conv_w, conv_b, lru_wa, lru_ba,
           lru_wx, lru_bx, lru_lambda, w_up_attn, w_up_lru, w_out, wg, bg, we, be, w1, w3, w2, g_final):
    bsz, seq, d = x.shape
    n_ctx = ctx.shape[1]
    t = bsz * seq
    assert d == D_MODEL and seq % ATT_TQ == 0 and seq // ATT_TQ > K_ROW_BLOCKS and n_ctx % ATT_TQ == 0
    assert bsz + 1 <= MOD_ROWS and seq % PROJ_TM == 0 and seq % MERGE_TM == 0
    assert t % ROUTE_TB == 0 and t % COMBINE_TM == 0 and t % DISPATCH_TM == 0
    assert (bsz * n_ctx) % PROJ_TM == 0 and n_ctx <= seq
    assert seq % (SUBLANES * SUBLANES) == 0 and n_ctx % (SUBLANES * SUBLANES) == 0
    assert (seq // SUBLANES) % SCAN_STEPS == 0 and (n_ctx // SUBLANES) % SCAN_STEPS == 0

    cc = jnp.concatenate([c, c_ctx[None, :], jnp.zeros((MOD_ROWS - bsz - 1, d), F32)], axis=0)
    mod = _mod_call(cc, w_mod, b_mod)
    mod3 = mod[:bsz].reshape(bsz, 1, 6 * d)
    mod_c = mod[bsz:bsz + 1]

    x2 = x.reshape(t, d)
    g_mix2 = g_mix.reshape(1, d)
    w_in_bf = w_in.astype(BF16)
    kc, vc, lxc = _ctxproj_call(ctx.reshape(bsz * n_ctx, d), mod_c, g_mix2, w_in_bf[:, :CTX_COLS])
    cos_t, sin_t = _rope_tables(seq)
    qre, qro, qpe, qpo, k, v, lx, glu, ga, gb = _inproj_call(x2, mod3, g_mix2, w_in_bf, cos_t, sin_t, seq)

    bias = _bias_tables(_rpbcol_call(rpb))
    o_att = _attn_call(qre, qro, qpe, qpo, k, v, kc, vc, bias, bsz, seq, n_ctx)

    hs = _lru_call(lx.reshape(bsz, seq, LRU_WIDTH), lxc.reshape(bsz, n_ctx, LRU_WIDTH),
                   conv_w, conv_b.reshape(1, LRU_WIDTH), lru_wa, lru_wx, lru_ba, lru_bx, lru_lambda)

    wr_t = jnp.concatenate([wg.T, we.T, jnp.zeros((ROUTE_ROWS - N_GROUPS - N_EXPERTS, d), F32)], axis=0)
    br = jnp.concatenate([bg, be, jnp.zeros((ROUTE_ROWS - N_GROUPS - N_EXPERTS,), F32)])
    br = jnp.broadcast_to(br[:, None], (ROUTE_ROWS, LANES))
    x1, h2, logits_t = _merge_call(x2, o_att, hs.reshape(t, LRU_WIDTH), glu, ga, gb, mod3,
                                   g_ffn.reshape(1, d), w_up_attn.astype(BF16), w_up_lru.astype(BF16),
                                   w_out.astype(BF16), wr_t, br, seq)

    eid, gate, rank, cnt = _route_call(logits_t)
    n_blk = -(-(t * TOP_K + N_EXPERTS * (MOE_BLK - 1)) // MOE_BLK)
    n_blk = -(-n_blk // EXPERT_BLKS_PER_STEP) * EXPERT_BLKS_PER_STEP
    nb_pad = -(-n_blk // LANES) * LANES
    dest, blk = _dest_call(cnt, eid, rank, nb_pad)
    dest_flat = dest.reshape(TOP_K * t)
    xs = _dispatch_call(dest_flat, blk[2, :N_EXPERTS], blk[3, :N_EXPERTS], h2, n_blk * MOE_BLK)
    y = _experts_call(blk[0, :n_blk], blk[1, :1], xs, w1, w3, w2)
    return _combine_call(dest_flat, x1, gate, mod3, g_final.reshape(1, d), y, seq).reshape(bsz, seq, d)


def kernel(x, c, ctx, c_ctx, w_mod, b_mod, g_mix, g_ffn, w_in, rpb, conv_w, conv_b, lru_wa, lru_ba, lru_wx,
           lru_bx, lru_lambda, w_up_attn, w_up_lru, w_out, router_group_w, router_group_b, router_expert_w,
           router_expert_b, expert_w_gate, expert_w_up, expert_w_down, g_final):
    assert w_mod.shape[0] == 1, "single-layer block"
    return _layer(x, c, ctx, c_ctx, w_mod[0], b_mod[0], g_mix[0], g_ffn[0], w_in[0], rpb[0], conv_w[0],
                  conv_b[0], lru_wa[0], lru_ba[0], lru_wx[0], lru_bx[0], lru_lambda[0], w_up_attn[0],
                  w_up_lru[0], w_out[0], router_group_w[0], router_group_b[0], router_expert_w[0],
                  router_expert_b[0], expert_w_gate[0], expert_w_up[0], expert_w_down[0], g_final)
```

```python
import functools

import numpy as np
import jax
import jax.numpy as jnp
from jax import lax
from jax.experimental import pallas as pl
from jax.experimental.pallas import tpu as pltpu

F32 = jnp.float32
BF16 = jnp.bfloat16
I32 = jnp.int32
U32 = jnp.uint32

D_MODEL = 1024
GRID_W = 64
EPS = 1e-6
NEG_INF = -1e30

NA_HEADS = 8
NA_HEAD_DIM = 64
NA_WIDTH = NA_HEADS * NA_HEAD_DIM
NA_WIN_ROWS = 8
NA_WIN_COLS = 16
ROPE_THETA = 10000.0

LRU_WIDTH = D_MODEL
LRU_BLOCKS = 8
LRU_BLOCK = LRU_WIDTH // LRU_BLOCKS
LRU_CONV = 4
LRU_C = 8.0

N_GROUPS = 4
EXPERTS_PER_GROUP = 8
N_EXPERTS = N_GROUPS * EXPERTS_PER_GROUP
TOP_K = 2
D_EXPERT = 512

K_OFF = 0
V_OFF = K_OFF + NA_WIDTH
LX_OFF = V_OFF + NA_WIDTH
CTX_COLS = LX_OFF + LRU_WIDTH
Q_OFF = CTX_COLS
LG_OFF = Q_OFF + NA_WIDTH
GA_OFF = LG_OFF + LRU_WIDTH
GB_OFF = GA_OFF + D_MODEL
PROJ_COLS = GB_OFF + D_MODEL

LANES = 128
SUBLANES = 8

Q_ROWS = 4
K_ROW_BLOCKS = 3
ATT_TQ = Q_ROWS * GRID_W
ATT_TK = K_ROW_BLOCKS * ATT_TQ

MOE_BLK_LOG2 = 8
MOE_BLK = 1 << MOE_BLK_LOG2
MOD_ROWS = 24
ROUTE_ROWS = 64

PROJ_TM = 512
MERGE_TM = 512
ROUTE_TB = 2048
DISPATCH_TM = 1024
COMBINE_TM = 512

VMEM_LIMIT = 56 * 1024 * 1024


def _cparams(sem, vmem=VMEM_LIMIT):
    return pltpu.CompilerParams(dimension_semantics=sem, vmem_limit_bytes=vmem)


def _dot(a, b):
    return jnp.dot(a, b, preferred_element_type=F32)


def _dot_nt(a, b):
    return lax.dot_general(a, b, (((1,), (1,)), ((), ())), preferred_element_type=F32)


def _split2(a):
    hi = a.astype(BF16)
    lo = (a - hi.astype(F32)).astype(BF16)
    return hi, lo


def _dot3(a, b):
    ah, al = _split2(a)
    bh, bl = _split2(b)
    return _dot(ah, bh) + (_dot(ah, bl) + _dot(al, bh))


def _dot3_nt(a, b):
    ah, al = _split2(a)
    bh, bl = _split2(b)
    return _dot_nt(ah, bh) + (_dot_nt(ah, bl) + _dot_nt(al, bh))


def _sigmoid(x):
    return 1.0 / (1.0 + jnp.exp(-x))


def _ada_norm(x, g, sc, sh):
    ms = jnp.mean(x * x, axis=-1, keepdims=True)
    return (x * lax.rsqrt(ms + EPS) * g) * (1.0 + sc) + sh


def _mod_body(cc_ref, w_ref, b_ref, o_ref):
    cc = cc_ref[...]
    o_ref[...] = _dot3(cc * _sigmoid(cc), w_ref[...]) + b_ref[...]


def _mod_call(cc, w_mod, b_mod):
    n = w_mod.shape[1]
    bn = 1024
    return pl.pallas_call(
        _mod_body,
        grid=(n // bn,),
        in_specs=[pl.BlockSpec((MOD_ROWS, D_MODEL), lambda j: (0, 0)),
                  pl.BlockSpec((D_MODEL, bn), lambda j: (0, j)),
                  pl.BlockSpec((1, bn), lambda j: (0, j))],
        out_specs=pl.BlockSpec((MOD_ROWS, bn), lambda j: (0, j)),
        out_shape=jax.ShapeDtypeStruct((MOD_ROWS, n), F32),
        compiler_params=_cparams(("arbitrary",)),
        name="mod",
    )(cc, w_mod, b_mod.reshape(1, n))


def _rope(t, cos, sin):
    lane = lax.broadcasted_iota(I32, (t.shape[0], LANES), 1)
    first = (lane & 16) == 0
    parts = []
    for c in range(t.shape[1] // LANES):
        tc = t[:, c * LANES:(c + 1) * LANES]
        parts.append(jnp.where(first, pltpu.roll(tc, LANES - 16, 1), pltpu.roll(tc, 16, 1)))
    partner = jnp.concatenate(parts, axis=1)
    return t * cos + partner * sin


def _gelu_tanh(x):
    return 0.5 * x * (1.0 + jnp.tanh(0.7978845608028654 * (x + 0.044715 * (x * x * x))))


def _inproj_body(x_ref, mod_ref, g_ref, w_ref, cos_ref, sin_ref,
                 qre_ref, qro_ref, qpe_ref, qpo_ref, k_ref, v_ref, lx_ref, glu_ref, ga_ref, gb_ref):
    sh = mod_ref[0, :, 0:D_MODEL]
    sc = mod_ref[0, :, D_MODEL:2 * D_MODEL]
    h = _ada_norm(x_ref[...], g_ref[...], sc, sh).astype(BF16)
    cos = cos_ref[...]
    sin = sin_ref[...]
    scale = NA_HEAD_DIM ** -0.5

    k_ref[...] = _rope(_dot(h, w_ref[:, K_OFF:K_OFF + NA_WIDTH]), cos, sin).astype(BF16)
    v_ref[...] = _dot(h, w_ref[:, V_OFF:V_OFF + NA_WIDTH]).astype(BF16)
    lx_ref[...] = _dot(h, w_ref[:, LX_OFF:LX_OFF + LRU_WIDTH])

    q = _dot(h, w_ref[:, Q_OFF:Q_OFF + NA_WIDTH]) * scale
    qr = _rope(q, cos, sin)
    lane = lax.broadcasted_iota(I32, q.shape, 1)
    even = (lane & NA_HEAD_DIM) == 0
    qre_ref[...] = jnp.where(even, qr, 0.0).astype(BF16)
    qro_ref[...] = jnp.where(even, 0.0, qr).astype(BF16)
    qpe_ref[...] = jnp.where(even, q, 0.0).astype(BF16)
    qpo_ref[...] = jnp.where(even, 0.0, q).astype(BF16)

    glu_ref[...] = _gelu_tanh(_dot(h, w_ref[:, LG_OFF:LG_OFF + LRU_WIDTH])).astype(BF16)
    ga_ref[...] = _sigmoid(_dot(h, w_ref[:, GA_OFF:GA_OFF + D_MODEL])).astype(BF16)
    gb_ref[...] = _sigmoid(_dot(h, w_ref[:, GB_OFF:GB_OFF + D_MODEL])).astype(BF16)


def _inproj_call(x2, mod3, g_mix, w_in_bf, cos_t, sin_t, seq):
    t = x2.shape[0]
    tm = PROJ_TM
    per_b = seq // tm
    row = lambda i: (i, 0)
    wide = lambda n, dt: jax.ShapeDtypeStruct((t, n), dt)
    return pl.pallas_call(
        _inproj_body,
        grid=(t // tm,),
        in_specs=[pl.BlockSpec((tm, D_MODEL), row),
                  pl.BlockSpec((1, 1, 6 * D_MODEL), lambda i: (i // per_b, 0, 0)),
                  pl.BlockSpec((1, D_MODEL), lambda i: (0, 0)),
                  pl.BlockSpec((D_MODEL, PROJ_COLS), lambda i: (0, 0), pipeline_mode=pl.Buffered(1)),
                  pl.BlockSpec((tm, NA_WIDTH), lambda i: (i % per_b, 0)),
                  pl.BlockSpec((tm, NA_WIDTH), lambda i: (i % per_b, 0))],
        out_specs=[pl.BlockSpec((tm, NA_WIDTH), row)] * 6
                  + [pl.BlockSpec((tm, LRU_WIDTH), row)] * 4,
        out_shape=[wide(NA_WIDTH, BF16)] * 6
                  + [wide(LRU_WIDTH, F32), wide(LRU_WIDTH, BF16), wide(D_MODEL, BF16), wide(D_MODEL, BF16)],
        compiler_params=_cparams(("parallel",)),
        name="inproj",
    )(x2, mod3, g_mix, w_in_bf, cos_t, sin_t)


def _ctxproj_body(x_ref, mod_ref, g_ref, w_ref, k_ref, v_ref, lx_ref):
    sh = mod_ref[:, 0:D_MODEL]
    sc = mod_ref[:, D_MODEL:2 * D_MODEL]
    h = _ada_norm(x_ref[...], g_ref[...], sc, sh).astype(BF16)
    k_ref[...] = _dot(h, w_ref[:, K_OFF:K_OFF + NA_WIDTH]).astype(BF16)
    v_ref[...] = _dot(h, w_ref[:, V_OFF:V_OFF + NA_WIDTH]).astype(BF16)
    lx_ref[...] = _dot(h, w_ref[:, LX_OFF:LX_OFF + LRU_WIDTH])


def _ctxproj_call(c2, mod_c, g_mix, w_ctx_bf):
    t = c2.shape[0]
    tm = PROJ_TM
    row = lambda i: (i, 0)
    return pl.pallas_call(
        _ctxproj_body,
        grid=(t // tm,),
        in_specs=[pl.BlockSpec((tm, D_MODEL), row),
                  pl.BlockSpec((1, 6 * D_MODEL), lambda i: (0, 0)),
                  pl.BlockSpec((1, D_MODEL), lambda i: (0, 0)),
                  pl.BlockSpec((D_MODEL, CTX_COLS), lambda i: (0, 0))],
        out_specs=[pl.BlockSpec((tm, NA_WIDTH), row), pl.BlockSpec((tm, NA_WIDTH), row),
                   pl.BlockSpec((tm, LRU_WIDTH), row)],
        out_shape=[jax.ShapeDtypeStruct((t, NA_WIDTH), BF16), jax.ShapeDtypeStruct((t, NA_WIDTH), BF16),
                   jax.ShapeDtypeStruct((t, LRU_WIDTH), F32)],
        compiler_params=_cparams(("parallel",)),
        name="ctxproj",
    )(c2, mod_c, g_mix, w_ctx_bf)


N_DR = 2 * NA_WIN_ROWS - 1
N_DC = 2 * NA_WIN_COLS - 1


def _rpbcol_body(rpb_ref, o_ref):
    n = GRID_W * GRID_W
    flat = lax.broadcasted_iota(I32, (32, n), 1)
    qc = flat >> 6
    kc = flat & (GRID_W - 1)
    dc = jnp.clip(kc - qc, 1 - NA_WIN_COLS, NA_WIN_COLS - 1) + (NA_WIN_COLS - 1)
    d_iota = lax.broadcasted_iota(I32, (32, n), 0)
    onehot = jnp.where(dc == d_iota, 1.0, 0.0).astype(BF16)
    r = rpb_ref[...]
    r1 = r.astype(BF16)
    rem = r - r1.astype(F32)
    r2 = rem.astype(BF16)
    r3 = (rem - r2.astype(F32)).astype(BF16)
    val = _dot(r1, onehot) + (_dot(r2, onehot) + _dot(r3, onehot))
    qc1 = qc[0:1, :]
    kc1 = kc[0:1, :]
    c_start = jnp.clip(qc1 - NA_WIN_COLS // 2, 0, GRID_W - NA_WIN_COLS)
    band = (kc1 >= c_start) & (kc1 < c_start + NA_WIN_COLS)
    o_ref[...] = jnp.where(band, val, NEG_INF)


def _rpbcol_call(rpb):
    rows = NA_HEADS * N_DR
    r2 = jnp.pad(rpb.reshape(rows, N_DC), ((0, 0), (0, 32 - N_DC)))
    n = GRID_W * GRID_W
    return pl.pallas_call(
        _rpbcol_body,
        in_specs=[pl.BlockSpec((rows, 32), lambda: (0, 0))],
        out_specs=pl.BlockSpec((rows, n), lambda: (0, 0)),
        out_shape=jax.ShapeDtypeStruct((rows, n), F32),
        name="rpbcol",
    )(r2)


def _bias_tables(rpbcol):
    t = rpbcol.reshape(NA_HEADS, N_DR, GRID_W, GRID_W)
    neg = jnp.full((NA_HEADS, GRID_W, GRID_W), NEG_INF, F32)
    n_kj = K_ROW_BLOCKS * Q_ROWS
    classes = []
    for lo_fn, dr_off in ((lambda ri: 0, 7), (lambda ri: ri, 3), (lambda ri: 4, -1)):
        rows = []
        for ri in range(Q_ROWS):
            lo = lo_fn(ri)
            blocks = []
            for kj in range(n_kj):
                inside = lo <= kj < lo + NA_WIN_ROWS
                blocks.append(t[:, kj - ri + dr_off] if inside else neg)
            rows.append(jnp.concatenate(blocks, axis=2))
        classes.append(jnp.concatenate(rows, axis=1))
    return jnp.stack(classes, axis=0)


def _attn_body(qre_ref, qro_ref, qpe_ref, qpo_ref, k0_ref, k1_ref, k2_ref, v0_ref, v1_ref, v2_ref,
               kc_ref, vc_ref, bias_ref, o_ref):
    lane = lax.broadcasted_iota(I32, (ATT_TQ, LANES), 1)
    lane1 = lax.broadcasted_iota(I32, (1, LANES), 1)
    head_lanes = [jnp.where(lane1 < NA_HEAD_DIM, 1.0, 0.0).astype(BF16),
                  jnp.where(lane1 < NA_HEAD_DIM, 0.0, 1.0).astype(BF16)]
    k_refs = (k0_ref, k1_ref, k2_ref)
    v_refs = (v0_ref, v1_ref, v2_ref)
    for p in range(NA_HEADS // 2):
        sl = slice(p * LANES, (p + 1) * LANES)
        k_lat = jnp.concatenate([r[:, sl] for r in k_refs], axis=0)
        kc = kc_ref[:, sl]
        v_all = jnp.concatenate([r[:, sl] for r in v_refs] + [vc_ref[:, sl]], axis=0)
        outs = []
        for hh, (qr_ref, qp_ref) in enumerate(((qre_ref, qpe_ref), (qro_ref, qpo_ref))):
            h = 2 * p + hh
            s_lat = _dot_nt(qr_ref[:, sl], k_lat) + bias_ref[0, h]
            s_ctx = _dot_nt(qp_ref[:, sl], kc)
            tiles = ([s_lat[:, j * ATT_TQ:(j + 1) * ATT_TQ] for j in range(K_ROW_BLOCKS)]
                     + [s_ctx[:, j * ATT_TQ:(j + 1) * ATT_TQ] for j in range(s_ctx.shape[1] // ATT_TQ)])
            m = functools.reduce(jnp.maximum, tiles).max(axis=-1, keepdims=True)
            prob = jnp.exp(jnp.concatenate([s_lat, s_ctx], axis=1) - m).astype(BF16)
            mine = head_lanes[hh]
            acc = _dot(prob, v_all * mine + (1.0 - mine).astype(BF16))
            outs.append(acc / pltpu.roll(acc, NA_HEAD_DIM, 1))
        o_ref[:, sl] = jnp.where(lane < NA_HEAD_DIM, outs[0], outs[1]).astype(BF16)


def _attn_call(qre, qro, qpe, qpo, k, v, kc, vc, bias, bsz, seq, n_ctx):
    t = bsz * seq
    n_grp = seq // ATT_TQ
    per_b = n_grp
    max_kb = n_grp - K_ROW_BLOCKS

    def qmap(g, b):
        return (b * per_b + g, 0)

    def kmap(j):
        return lambda g, b: (b * per_b + jnp.clip(g - 1, 0, max_kb) + j, 0)

    def cls(g, b):
        return (jnp.where(g == 0, 0, jnp.where(g == n_grp - 1, 2, 1)), 0, 0, 0)

    qspec = pl.BlockSpec((ATT_TQ, NA_WIDTH), qmap)
    cspec = pl.BlockSpec((n_ctx, NA_WIDTH), lambda g, b: (b, 0))
    return pl.pallas_call(
        _attn_body,
        grid=(n_grp, bsz),
        in_specs=[qspec] * 4
                 + [pl.BlockSpec((ATT_TQ, NA_WIDTH), kmap(j)) for j in range(K_ROW_BLOCKS)] * 2
                 + [cspec, cspec, pl.BlockSpec((1, NA_HEADS, ATT_TQ, ATT_TK), cls)],
        out_specs=pl.BlockSpec((ATT_TQ, NA_WIDTH), qmap),
        out_shape=jax.ShapeDtypeStruct((t, NA_WIDTH), BF16),
        compiler_params=_cparams(("arbitrary", "arbitrary")),
        name="attn",
    )(qre, qro, qpe, qpo, k, k, k, v, v, v, kc, vc, bias)


def _shift_down(v, row):
    return jnp.where(row >= 1, pltpu.roll(v, 1, 0), 0.0)


def _shift_up(v, row):
    return jnp.where(row < SUBLANES - 1, pltpu.roll(v, SUBLANES - 1, 0), 0.0)


def _conv4(x, w, b):
    n = x.shape[0]
    s = SUBLANES
    row = lax.broadcasted_iota(I32, (s, LANES), 0)
    last = _shift_down(x[n - s:n], row)
    last2 = _shift_down(x[n - 2 * s:n - s], row)
    first = _shift_up(x[0:s], row)
    xm1 = jnp.concatenate([last, x[0:n - s]], axis=0)
    xm2 = jnp.concatenate([last2, last, x[0:n - 2 * s]], axis=0)
    xp1 = jnp.concatenate([x[s:n], first], axis=0)
    return (w[0:1, :] * xm2 + w[1:2, :] * xm1 + w[2:3, :] * x + w[3:4, :] * xp1) + b


def _softplus(z):
    return jnp.maximum(z, 0.0) + jnp.log1p(jnp.exp(-jnp.abs(z)))


def _gates(xc, wa, wx, ba, bx, lam, a_ref, u_ref):
    n = xc.shape[0]
    xb = xc.astype(BF16)
    tr = jnp.tanh(_dot(xb, (0.5 * wa).astype(BF16)) + 0.5 * ba)
    ti = jnp.tanh(_dot(xb, (0.5 * wx).astype(BF16)) + 0.5 * bx)
    half_c = (0.5 * LRU_C) * _softplus(-lam)
    neg_log_a = half_c * tr + half_c
    a = jnp.exp(-neg_log_a)
    a_ref[0:n, :] = a
    s2 = jnp.tanh(neg_log_a) * (a * a + 1.0)
    root = jnp.where(s2 > 0.0, s2 * lax.rsqrt(s2), 0.0)
    xh = 0.5 * xc
    u_ref[0:n, :] = root * (xh * ti + xh)


def _scan4(a, u, h, p):
    a01 = a[1] * a[0]
    u01 = a[1] * u[0] + u[1]
    a23 = a[3] * a[2]
    u23 = a[3] * u[2] + u[3]
    a012 = a[2] * a01
    u012 = a[2] * u01 + u[2]
    a0123 = a23 * a01
    u0123 = a23 * u01 + u23
    hs = [a[0] * h + u[0], a01 * h + u01, a012 * h + u012, a0123 * h + u0123]
    ps = [a[0] * p, a01 * p, a012 * p, a0123 * p]
    return hs, ps


SCAN_STEPS = 4


def _scan_local(af_ref, uf_ref, ab_ref, ub_ref, n_vreg):
    s = SUBLANES
    zero = jnp.zeros((s, LANES), F32)
    one = jnp.ones((s, LANES), F32)
    span = SCAN_STEPS * s

    def body(q, carry):
        hf, pf, hb, pb = carry
        base = pl.multiple_of(q * span, span)
        rows = [pl.ds(base + i * s, s) for i in range(SCAN_STEPS)]
        hs, ps = _scan4([af_ref[r, :] for r in rows], [uf_ref[r, :] for r in rows], hf, pf)
        for r, h, p in zip(rows, hs, ps):
            uf_ref[r, :] = h
            af_ref[r, :] = p
        hf, pf = hs[-1], ps[-1]
        base = pl.multiple_of((n_vreg - SCAN_STEPS) * s - q * span, span)
        rows = [pl.ds(base + (SCAN_STEPS - 1 - i) * s, s) for i in range(SCAN_STEPS)]
        hs, ps = _scan4([ab_ref[r, :] for r in rows], [ub_ref[r, :] for r in rows], hb, pb)
        for r, h, p in zip(rows, hs, ps):
            ub_ref[r, :] = h
            ab_ref[r, :] = p
        return hf, pf, hs[-1], ps[-1]

    return lax.fori_loop(0, n_vreg // SCAN_STEPS, body, (zero, one, zero, one), unroll=2)


def _link_states(hf, pf, hb, pb, h0f, h0b):
    s = SUBLANES
    row = lax.broadcasted_iota(I32, (s, LANES), 0)
    a, u = pf, hf
    for k in (1, 2, 4):
        keep = row >= k
        u = u + a * jnp.where(keep, pltpu.roll(u, k, 0), 0.0)
        a = a * jnp.where(keep, pltpu.roll(a, k, 0), 1.0)
    end_f = u + a * h0f
    in_f = jnp.where(row >= 1, pltpu.roll(end_f, 1, 0), h0f)
    a, u = pb, hb
    for k in (1, 2, 4):
        keep = row < s - k
        u = u + a * jnp.where(keep, pltpu.roll(u, s - k, 0), 0.0)
        a = a * jnp.where(keep, pltpu.roll(a, s - k, 0), 1.0)
    end_b = u + a * h0b
    in_b = jnp.where(row < s - 1, pltpu.roll(end_b, s - 1, 0), h0b)
    return in_f, in_b, end_f[s - 1:s, :], end_b[0:1, :]


PITCH_PAD = 4


def _to_split(x_ref, lanes, pad_ref, dst_ref, n):
    s = SUBLANES
    n_j = n // s
    pitch = n_j + PITCH_PAD
    for q in range(s):
        pad_ref[pl.ds(q * pitch, n_j), :] = x_ref[0, pl.ds(q * n_j, n_j), lanes]

    def body(j, c):
        dst_ref[pl.ds(pl.multiple_of(j * s, s), s), :] = pad_ref[pl.ds(j, s, stride=pitch), :]
        return c

    lax.fori_loop(0, n_j, body, 0, unroll=8)


LRU_BLOCKS_PER_STEP = 2


def _lru_body(lx_ref, lxc_ref, cw_ref, cb_ref, wa_ref, wx_ref, ba_ref, bx_ref, lam_ref,
              o_ref, af_ref, uf_ref, ab_ref, ub_ref, pad_ref):
    n = lx_ref.shape[1]
    n_c = lxc_ref.shape[1]
    s = SUBLANES
    n_j = n // s
    pitch = n_j + PITCH_PAD
    zero = jnp.zeros((1, LANES), F32)

    for blk in range(LRU_BLOCKS_PER_STEP):
        lanes = slice(blk * LRU_BLOCK, (blk + 1) * LRU_BLOCK)
        conv_w = cw_ref[:, lanes]
        conv_b = cb_ref[:, lanes]
        gate = [(wa_ref[d, blk], wx_ref[d, blk], ba_ref[d:d + 1, lanes], bx_ref[d:d + 1, lanes],
                 lam_ref[d:d + 1, lanes]) for d in range(2)]

        _to_split(lxc_ref, lanes, pad_ref, uf_ref, n_c)
        xc = _conv4(uf_ref[0:n_c, :], conv_w, conv_b)
        _gates(xc, *gate[0], af_ref, uf_ref)
        _gates(xc, *gate[1], ab_ref, ub_ref)
        ends = _scan_local(af_ref, uf_ref, ab_ref, ub_ref, n_c // s)
        _, _, cf, cb = _link_states(*ends, zero, zero)

        _to_split(lx_ref, lanes, pad_ref, uf_ref, n)
        xl = _conv4(uf_ref[...], conv_w, conv_b)
        _gates(xl, *gate[0], af_ref, uf_ref)
        _gates(xl, *gate[1], ab_ref, ub_ref)
        ends = _scan_local(af_ref, uf_ref, ab_ref, ub_ref, n_j)
        in_f, in_b, _, _ = _link_states(*ends, cf, cb)

        def finish(j, c):
            rows = pl.ds(pl.multiple_of(j * s, s), s)
            h = (uf_ref[rows, :] + af_ref[rows, :] * in_f) + (ub_ref[rows, :] + ab_ref[rows, :] * in_b)
            pad_ref[pl.ds(j, s, stride=pitch), :] = h
            return c

        lax.fori_loop(0, n_j, finish, 0, unroll=8)
        for q in range(s):
            o_ref[0, pl.ds(q * n_j, n_j), lanes] = pad_ref[pl.ds(q * pitch, n_j), :].astype(o_ref.dtype)


def _lru_call(lx3, lxc3, conv_w, conv_b, wa, wx, ba, bx, lam):
    bsz, seq, _ = lx3.shape
    n_ctx = lxc3.shape[1]
    per = LRU_BLOCKS_PER_STEP
    width = per * LRU_BLOCK
    col = lambda b, n: (b, 0, n)
    par = lambda b, n: (0, n)
    wspec = pl.BlockSpec((2, per, LRU_BLOCK, LRU_BLOCK), lambda b, n: (0, n, 0, 0))
    return pl.pallas_call(
        _lru_body,
        grid=(bsz, LRU_BLOCKS // per),
        in_specs=[pl.BlockSpec((1, seq, width), col),
                  pl.BlockSpec((1, n_ctx, width), col),
                  pl.BlockSpec((LRU_CONV, width), par),
                  pl.BlockSpec((1, width), par),
                  wspec, wspec,
                  pl.BlockSpec((2, width), par),
                  pl.BlockSpec((2, width), par),
                  pl.BlockSpec((2, width), par)],
        out_specs=pl.BlockSpec((1, seq, width), col),
        out_shape=jax.ShapeDtypeStruct((bsz, seq, LRU_WIDTH), BF16),
        scratch_shapes=[pltpu.VMEM((seq, LRU_BLOCK), F32)] * 4
                       + [pltpu.VMEM((seq + SUBLANES * PITCH_PAD, LRU_BLOCK), F32)],
        compiler_params=_cparams(("parallel", "arbitrary")),
        name="lru",
    )(lx3, lxc3, conv_w, conv_b, wa, wx, ba, bx, lam)


TOKEN_TILE = D_MODEL // LANES


def _store_token_tiles(ref, row0, x):
    m = x.shape[0]
    for c in range(TOKEN_TILE):
        ref[pl.ds(row0 + c, m, stride=TOKEN_TILE), :] = x[:, c * LANES:(c + 1) * LANES]


def _load_token_tiles(ref, row0, m):
    return jnp.concatenate([ref[pl.ds(row0 + c, m, stride=TOKEN_TILE), :] for c in range(TOKEN_TILE)], axis=1)


def _merge_body(x_ref, oa_ref, hs_ref, glu_ref, ga_ref, gb_ref, mod_ref, g_ref, wua_ref, wul_ref, wo_ref,
                wr_ref, br_ref, x1_ref, h2_ref, lt_ref):
    ga1 = mod_ref[0, :, 2 * D_MODEL:3 * D_MODEL]
    sh2 = mod_ref[0, :, 3 * D_MODEL:4 * D_MODEL]
    sc2 = mod_ref[0, :, 4 * D_MODEL:5 * D_MODEL]
    o_lru = hs_ref[...] * glu_ref[...]
    y = (ga_ref[...].astype(F32) * _dot(oa_ref[...], wua_ref[...])
         + gb_ref[...].astype(F32) * _dot(o_lru, wul_ref[...]))
    x1 = x_ref[...] + ga1 * _dot(y.astype(BF16), wo_ref[...])
    x1_ref[...] = x1
    h2 = _ada_norm(x1, g_ref[...], sc2, sh2)
    lt_ref[...] = _dot3_nt(wr_ref[...], h2) + br_ref[:, 0:1]
    _store_token_tiles(h2_ref, 0, h2)


def _merge_call(x2, o_att, hs, glu, ga, gb, mod3, g_ffn, wua, wul, wo, wr_t, br, seq):
    t = x2.shape[0]
    tm = MERGE_TM
    per_b = seq // tm
    row = lambda i: (i, 0)
    full = lambda i: (0, 0)
    resident = lambda shape: pl.BlockSpec(shape, full, pipeline_mode=pl.Buffered(1))
    return pl.pallas_call(
        _merge_body,
        grid=(t // tm,),
        in_specs=[pl.BlockSpec((tm, D_MODEL), row),
                  pl.BlockSpec((tm, NA_WIDTH), row),
                  pl.BlockSpec((tm, LRU_WIDTH), row),
                  pl.BlockSpec((tm, LRU_WIDTH), row),
                  pl.BlockSpec((tm, D_MODEL), row),
                  pl.BlockSpec((tm, D_MODEL), row),
                  pl.BlockSpec((1, 1, 6 * D_MODEL), lambda i: (i // per_b, 0, 0)),
                  pl.BlockSpec((1, D_MODEL), full),
                  resident((NA_WIDTH, D_MODEL)),
                  resident((LRU_WIDTH, D_MODEL)),
                  resident((D_MODEL, D_MODEL)),
                  pl.BlockSpec((ROUTE_ROWS, D_MODEL), full),
                  pl.BlockSpec((ROUTE_ROWS, LANES), full)],
        out_specs=[pl.BlockSpec((tm, D_MODEL), row),
                   pl.BlockSpec((tm * TOKEN_TILE, LANES), row),
                   pl.BlockSpec((ROUTE_ROWS, tm), lambda i: (0, i))],
        out_shape=[jax.ShapeDtypeStruct((t, D_MODEL), F32),
                   jax.ShapeDtypeStruct((t * TOKEN_TILE, LANES), F32),
                   jax.ShapeDtypeStruct((ROUTE_ROWS, t), F32)],
        compiler_params=_cparams(("parallel",)),
        name="merge",
    )(x2, o_att, hs, glu, ga, gb, mod3, g_ffn, wua, wul, wo, wr_t, br)


def _route_body(lt_ref, eid_ref, gate_ref, rank_ref, cnt_ref, carry_ref):
    step = pl.program_id(0)

    @pl.when(step == 0)
    def _():
        carry_ref[...] = jnp.zeros_like(carry_ref)

    tb = lt_ref.shape[1]
    lg = [lt_ref[r:r + 1, :] for r in range(N_GROUPS)]
    best = lg[0]
    gidx = jnp.zeros((1, tb), I32)
    for r in range(1, N_GROUPS):
        better = lg[r] > best
        gidx = jnp.where(better, r, gidx)
        best = jnp.maximum(best, lg[r])
    den = jnp.exp(lg[0] - best)
    for r in range(1, N_GROUPS):
        den = den + jnp.exp(lg[r] - best)
    p_top = 1.0 / den

    ev = []
    for j in range(EXPERTS_PER_GROUP):
        sel = lt_ref[N_GROUPS + j:N_GROUPS + j + 1, :]
        for g in range(1, N_GROUPS):
            row = N_GROUPS + g * EXPERTS_PER_GROUP + j
            sel = jnp.where(gidx == g, lt_ref[row:row + 1, :], sel)
        ev.append(sel)
    v0 = ev[0]
    i0 = jnp.zeros((1, tb), I32)
    for j in range(1, EXPERTS_PER_GROUP):
        better = ev[j] > v0
        i0 = jnp.where(better, j, i0)
        v0 = jnp.maximum(v0, ev[j])
    v1 = jnp.full((1, tb), -jnp.inf, F32)
    i1 = jnp.zeros((1, tb), I32)
    for j in range(EXPERTS_PER_GROUP):
        better = (ev[j] > v1) & (i0 != j)
        i1 = jnp.where(better, j, i1)
        v1 = jnp.where(better, ev[j], v1)
    e1 = jnp.exp(v1 - v0)
    inv = 1.0 / (1.0 + e1)
    eid0 = gidx * EXPERTS_PER_GROUP + i0
    eid1 = gidx * EXPERTS_PER_GROUP + i1
    eid_ref[0:1, :] = eid0
    eid_ref[1:2, :] = eid1
    gate_ref[...] = jnp.zeros_like(gate_ref)
    gate_ref[0:1, :] = p_top * inv
    gate_ref[1:2, :] = p_top * (e1 * inv)

    sub = 256
    e_iota = lax.broadcasted_iota(I32, (N_EXPERTS, sub), 0)
    tri = jnp.where(lax.broadcasted_iota(I32, (sub, sub), 0) <= lax.broadcasted_iota(I32, (sub, sub), 1),
                    1.0, 0.0).astype(BF16)
    carry = carry_ref[...]
    for c in range(tb // sub):
        sl = slice(c * sub, (c + 1) * sub)
        m0 = eid0[:, sl] == e_iota
        m1 = eid1[:, sl] == e_iota
        oh = jnp.where(m0 | m1, 1.0, 0.0)
        incl = _dot(oh.astype(BF16), tri)
        excl = incl - oh + carry[:, 0:1]
        rank_ref[0:1, sl] = jnp.sum(jnp.where(m0, excl, 0.0), axis=0, keepdims=True).astype(I32)
        rank_ref[1:2, sl] = jnp.sum(jnp.where(m1, excl, 0.0), axis=0, keepdims=True).astype(I32)
        carry = carry + incl[:, sub - 1:sub]
    carry_ref[...] = carry
    cnt_ref[...] = carry


def _route_call(logits_t):
    t = logits_t.shape[1]
    tb = ROUTE_TB
    col = lambda i: (0, i)
    return pl.pallas_call(
        _route_body,
        grid=(t // tb,),
        in_specs=[pl.BlockSpec((ROUTE_ROWS, tb), col)],
        out_specs=[pl.BlockSpec((TOP_K, tb), col), pl.BlockSpec((SUBLANES, tb), col),
                   pl.BlockSpec((TOP_K, tb), col), pl.BlockSpec((N_EXPERTS, LANES), lambda i: (0, 0))],
        out_shape=[jax.ShapeDtypeStruct((TOP_K, t), I32), jax.ShapeDtypeStruct((SUBLANES, t), F32),
                   jax.ShapeDtypeStruct((TOP_K, t), I32), jax.ShapeDtypeStruct((N_EXPERTS, LANES), F32)],
        scratch_shapes=[pltpu.VMEM((N_EXPERTS, LANES), F32)],
        compiler_params=_cparams(("arbitrary",)),
        name="route",
    )(logits_t)


def _dest_body(cnt_ref, eid_ref, rank_ref, dest_ref, blk_ref):
    cnt = cnt_ref[...].astype(I32)
    padded = ((cnt + (MOE_BLK - 1)) >> MOE_BLK_LOG2) << MOE_BLK_LOG2
    e_iota = lax.broadcasted_iota(I32, (N_EXPERTS, LANES), 0)
    p_end = jnp.zeros((N_EXPERTS, LANES), I32)
    for e in range(N_EXPERTS):
        tot = jnp.sum(jnp.where(e_iota <= e, padded, 0), axis=0, keepdims=True)
        p_end = jnp.where(e_iota == e, tot, p_end)
    p_start = p_end - padded
    tb = eid_ref.shape[1]
    ps = jnp.concatenate([p_start] * (tb // LANES), axis=1)
    e_wide = lax.broadcasted_iota(I32, (N_EXPERTS, tb), 0)
    for k in range(TOP_K):
        start = jnp.sum(jnp.where(eid_ref[k:k + 1, :] == e_wide, ps, 0), axis=0, keepdims=True)
        dest_ref[k:k + 1, :] = start + rank_ref[k:k + 1, :]
    nb = blk_ref.shape[1]
    pe = jnp.concatenate([p_end] * (nb // LANES), axis=1)
    first_row = lax.broadcasted_iota(I32, (N_EXPERTS, nb), 1) * MOE_BLK
    n_before = jnp.sum(jnp.where(pe <= first_row, 1, 0), axis=0, keepdims=True)
    blk = jnp.minimum(n_before, N_EXPERTS - 1)
    blk_ref[...] = jnp.broadcast_to(blk, blk_ref.shape)
    blk_ref[1:2, :] = jnp.broadcast_to(p_end[N_EXPERTS - 1:N_EXPERTS, 0:1] >> MOE_BLK_LOG2, (1, nb))
    on_diag = e_iota == lax.broadcasted_iota(I32, (N_EXPERTS, LANES), 1)
    blk_ref[2:3, 0:LANES] = jnp.sum(jnp.where(on_diag, p_end, 0), axis=0, keepdims=True)
    blk_ref[3:4, 0:LANES] = jnp.sum(jnp.where(on_diag, padded, 0), axis=0, keepdims=True)


def _dest_call(cnt, eid, rank, nb_pad):
    t = eid.shape[1]
    tb = ROUTE_TB
    col = lambda i: (0, i)
    return pl.pallas_call(
        _dest_body,
        grid=(t // tb,),
        in_specs=[pl.BlockSpec((N_EXPERTS, LANES), lambda i: (0, 0)),
                  pl.BlockSpec((TOP_K, tb), col), pl.BlockSpec((TOP_K, tb), col)],
        out_specs=[pl.BlockSpec((TOP_K, tb), col), pl.BlockSpec((SUBLANES, nb_pad), lambda i: (0, 0))],
        out_shape=[jax.ShapeDtypeStruct((TOP_K, t), I32), jax.ShapeDtypeStruct((SUBLANES, nb_pad), I32)],
        compiler_params=_cparams(("arbitrary",)),
        name="dest",
    )(cnt, eid, rank)


def _token_tile(ref, t):
    return ref.at[pl.ds(pl.multiple_of(t * TOKEN_TILE, TOKEN_TILE), TOKEN_TILE)]


def _dispatch_body(dest_ref, pend_ref, plen_ref, h_ref, xs_ref, zero_ref, sem, zsem):
    tm = h_ref.shape[0] // TOKEN_TILE
    blk_rows = MOE_BLK * TOKEN_TILE
    n_tok = pl.num_programs(0) * tm
    base = pl.program_id(0) * tm

    @pl.when(pl.program_id(0) == 0)
    def _():
        zero_ref[...] = jnp.zeros_like(zero_ref)

        def zero_block(start):
            rows = pl.ds(pl.multiple_of(start * TOKEN_TILE, blk_rows), blk_rows)
            return pltpu.make_async_copy(zero_ref, xs_ref.at[rows], zsem)

        def fill(e, c):
            @pl.when(plen_ref[e] > 0)
            def _():
                zero_block(pend_ref[e] - MOE_BLK).start()
            return c

        def drain(e, c):
            @pl.when(plen_ref[e] > 0)
            def _():
                zero_block(pend_ref[e] - MOE_BLK).wait()
            return c

        lax.fori_loop(0, N_EXPERTS, fill, 0)
        n_used = pend_ref[N_EXPERTS - 1] >> MOE_BLK_LOG2
        n_blk = xs_ref.shape[0] // blk_rows
        lax.fori_loop(n_used, n_blk, lambda j, c: (zero_block(j * MOE_BLK).start(), c)[1], 0)
        lax.fori_loop(0, N_EXPERTS, drain, 0)
        lax.fori_loop(n_used, n_blk, lambda j, c: (zero_block(j * MOE_BLK).wait(), c)[1], 0)

    def issue(r, c):
        for k in range(TOP_K):
            d = dest_ref[k * n_tok + base + r]
            pltpu.make_async_copy(_token_tile(h_ref, r), _token_tile(xs_ref, d), sem).start(priority=k)
        return c

    lax.fori_loop(0, tm, issue, 0, unroll=8)
    for k in range(TOP_K):
        pltpu.make_async_copy(h_ref, xs_ref.at[pl.ds(0, tm * TOKEN_TILE)], sem).wait()


def _dispatch_call(dest_flat, p_end, p_len, h2t, n_slots):
    rows = DISPATCH_TM * TOKEN_TILE
    return pl.pallas_call(
        _dispatch_body,
        grid_spec=pltpu.PrefetchScalarGridSpec(
            num_scalar_prefetch=3,
            grid=(h2t.shape[0] // rows,),
            in_specs=[pl.BlockSpec((rows, LANES), lambda i, d, pe, pn: (i, 0))],
            out_specs=pl.BlockSpec(memory_space=pl.ANY),
            scratch_shapes=[pltpu.VMEM((MOE_BLK * TOKEN_TILE, LANES), F32),
                            pltpu.SemaphoreType.DMA(()), pltpu.SemaphoreType.DMA(())]),
        out_shape=jax.ShapeDtypeStruct((n_slots * TOKEN_TILE, LANES), F32),
        compiler_params=_cparams(("arbitrary",)),
        name="dispatch",
    )(dest_flat, p_end, p_len, h2t)


EXPERT_BLKS_PER_STEP = 2


def _experts_body(blk_ref, used_ref, xs_ref, w1a_ref, w3a_ref, w2a_ref, w1b_ref, w3b_ref, w2b_ref, y_ref,
                  b1_ref, b3_ref, b2_ref):
    i = pl.program_id(0)
    per = EXPERT_BLKS_PER_STEP
    for half, (w1_ref, w3_ref, w2_ref) in enumerate(((w1a_ref, w3a_ref, w2a_ref), (w1b_ref, w3b_ref, w2b_ref))):
        blk = per * i + half
        row0 = half * MOE_BLK * TOKEN_TILE
        stale = (i == 0) | (blk_ref[blk] != blk_ref[jnp.maximum(blk - per, 0)])

        @pl.when((blk < used_ref[0]) & stale)
        def _():
            b1_ref[half] = w1_ref[0].astype(BF16)
            b3_ref[half] = w3_ref[0].astype(BF16)
            b2_ref[half] = w2_ref[0].astype(BF16)

        @pl.when(blk < used_ref[0])
        def _():
            x = _load_token_tiles(xs_ref, row0, MOE_BLK).astype(BF16)
            g = _dot(x, b1_ref[half])
            u = _dot(x, b3_ref[half])
            mid = (g * _sigmoid(g)) * u
            _store_token_tiles(y_ref, row0, _dot(mid.astype(BF16), b2_ref[half]))

        @pl.when(blk >= used_ref[0])
        def _():
            y_ref[pl.ds(row0, MOE_BLK * TOKEN_TILE), :] = jnp.zeros((MOE_BLK * TOKEN_TILE, LANES), F32)


def _experts_call(blk_e, n_used, xs, w1, w3, w2):
    per = EXPERT_BLKS_PER_STEP
    rows = per * MOE_BLK * TOKEN_TILE
    nb = xs.shape[0] // (MOE_BLK * TOKEN_TILE)
    assert nb % per == 0 and blk_e.shape[0] == nb
    wmap = lambda half: (lambda i, blk, used: (blk[per * i + half], 0, 0))
    wspecs = lambda half: [pl.BlockSpec((1, D_MODEL, D_EXPERT), wmap(half)),
                           pl.BlockSpec((1, D_MODEL, D_EXPERT), wmap(half)),
                           pl.BlockSpec((1, D_EXPERT, D_MODEL), wmap(half))]
    return pl.pallas_call(
        _experts_body,
        grid_spec=pltpu.PrefetchScalarGridSpec(
            num_scalar_prefetch=2,
            grid=(nb // per,),
            in_specs=[pl.BlockSpec((rows, LANES), lambda i, blk, used: (i, 0))] + wspecs(0) + wspecs(1),
            out_specs=pl.BlockSpec((rows, LANES), lambda i, blk, used: (i, 0)),
            scratch_shapes=[pltpu.VMEM((per, D_MODEL, D_EXPERT), BF16), pltpu.VMEM((per, D_MODEL, D_EXPERT), BF16),
                            pltpu.VMEM((per, D_EXPERT, D_MODEL), BF16)]),
        out_shape=jax.ShapeDtypeStruct(xs.shape, F32),
        compiler_params=_cparams(("arbitrary",)),
        name="experts",
    )(blk_e, n_used, xs, w1, w3, w2, w1, w3, w2)


def _combine_body(dest_ref, x1_ref, gate_ref, mod_ref, gf_ref, y_ref, o_ref, buf_ref, sem):
    tm = x1_ref.shape[0]
    step = pl.program_id(0)
    n_step = pl.num_programs(0)
    n_tok = n_step * tm
    slot = step % 2
    region = tm * TOKEN_TILE

    def region_row0(buf, k):
        return pl.multiple_of((buf * TOP_K + k) * region, region)

    def start_gather(for_step):
        def issue(r, c):
            for k in range(TOP_K):
                d = dest_ref[k * n_tok + for_step * tm + r]
                dst = buf_ref.at[pl.ds(pl.multiple_of(region_row0(for_step % 2, k) + r * TOKEN_TILE, TOKEN_TILE),
                                       TOKEN_TILE)]
                pltpu.make_async_copy(_token_tile(y_ref, d), dst, sem.at[for_step % 2]).start(priority=k)
            return c

        lax.fori_loop(0, tm, issue, 0, unroll=8)

    @pl.when(step == 0)
    def _():
        start_gather(step)

    @pl.when(step + 1 < n_step)
    def _():
        start_gather(step + 1)

    eye = jnp.where(lax.broadcasted_iota(I32, (tm, tm), 0) == lax.broadcasted_iota(I32, (tm, tm), 1),
                    1.0, 0.0).astype(BF16)
    g = gate_ref[...]
    g1 = g.astype(BF16)
    rem = g - g1.astype(F32)
    g2 = rem.astype(BF16)
    g3 = (rem - g2.astype(F32)).astype(BF16)
    gt = _dot_nt(eye, g1) + (_dot_nt(eye, g2) + _dot_nt(eye, g3))

    for k in range(TOP_K):
        pltpu.make_async_copy(y_ref.at[pl.ds(0, region)], buf_ref.at[pl.ds(region_row0(slot, k), region)],
                              sem.at[slot]).wait()

    ga2 = mod_ref[0, :, 5 * D_MODEL:6 * D_MODEL]
    moe = (gt[:, 0:1] * _load_token_tiles(buf_ref, region_row0(slot, 0), tm)
           + gt[:, 1:2] * _load_token_tiles(buf_ref, region_row0(slot, 1), tm))
    x2 = x1_ref[...] + ga2 * moe
    ms = jnp.mean(x2 * x2, axis=-1, keepdims=True)
    o_ref[...] = x2 * lax.rsqrt(ms + EPS) * gf_ref[...]


def _combine_call(dest_flat, x1, gate, mod3, g_final, y, seq):
    t = x1.shape[0]
    tm = COMBINE_TM
    per_b = seq // tm
    return pl.pallas_call(
        _combine_body,
        grid_spec=pltpu.PrefetchScalarGridSpec(
            num_scalar_prefetch=1,
            grid=(t // tm,),
            in_specs=[pl.BlockSpec((tm, D_MODEL), lambda i, d: (i, 0)),
                      pl.BlockSpec((SUBLANES, tm), lambda i, d: (0, i)),
                      pl.BlockSpec((1, 1, 6 * D_MODEL), lambda i, d: (i // per_b, 0, 0)),
                      pl.BlockSpec((1, D_MODEL), lambda i, d: (0, 0)),
                      pl.BlockSpec(memory_space=pl.ANY)],
            out_specs=pl.BlockSpec((tm, D_MODEL), lambda i, d: (i, 0)),
            scratch_shapes=[pltpu.VMEM((2 * TOP_K * tm * TOKEN_TILE, LANES), F32),
                            pltpu.SemaphoreType.DMA((2,))]),
        out_shape=jax.ShapeDtypeStruct((t, D_MODEL), F32),
        compiler_params=_cparams(("arbitrary",)),
        name="combine",
    )(dest_flat, x1, gate, mod3, g_final, y)


def _rope_tables(seq):
    half = NA_HEAD_DIM // 2
    nf = half // 2
    inv_freq = ROPE_THETA ** (-jnp.arange(nf, dtype=F32) / nf)
    t = jnp.arange(seq)
    row_pos = (t // GRID_W).astype(F32)
    col_pos = (t % GRID_W).astype(F32)
    ang_r = row_pos[:, None] * inv_freq
    ang_c = col_pos[:, None] * inv_freq
    cos = jnp.concatenate([jnp.cos(ang_r), jnp.cos(ang_r), jnp.cos(ang_c), jnp.cos(ang_c)], axis=-1)
    sin = jnp.concatenate([-jnp.sin(ang_r), jnp.sin(ang_r), -jnp.sin(ang_c), jnp.sin(ang_c)], axis=-1)
    return jnp.tile(cos, (1, NA_HEADS)), jnp.tile(sin, (1, NA_HEADS))


def _layer(x, c, ctx, c_ctx, w_mod, b_mod, g_mix, g_ffn, w_in, rpb, conv_w, conv_b, lru_wa, lru_ba,
           lru_wx, lru_bx, lru_lambda, w_up_attn, w_up_lru, w_out, wg, bg, we, be, w1, w3, w2, g_final):
    bsz, seq, d = x.shape
    n_ctx = ctx.shape[1]
    t = bsz * seq
    assert d == D_MODEL and seq % ATT_TQ == 0 and seq // ATT_TQ > K_ROW_BLOCKS and n_ctx % ATT_TQ == 0
    assert bsz + 1 <= MOD_ROWS and seq % PROJ_TM == 0 and seq % MERGE_TM == 0
    assert t % ROUTE_TB == 0 and t % COMBINE_TM == 0 and t % DISPATCH_TM == 0
    assert (bsz * n_ctx) % PROJ_TM == 0 and n_ctx <= seq
    assert seq % (SUBLANES * SUBLANES) == 0 and n_ctx % (SUBLANES * SUBLANES) == 0
    assert (seq // SUBLANES) % SCAN_STEPS == 0 and (n_ctx // SUBLANES) % SCAN_STEPS == 0

    cc = jnp.concatenate([c, c_ctx[None, :], jnp.zeros((MOD_ROWS - bsz - 1, d), F32)], axis=0)
    mod = _mod_call(cc, w_mod, b_mod)
    mod3 = mod[:bsz].reshape(bsz, 1, 6 * d)
    mod_c = mod[bsz:bsz + 1]

    x2 = x.reshape(t, d)
    g_mix2 = g_mix.reshape(1, d)
    w_in_bf = w_in.astype(BF16)
    kc, vc, lxc = _ctxproj_call(ctx.reshape(bsz * n_ctx, d), mod_c, g_mix2, w_in_bf[:, :CTX_COLS])
    cos_t, sin_t = _rope_tables(seq)
    qre, qro, qpe, qpo, k, v, lx, glu, ga, gb = _inproj_call(x2, mod3, g_mix2, w_in_bf, cos_t, sin_t, seq)

    bias = _bias_tables(_rpbcol_call(rpb))
    o_att = _attn_call(qre, qro, qpe, qpo, k, v, kc, vc, bias, bsz, seq, n_ctx)

    hs = _lru_call(lx.reshape(bsz, seq, LRU_WIDTH), lxc.reshape(bsz, n_ctx, LRU_WIDTH),
                   conv_w, conv_b.reshape(1, LRU_WIDTH), lru_wa, lru_wx, lru_ba, lru_bx, lru_lambda)

    wr_t = jnp.concatenate([wg.T, we.T, jnp.zeros((ROUTE_ROWS - N_GROUPS - N_EXPERTS, d), F32)], axis=0)
    br = jnp.concatenate([bg, be, jnp.zeros((ROUTE_ROWS - N_GROUPS - N_EXPERTS,), F32)])
    br = jnp.broadcast_to(br[:, None], (ROUTE_ROWS, LANES))
    x1, h2, logits_t = _merge_call(x2, o_att, hs.reshape(t, LRU_WIDTH), glu, ga, gb, mod3,
                                   g_ffn.reshape(1, d), w_up_attn.astype(BF16), w_up_lru.astype(BF16),
                                   w_out.astype(BF16), wr_t, br, seq)

    eid, gate, rank, cnt = _route_call(logits_t)
    n_blk = -(-(t * TOP_K + N_EXPERTS * (MOE_BLK - 1)) // MOE_BLK)
    n_blk = -(-n_blk // EXPERT_BLKS_PER_STEP) * EXPERT_BLKS_PER_STEP
    nb_pad = -(-n_blk // LANES) * LANES
    dest, blk = _dest_call(cnt, eid, rank, nb_pad)
    dest_flat = dest.reshape(TOP_K * t)
    xs = _dispatch_call(dest_flat, blk[2, :N_EXPERTS], blk[3, :N_EXPERTS], h2, n_blk * MOE_BLK)
    y = _experts_call(blk[0, :n_blk], blk[1, :1], xs, w1, w3, w2)
    return _combine_call(dest_flat, x1, gate, mod3, g_final.reshape(1, d), y, seq).reshape(bsz, seq, d)


def kernel(x, c, ctx, c_ctx, w_mod, b_mod, g_mix, g_ffn, w_in, rpb, conv_w, conv_b, lru_wa, lru_ba, lru_wx,
           lru_bx, lru_lambda, w_up_attn, w_up_lru, w_out, router_group_w, router_group_b, router_expert_w,
           router_expert_b, expert_w_gate, expert_w_up, expert_w_down, g_final):
    assert w_mod.shape[0] == 1, "single-layer block"
    return _layer(x, c, ctx, c_ctx, w_mod[0], b_mod[0], g_mix[0], g_ffn[0], w_in[0], rpb[0], conv_w[0],
                  conv_b[0], lru_wa[0], lru_ba[0], lru_wx[0], lru_bx[0], lru_lambda[0], w_up_attn[0],
                  w_up_lru[0], w_out[0], router_group_w[0], router_group_b[0], router_expert_w[0],
                  router_expert_b[0], expert_w_gate[0], expert_w_up[0], expert_w_down[0], g_final)
```

```python
import functools

import numpy as np
import jax
import jax.numpy as jnp
from jax import lax
from jax.experimental import pallas as pl
from jax.experimental.pallas import tpu as pltpu

F32 = jnp.float32
BF16 = jnp.bfloat16
I32 = jnp.int32
U32 = jnp.uint32

D_MODEL = 1024
GRID_W = 64
EPS = 1e-6
NEG_INF = -1e30

NA_HEADS = 8
NA_HEAD_DIM = 64
NA_WIDTH = NA_HEADS * NA_HEAD_DIM
NA_WIN_ROWS = 8
NA_WIN_COLS = 16
ROPE_THETA = 10000.0

LRU_WIDTH = D_MODEL
LRU_BLOCKS = 8
LRU_BLOCK = LRU_WIDTH // LRU_BLOCKS
LRU_CONV = 4
LRU_C = 8.0

N_GROUPS = 4
EXPERTS_PER_GROUP = 8
N_EXPERTS = N_GROUPS * EXPERTS_PER_GROUP
TOP_K = 2
D_EXPERT = 512

K_OFF = 0
V_OFF = K_OFF + NA_WIDTH
LX_OFF = V_OFF + NA_WIDTH
CTX_COLS = LX_OFF + LRU_WIDTH
Q_OFF = CTX_COLS
LG_OFF = Q_OFF + NA_WIDTH
GA_OFF = LG_OFF + LRU_WIDTH
GB_OFF = GA_OFF + D_MODEL
PROJ_COLS = GB_OFF + D_MODEL

LANES = 128
SUBLANES = 8

Q_ROWS = 4
K_ROW_BLOCKS = 3
ATT_TQ = Q_ROWS * GRID_W
ATT_TK = K_ROW_BLOCKS * ATT_TQ

MOE_BLK_LOG2 = 8
MOE_BLK = 1 << MOE_BLK_LOG2
MOD_ROWS = 24
ROUTE_ROWS = 64

PROJ_TM = 512
MERGE_TM = 512
ROUTE_TB = 2048
DISPATCH_TM = 1024
COMBINE_TM = 512

VMEM_LIMIT = 56 * 1024 * 1024


def _cparams(sem, vmem=VMEM_LIMIT):
    return pltpu.CompilerParams(dimension_semantics=sem, vmem_limit_bytes=vmem)


def _dot(a, b):
    return jnp.dot(a, b, preferred_element_type=F32)


def _dot_nt(a, b):
    return lax.dot_general(a, b, (((1,), (1,)), ((), ())), preferred_element_type=F32)


def _split2(a):
    hi = a.astype(BF16)
    lo = (a - hi.astype(F32)).astype(BF16)
    return hi, lo


def _dot3(a, b):
    ah, al = _split2(a)
    bh, bl = _split2(b)
    return _dot(ah, bh) + (_dot(ah, bl) + _dot(al, bh))


def _dot3_nt(a, b):
    ah, al = _split2(a)
    bh, bl = _split2(b)
    return _dot_nt(ah, bh) + (_dot_nt(ah, bl) + _dot_nt(al, bh))


def _sigmoid(x):
    return 1.0 / (1.0 + jnp.exp(-x))


def _ada_norm(x, g, sc, sh):
    ms = jnp.mean(x * x, axis=-1, keepdims=True)
    return (x * lax.rsqrt(ms + EPS) * g) * (1.0 + sc) + sh


def _mod_body(cc_ref, w_ref, b_ref, o_ref):
    cc = cc_ref[...]
    o_ref[...] = _dot3(cc * _sigmoid(cc), w_ref[...]) + b_ref[...]


def _mod_call(cc, w_mod, b_mod):
    n = w_mod.shape[1]
    bn = 1024
    return pl.pallas_call(
        _mod_body,
        grid=(n // bn,),
        in_specs=[pl.BlockSpec((MOD_ROWS, D_MODEL), lambda j: (0, 0)),
                  pl.BlockSpec((D_MODEL, bn), lambda j: (0, j)),
                  pl.BlockSpec((1, bn), lambda j: (0, j))],
        out_specs=pl.BlockSpec((MOD_ROWS, bn), lambda j: (0, j)),
        out_shape=jax.ShapeDtypeStruct((MOD_ROWS, n), F32),
        compiler_params=_cparams(("arbitrary",)),
        name="mod",
    )(cc, w_mod, b_mod.reshape(1, n))


def _rope(t, cos, sin):
    lane = lax.broadcasted_iota(I32, (t.shape[0], LANES), 1)
    first = (lane & 16) == 0
    parts = []
    for c in range(t.shape[1] // LANES):
        tc = t[:, c * LANES:(c + 1) * LANES]
        parts.append(jnp.where(first, pltpu.roll(tc, LANES - 16, 1), pltpu.roll(tc, 16, 1)))
    partner = jnp.concatenate(parts, axis=1)
    return t * cos + partner * sin


def _gelu_tanh(x):
    return 0.5 * x * (1.0 + jnp.tanh(0.7978845608028654 * (x + 0.044715 * (x * x * x))))


def _inproj_body(x_ref, mod_ref, g_ref, w_ref, cos_ref, sin_ref,
                 qre_ref, qro_ref, qpe_ref, qpo_ref, k_ref, v_ref, lx_ref, glu_ref, ga_ref, gb_ref):
    sh = mod_ref[0, :, 0:D_MODEL]
    sc = mod_ref[0, :, D_MODEL:2 * D_MODEL]
    h = _ada_norm(x_ref[...], g_ref[...], sc, sh).astype(BF16)
    cos = cos_ref[...]
    sin = sin_ref[...]
    scale = NA_HEAD_DIM ** -0.5

    k_ref[...] = _rope(_dot(h, w_ref[:, K_OFF:K_OFF + NA_WIDTH]), cos, sin).astype(BF16)
    v_ref[...] = _dot(h, w_ref[:, V_OFF:V_OFF + NA_WIDTH]).astype(BF16)
    lx_ref[...] = _dot(h, w_ref[:, LX_OFF:LX_OFF + LRU_WIDTH])

    q = _dot(h, w_ref[:, Q_OFF:Q_OFF + NA_WIDTH]) * scale
    qr = _rope(q, cos, sin)
    lane = lax.broadcasted_iota(I32, q.shape, 1)
    even = (lane & NA_HEAD_DIM) == 0
    qre_ref[...] = jnp.where(even, qr, 0.0).astype(BF16)
    qro_ref[...] = jnp.where(even, 0.0, qr).astype(BF16)
    qpe_ref[...] = jnp.where(even, q, 0.0).astype(BF16)
    qpo_ref[...] = jnp.where(even, 0.0, q).astype(BF16)

    glu_ref[...] = _gelu_tanh(_dot(h, w_ref[:, LG_OFF:LG_OFF + LRU_WIDTH])).astype(BF16)
    ga_ref[...] = _sigmoid(_dot(h, w_ref[:, GA_OFF:GA_OFF + D_MODEL])).astype(BF16)
    gb_ref[...] = _sigmoid(_dot(h, w_ref[:, GB_OFF:GB_OFF + D_MODEL])).astype(BF16)


def _inproj_call(x2, mod3, g_mix, w_in_bf, cos_t, sin_t, seq):
    t = x2.shape[0]
    tm = PROJ_TM
    per_b = seq // tm
    row = lambda i: (i, 0)
    wide = lambda n, dt: jax.ShapeDtypeStruct((t, n), dt)
    return pl.pallas_call(
        _inproj_body,
        grid=(t // tm,),
        in_specs=[pl.BlockSpec((tm, D_MODEL), row),
                  pl.BlockSpec((1, 1, 6 * D_MODEL), lambda i: (i // per_b, 0, 0)),
                  pl.BlockSpec((1, D_MODEL), lambda i: (0, 0)),
                  pl.BlockSpec((D_MODEL, PROJ_COLS), lambda i: (0, 0), pipeline_mode=pl.Buffered(1)),
                  pl.BlockSpec((tm, NA_WIDTH), lambda i: (i % per_b, 0)),
                  pl.BlockSpec((tm, NA_WIDTH), lambda i: (i % per_b, 0))],
        out_specs=[pl.BlockSpec((tm, NA_WIDTH), row)] * 6
                  + [pl.BlockSpec((tm, LRU_WIDTH), row)] * 4,
        out_shape=[wide(NA_WIDTH, BF16)] * 6
                  + [wide(LRU_WIDTH, F32), wide(LRU_WIDTH, BF16), wide(D_MODEL, BF16), wide(D_MODEL, BF16)],
        compiler_params=_cparams(("parallel",)),
        name="inproj",
    )(x2, mod3, g_mix, w_in_bf, cos_t, sin_t)


def _ctxproj_body(x_ref, mod_ref, g_ref, w_ref, k_ref, v_ref, lx_ref):
    sh = mod_ref[:, 0:D_MODEL]
    sc = mod_ref[:, D_MODEL:2 * D_MODEL]
    h = _ada_norm(x_ref[...], g_ref[...], sc, sh).astype(BF16)
    k_ref[...] = _dot(h, w_ref[:, K_OFF:K_OFF + NA_WIDTH]).astype(BF16)
    v_ref[...] = _dot(h, w_ref[:, V_OFF:V_OFF + NA_WIDTH]).astype(BF16)
    lx_ref[...] = _dot(h, w_ref[:, LX_OFF:LX_OFF + LRU_WIDTH])


def _ctxproj_call(c2, mod_c, g_mix, w_ctx_bf):
    t = c2.shape[0]
    tm = PROJ_TM
    row = lambda i: (i, 0)
    return pl.pallas_call(
        _ctxproj_body,
        grid=(t // tm,),
        in_specs=[pl.BlockSpec((tm, D_MODEL), row),
                  pl.BlockSpec((1, 6 * D_MODEL), lambda i: (0, 0)),
                  pl.BlockSpec((1, D_MODEL), lambda i: (0, 0)),
                  pl.BlockSpec((D_MODEL, CTX_COLS), lambda i: (0, 0))],
        out_specs=[pl.BlockSpec((tm, NA_WIDTH), row), pl.BlockSpec((tm, NA_WIDTH), row),
                   pl.BlockSpec((tm, LRU_WIDTH), row)],
        out_shape=[jax.ShapeDtypeStruct((t, NA_WIDTH), BF16), jax.ShapeDtypeStruct((t, NA_WIDTH), BF16),
                   jax.ShapeDtypeStruct((t, LRU_WIDTH), F32)],
        compiler_params=_cparams(("parallel",)),
        name="ctxproj",
    )(c2, mod_c, g_mix, w_ctx_bf)


N_DR = 2 * NA_WIN_ROWS - 1
N_DC = 2 * NA_WIN_COLS - 1


def _rpbcol_body(rpb_ref, o_ref):
    n = GRID_W * GRID_W
    flat = lax.broadcasted_iota(I32, (32, n), 1)
    qc = flat >> 6
    kc = flat & (GRID_W - 1)
    dc = jnp.clip(kc - qc, 1 - NA_WIN_COLS, NA_WIN_COLS - 1) + (NA_WIN_COLS - 1)
    d_iota = lax.broadcasted_iota(I32, (32, n), 0)
    onehot = jnp.where(dc == d_iota, 1.0, 0.0).astype(BF16)
    r = rpb_ref[...]
    r1 = r.astype(BF16)
    rem = r - r1.astype(F32)
    r2 = rem.astype(BF16)
    r3 = (rem - r2.astype(F32)).astype(BF16)
    val = _dot(r1, onehot) + (_dot(r2, onehot) + _dot(r3, onehot))
    qc1 = qc[0:1, :]
    kc1 = kc[0:1, :]
    c_start = jnp.clip(qc1 - NA_WIN_COLS // 2, 0, GRID_W - NA_WIN_COLS)
    band = (kc1 >= c_start) & (kc1 < c_start + NA_WIN_COLS)
    o_ref[...] = jnp.where(band, val, NEG_INF)


def _rpbcol_call(rpb):
    rows = NA_HEADS * N_DR
    r2 = jnp.pad(rpb.reshape(rows, N_DC), ((0, 0), (0, 32 - N_DC)))
    n = GRID_W * GRID_W
    return pl.pallas_call(
        _rpbcol_body,
        in_specs=[pl.BlockSpec((rows, 32), lambda: (0, 0))],
        out_specs=pl.BlockSpec((rows, n), lambda: (0, 0)),
        out_shape=jax.ShapeDtypeStruct((rows, n), F32),
        name="rpbcol",
    )(r2)


def _bias_tables(rpbcol):
    t = rpbcol.reshape(NA_HEADS, N_DR, GRID_W, GRID_W)
    neg = jnp.full((NA_HEADS, GRID_W, GRID_W), NEG_INF, F32)
    n_kj = K_ROW_BLOCKS * Q_ROWS
    classes = []
    for lo_fn, dr_off in ((lambda ri: 0, 7), (lambda ri: ri, 3), (lambda ri: 4, -1)):
        rows = []
        for ri in range(Q_ROWS):
            lo = lo_fn(ri)
            blocks = []
            for kj in range(n_kj):
                inside = lo <= kj < lo + NA_WIN_ROWS
                blocks.append(t[:, kj - ri + dr_off] if inside else neg)
            rows.append(jnp.concatenate(blocks, axis=2))
        classes.append(jnp.concatenate(rows, axis=1))
    return jnp.stack(classes, axis=0)


def _attn_body(qre_ref, qro_ref, qpe_ref, qpo_ref, k0_ref, k1_ref, k2_ref, v0_ref, v1_ref, v2_ref,
               kc_ref, vc_ref, bias_ref, o_ref):
    lane = lax.broadcasted_iota(I32, (ATT_TQ, LANES), 1)
    lane1 = lax.broadcasted_iota(I32, (1, LANES), 1)
    head_lanes = [jnp.where(lane1 < NA_HEAD_DIM, 1.0, 0.0).astype(BF16),
                  jnp.where(lane1 < NA_HEAD_DIM, 0.0, 1.0).astype(BF16)]
    k_refs = (k0_ref, k1_ref, k2_ref)
    v_refs = (v0_ref, v1_ref, v2_ref)
    for bi, p in [(bi, p) for bi in range(ATT_BATCH) for p in range(NA_HEADS // 2)]:
        sl = slice(p * LANES, (p + 1) * LANES)
        k_lat = jnp.concatenate([r[bi, :, sl] for r in k_refs], axis=0)
        kc = kc_ref[bi, :, sl]
        v_all = jnp.concatenate([r[bi, :, sl] for r in v_refs] + [vc_ref[bi, :, sl]], axis=0)
        outs = []
        for hh, (qr_ref, qp_ref) in enumerate(((qre_ref, qpe_ref), (qro_ref, qpo_ref))):
            h = 2 * p + hh
            s_lat = _dot_nt(qr_ref[bi, :, sl], k_lat) + bias_ref[0, h]
            s_ctx = _dot_nt(qp_ref[bi, :, sl], kc)
            tiles = ([s_lat[:, j * ATT_TQ:(j + 1) * ATT_TQ] for j in range(K_ROW_BLOCKS)]
                     + [s_ctx[:, j * ATT_TQ:(j + 1) * ATT_TQ] for j in range(s_ctx.shape[1] // ATT_TQ)])
            m = functools.reduce(jnp.maximum, tiles).max(axis=-1, keepdims=True)
            prob = jnp.exp(jnp.concatenate([s_lat, s_ctx], axis=1) - m).astype(BF16)
            mine = head_lanes[hh]
            acc = _dot(prob, v_all * mine + (1.0 - mine).astype(BF16))
            outs.append(acc / pltpu.roll(acc, NA_HEAD_DIM, 1))
        o_ref[bi, :, sl] = jnp.where(lane < NA_HEAD_DIM, outs[0], outs[1]).astype(BF16)


ATT_BATCH = 2


def _attn_call(qre, qro, qpe, qpo, k, v, kc, vc, bias, bsz, seq, n_ctx):
    n_grp = seq // ATT_TQ
    max_kb = n_grp - K_ROW_BLOCKS
    assert bsz % ATT_BATCH == 0
    by_batch = lambda a, n: a.reshape(bsz, n, NA_WIDTH)

    def qmap(g, b):
        return (b, g, 0)

    def kmap(j):
        return lambda g, b: (b, jnp.clip(g - 1, 0, max_kb) + j, 0)

    def cls(g, b):
        return (jnp.where(g == 0, 0, jnp.where(g == n_grp - 1, 2, 1)), 0, 0, 0)

    qspec = pl.BlockSpec((ATT_BATCH, ATT_TQ, NA_WIDTH), qmap)
    cspec = pl.BlockSpec((ATT_BATCH, n_ctx, NA_WIDTH), lambda g, b: (b, 0, 0))
    k3, v3 = by_batch(k, seq), by_batch(v, seq)
    return pl.pallas_call(
        _attn_body,
        grid=(n_grp, bsz // ATT_BATCH),
        in_specs=[qspec] * 4
                 + [pl.BlockSpec((ATT_BATCH, ATT_TQ, NA_WIDTH), kmap(j)) for j in range(K_ROW_BLOCKS)] * 2
                 + [cspec, cspec, pl.BlockSpec((1, NA_HEADS, ATT_TQ, ATT_TK), cls)],
        out_specs=qspec,
        out_shape=jax.ShapeDtypeStruct((bsz, seq, NA_WIDTH), BF16),
        compiler_params=_cparams(("arbitrary", "arbitrary")),
        name="attn",
    )(by_batch(qre, seq), by_batch(qro, seq), by_batch(qpe, seq), by_batch(qpo, seq), k3, k3, k3, v3, v3, v3,
      by_batch(kc, n_ctx), by_batch(vc, n_ctx), bias).reshape(bsz * seq, NA_WIDTH)


def _shift_down(v, row):
    return jnp.where(row >= 1, pltpu.roll(v, 1, 0), 0.0)


def _shift_up(v, row):
    return jnp.where(row < SUBLANES - 1, pltpu.roll(v, SUBLANES - 1, 0), 0.0)


def _conv4(x, w, b):
    n = x.shape[0]
    s = SUBLANES
    row = lax.broadcasted_iota(I32, (s, LANES), 0)
    last = _shift_down(x[n - s:n], row)
    last2 = _shift_down(x[n - 2 * s:n - s], row)
    first = _shift_up(x[0:s], row)
    xm1 = jnp.concatenate([last, x[0:n - s]], axis=0)
    xm2 = jnp.concatenate([last2, last, x[0:n - 2 * s]], axis=0)
    xp1 = jnp.concatenate([x[s:n], first], axis=0)
    return (w[0:1, :] * xm2 + w[1:2, :] * xm1 + w[2:3, :] * x + w[3:4, :] * xp1) + b


def _softplus(z):
    return jnp.maximum(z, 0.0) + jnp.log1p(jnp.exp(-jnp.abs(z)))


def _gates(xc, wa, wx, ba, bx, lam, a_ref, u_ref):
    n = xc.shape[0]
    xb = xc.astype(BF16)
    tr = jnp.tanh(_dot(xb, (0.5 * wa).astype(BF16)) + 0.5 * ba)
    ti = jnp.tanh(_dot(xb, (0.5 * wx).astype(BF16)) + 0.5 * bx)
    half_c = (0.5 * LRU_C) * _softplus(-lam)
    neg_log_a = half_c * tr + half_c
    a = jnp.exp(-neg_log_a)
    a_ref[0:n, :] = a
    s2 = jnp.tanh(neg_log_a) * (a * a + 1.0)
    root = jnp.where(s2 > 0.0, s2 * lax.rsqrt(s2), 0.0)
    xh = 0.5 * xc
    u_ref[0:n, :] = root * (xh * ti + xh)


def _scan4(a, u, h, p):
    a01 = a[1] * a[0]
    u01 = a[1] * u[0] + u[1]
    a23 = a[3] * a[2]
    u23 = a[3] * u[2] + u[3]
    a012 = a[2] * a01
    u012 = a[2] * u01 + u[2]
    a0123 = a23 * a01
    u0123 = a23 * u01 + u23
    hs = [a[0] * h + u[0], a01 * h + u01, a012 * h + u012, a0123 * h + u0123]
    ps = [a[0] * p, a01 * p, a012 * p, a0123 * p]
    return hs, ps


SCAN_STEPS = 4


def _scan_local(af_ref, uf_ref, ab_ref, ub_ref, n_vreg):
    s = SUBLANES
    zero = jnp.zeros((s, LANES), F32)
    one = jnp.ones((s, LANES), F32)
    span = SCAN_STEPS * s

    def body(q, carry):
        hf, pf, hb, pb = carry
        base = pl.multiple_of(q * span, span)
        rows = [pl.ds(base + i * s, s) for i in range(SCAN_STEPS)]
        hs, ps = _scan4([af_ref[r, :] for r in rows], [uf_ref[r, :] for r in rows], hf, pf)
        for r, h, p in zip(rows, hs, ps):
            uf_ref[r, :] = h
            af_ref[r, :] = p
        hf, pf = hs[-1], ps[-1]
        base = pl.multiple_of((n_vreg - SCAN_STEPS) * s - q * span, span)
        rows = [pl.ds(base + (SCAN_STEPS - 1 - i) * s, s) for i in range(SCAN_STEPS)]
        hs, ps = _scan4([ab_ref[r, :] for r in rows], [ub_ref[r, :] for r in rows], hb, pb)
        for r, h, p in zip(rows, hs, ps):
            ub_ref[r, :] = h
            ab_ref[r, :] = p
        return hf, pf, hs[-1], ps[-1]

    return lax.fori_loop(0, n_vreg // SCAN_STEPS, body, (zero, one, zero, one), unroll=2)


def _link_states(hf, pf, hb, pb, h0f, h0b):
    s = SUBLANES
    row = lax.broadcasted_iota(I32, (s, LANES), 0)
    a, u = pf, hf
    for k in (1, 2, 4):
        keep = row >= k
        u = u + a * jnp.where(keep, pltpu.roll(u, k, 0), 0.0)
        a = a * jnp.where(keep, pltpu.roll(a, k, 0), 1.0)
    end_f = u + a * h0f
    in_f = jnp.where(row >= 1, pltpu.roll(end_f, 1, 0), h0f)
    a, u = pb, hb
    for k in (1, 2, 4):
        keep = row < s - k
        u = u + a * jnp.where(keep, pltpu.roll(u, s - k, 0), 0.0)
        a = a * jnp.where(keep, pltpu.roll(a, s - k, 0), 1.0)
    end_b = u + a * h0b
    in_b = jnp.where(row < s - 1, pltpu.roll(end_b, s - 1, 0), h0b)
    return in_f, in_b, end_f[s - 1:s, :], end_b[0:1, :]


PITCH_PAD = 4


def _to_split(x_ref, lanes, pad_ref, dst_ref, n):
    s = SUBLANES
    n_j = n // s
    pitch = n_j + PITCH_PAD
    for q in range(s):
        pad_ref[pl.ds(q * pitch, n_j), :] = x_ref[0, pl.ds(q * n_j, n_j), lanes]

    def body(j, c):
        dst_ref[pl.ds(pl.multiple_of(j * s, s), s), :] = pad_ref[pl.ds(j, s, stride=pitch), :]
        return c

    lax.fori_loop(0, n_j, body, 0, unroll=8)


LRU_BLOCKS_PER_STEP = 2


def _lru_body(lx_ref, lxc_ref, cw_ref, cb_ref, wa_ref, wx_ref, ba_ref, bx_ref, lam_ref,
              o_ref, af_ref, uf_ref, ab_ref, ub_ref, pad_ref):
    n = lx_ref.shape[1]
    n_c = lxc_ref.shape[1]
    s = SUBLANES
    n_j = n // s
    pitch = n_j + PITCH_PAD
    zero = jnp.zeros((1, LANES), F32)

    for blk in range(LRU_BLOCKS_PER_STEP):
        lanes = slice(blk * LRU_BLOCK, (blk + 1) * LRU_BLOCK)
        conv_w = cw_ref[:, lanes]
        conv_b = cb_ref[:, lanes]
        gate = [(wa_ref[d, blk], wx_ref[d, blk], ba_ref[d:d + 1, lanes], bx_ref[d:d + 1, lanes],
                 lam_ref[d:d + 1, lanes]) for d in range(2)]

        _to_split(lxc_ref, lanes, pad_ref, uf_ref, n_c)
        xc = _conv4(uf_ref[0:n_c, :], conv_w, conv_b)
        _gates(xc, *gate[0], af_ref, uf_ref)
        _gates(xc, *gate[1], ab_ref, ub_ref)
        ends = _scan_local(af_ref, uf_ref, ab_ref, ub_ref, n_c // s)
        _, _, cf, cb = _link_states(*ends, zero, zero)

        _to_split(lx_ref, lanes, pad_ref, uf_ref, n)
        xl = _conv4(uf_ref[...], conv_w, conv_b)
        _gates(xl, *gate[0], af_ref, uf_ref)
        _gates(xl, *gate[1], ab_ref, ub_ref)
        ends = _scan_local(af_ref, uf_ref, ab_ref, ub_ref, n_j)
        in_f, in_b, _, _ = _link_states(*ends, cf, cb)

        def finish(j, c):
            rows = pl.ds(pl.multiple_of(j * s, s), s)
            h = (uf_ref[rows, :] + af_ref[rows, :] * in_f) + (ub_ref[rows, :] + ab_ref[rows, :] * in_b)
            pad_ref[pl.ds(j, s, stride=pitch), :] = h
            return c

        lax.fori_loop(0, n_j, finish, 0, unroll=8)
        for q in range(s):
            o_ref[0, pl.ds(q * n_j, n_j), lanes] = pad_ref[pl.ds(q * pitch, n_j), :].astype(o_ref.dtype)


def _lru_call(lx3, lxc3, conv_w, conv_b, wa, wx, ba, bx, lam):
    bsz, seq, _ = lx3.shape
    n_ctx = lxc3.shape[1]
    per = LRU_BLOCKS_PER_STEP
    width = per * LRU_BLOCK
    col = lambda b, n: (b, 0, n)
    par = lambda b, n: (0, n)
    wspec = pl.BlockSpec((2, per, LRU_BLOCK, LRU_BLOCK), lambda b, n: (0, n, 0, 0))
    return pl.pallas_call(
        _lru_body,
        grid=(bsz, LRU_BLOCKS // per),
        in_specs=[pl.BlockSpec((1, seq, width), col),
                  pl.BlockSpec((1, n_ctx, width), col),
                  pl.BlockSpec((LRU_CONV, width), par),
                  pl.BlockSpec((1, width), par),
                  wspec, wspec,
                  pl.BlockSpec((2, width), par),
                  pl.BlockSpec((2, width), par),
                  pl.BlockSpec((2, width), par)],
        out_specs=pl.BlockSpec((1, seq, width), col),
        out_shape=jax.ShapeDtypeStruct((bsz, seq, LRU_WIDTH), BF16),
        scratch_shapes=[pltpu.VMEM((seq, LRU_BLOCK), F32)] * 4
                       + [pltpu.VMEM((seq + SUBLANES * PITCH_PAD, LRU_BLOCK), F32)],
        compiler_params=_cparams(("parallel", "arbitrary")),
        name="lru",
    )(lx3, lxc3, conv_w, conv_b, wa, wx, ba, bx, lam)


TOKEN_TILE = D_MODEL // LANES


def _store_token_tiles(ref, row0, x):
    m = x.shape[0]
    for c in range(TOKEN_TILE):
        ref[pl.ds(row0 + c, m, stride=TOKEN_TILE), :] = x[:, c * LANES:(c + 1) * LANES]


def _load_token_tiles(ref, row0, m):
    return jnp.concatenate([ref[pl.ds(row0 + c, m, stride=TOKEN_TILE), :] for c in range(TOKEN_TILE)], axis=1)


def _merge_body(x_ref, oa_ref, hs_ref, glu_ref, ga_ref, gb_ref, mod_ref, g_ref, wua_ref, wul_ref, wo_ref,
                wr_ref, br_ref, x1_ref, h2_ref, lt_ref):
    ga1 = mod_ref[0, :, 2 * D_MODEL:3 * D_MODEL]
    sh2 = mod_ref[0, :, 3 * D_MODEL:4 * D_MODEL]
    sc2 = mod_ref[0, :, 4 * D_MODEL:5 * D_MODEL]
    o_lru = hs_ref[...] * glu_ref[...]
    y = (ga_ref[...].astype(F32) * _dot(oa_ref[...], wua_ref[...])
         + gb_ref[...].astype(F32) * _dot(o_lru, wul_ref[...]))
    x1 = x_ref[...] + ga1 * _dot(y.astype(BF16), wo_ref[...])
    x1_ref[...] = x1
    h2 = _ada_norm(x1, g_ref[...], sc2, sh2)
    lt_ref[...] = _dot3_nt(wr_ref[...], h2) + br_ref[:, 0:1]
    _store_token_tiles(h2_ref, 0, h2)


def _merge_call(x2, o_att, hs, glu, ga, gb, mod3, g_ffn, wua, wul, wo, wr_t, br, seq):
    t = x2.shape[0]
    tm = MERGE_TM
    per_b = seq // tm
    row = lambda i: (i, 0)
    full = lambda i: (0, 0)
    resident = lambda shape: pl.BlockSpec(shape, full, pipeline_mode=pl.Buffered(1))
    return pl.pallas_call(
        _merge_body,
        grid=(t // tm,),
        in_specs=[pl.BlockSpec((tm, D_MODEL), row),
                  pl.BlockSpec((tm, NA_WIDTH), row),
                  pl.BlockSpec((tm, LRU_WIDTH), row),
                  pl.BlockSpec((tm, LRU_WIDTH), row),
                  pl.BlockSpec((tm, D_MODEL), row),
                  pl.BlockSpec((tm, D_MODEL), row),
                  pl.BlockSpec((1, 1, 6 * D_MODEL), lambda i: (i // per_b, 0, 0)),
                  pl.BlockSpec((1, D_MODEL), full),
                  resident((NA_WIDTH, D_MODEL)),
                  resident((LRU_WIDTH, D_MODEL)),
                  resident((D_MODEL, D_MODEL)),
                  pl.BlockSpec((ROUTE_ROWS, D_MODEL), full),
                  pl.BlockSpec((ROUTE_ROWS, LANES), full)],
        out_specs=[pl.BlockSpec((tm, D_MODEL), row),
                   pl.BlockSpec((tm * TOKEN_TILE, LANES), row),
                   pl.BlockSpec((ROUTE_ROWS, tm), lambda i: (0, i))],
        out_shape=[jax.ShapeDtypeStruct((t, D_MODEL), F32),
                   jax.ShapeDtypeStruct((t * TOKEN_TILE, LANES), F32),
                   jax.ShapeDtypeStruct((ROUTE_ROWS, t), F32)],
        compiler_params=_cparams(("parallel",)),
        name="merge",
    )(x2, o_att, hs, glu, ga, gb, mod3, g_ffn, wua, wul, wo, wr_t, br)


def _route_body(lt_ref, eid_ref, gate_ref, rank_ref, cnt_ref, carry_ref):
    step = pl.program_id(0)

    @pl.when(step == 0)
    def _():
        carry_ref[...] = jnp.zeros_like(carry_ref)

    tb = lt_ref.shape[1]
    lg = [lt_ref[r:r + 1, :] for r in range(N_GROUPS)]
    best = lg[0]
    gidx = jnp.zeros((1, tb), I32)
    for r in range(1, N_GROUPS):
        better = lg[r] > best
        gidx = jnp.where(better, r, gidx)
        best = jnp.maximum(best, lg[r])
    den = jnp.exp(lg[0] - best)
    for r in range(1, N_GROUPS):
        den = den + jnp.exp(lg[r] - best)
    p_top = 1.0 / den

    ev = []
    for j in range(EXPERTS_PER_GROUP):
        sel = lt_ref[N_GROUPS + j:N_GROUPS + j + 1, :]
        for g in range(1, N_GROUPS):
            row = N_GROUPS + g * EXPERTS_PER_GROUP + j
            sel = jnp.where(gidx == g, lt_ref[row:row + 1, :], sel)
        ev.append(sel)
    v0 = ev[0]
    i0 = jnp.zeros((1, tb), I32)
    for j in range(1, EXPERTS_PER_GROUP):
        better = ev[j] > v0
        i0 = jnp.where(better, j, i0)
        v0 = jnp.maximum(v0, ev[j])
    v1 = jnp.full((1, tb), -jnp.inf, F32)
    i1 = jnp.zeros((1, tb), I32)
    for j in range(EXPERTS_PER_GROUP):
        better = (ev[j] > v1) & (i0 != j)
        i1 = jnp.where(better, j, i1)
        v1 = jnp.where(better, ev[j], v1)
    e1 = jnp.exp(v1 - v0)
    inv = 1.0 / (1.0 + e1)
    eid0 = gidx * EXPERTS_PER_GROUP + i0
    eid1 = gidx * EXPERTS_PER_GROUP + i1
    eid_ref[0:1, :] = eid0
    eid_ref[1:2, :] = eid1
    gate_ref[...] = jnp.zeros_like(gate_ref)
    gate_ref[0:1, :] = p_top * inv
    gate_ref[1:2, :] = p_top * (e1 * inv)

    sub = 256
    e_iota = lax.broadcasted_iota(I32, (N_EXPERTS, sub), 0)
    tri = jnp.where(lax.broadcasted_iota(I32, (sub, sub), 0) <= lax.broadcasted_iota(I32, (sub, sub), 1),
                    1.0, 0.0).astype(BF16)
    carry = carry_ref[...]
    for c in range(tb // sub):
        sl = slice(c * sub, (c + 1) * sub)
        m0 = eid0[:, sl] == e_iota
        m1 = eid1[:, sl] == e_iota
        oh = jnp.where(m0 | m1, 1.0, 0.0)
        incl = _dot(oh.astype(BF16), tri)
        excl = incl - oh + carry[:, 0:1]
        rank_ref[0:1, sl] = jnp.sum(jnp.where(m0, excl, 0.0), axis=0, keepdims=True).astype(I32)
        rank_ref[1:2, sl] = jnp.sum(jnp.where(m1, excl, 0.0), axis=0, keepdims=True).astype(I32)
        carry = carry + incl[:, sub - 1:sub]
    carry_ref[...] = carry
    cnt_ref[...] = carry


def _route_call(logits_t):
    t = logits_t.shape[1]
    tb = ROUTE_TB
    col = lambda i: (0, i)
    return pl.pallas_call(
        _route_body,
        grid=(t // tb,),
        in_specs=[pl.BlockSpec((ROUTE_ROWS, tb), col)],
        out_specs=[pl.BlockSpec((TOP_K, tb), col), pl.BlockSpec((SUBLANES, tb), col),
                   pl.BlockSpec((TOP_K, tb), col), pl.BlockSpec((N_EXPERTS, LANES), lambda i: (0, 0))],
        out_shape=[jax.ShapeDtypeStruct((TOP_K, t), I32), jax.ShapeDtypeStruct((SUBLANES, t), F32),
                   jax.ShapeDtypeStruct((TOP_K, t), I32), jax.ShapeDtypeStruct((N_EXPERTS, LANES), F32)],
        scratch_shapes=[pltpu.VMEM((N_EXPERTS, LANES), F32)],
        compiler_params=_cparams(("arbitrary",)),
        name="route",
    )(logits_t)


def _dest_body(cnt_ref, eid_ref, rank_ref, dest_ref, blk_ref):
    cnt = cnt_ref[...].astype(I32)
    padded = ((cnt + (MOE_BLK - 1)) >> MOE_BLK_LOG2) << MOE_BLK_LOG2
    e_iota = lax.broadcasted_iota(I32, (N_EXPERTS, LANES), 0)
    p_end = jnp.zeros((N_EXPERTS, LANES), I32)
    for e in range(N_EXPERTS):
        tot = jnp.sum(jnp.where(e_iota <= e, padded, 0), axis=0, keepdims=True)
        p_end = jnp.where(e_iota == e, tot, p_end)
    p_start = p_end - padded
    tb = eid_ref.shape[1]
    ps = jnp.concatenate([p_start] * (tb // LANES), axis=1)
    e_wide = lax.broadcasted_iota(I32, (N_EXPERTS, tb), 0)
    for k in range(TOP_K):
        start = jnp.sum(jnp.where(eid_ref[k:k + 1, :] == e_wide, ps, 0), axis=0, keepdims=True)
        dest_ref[k:k + 1, :] = start + rank_ref[k:k + 1, :]
    nb = blk_ref.shape[1]
    pe = jnp.concatenate([p_end] * (nb // LANES), axis=1)
    first_row = lax.broadcasted_iota(I32, (N_EXPERTS, nb), 1) * MOE_BLK
    n_before = jnp.sum(jnp.where(pe <= first_row, 1, 0), axis=0, keepdims=True)
    blk = jnp.minimum(n_before, N_EXPERTS - 1)
    blk_ref[...] = jnp.broadcast_to(blk, blk_ref.shape)
    blk_ref[1:2, :] = jnp.broadcast_to(p_end[N_EXPERTS - 1:N_EXPERTS, 0:1] >> MOE_BLK_LOG2, (1, nb))
    on_diag = e_iota == lax.broadcasted_iota(I32, (N_EXPERTS, LANES), 1)
    blk_ref[2:3, 0:LANES] = jnp.sum(jnp.where(on_diag, p_end, 0), axis=0, keepdims=True)
    blk_ref[3:4, 0:LANES] = jnp.sum(jnp.where(on_diag, padded, 0), axis=0, keepdims=True)


def _dest_call(cnt, eid, rank, nb_pad):
    t = eid.shape[1]
    tb = ROUTE_TB
    col = lambda i: (0, i)
    return pl.pallas_call(
        _dest_body,
        grid=(t // tb,),
        in_specs=[pl.BlockSpec((N_EXPERTS, LANES), lambda i: (0, 0)),
                  pl.BlockSpec((TOP_K, tb), col), pl.BlockSpec((TOP_K, tb), col)],
        out_specs=[pl.BlockSpec((TOP_K, tb), col), pl.BlockSpec((SUBLANES, nb_pad), lambda i: (0, 0))],
        out_shape=[jax.ShapeDtypeStruct((TOP_K, t), I32), jax.ShapeDtypeStruct((SUBLANES, nb_pad), I32)],
        compiler_params=_cparams(("arbitrary",)),
        name="dest",
    )(cnt, eid, rank)


def _token_tile(ref, t):
    return ref.at[pl.ds(pl.multiple_of(t * TOKEN_TILE, TOKEN_TILE), TOKEN_TILE)]


def _dispatch_body(dest_ref, pend_ref, plen_ref, h_ref, xs_ref, zero_ref, sem, zsem):
    tm = h_ref.shape[0] // TOKEN_TILE
    blk_rows = MOE_BLK * TOKEN_TILE
    n_tok = pl.num_programs(0) * tm
    base = pl.program_id(0) * tm

    @pl.when(pl.program_id(0) == 0)
    def _():
        zero_ref[...] = jnp.zeros_like(zero_ref)

        def zero_block(start):
            rows = pl.ds(pl.multiple_of(start * TOKEN_TILE, blk_rows), blk_rows)
            return pltpu.make_async_copy(zero_ref, xs_ref.at[rows], zsem)

        def fill(e, c):
            @pl.when(plen_ref[e] > 0)
            def _():
                zero_block(pend_ref[e] - MOE_BLK).start()
            return c

        def drain(e, c):
            @pl.when(plen_ref[e] > 0)
            def _():
                zero_block(pend_ref[e] - MOE_BLK).wait()
            return c

        lax.fori_loop(0, N_EXPERTS, fill, 0)
        n_used = pend_ref[N_EXPERTS - 1] >> MOE_BLK_LOG2
        n_blk = xs_ref.shape[0] // blk_rows
        lax.fori_loop(n_used, n_blk, lambda j, c: (zero_block(j * MOE_BLK).start(), c)[1], 0)
        lax.fori_loop(0, N_EXPERTS, drain, 0)
        lax.fori_loop(n_used, n_blk, lambda j, c: (zero_block(j * MOE_BLK).wait(), c)[1], 0)

    def issue(r, c):
        for k in range(TOP_K):
            d = dest_ref[k * n_tok + base + r]
            pltpu.make_async_copy(_token_tile(h_ref, r), _token_tile(xs_ref, d), sem).start(priority=k)
        return c

    lax.fori_loop(0, tm, issue, 0, unroll=8)
    for k in range(TOP_K):
        pltpu.make_async_copy(h_ref, xs_ref.at[pl.ds(0, tm * TOKEN_TILE)], sem).wait()


def _dispatch_call(dest_flat, p_end, p_len, h2t, n_slots):
    rows = DISPATCH_TM * TOKEN_TILE
    return pl.pallas_call(
        _dispatch_body,
        grid_spec=pltpu.PrefetchScalarGridSpec(
            num_scalar_prefetch=3,
            grid=(h2t.shape[0] // rows,),
            in_specs=[pl.BlockSpec((rows, LANES), lambda i, d, pe, pn: (i, 0))],
            out_specs=pl.BlockSpec(memory_space=pl.ANY),
            scratch_shapes=[pltpu.VMEM((MOE_BLK * TOKEN_TILE, LANES), F32),
                            pltpu.SemaphoreType.DMA(()), pltpu.SemaphoreType.DMA(())]),
        out_shape=jax.ShapeDtypeStruct((n_slots * TOKEN_TILE, LANES), F32),
        compiler_params=_cparams(("arbitrary",)),
        name="dispatch",
    )(dest_flat, p_end, p_len, h2t)


EXPERT_BLKS_PER_STEP = 2


def _experts_body(blk_ref, used_ref, xs_ref, w1_hbm, w3_hbm, w2_hbm, y_ref,
                  f1_ref, f3_ref, f2_ref, b1_ref, b3_ref, b2_ref, loaded_ref, sem):
    i = pl.program_id(0)
    per = EXPERT_BLKS_PER_STEP
    used = used_ref[0]

    def weight_copies(e):
        return [pltpu.make_async_copy(w_hbm.at[e], f_ref, sem)
                for w_hbm, f_ref in ((w1_hbm, f1_ref), (w3_hbm, f3_ref), (w2_hbm, f2_ref))]

    @pl.when(i == 0)
    def _():
        loaded_ref[0] = -1
        for cp in weight_copies(blk_ref[0]):
            cp.start()

    for half in range(per):
        blk = per * i + half
        row0 = half * MOE_BLK * TOKEN_TILE
        e = blk_ref[blk]

        @pl.when((blk < used) & (e != loaded_ref[0]))
        def _():
            for cp in weight_copies(e):
                cp.wait()
            b1_ref[...] = f1_ref[...].astype(BF16)
            b3_ref[...] = f3_ref[...].astype(BF16)
            b2_ref[...] = f2_ref[...].astype(BF16)
            loaded_ref[0] = e
            nxt = lax.while_loop(lambda j: (j < used) & (blk_ref[jnp.minimum(j, blk_ref.shape[0] - 1)] == e),
                                 lambda j: j + 1, blk + 1)

            @pl.when(nxt < used)
            def _():
                for cp in weight_copies(blk_ref[jnp.minimum(nxt, blk_ref.shape[0] - 1)]):
                    cp.start()

        @pl.when(blk < used)
        def _():
            x = _load_token_tiles(xs_ref, row0, MOE_BLK).astype(BF16)
            g = _dot(x, b1_ref[...])
            u = _dot(x, b3_ref[...])
            mid = (g * _sigmoid(g)) * u
            _store_token_tiles(y_ref, row0, _dot(mid.astype(BF16), b2_ref[...]))

        @pl.when(blk >= used)
        def _():
            y_ref[pl.ds(row0, MOE_BLK * TOKEN_TILE), :] = jnp.zeros((MOE_BLK * TOKEN_TILE, LANES), F32)


def _experts_call(blk_e, n_used, xs, w1, w3, w2):
    per = EXPERT_BLKS_PER_STEP
    rows = per * MOE_BLK * TOKEN_TILE
    nb = xs.shape[0] // (MOE_BLK * TOKEN_TILE)
    assert nb % per == 0 and blk_e.shape[0] == nb
    hbm = pl.BlockSpec(memory_space=pl.ANY)
    return pl.pallas_call(
        _experts_body,
        grid_spec=pltpu.PrefetchScalarGridSpec(
            num_scalar_prefetch=2,
            grid=(nb // per,),
            in_specs=[pl.BlockSpec((rows, LANES), lambda i, blk, used: (i, 0)), hbm, hbm, hbm],
            out_specs=pl.BlockSpec((rows, LANES), lambda i, blk, used: (i, 0)),
            scratch_shapes=[pltpu.VMEM((D_MODEL, D_EXPERT), F32), pltpu.VMEM((D_MODEL, D_EXPERT), F32),
                            pltpu.VMEM((D_EXPERT, D_MODEL), F32),
                            pltpu.VMEM((D_MODEL, D_EXPERT), BF16), pltpu.VMEM((D_MODEL, D_EXPERT), BF16),
                            pltpu.VMEM((D_EXPERT, D_MODEL), BF16),
                            pltpu.SMEM((1,), I32), pltpu.SemaphoreType.DMA(())]),
        out_shape=jax.ShapeDtypeStruct(xs.shape, F32),
        compiler_params=_cparams(("arbitrary",)),
        name="experts",
    )(blk_e, n_used, xs, w1, w3, w2)


def _combine_body(dest_ref, x1_ref, gate_ref, mod_ref, gf_ref, y_ref, o_ref, buf_ref, sem):
    tm = x1_ref.shape[0]
    step = pl.program_id(0)
    n_step = pl.num_programs(0)
    n_tok = n_step * tm
    slot = step % 2
    region = tm * TOKEN_TILE

    def region_row0(buf, k):
        return pl.multiple_of((buf * TOP_K + k) * region, region)

    def start_gather(for_step):
        def issue(r, c):
            for k in range(TOP_K):
                d = dest_ref[k * n_tok + for_step * tm + r]
                dst = buf_ref.at[pl.ds(pl.multiple_of(region_row0(for_step % 2, k) + r * TOKEN_TILE, TOKEN_TILE),
                                       TOKEN_TILE)]
                pltpu.make_async_copy(_token_tile(y_ref, d), dst, sem.at[for_step % 2]).start(priority=k)
            return c

        lax.fori_loop(0, tm, issue, 0, unroll=8)

    @pl.when(step == 0)
    def _():
        start_gather(step)

    @pl.when(step + 1 < n_step)
    def _():
        start_gather(step + 1)

    eye = jnp.where(lax.broadcasted_iota(I32, (tm, tm), 0) == lax.broadcasted_iota(I32, (tm, tm), 1),
                    1.0, 0.0).astype(BF16)
    g = gate_ref[...]
    g1 = g.astype(BF16)
    rem = g - g1.astype(F32)
    g2 = rem.astype(BF16)
    g3 = (rem - g2.astype(F32)).astype(BF16)
    gt = _dot_nt(eye, g1) + (_dot_nt(eye, g2) + _dot_nt(eye, g3))

    for k in range(TOP_K):
        pltpu.make_async_copy(y_ref.at[pl.ds(0, region)], buf_ref.at[pl.ds(region_row0(slot, k), region)],
                              sem.at[slot]).wait()

    ga2 = mod_ref[0, :, 5 * D_MODEL:6 * D_MODEL]
    moe = (gt[:, 0:1] * _load_token_tiles(buf_ref, region_row0(slot, 0), tm)
           + gt[:, 1:2] * _load_token_tiles(buf_ref, region_row0(slot, 1), tm))
    x2 = x1_ref[...] + ga2 * moe
    ms = jnp.mean(x2 * x2, axis=-1, keepdims=True)
    o_ref[...] = x2 * lax.rsqrt(ms + EPS) * gf_ref[...]


def _combine_call(dest_flat, x1, gate, mod3, g_final, y, seq):
    t = x1.shape[0]
    tm = COMBINE_TM
    per_b = seq // tm
    return pl.pallas_call(
        _combine_body,
        grid_spec=pltpu.PrefetchScalarGridSpec(
            num_scalar_prefetch=1,
            grid=(t // tm,),
            in_specs=[pl.BlockSpec((tm, D_MODEL), lambda i, d: (i, 0)),
                      pl.BlockSpec((SUBLANES, tm), lambda i, d: (0, i)),
                      pl.BlockSpec((1, 1, 6 * D_MODEL), lambda i, d: (i // per_b, 0, 0)),
                      pl.BlockSpec((1, D_MODEL), lambda i, d: (0, 0)),
                      pl.BlockSpec(memory_space=pl.ANY)],
            out_specs=pl.BlockSpec((tm, D_MODEL), lambda i, d: (i, 0)),
            scratch_shapes=[pltpu.VMEM((2 * TOP_K * tm * TOKEN_TILE, LANES), F32),
                            pltpu.SemaphoreType.DMA((2,))]),
        out_shape=jax.ShapeDtypeStruct((t, D_MODEL), F32),
        compiler_params=_cparams(("arbitrary",)),
        name="combine",
    )(dest_flat, x1, gate, mod3, g_final, y)


def _rope_tables(seq):
    half = NA_HEAD_DIM // 2
    nf = half // 2
    inv_freq = ROPE_THETA ** (-jnp.arange(nf, dtype=F32) / nf)
    t = jnp.arange(seq)
    row_pos = (t // GRID_W).astype(F32)
    col_pos = (t % GRID_W).astype(F32)
    ang_r = row_pos[:, None] * inv_freq
    ang_c = col_pos[:, None] * inv_freq
    cos = jnp.concatenate([jnp.cos(ang_r), jnp.cos(ang_r), jnp.cos(ang_c), jnp.cos(ang_c)], axis=-1)
    sin = jnp.concatenate([-jnp.sin(ang_r), jnp.sin(ang_r), -jnp.sin(ang_c), jnp.sin(ang_c)], axis=-1)
    return jnp.tile(cos, (1, NA_HEADS)), jnp.tile(sin, (1, NA_HEADS))


def _layer(x, c, ctx, c_ctx, w_mod, b_mod, g_mix, g_ffn, w_in, rpb, conv_w, conv_b, lru_wa, lru_ba,
           lru_wx, lru_bx, lru_lambda, w_up_attn, w_up_lru, w_out, wg, bg, we, be, w1, w3, w2, g_final):
    bsz, seq, d = x.shape
    n_ctx = ctx.shape[1]
    t = bsz * seq
    assert d == D_MODEL and seq % ATT_TQ == 0 and seq // ATT_TQ > K_ROW_BLOCKS and n_ctx % ATT_TQ == 0
    assert bsz + 1 <= MOD_ROWS and seq % PROJ_TM == 0 and seq % MERGE_TM == 0
    assert t % ROUTE_TB == 0 and t % COMBINE_TM == 0 and t % DISPATCH_TM == 0
    assert (bsz * n_ctx) % PROJ_TM == 0 and n_ctx <= seq
    assert seq % (SUBLANES * SUBLANES) == 0 and n_ctx % (SUBLANES * SUBLANES) == 0
    assert (seq // SUBLANES) % SCAN_STEPS == 0 and (n_ctx // SUBLANES) % SCAN_STEPS == 0

    cc = jnp.concatenate([c, c_ctx[None, :], jnp.zeros((MOD_ROWS - bsz - 1, d), F32)], axis=0)
    mod = _mod_call(cc, w_mod, b_mod)
    mod3 = mod[:bsz].reshape(bsz, 1, 6 * d)
    mod_c = mod[bsz:bsz + 1]

    x2 = x.reshape(t, d)
    g_mix2 = g_mix.reshape(1, d)
    w_in_bf = w_in.astype(BF16)
    kc, vc, lxc = _ctxproj_call(ctx.reshape(bsz * n_ctx, d), mod_c, g_mix2, w_in_bf[:, :CTX_COLS])
    cos_t, sin_t = _rope_tables(seq)
    qre, qro, qpe, qpo, k, v, lx, glu, ga, gb = _inproj_call(x2, mod3, g_mix2, w_in_bf, cos_t, sin_t, seq)

    bias = _bias_tables(_rpbcol_call(rpb))
    o_att = _attn_call(qre, qro, qpe, qpo, k, v, kc, vc, bias, bsz, seq, n_ctx)

    hs = _lru_call(lx.reshape(bsz, seq, LRU_WIDTH), lxc.reshape(bsz, n_ctx, LRU_WIDTH),
                   conv_w, conv_b.reshape(1, LRU_WIDTH), lru_wa, lru_wx, lru_ba, lru_bx, lru_lambda)

    wr_t = jnp.concatenate([wg.T, we.T, jnp.zeros((ROUTE_ROWS - N_GROUPS - N_EXPERTS, d), F32)], axis=0)
    br = jnp.concatenate([bg, be, jnp.zeros((ROUTE_ROWS - N_GROUPS - N_EXPERTS,), F32)])
    br = jnp.broadcast_to(br[:, None], (ROUTE_ROWS, LANES))
    x1, h2, logits_t = _merge_call(x2, o_att, hs.reshape(t, LRU_WIDTH), glu, ga, gb, mod3,
                                   g_ffn.reshape(1, d), w_up_attn.astype(BF16), w_up_lru.astype(BF16),
                                   w_out.astype(BF16), wr_t, br, seq)

    eid, gate, rank, cnt = _route_call(logits_t)
    n_blk = -(-(t * TOP_K + N_EXPERTS * (MOE_BLK - 1)) // MOE_BLK)
    n_blk = -(-n_blk // EXPERT_BLKS_PER_STEP) * EXPERT_BLKS_PER_STEP
    nb_pad = -(-n_blk // LANES) * LANES
    dest, blk = _dest_call(cnt, eid, rank, nb_pad)
    dest_flat = dest.reshape(TOP_K * t)
    xs = _dispatch_call(dest_flat, blk[2, :N_EXPERTS], blk[3, :N_EXPERTS], h2, n_blk * MOE_BLK)
    y = _experts_call(blk[0, :n_blk], blk[1, :1], xs, w1, w3, w2)
    return _combine_call(dest_flat, x1, gate, mod3, g_final.reshape(1, d), y, seq).reshape(bsz, seq, d)


def kernel(x, c, ctx, c_ctx, w_mod, b_mod, g_mix, g_ffn, w_in, rpb, conv_w, conv_b, lru_wa, lru_ba, lru_wx,
           lru_bx, lru_lambda, w_up_attn, w_up_lru, w_out, router_group_w, router_group_b, router_expert_w,
           router_expert_b, expert_w_gate, expert_w_up, expert_w_down, g_final):
    assert w_mod.shape[0] == 1, "single-layer block"
    return _layer(x, c, ctx, c_ctx, w_mod[0], b_mod[0], g_mix[0], g_ffn[0], w_in[0], rpb[0], conv_w[0],
                  conv_b[0], lru_wa[0], lru_ba[0], lru_wx[0], lru_bx[0], lru_lambda[0], w_up_attn[0],
                  w_up_lru[0], w_out[0], router_group_w[0], router_group_b[0], router_expert_w[0],
                  router_expert_b[0], expert_w_gate[0], expert_w_up[0], expert_w_down[0], g_final)
```

```python
import functools

import numpy as np
import jax
import jax.numpy as jnp
from jax import lax
from jax.experimental import pallas as pl
from jax.experimental.pallas import tpu as pltpu

F32 = jnp.float32
BF16 = jnp.bfloat16
I32 = jnp.int32
U32 = jnp.uint32

D_MODEL = 1024
GRID_W = 64
EPS = 1e-6
NEG_INF = -1e30

NA_HEADS = 8
NA_HEAD_DIM = 64
NA_WIDTH = NA_HEADS * NA_HEAD_DIM
NA_WIN_ROWS = 8
NA_WIN_COLS = 16
ROPE_THETA = 10000.0

LRU_WIDTH = D_MODEL
LRU_BLOCKS = 8
LRU_BLOCK = LRU_WIDTH // LRU_BLOCKS
LRU_CONV = 4
LRU_C = 8.0

N_GROUPS = 4
EXPERTS_PER_GROUP = 8
N_EXPERTS = N_GROUPS * EXPERTS_PER_GROUP
TOP_K = 2
D_EXPERT = 512

K_OFF = 0
V_OFF = K_OFF + NA_WIDTH
LX_OFF = V_OFF + NA_WIDTH
CTX_COLS = LX_OFF + LRU_WIDTH
Q_OFF = CTX_COLS
LG_OFF = Q_OFF + NA_WIDTH
GA_OFF = LG_OFF + LRU_WIDTH
GB_OFF = GA_OFF + D_MODEL
PROJ_COLS = GB_OFF + D_MODEL

LANES = 128
SUBLANES = 8

Q_ROWS = 4
K_ROW_BLOCKS = 3
ATT_TQ = Q_ROWS * GRID_W
ATT_TK = K_ROW_BLOCKS * ATT_TQ

MOE_BLK_LOG2 = 8
MOE_BLK = 1 << MOE_BLK_LOG2
MOD_ROWS = 24
ROUTE_ROWS = 64

PROJ_TM = 512
MERGE_TM = 512
ROUTE_TB = 2048
DISPATCH_TM = 2048
COMBINE_TM = 1024

VMEM_LIMIT = 56 * 1024 * 1024


def _cparams(sem, vmem=VMEM_LIMIT):
    return pltpu.CompilerParams(dimension_semantics=sem, vmem_limit_bytes=vmem)


def _dot(a, b):
    return jnp.dot(a, b, preferred_element_type=F32)


def _dot_nt(a, b):
    return lax.dot_general(a, b, (((1,), (1,)), ((), ())), preferred_element_type=F32)


def _split2(a):
    hi = a.astype(BF16)
    lo = (a - hi.astype(F32)).astype(BF16)
    return hi, lo


def _dot3(a, b):
    ah, al = _split2(a)
    bh, bl = _split2(b)
    return _dot(ah, bh) + (_dot(ah, bl) + _dot(al, bh))


def _dot3_nt(a, b):
    ah, al = _split2(a)
    bh, bl = _split2(b)
    return _dot_nt(ah, bh) + (_dot_nt(ah, bl) + _dot_nt(al, bh))


def _sigmoid(x):
    return 1.0 / (1.0 + jnp.exp(-x))


def _ada_norm(x, g, sc, sh):
    ms = jnp.mean(x * x, axis=-1, keepdims=True)
    return (x * lax.rsqrt(ms + EPS) * g) * (1.0 + sc) + sh


def _mod_body(cc_ref, w_ref, b_ref, o_ref):
    cc = cc_ref[...]
    o_ref[...] = _dot3(cc * _sigmoid(cc), w_ref[...]) + b_ref[...]


def _mod_call(cc, w_mod, b_mod):
    n = w_mod.shape[1]
    bn = 1024
    return pl.pallas_call(
        _mod_body,
        grid=(n // bn,),
        in_specs=[pl.BlockSpec((MOD_ROWS, D_MODEL), lambda j: (0, 0)),
                  pl.BlockSpec((D_MODEL, bn), lambda j: (0, j)),
                  pl.BlockSpec((1, bn), lambda j: (0, j))],
        out_specs=pl.BlockSpec((MOD_ROWS, bn), lambda j: (0, j)),
        out_shape=jax.ShapeDtypeStruct((MOD_ROWS, n), F32),
        compiler_params=_cparams(("arbitrary",)),
        name="mod",
    )(cc, w_mod, b_mod.reshape(1, n))


def _rope(t, cos, sin):
    lane = lax.broadcasted_iota(I32, (t.shape[0], LANES), 1)
    first = (lane & 16) == 0
    parts = []
    for c in range(t.shape[1] // LANES):
        tc = t[:, c * LANES:(c + 1) * LANES]
        parts.append(jnp.where(first, pltpu.roll(tc, LANES - 16, 1), pltpu.roll(tc, 16, 1)))
    partner = jnp.concatenate(parts, axis=1)
    return t * cos + partner * sin


def _gelu_tanh(x):
    return 0.5 * x * (1.0 + jnp.tanh(0.7978845608028654 * (x + 0.044715 * (x * x * x))))


def _inproj_body(x_ref, mod_ref, g_ref, w_ref, cos_ref, sin_ref,
                 qre_ref, qro_ref, qpe_ref, qpo_ref, k_ref, v_ref, lx_ref, glu_ref, ga_ref, gb_ref):
    sh = mod_ref[0, :, 0:D_MODEL]
    sc = mod_ref[0, :, D_MODEL:2 * D_MODEL]
    h = _ada_norm(x_ref[...], g_ref[...], sc, sh).astype(BF16)
    cos = cos_ref[...]
    sin = sin_ref[...]
    scale = NA_HEAD_DIM ** -0.5

    k_ref[...] = _rope(_dot(h, w_ref[:, K_OFF:K_OFF + NA_WIDTH]), cos, sin).astype(BF16)
    v_ref[...] = _dot(h, w_ref[:, V_OFF:V_OFF + NA_WIDTH]).astype(BF16)
    lx_ref[...] = _dot(h, w_ref[:, LX_OFF:LX_OFF + LRU_WIDTH])

    q = _dot(h, w_ref[:, Q_OFF:Q_OFF + NA_WIDTH]) * scale
    qr = _rope(q, cos, sin)
    lane = lax.broadcasted_iota(I32, q.shape, 1)
    even = (lane & NA_HEAD_DIM) == 0
    qre_ref[...] = jnp.where(even, qr, 0.0).astype(BF16)
    qro_ref[...] = jnp.where(even, 0.0, qr).astype(BF16)
    qpe_ref[...] = jnp.where(even, q, 0.0).astype(BF16)
    qpo_ref[...] = jnp.where(even, 0.0, q).astype(BF16)

    glu_ref[...] = _gelu_tanh(_dot(h, w_ref[:, LG_OFF:LG_OFF + LRU_WIDTH])).astype(BF16)
    ga_ref[...] = _sigmoid(_dot(h, w_ref[:, GA_OFF:GA_OFF + D_MODEL])).astype(BF16)
    gb_ref[...] = _sigmoid(_dot(h, w_ref[:, GB_OFF:GB_OFF + D_MODEL])).astype(BF16)


def _inproj_call(x2, mod3, g_mix, w_in_bf, cos_t, sin_t, seq):
    t = x2.shape[0]
    tm = PROJ_TM
    per_b = seq // tm
    row = lambda i: (i, 0)
    wide = lambda n, dt: jax.ShapeDtypeStruct((t, n), dt)
    return pl.pallas_call(
        _inproj_body,
        grid=(t // tm,),
        in_specs=[pl.BlockSpec((tm, D_MODEL), row),
                  pl.BlockSpec((1, 1, 6 * D_MODEL), lambda i: (i // per_b, 0, 0)),
                  pl.BlockSpec((1, D_MODEL), lambda i: (0, 0)),
                  pl.BlockSpec((D_MODEL, PROJ_COLS), lambda i: (0, 0), pipeline_mode=pl.Buffered(1)),
                  pl.BlockSpec((tm, NA_WIDTH), lambda i: (i % per_b, 0)),
                  pl.BlockSpec((tm, NA_WIDTH), lambda i: (i % per_b, 0))],
        out_specs=[pl.BlockSpec((tm, NA_WIDTH), row)] * 6
                  + [pl.BlockSpec((tm, LRU_WIDTH), row)] * 4,
        out_shape=[wide(NA_WIDTH, BF16)] * 6
                  + [wide(LRU_WIDTH, F32), wide(LRU_WIDTH, BF16), wide(D_MODEL, BF16), wide(D_MODEL, BF16)],
        compiler_params=_cparams(("parallel",)),
        name="inproj",
    )(x2, mod3, g_mix, w_in_bf, cos_t, sin_t)


def _ctxproj_body(x_ref, mod_ref, g_ref, w_ref, k_ref, v_ref, lx_ref):
    sh = mod_ref[:, 0:D_MODEL]
    sc = mod_ref[:, D_MODEL:2 * D_MODEL]
    h = _ada_norm(x_ref[...], g_ref[...], sc, sh).astype(BF16)
    k_ref[...] = _dot(h, w_ref[:, K_OFF:K_OFF + NA_WIDTH]).astype(BF16)
    v_ref[...] = _dot(h, w_ref[:, V_OFF:V_OFF + NA_WIDTH]).astype(BF16)
    lx_ref[...] = _dot(h, w_ref[:, LX_OFF:LX_OFF + LRU_WIDTH])


def _ctxproj_call(c2, mod_c, g_mix, w_ctx_bf):
    t = c2.shape[0]
    tm = PROJ_TM
    row = lambda i: (i, 0)
    return pl.pallas_call(
        _ctxproj_body,
        grid=(t // tm,),
        in_specs=[pl.BlockSpec((tm, D_MODEL), row),
                  pl.BlockSpec((1, 6 * D_MODEL), lambda i: (0, 0)),
                  pl.BlockSpec((1, D_MODEL), lambda i: (0, 0)),
                  pl.BlockSpec((D_MODEL, CTX_COLS), lambda i: (0, 0))],
        out_specs=[pl.BlockSpec((tm, NA_WIDTH), row), pl.BlockSpec((tm, NA_WIDTH), row),
                   pl.BlockSpec((tm, LRU_WIDTH), row)],
        out_shape=[jax.ShapeDtypeStruct((t, NA_WIDTH), BF16), jax.ShapeDtypeStruct((t, NA_WIDTH), BF16),
                   jax.ShapeDtypeStruct((t, LRU_WIDTH), F32)],
        compiler_params=_cparams(("parallel",)),
        name="ctxproj",
    )(c2, mod_c, g_mix, w_ctx_bf)


N_DR = 2 * NA_WIN_ROWS - 1
N_DC = 2 * NA_WIN_COLS - 1


def _rpbcol_body(rpb_ref, o_ref):
    n = GRID_W * GRID_W
    flat = lax.broadcasted_iota(I32, (32, n), 1)
    qc = flat >> 6
    kc = flat & (GRID_W - 1)
    dc = jnp.clip(kc - qc, 1 - NA_WIN_COLS, NA_WIN_COLS - 1) + (NA_WIN_COLS - 1)
    d_iota = lax.broadcasted_iota(I32, (32, n), 0)
    onehot = jnp.where(dc == d_iota, 1.0, 0.0).astype(BF16)
    r = rpb_ref[...]
    r1 = r.astype(BF16)
    rem = r - r1.astype(F32)
    r2 = rem.astype(BF16)
    r3 = (rem - r2.astype(F32)).astype(BF16)
    val = _dot(r1, onehot) + (_dot(r2, onehot) + _dot(r3, onehot))
    qc1 = qc[0:1, :]
    kc1 = kc[0:1, :]
    c_start = jnp.clip(qc1 - NA_WIN_COLS // 2, 0, GRID_W - NA_WIN_COLS)
    band = (kc1 >= c_start) & (kc1 < c_start + NA_WIN_COLS)
    o_ref[...] = jnp.where(band, val, NEG_INF)


def _rpbcol_call(rpb):
    rows = NA_HEADS * N_DR
    r2 = jnp.pad(rpb.reshape(rows, N_DC), ((0, 0), (0, 32 - N_DC)))
    n = GRID_W * GRID_W
    return pl.pallas_call(
        _rpbcol_body,
        in_specs=[pl.BlockSpec((rows, 32), lambda: (0, 0))],
        out_specs=pl.BlockSpec((rows, n), lambda: (0, 0)),
        out_shape=jax.ShapeDtypeStruct((rows, n), F32),
        name="rpbcol",
    )(r2)


def _bias_tables(rpbcol):
    t = rpbcol.reshape(NA_HEADS, N_DR, GRID_W, GRID_W)
    neg = jnp.full((NA_HEADS, GRID_W, GRID_W), NEG_INF, F32)
    n_kj = K_ROW_BLOCKS * Q_ROWS
    classes = []
    for lo_fn, dr_off in ((lambda ri: 0, 7), (lambda ri: ri, 3), (lambda ri: 4, -1)):
        rows = []
        for ri in range(Q_ROWS):
            lo = lo_fn(ri)
            blocks = []
            for kj in range(n_kj):
                inside = lo <= kj < lo + NA_WIN_ROWS
                blocks.append(t[:, kj - ri + dr_off] if inside else neg)
            rows.append(jnp.concatenate(blocks, axis=2))
        classes.append(jnp.concatenate(rows, axis=1))
    return jnp.stack(classes, axis=0)


def _attn_body(qre_ref, qro_ref, qpe_ref, qpo_ref, k0_ref, k1_ref, k2_ref, v0_ref, v1_ref, v2_ref,
               kc_ref, vc_ref, bias_ref, o_ref):
    lane = lax.broadcasted_iota(I32, (ATT_TQ, LANES), 1)
    lane1 = lax.broadcasted_iota(I32, (1, LANES), 1)
    head_lanes = [jnp.where(lane1 < NA_HEAD_DIM, 1.0, 0.0).astype(BF16),
                  jnp.where(lane1 < NA_HEAD_DIM, 0.0, 1.0).astype(BF16)]
    k_refs = (k0_ref, k1_ref, k2_ref)
    v_refs = (v0_ref, v1_ref, v2_ref)
    for bi, p in [(bi, p) for bi in range(ATT_BATCH) for p in range(NA_HEADS // 2)]:
        sl = slice(p * LANES, (p + 1) * LANES)
        k_lat = jnp.concatenate([r[bi, :, sl] for r in k_refs], axis=0)
        kc = kc_ref[bi, :, sl]
        v_all = jnp.concatenate([r[bi, :, sl] for r in v_refs] + [vc_ref[bi, :, sl]], axis=0)
        outs = []
        for hh, (qr_ref, qp_ref) in enumerate(((qre_ref, qpe_ref), (qro_ref, qpo_ref))):
            h = 2 * p + hh
            s_lat = _dot_nt(qr_ref[bi, :, sl], k_lat) + bias_ref[0, h]
            s_ctx = _dot_nt(qp_ref[bi, :, sl], kc)
            tiles = ([s_lat[:, j * ATT_TQ:(j + 1) * ATT_TQ] for j in range(K_ROW_BLOCKS)]
                     + [s_ctx[:, j * ATT_TQ:(j + 1) * ATT_TQ] for j in range(s_ctx.shape[1] // ATT_TQ)])
            m = functools.reduce(jnp.maximum, tiles).max(axis=-1, keepdims=True)
            prob = jnp.exp(jnp.concatenate([s_lat, s_ctx], axis=1) - m).astype(BF16)
            mine = head_lanes[hh]
            acc = _dot(prob, v_all * mine + (1.0 - mine).astype(BF16))
            outs.append(acc / pltpu.roll(acc, NA_HEAD_DIM, 1))
        o_ref[bi, :, sl] = jnp.where(lane < NA_HEAD_DIM, outs[0], outs[1]).astype(BF16)


ATT_BATCH = 4


def _attn_call(qre, qro, qpe, qpo, k, v, kc, vc, bias, bsz, seq, n_ctx):
    n_grp = seq // ATT_TQ
    max_kb = n_grp - K_ROW_BLOCKS
    assert bsz % ATT_BATCH == 0
    by_batch = lambda a, n: a.reshape(bsz, n, NA_WIDTH)

    def qmap(g, b):
        return (b, g, 0)

    def kmap(j):
        return lambda g, b: (b, jnp.clip(g - 1, 0, max_kb) + j, 0)

    def cls(g, b):
        return (jnp.where(g == 0, 0, jnp.where(g == n_grp - 1, 2, 1)), 0, 0, 0)

    qspec = pl.BlockSpec((ATT_BATCH, ATT_TQ, NA_WIDTH), qmap)
    cspec = pl.BlockSpec((ATT_BATCH, n_ctx, NA_WIDTH), lambda g, b: (b, 0, 0))
    k3, v3 = by_batch(k, seq), by_batch(v, seq)
    return pl.pallas_call(
        _attn_body,
        grid=(n_grp, bsz // ATT_BATCH),
        in_specs=[qspec] * 4
                 + [pl.BlockSpec((ATT_BATCH, ATT_TQ, NA_WIDTH), kmap(j)) for j in range(K_ROW_BLOCKS)] * 2
                 + [cspec, cspec, pl.BlockSpec((1, NA_HEADS, ATT_TQ, ATT_TK), cls)],
        out_specs=qspec,
        out_shape=jax.ShapeDtypeStruct((bsz, seq, NA_WIDTH), BF16),
        compiler_params=_cparams(("arbitrary", "arbitrary")),
        name="attn",
    )(by_batch(qre, seq), by_batch(qro, seq), by_batch(qpe, seq), by_batch(qpo, seq), k3, k3, k3, v3, v3, v3,
      by_batch(kc, n_ctx), by_batch(vc, n_ctx), bias).reshape(bsz * seq, NA_WIDTH)


def _shift_down(v, row):
    return jnp.where(row >= 1, pltpu.roll(v, 1, 0), 0.0)


def _shift_up(v, row):
    return jnp.where(row < SUBLANES - 1, pltpu.roll(v, SUBLANES - 1, 0), 0.0)


def _conv4(x, w, b):
    n = x.shape[0]
    s = SUBLANES
    row = lax.broadcasted_iota(I32, (s, LANES), 0)
    last = _shift_down(x[n - s:n], row)
    last2 = _shift_down(x[n - 2 * s:n - s], row)
    first = _shift_up(x[0:s], row)
    xm1 = jnp.concatenate([last, x[0:n - s]], axis=0)
    xm2 = jnp.concatenate([last2, last, x[0:n - 2 * s]], axis=0)
    xp1 = jnp.concatenate([x[s:n], first], axis=0)
    return (w[0:1, :] * xm2 + w[1:2, :] * xm1 + w[2:3, :] * x + w[3:4, :] * xp1) + b


def _softplus(z):
    return jnp.maximum(z, 0.0) + jnp.log1p(jnp.exp(-jnp.abs(z)))


def _gates(xc, wa, wx, ba, bx, lam, a_ref, u_ref):
    n = xc.shape[0]
    xb = xc.astype(BF16)
    tr = jnp.tanh(_dot(xb, (0.5 * wa).astype(BF16)) + 0.5 * ba)
    ti = jnp.tanh(_dot(xb, (0.5 * wx).astype(BF16)) + 0.5 * bx)
    half_c = (0.5 * LRU_C) * _softplus(-lam)
    neg_log_a = half_c * tr + half_c
    a = jnp.exp(-neg_log_a)
    a_ref[0:n, :] = a
    s2 = jnp.tanh(neg_log_a) * (a * a + 1.0)
    root = jnp.where(s2 > 0.0, s2 * lax.rsqrt(s2), 0.0)
    xh = 0.5 * xc
    u_ref[0:n, :] = root * (xh * ti + xh)


def _scan4(a, u, h, p):
    a01 = a[1] * a[0]
    u01 = a[1] * u[0] + u[1]
    a23 = a[3] * a[2]
    u23 = a[3] * u[2] + u[3]
    a012 = a[2] * a01
    u012 = a[2] * u01 + u[2]
    a0123 = a23 * a01
    u0123 = a23 * u01 + u23
    hs = [a[0] * h + u[0], a01 * h + u01, a012 * h + u012, a0123 * h + u0123]
    ps = [a[0] * p, a01 * p, a012 * p, a0123 * p]
    return hs, ps


SCAN_STEPS = 4


def _scan_local(af_ref, uf_ref, ab_ref, ub_ref, n_vreg):
    s = SUBLANES
    zero = jnp.zeros((s, LANES), F32)
    one = jnp.ones((s, LANES), F32)
    span = SCAN_STEPS * s

    def body(q, carry):
        hf, pf, hb, pb = carry
        base = pl.multiple_of(q * span, span)
        rows = [pl.ds(base + i * s, s) for i in range(SCAN_STEPS)]
        hs, ps = _scan4([af_ref[r, :] for r in rows], [uf_ref[r, :] for r in rows], hf, pf)
        for r, h, p in zip(rows, hs, ps):
            uf_ref[r, :] = h
            af_ref[r, :] = p
        hf, pf = hs[-1], ps[-1]
        base = pl.multiple_of((n_vreg - SCAN_STEPS) * s - q * span, span)
        rows = [pl.ds(base + (SCAN_STEPS - 1 - i) * s, s) for i in range(SCAN_STEPS)]
        hs, ps = _scan4([ab_ref[r, :] for r in rows], [ub_ref[r, :] for r in rows], hb, pb)
        for r, h, p in zip(rows, hs, ps):
            ub_ref[r, :] = h
            ab_ref[r, :] = p
        return hf, pf, hs[-1], ps[-1]

    return lax.fori_loop(0, n_vreg // SCAN_STEPS, body, (zero, one, zero, one), unroll=2)


def _link_states(hf, pf, hb, pb, h0f, h0b):
    s = SUBLANES
    row = lax.broadcasted_iota(I32, (s, LANES), 0)
    a, u = pf, hf
    for k in (1, 2, 4):
        keep = row >= k
        u = u + a * jnp.where(keep, pltpu.roll(u, k, 0), 0.0)
        a = a * jnp.where(keep, pltpu.roll(a, k, 0), 1.0)
    end_f = u + a * h0f
    in_f = jnp.where(row >= 1, pltpu.roll(end_f, 1, 0), h0f)
    a, u = pb, hb
    for k in (1, 2, 4):
        keep = row < s - k
        u = u + a * jnp.where(keep, pltpu.roll(u, s - k, 0), 0.0)
        a = a * jnp.where(keep, pltpu.roll(a, s - k, 0), 1.0)
    end_b = u + a * h0b
    in_b = jnp.where(row < s - 1, pltpu.roll(end_b, s - 1, 0), h0b)
    return in_f, in_b, end_f[s - 1:s, :], end_b[0:1, :]


PITCH_PAD = 4


def _to_split(x_ref, lanes, pad_ref, dst_ref, n):
    s = SUBLANES
    n_j = n // s
    pitch = n_j + PITCH_PAD
    for q in range(s):
        pad_ref[pl.ds(q * pitch, n_j), :] = x_ref[0, pl.ds(q * n_j, n_j), lanes]

    def body(j, c):
        dst_ref[pl.ds(pl.multiple_of(j * s, s), s), :] = pad_ref[pl.ds(j, s, stride=pitch), :]
        return c

    lax.fori_loop(0, n_j, body, 0, unroll=8)


LRU_BLOCKS_PER_STEP = 4


def _lru_body(lx_ref, lxc_ref, cw_ref, cb_ref, wa_ref, wx_ref, ba_ref, bx_ref, lam_ref,
              o_ref, af_ref, uf_ref, ab_ref, ub_ref, pad_ref):
    n = lx_ref.shape[1]
    n_c = lxc_ref.shape[1]
    s = SUBLANES
    n_j = n // s
    pitch = n_j + PITCH_PAD
    zero = jnp.zeros((1, LANES), F32)

    for blk in range(LRU_BLOCKS_PER_STEP):
        lanes = slice(blk * LRU_BLOCK, (blk + 1) * LRU_BLOCK)
        conv_w = cw_ref[:, lanes]
        conv_b = cb_ref[:, lanes]
        gate = [(wa_ref[d, blk], wx_ref[d, blk], ba_ref[d:d + 1, lanes], bx_ref[d:d + 1, lanes],
                 lam_ref[d:d + 1, lanes]) for d in range(2)]

        _to_split(lxc_ref, lanes, pad_ref, uf_ref, n_c)
        xc = _conv4(uf_ref[0:n_c, :], conv_w, conv_b)
        _gates(xc, *gate[0], af_ref, uf_ref)
        _gates(xc, *gate[1], ab_ref, ub_ref)
        ends = _scan_local(af_ref, uf_ref, ab_ref, ub_ref, n_c // s)
        _, _, cf, cb = _link_states(*ends, zero, zero)

        _to_split(lx_ref, lanes, pad_ref, uf_ref, n)
        xl = _conv4(uf_ref[...], conv_w, conv_b)
        _gates(xl, *gate[0], af_ref, uf_ref)
        _gates(xl, *gate[1], ab_ref, ub_ref)
        ends = _scan_local(af_ref, uf_ref, ab_ref, ub_ref, n_j)
        in_f, in_b, _, _ = _link_states(*ends, cf, cb)

        def finish(j, c):
            rows = pl.ds(pl.multiple_of(j * s, s), s)
            h = (uf_ref[rows, :] + af_ref[rows, :] * in_f) + (ub_ref[rows, :] + ab_ref[rows, :] * in_b)
            pad_ref[pl.ds(j, s, stride=pitch), :] = h
            return c

        lax.fori_loop(0, n_j, finish, 0, unroll=8)
        for q in range(s):
            o_ref[0, pl.ds(q * n_j, n_j), lanes] = pad_ref[pl.ds(q * pitch, n_j), :].astype(o_ref.dtype)


def _lru_call(lx3, lxc3, conv_w, conv_b, wa, wx, ba, bx, lam):
    bsz, seq, _ = lx3.shape
    n_ctx = lxc3.shape[1]
    per = LRU_BLOCKS_PER_STEP
    width = per * LRU_BLOCK
    col = lambda b, n: (b, 0, n)
    par = lambda b, n: (0, n)
    wspec = pl.BlockSpec((2, per, LRU_BLOCK, LRU_BLOCK), lambda b, n: (0, n, 0, 0))
    return pl.pallas_call(
        _lru_body,
        grid=(bsz, LRU_BLOCKS // per),
        in_specs=[pl.BlockSpec((1, seq, width), col),
                  pl.BlockSpec((1, n_ctx, width), col),
                  pl.BlockSpec((LRU_CONV, width), par),
                  pl.BlockSpec((1, width), par),
                  wspec, wspec,
                  pl.BlockSpec((2, width), par),
                  pl.BlockSpec((2, width), par),
                  pl.BlockSpec((2, width), par)],
        out_specs=pl.BlockSpec((1, seq, width), col),
        out_shape=jax.ShapeDtypeStruct((bsz, seq, LRU_WIDTH), BF16),
        scratch_shapes=[pltpu.VMEM((seq, LRU_BLOCK), F32)] * 4
                       + [pltpu.VMEM((seq + SUBLANES * PITCH_PAD, LRU_BLOCK), F32)],
        compiler_params=_cparams(("parallel", "arbitrary")),
        name="lru",
    )(lx3, lxc3, conv_w, conv_b, wa, wx, ba, bx, lam)


TOKEN_TILE = D_MODEL // LANES


def _store_token_tiles(ref, row0, x):
    m = x.shape[0]
    for c in range(TOKEN_TILE):
        ref[pl.ds(row0 + c, m, stride=TOKEN_TILE), :] = x[:, c * LANES:(c + 1) * LANES]


def _load_token_tiles(ref, row0, m):
    return jnp.concatenate([ref[pl.ds(row0 + c, m, stride=TOKEN_TILE), :] for c in range(TOKEN_TILE)], axis=1)


def _merge_body(x_ref, oa_ref, hs_ref, glu_ref, ga_ref, gb_ref, mod_ref, g_ref, wua_ref, wul_ref, wo_ref,
                wr_ref, br_ref, x1_ref, h2_ref, lt_ref):
    ga1 = mod_ref[0, :, 2 * D_MODEL:3 * D_MODEL]
    sh2 = mod_ref[0, :, 3 * D_MODEL:4 * D_MODEL]
    sc2 = mod_ref[0, :, 4 * D_MODEL:5 * D_MODEL]
    o_lru = hs_ref[...] * glu_ref[...]
    y = (ga_ref[...].astype(F32) * _dot(oa_ref[...], wua_ref[...])
         + gb_ref[...].astype(F32) * _dot(o_lru, wul_ref[...]))
    x1 = x_ref[...] + ga1 * _dot(y.astype(BF16), wo_ref[...])
    x1_ref[...] = x1
    h2 = _ada_norm(x1, g_ref[...], sc2, sh2)
    lt_ref[...] = _dot3_nt(wr_ref[...], h2) + br_ref[:, 0:1]
    _store_token_tiles(h2_ref, 0, h2)


def _merge_call(x2, o_att, hs, glu, ga, gb, mod3, g_ffn, wua, wul, wo, wr_t, br, seq):
    t = x2.shape[0]
    tm = MERGE_TM
    per_b = seq // tm
    row = lambda i: (i, 0)
    full = lambda i: (0, 0)
    resident = lambda shape: pl.BlockSpec(shape, full, pipeline_mode=pl.Buffered(1))
    return pl.pallas_call(
        _merge_body,
        grid=(t // tm,),
        in_specs=[pl.BlockSpec((tm, D_MODEL), row),
                  pl.BlockSpec((tm, NA_WIDTH), row),
                  pl.BlockSpec((tm, LRU_WIDTH), row),
                  pl.BlockSpec((tm, LRU_WIDTH), row),
                  pl.BlockSpec((tm, D_MODEL), row),
                  pl.BlockSpec((tm, D_MODEL), row),
                  pl.BlockSpec((1, 1, 6 * D_MODEL), lambda i: (i // per_b, 0, 0)),
                  pl.BlockSpec((1, D_MODEL), full),
                  resident((NA_WIDTH, D_MODEL)),
                  resident((LRU_WIDTH, D_MODEL)),
                  resident((D_MODEL, D_MODEL)),
                  pl.BlockSpec((ROUTE_ROWS, D_MODEL), full),
                  pl.BlockSpec((ROUTE_ROWS, LANES), full)],
        out_specs=[pl.BlockSpec((tm, D_MODEL), row),
                   pl.BlockSpec((tm * TOKEN_TILE, LANES), row),
                   pl.BlockSpec((ROUTE_ROWS, tm), lambda i: (0, i))],
        out_shape=[jax.ShapeDtypeStruct((t, D_MODEL), F32),
                   jax.ShapeDtypeStruct((t * TOKEN_TILE, LANES), F32),
                   jax.ShapeDtypeStruct((ROUTE_ROWS, t), F32)],
        compiler_params=_cparams(("parallel",)),
        name="merge",
    )(x2, o_att, hs, glu, ga, gb, mod3, g_ffn, wua, wul, wo, wr_t, br)


def _route_body(lt_ref, eid_ref, gate_ref, rank_ref, cnt_ref, carry_ref):
    step = pl.program_id(0)

    @pl.when(step == 0)
    def _():
        carry_ref[...] = jnp.zeros_like(carry_ref)

    tb = lt_ref.shape[1]
    lg = [lt_ref[r:r + 1, :] for r in range(N_GROUPS)]
    best = lg[0]
    gidx = jnp.zeros((1, tb), I32)
    for r in range(1, N_GROUPS):
        better = lg[r] > best
        gidx = jnp.where(better, r, gidx)
        best = jnp.maximum(best, lg[r])
    den = jnp.exp(lg[0] - best)
    for r in range(1, N_GROUPS):
        den = den + jnp.exp(lg[r] - best)
    p_top = 1.0 / den

    ev = []
    for j in range(EXPERTS_PER_GROUP):
        sel = lt_ref[N_GROUPS + j:N_GROUPS + j + 1, :]
        for g in range(1, N_GROUPS):
            row = N_GROUPS + g * EXPERTS_PER_GROUP + j
            sel = jnp.where(gidx == g, lt_ref[row:row + 1, :], sel)
        ev.append(sel)
    v0 = ev[0]
    i0 = jnp.zeros((1, tb), I32)
    for j in range(1, EXPERTS_PER_GROUP):
        better = ev[j] > v0
        i0 = jnp.where(better, j, i0)
        v0 = jnp.maximum(v0, ev[j])
    v1 = jnp.full((1, tb), -jnp.inf, F32)
    i1 = jnp.zeros((1, tb), I32)
    for j in range(EXPERTS_PER_GROUP):
        better = (ev[j] > v1) & (i0 != j)
        i1 = jnp.where(better, j, i1)
        v1 = jnp.where(better, ev[j], v1)
    e1 = jnp.exp(v1 - v0)
    inv = 1.0 / (1.0 + e1)
    eid0 = gidx * EXPERTS_PER_GROUP + i0
    eid1 = gidx * EXPERTS_PER_GROUP + i1
    eid_ref[0:1, :] = eid0
    eid_ref[1:2, :] = eid1
    gate_ref[...] = jnp.zeros_like(gate_ref)
    gate_ref[0:1, :] = p_top * inv
    gate_ref[1:2, :] = p_top * (e1 * inv)

    sub = 256
    e_iota = lax.broadcasted_iota(I32, (N_EXPERTS, sub), 0)
    tri = jnp.where(lax.broadcasted_iota(I32, (sub, sub), 0) <= lax.broadcasted_iota(I32, (sub, sub), 1),
                    1.0, 0.0).astype(BF16)
    carry = carry_ref[...]
    for c in range(tb // sub):
        sl = slice(c * sub, (c + 1) * sub)
        m0 = eid0[:, sl] == e_iota
        m1 = eid1[:, sl] == e_iota
        oh = jnp.where(m0 | m1, 1.0, 0.0)
        incl = _dot(oh.astype(BF16), tri)
        excl = incl - oh + carry[:, 0:1]
        rank_ref[0:1, sl] = jnp.sum(jnp.where(m0, excl, 0.0), axis=0, keepdims=True).astype(I32)
        rank_ref[1:2, sl] = jnp.sum(jnp.where(m1, excl, 0.0), axis=0, keepdims=True).astype(I32)
        carry = carry + incl[:, sub - 1:sub]
    carry_ref[...] = carry
    cnt_ref[...] = carry


def _route_call(logits_t):
    t = logits_t.shape[1]
    tb = ROUTE_TB
    col = lambda i: (0, i)
    return pl.pallas_call(
        _route_body,
        grid=(t // tb,),
        in_specs=[pl.BlockSpec((ROUTE_ROWS, tb), col)],
        out_specs=[pl.BlockSpec((TOP_K, tb), col), pl.BlockSpec((SUBLANES, tb), col),
                   pl.BlockSpec((TOP_K, tb), col), pl.BlockSpec((N_EXPERTS, LANES), lambda i: (0, 0))],
        out_shape=[jax.ShapeDtypeStruct((TOP_K, t), I32), jax.ShapeDtypeStruct((SUBLANES, t), F32),
                   jax.ShapeDtypeStruct((TOP_K, t), I32), jax.ShapeDtypeStruct((N_EXPERTS, LANES), F32)],
        scratch_shapes=[pltpu.VMEM((N_EXPERTS, LANES), F32)],
        compiler_params=_cparams(("arbitrary",)),
        name="route",
    )(logits_t)


def _dest_body(cnt_ref, eid_ref, rank_ref, dest_ref, blk_ref):
    cnt = cnt_ref[...].astype(I32)
    padded = ((cnt + (MOE_BLK - 1)) >> MOE_BLK_LOG2) << MOE_BLK_LOG2
    e_iota = lax.broadcasted_iota(I32, (N_EXPERTS, LANES), 0)
    p_end = jnp.zeros((N_EXPERTS, LANES), I32)
    for e in range(N_EXPERTS):
        tot = jnp.sum(jnp.where(e_iota <= e, padded, 0), axis=0, keepdims=True)
        p_end = jnp.where(e_iota == e, tot, p_end)
    p_start = p_end - padded
    tb = eid_ref.shape[1]
    ps = jnp.concatenate([p_start] * (tb // LANES), axis=1)
    e_wide = lax.broadcasted_iota(I32, (N_EXPERTS, tb), 0)
    for k in range(TOP_K):
        start = jnp.sum(jnp.where(eid_ref[k:k + 1, :] == e_wide, ps, 0), axis=0, keepdims=True)
        dest_ref[k:k + 1, :] = start + rank_ref[k:k + 1, :]
    nb = blk_ref.shape[1]
    pe = jnp.concatenate([p_end] * (nb // LANES), axis=1)
    first_row = lax.broadcasted_iota(I32, (N_EXPERTS, nb), 1) * MOE_BLK
    n_before = jnp.sum(jnp.where(pe <= first_row, 1, 0), axis=0, keepdims=True)
    blk = jnp.minimum(n_before, N_EXPERTS - 1)
    blk_ref[...] = jnp.broadcast_to(blk, blk_ref.shape)
    blk_ref[1:2, :] = jnp.broadcast_to(p_end[N_EXPERTS - 1:N_EXPERTS, 0:1] >> MOE_BLK_LOG2, (1, nb))
    on_diag = e_iota == lax.broadcasted_iota(I32, (N_EXPERTS, LANES), 1)
    blk_ref[2:3, 0:LANES] = jnp.sum(jnp.where(on_diag, p_end, 0), axis=0, keepdims=True)
    blk_ref[3:4, 0:LANES] = jnp.sum(jnp.where(on_diag, padded, 0), axis=0, keepdims=True)


def _dest_call(cnt, eid, rank, nb_pad):
    t = eid.shape[1]
    tb = ROUTE_TB
    col = lambda i: (0, i)
    return pl.pallas_call(
        _dest_body,
        grid=(t // tb,),
        in_specs=[pl.BlockSpec((N_EXPERTS, LANES), lambda i: (0, 0)),
                  pl.BlockSpec((TOP_K, tb), col), pl.BlockSpec((TOP_K, tb), col)],
        out_specs=[pl.BlockSpec((TOP_K, tb), col), pl.BlockSpec((SUBLANES, nb_pad), lambda i: (0, 0))],
        out_shape=[jax.ShapeDtypeStruct((TOP_K, t), I32), jax.ShapeDtypeStruct((SUBLANES, nb_pad), I32)],
        compiler_params=_cparams(("arbitrary",)),
        name="dest",
    )(cnt, eid, rank)


def _token_tile(ref, t):
    return ref.at[pl.ds(pl.multiple_of(t * TOKEN_TILE, TOKEN_TILE), TOKEN_TILE)]


def _dispatch_body(dest_ref, pend_ref, plen_ref, h_ref, xs_ref, zero_ref, sem, zsem):
    tm = h_ref.shape[0] // TOKEN_TILE
    blk_rows = MOE_BLK * TOKEN_TILE
    n_tok = pl.num_programs(0) * tm
    base = pl.program_id(0) * tm

    @pl.when(pl.program_id(0) == 0)
    def _():
        zero_ref[...] = jnp.zeros_like(zero_ref)

        def zero_block(start):
            rows = pl.ds(pl.multiple_of(start * TOKEN_TILE, blk_rows), blk_rows)
            return pltpu.make_async_copy(zero_ref, xs_ref.at[rows], zsem)

        def fill(e, c):
            @pl.when(plen_ref[e] > 0)
            def _():
                zero_block(pend_ref[e] - MOE_BLK).start()
            return c

        def drain(e, c):
            @pl.when(plen_ref[e] > 0)
            def _():
                zero_block(pend_ref[e] - MOE_BLK).wait()
            return c

        lax.fori_loop(0, N_EXPERTS, fill, 0)
        n_used = pend_ref[N_EXPERTS - 1] >> MOE_BLK_LOG2
        n_blk = xs_ref.shape[0] // blk_rows
        lax.fori_loop(n_used, n_blk, lambda j, c: (zero_block(j * MOE_BLK).start(), c)[1], 0)
        lax.fori_loop(0, N_EXPERTS, drain, 0)
        lax.fori_loop(n_used, n_blk, lambda j, c: (zero_block(j * MOE_BLK).wait(), c)[1], 0)

    def issue(r, c):
        for k in range(TOP_K):
            d = dest_ref[k * n_tok + base + r]
            pltpu.make_async_copy(_token_tile(h_ref, r), _token_tile(xs_ref, d), sem).start(priority=k)
        return c

    lax.fori_loop(0, tm, issue, 0, unroll=8)
    for k in range(TOP_K):
        pltpu.make_async_copy(h_ref, xs_ref.at[pl.ds(0, tm * TOKEN_TILE)], sem).wait()


def _dispatch_call(dest_flat, p_end, p_len, h2t, n_slots):
    rows = DISPATCH_TM * TOKEN_TILE
    return pl.pallas_call(
        _dispatch_body,
        grid_spec=pltpu.PrefetchScalarGridSpec(
            num_scalar_prefetch=3,
            grid=(h2t.shape[0] // rows,),
            in_specs=[pl.BlockSpec((rows, LANES), lambda i, d, pe, pn: (i, 0))],
            out_specs=pl.BlockSpec(memory_space=pl.ANY),
            scratch_shapes=[pltpu.VMEM((MOE_BLK * TOKEN_TILE, LANES), F32),
                            pltpu.SemaphoreType.DMA(()), pltpu.SemaphoreType.DMA(())]),
        out_shape=jax.ShapeDtypeStruct((n_slots * TOKEN_TILE, LANES), F32),
        compiler_params=_cparams(("arbitrary",)),
        name="dispatch",
    )(dest_flat, p_end, p_len, h2t)


EXPERT_BLKS_PER_STEP = 4


def _experts_body(blk_ref, used_ref, xs_ref, w1_hbm, w3_hbm, w2_hbm, y_ref,
                  f1_ref, f3_ref, f2_ref, b1_ref, b3_ref, b2_ref, loaded_ref, sem):
    i = pl.program_id(0)
    per = EXPERT_BLKS_PER_STEP
    used = used_ref[0]

    def weight_copies(e):
        return [pltpu.make_async_copy(w_hbm.at[e], f_ref, sem)
                for w_hbm, f_ref in ((w1_hbm, f1_ref), (w3_hbm, f3_ref), (w2_hbm, f2_ref))]

    @pl.when(i == 0)
    def _():
        loaded_ref[0] = -1
        for cp in weight_copies(blk_ref[0]):
            cp.start()

    for half in range(per):
        blk = per * i + half
        row0 = half * MOE_BLK * TOKEN_TILE
        e = blk_ref[blk]

        @pl.when((blk < used) & (e != loaded_ref[0]))
        def _():
            for cp in weight_copies(e):
                cp.wait()
            b1_ref[...] = f1_ref[...].astype(BF16)
            b3_ref[...] = f3_ref[...].astype(BF16)
            b2_ref[...] = f2_ref[...].astype(BF16)
            loaded_ref[0] = e
            nxt = lax.while_loop(lambda j: (j < used) & (blk_ref[jnp.minimum(j, blk_ref.shape[0] - 1)] == e),
                                 lambda j: j + 1, blk + 1)

            @pl.when(nxt < used)
            def _():
                for cp in weight_copies(blk_ref[jnp.minimum(nxt, blk_ref.shape[0] - 1)]):
                    cp.start()

        @pl.when(blk < used)
        def _():
            x = _load_token_tiles(xs_ref, row0, MOE_BLK).astype(BF16)
            g = _dot(x, b1_ref[...])
            u = _dot(x, b3_ref[...])
            mid = (g * _sigmoid(g)) * u
            _store_token_tiles(y_ref, row0, _dot(mid.astype(BF16), b2_ref[...]))

        @pl.when(blk >= used)
        def _():
            y_ref[pl.ds(row0, MOE_BLK * TOKEN_TILE), :] = jnp.zeros((MOE_BLK * TOKEN_TILE, LANES), F32)


def _experts_call(blk_e, n_used, xs, w1, w3, w2):
    per = EXPERT_BLKS_PER_STEP
    rows = per * MOE_BLK * TOKEN_TILE
    nb = xs.shape[0] // (MOE_BLK * TOKEN_TILE)
    assert nb % per == 0 and blk_e.shape[0] == nb
    hbm = pl.BlockSpec(memory_space=pl.ANY)
    return pl.pallas_call(
        _experts_body,
        grid_spec=pltpu.PrefetchScalarGridSpec(
            num_scalar_prefetch=2,
            grid=(nb // per,),
            in_specs=[pl.BlockSpec((rows, LANES), lambda i, blk, used: (i, 0)), hbm, hbm, hbm],
            out_specs=pl.BlockSpec((rows, LANES), lambda i, blk, used: (i, 0)),
            scratch_shapes=[pltpu.VMEM((D_MODEL, D_EXPERT), F32), pltpu.VMEM((D_MODEL, D_EXPERT), F32),
                            pltpu.VMEM((D_EXPERT, D_MODEL), F32),
                            pltpu.VMEM((D_MODEL, D_EXPERT), BF16), pltpu.VMEM((D_MODEL, D_EXPERT), BF16),
                            pltpu.VMEM((D_EXPERT, D_MODEL), BF16),
                            pltpu.SMEM((1,), I32), pltpu.SemaphoreType.DMA(())]),
        out_shape=jax.ShapeDtypeStruct(xs.shape, F32),
        compiler_params=_cparams(("arbitrary",)),
        name="experts",
    )(blk_e, n_used, xs, w1, w3, w2)


def _combine_body(dest_ref, x1_ref, gate_ref, mod_ref, gf_ref, y_ref, o_ref, buf_ref, sem):
    tm = x1_ref.shape[0]
    step = pl.program_id(0)
    n_step = pl.num_programs(0)
    n_tok = n_step * tm
    slot = step % 2
    region = tm * TOKEN_TILE

    def region_row0(buf, k):
        return pl.multiple_of((buf * TOP_K + k) * region, region)

    def start_gather(for_step):
        def issue(r, c):
            for k in range(TOP_K):
                d = dest_ref[k * n_tok + for_step * tm + r]
                dst = buf_ref.at[pl.ds(pl.multiple_of(region_row0(for_step % 2, k) + r * TOKEN_TILE, TOKEN_TILE),
                                       TOKEN_TILE)]
                pltpu.make_async_copy(_token_tile(y_ref, d), dst, sem.at[for_step % 2]).start(priority=k)
            return c

        lax.fori_loop(0, tm, issue, 0, unroll=8)

    @pl.when(step == 0)
    def _():
        start_gather(step)

    @pl.when(step + 1 < n_step)
    def _():
        start_gather(step + 1)

    eye = jnp.where(lax.broadcasted_iota(I32, (tm, tm), 0) == lax.broadcasted_iota(I32, (tm, tm), 1),
                    1.0, 0.0).astype(BF16)
    g = gate_ref[...]
    g1 = g.astype(BF16)
    rem = g - g1.astype(F32)
    g2 = rem.astype(BF16)
    g3 = (rem - g2.astype(F32)).astype(BF16)
    gt = _dot_nt(eye, g1) + (_dot_nt(eye, g2) + _dot_nt(eye, g3))

    for k in range(TOP_K):
        pltpu.make_async_copy(y_ref.at[pl.ds(0, region)], buf_ref.at[pl.ds(region_row0(slot, k), region)],
                              sem.at[slot]).wait()

    ga2 = mod_ref[0, :, 5 * D_MODEL:6 * D_MODEL]
    moe = (gt[:, 0:1] * _load_token_tiles(buf_ref, region_row0(slot, 0), tm)
           + gt[:, 1:2] * _load_token_tiles(buf_ref, region_row0(slot, 1), tm))
    x2 = x1_ref[...] + ga2 * moe
    ms = jnp.mean(x2 * x2, axis=-1, keepdims=True)
    o_ref[...] = x2 * lax.rsqrt(ms + EPS) * gf_ref[...]


def _combine_call(dest_flat, x1, gate, mod3, g_final, y, seq):
    t = x1.shape[0]
    tm = COMBINE_TM
    per_b = seq // tm
    return pl.pallas_call(
        _combine_body,
        grid_spec=pltpu.PrefetchScalarGridSpec(
            num_scalar_prefetch=1,
            grid=(t // tm,),
            in_specs=[pl.BlockSpec((tm, D_MODEL), lambda i, d: (i, 0)),
                      pl.BlockSpec((SUBLANES, tm), lambda i, d: (0, i)),
                      pl.BlockSpec((1, 1, 6 * D_MODEL), lambda i, d: (i // per_b, 0, 0)),
                      pl.BlockSpec((1, D_MODEL), lambda i, d: (0, 0)),
                      pl.BlockSpec(memory_space=pl.ANY)],
            out_specs=pl.BlockSpec((tm, D_MODEL), lambda i, d: (i, 0)),
            scratch_shapes=[pltpu.VMEM((2 * TOP_K * tm * TOKEN_TILE, LANES), F32),
                            pltpu.SemaphoreType.DMA((2,))]),
        out_shape=jax.ShapeDtypeStruct((t, D_MODEL), F32),
        compiler_params=_cparams(("arbitrary",)),
        name="combine",
    )(dest_flat, x1, gate, mod3, g_final, y)


def _rope_tables(seq):
    half = NA_HEAD_DIM // 2
    nf = half // 2
    inv_freq = ROPE_THETA ** (-jnp.arange(nf, dtype=F32) / nf)
    t = jnp.arange(seq)
    row_pos = (t // GRID_W).astype(F32)
    col_pos = (t % GRID_W).astype(F32)
    ang_r = row_pos[:, None] * inv_freq
    ang_c = col_pos[:, None] * inv_freq
    cos = jnp.concatenate([jnp.cos(ang_r), jnp.cos(ang_r), jnp.cos(ang_c), jnp.cos(ang_c)], axis=-1)
    sin = jnp.concatenate([-jnp.sin(ang_r), jnp.sin(ang_r), -jnp.sin(ang_c), jnp.sin(ang_c)], axis=-1)
    return jnp.tile(cos, (1, NA_HEADS)), jnp.tile(sin, (1, NA_HEADS))


def _layer(x, c, ctx, c_ctx, w_mod, b_mod, g_mix, g_ffn, w_in, rpb, conv_w, conv_b, lru_wa, lru_ba,
           lru_wx, lru_bx, lru_lambda, w_up_attn, w_up_lru, w_out, wg, bg, we, be, w1, w3, w2, g_final):
    bsz, seq, d = x.shape
    n_ctx = ctx.shape[1]
    t = bsz * seq
    assert d == D_MODEL and seq % ATT_TQ == 0 and seq // ATT_TQ > K_ROW_BLOCKS and n_ctx % ATT_TQ == 0
    assert bsz + 1 <= MOD_ROWS and seq % PROJ_TM == 0 and seq % MERGE_TM == 0
    assert t % ROUTE_TB == 0 and t % COMBINE_TM == 0 and t % DISPATCH_TM == 0
    assert (bsz * n_ctx) % PROJ_TM == 0 and n_ctx <= seq
    assert seq % (SUBLANES * SUBLANES) == 0 and n_ctx % (SUBLANES * SUBLANES) == 0
    assert (seq // SUBLANES) % SCAN_STEPS == 0 and (n_ctx // SUBLANES) % SCAN_STEPS == 0

    cc = jnp.concatenate([c, c_ctx[None, :], jnp.zeros((MOD_ROWS - bsz - 1, d), F32)], axis=0)
    mod = _mod_call(cc, w_mod, b_mod)
    mod3 = mod[:bsz].reshape(bsz, 1, 6 * d)
    mod_c = mod[bsz:bsz + 1]

    x2 = x.reshape(t, d)
    g_mix2 = g_mix.reshape(1, d)
    w_in_bf = w_in.astype(BF16)
    kc, vc, lxc = _ctxproj_call(ctx.reshape(bsz * n_ctx, d), mod_c, g_mix2, w_in_bf[:, :CTX_COLS])
    cos_t, sin_t = _rope_tables(seq)
    qre, qro, qpe, qpo, k, v, lx, glu, ga, gb = _inproj_call(x2, mod3, g_mix2, w_in_bf, cos_t, sin_t, seq)

    bias = _bias_tables(_rpbcol_call(rpb))
    o_att = _attn_call(qre, qro, qpe, qpo, k, v, kc, vc, bias, bsz, seq, n_ctx)

    hs = _lru_call(lx.reshape(bsz, seq, LRU_WIDTH), lxc.reshape(bsz, n_ctx, LRU_WIDTH),
                   conv_w, conv_b.reshape(1, LRU_WIDTH), lru_wa, lru_wx, lru_ba, lru_bx, lru_lambda)

    wr_t = jnp.concatenate([wg.T, we.T, jnp.zeros((ROUTE_ROWS - N_GROUPS - N_EXPERTS, d), F32)], axis=0)
    br = jnp.concatenate([bg, be, jnp.zeros((ROUTE_ROWS - N_GROUPS - N_EXPERTS,), F32)])
    br = jnp.broadcast_to(br[:, None], (ROUTE_ROWS, LANES))
    x1, h2, logits_t = _merge_call(x2, o_att, hs.reshape(t, LRU_WIDTH), glu, ga, gb, mod3,
                                   g_ffn.reshape(1, d), w_up_attn.astype(BF16), w_up_lru.astype(BF16),
                                   w_out.astype(BF16), wr_t, br, seq)

    eid, gate, rank, cnt = _route_call(logits_t)
    n_blk = -(-(t * TOP_K + N_EXPERTS * (MOE_BLK - 1)) // MOE_BLK)
    n_blk = -(-n_blk // EXPERT_BLKS_PER_STEP) * EXPERT_BLKS_PER_STEP
    nb_pad = -(-n_blk // LANES) * LANES
    dest, blk = _dest_call(cnt, eid, rank, nb_pad)
    dest_flat = dest.reshape(TOP_K * t)
    xs = _dispatch_call(dest_flat, blk[2, :N_EXPERTS], blk[3, :N_EXPERTS], h2, n_blk * MOE_BLK)
    y = _experts_call(blk[0, :n_blk], blk[1, :1], xs, w1, w3, w2)
    return _combine_call(dest_flat, x1, gate, mod3, g_final.reshape(1, d), y, seq).reshape(bsz, seq, d)


def kernel(x, c, ctx, c_ctx, w_mod, b_mod, g_mix, g_ffn, w_in, rpb, conv_w, conv_b, lru_wa, lru_ba, lru_wx,
           lru_bx, lru_lambda, w_up_attn, w_up_lru, w_out, router_group_w, router_group_b, router_expert_w,
           router_expert_b, expert_w_gate, expert_w_up, expert_w_down, g_final):
    assert w_mod.shape[0] == 1, "single-layer block"
    return _layer(x, c, ctx, c_ctx, w_mod[0], b_mod[0], g_mix[0], g_ffn[0], w_in[0], rpb[0], conv_w[0],
                  conv_b[0], lru_wa[0], lru_ba[0], lru_wx[0], lru_bx[0], lru_lambda[0], w_up_attn[0],
                  w_up_lru[0], w_out[0], router_group_w[0], router_group_b[0], router_expert_w[0],
                  router_expert_b[0], expert_w_gate[0], expert_w_up[0], expert_w_down[0], g_final)
```

```python
import functools

import numpy as np
import jax
import jax.numpy as jnp
from jax import lax
from jax.experimental import pallas as pl
from jax.experimental.pallas import tpu as pltpu

F32 = jnp.float32
BF16 = jnp.bfloat16
I32 = jnp.int32
U32 = jnp.uint32

D_MODEL = 1024
GRID_W = 64
EPS = 1e-6
NEG_INF = -1e30

NA_HEADS = 8
NA_HEAD_DIM = 64
NA_WIDTH = NA_HEADS * NA_HEAD_DIM
NA_WIN_ROWS = 8
NA_WIN_COLS = 16
ROPE_THETA = 10000.0

LRU_WIDTH = D_MODEL
LRU_BLOCKS = 8
LRU_BLOCK = LRU_WIDTH // LRU_BLOCKS
LRU_CONV = 4
LRU_C = 8.0

N_GROUPS = 4
EXPERTS_PER_GROUP = 8
N_EXPERTS = N_GROUPS * EXPERTS_PER_GROUP
TOP_K = 2
D_EXPERT = 512

K_OFF = 0
V_OFF = K_OFF + NA_WIDTH
LX_OFF = V_OFF + NA_WIDTH
CTX_COLS = LX_OFF + LRU_WIDTH
Q_OFF = CTX_COLS
LG_OFF = Q_OFF + NA_WIDTH
GA_OFF = LG_OFF + LRU_WIDTH
GB_OFF = GA_OFF + D_MODEL
PROJ_COLS = GB_OFF + D_MODEL

LANES = 128
SUBLANES = 8

Q_ROWS = 4
K_ROW_BLOCKS = 3
ATT_TQ = Q_ROWS * GRID_W
ATT_TK = K_ROW_BLOCKS * ATT_TQ

MOE_BLK_LOG2 = 8
MOE_BLK = 1 << MOE_BLK_LOG2
MOD_ROWS = 24
ROUTE_ROWS = 64

PROJ_TM = 512
MERGE_TM = 512
ROUTE_TB = 2048
DISPATCH_TM = 2048
COMBINE_TM = 512

VMEM_LIMIT = 56 * 1024 * 1024


def _cparams(sem, vmem=VMEM_LIMIT):
    return pltpu.CompilerParams(dimension_semantics=sem, vmem_limit_bytes=vmem)


def _dot(a, b):
    return jnp.dot(a, b, preferred_element_type=F32)


def _dot_nt(a, b):
    return lax.dot_general(a, b, (((1,), (1,)), ((), ())), preferred_element_type=F32)


def _split2(a):
    hi = a.astype(BF16)
    lo = (a - hi.astype(F32)).astype(BF16)
    return hi, lo


def _dot3(a, b):
    ah, al = _split2(a)
    bh, bl = _split2(b)
    return _dot(ah, bh) + (_dot(ah, bl) + _dot(al, bh))


def _dot3_nt(a, b):
    ah, al = _split2(a)
    bh, bl = _split2(b)
    return _dot_nt(ah, bh) + (_dot_nt(ah, bl) + _dot_nt(al, bh))


def _sigmoid(x):
    return 1.0 / (1.0 + jnp.exp(-x))


def _ada_norm(x, g, sc, sh):
    ms = jnp.mean(x * x, axis=-1, keepdims=True)
    return (x * lax.rsqrt(ms + EPS) * g) * (1.0 + sc) + sh


def _mod_body(cc_ref, w_ref, b_ref, o_ref):
    cc = cc_ref[...]
    o_ref[...] = _dot3(cc * _sigmoid(cc), w_ref[...]) + b_ref[...]


def _mod_call(cc, w_mod, b_mod):
    n = w_mod.shape[1]
    bn = 1024
    return pl.pallas_call(
        _mod_body,
        grid=(n // bn,),
        in_specs=[pl.BlockSpec((MOD_ROWS, D_MODEL), lambda j: (0, 0)),
                  pl.BlockSpec((D_MODEL, bn), lambda j: (0, j)),
                  pl.BlockSpec((1, bn), lambda j: (0, j))],
        out_specs=pl.BlockSpec((MOD_ROWS, bn), lambda j: (0, j)),
        out_shape=jax.ShapeDtypeStruct((MOD_ROWS, n), F32),
        compiler_params=_cparams(("arbitrary",)),
        name="mod",
    )(cc, w_mod, b_mod.reshape(1, n))


def _rope(t, cos, sin):
    lane = lax.broadcasted_iota(I32, (t.shape[0], LANES), 1)
    first = (lane & 16) == 0
    parts = []
    for c in range(t.shape[1] // LANES):
        tc = t[:, c * LANES:(c + 1) * LANES]
        parts.append(jnp.where(first, pltpu.roll(tc, LANES - 16, 1), pltpu.roll(tc, 16, 1)))
    partner = jnp.concatenate(parts, axis=1)
    return t * cos + partner * sin


def _gelu_tanh(x):
    return 0.5 * x * (1.0 + jnp.tanh(0.7978845608028654 * (x + 0.044715 * (x * x * x))))


def _inproj_body(x_ref, mod_ref, g_ref, w_ref, cos_ref, sin_ref,
                 qre_ref, qro_ref, qpe_ref, qpo_ref, k_ref, v_ref, lx_ref, glu_ref, ga_ref, gb_ref):
    sh = mod_ref[0, :, 0:D_MODEL]
    sc = mod_ref[0, :, D_MODEL:2 * D_MODEL]
    h = _ada_norm(x_ref[...], g_ref[...], sc, sh).astype(BF16)
    cos = cos_ref[...]
    sin = sin_ref[...]
    scale = NA_HEAD_DIM ** -0.5

    k_ref[...] = _rope(_dot(h, w_ref[:, K_OFF:K_OFF + NA_WIDTH]), cos, sin).astype(BF16)
    v_ref[...] = _dot(h, w_ref[:, V_OFF:V_OFF + NA_WIDTH]).astype(BF16)
    lx_ref[...] = _dot(h, w_ref[:, LX_OFF:LX_OFF + LRU_WIDTH])

    q = _dot(h, w_ref[:, Q_OFF:Q_OFF + NA_WIDTH]) * scale
    qr = _rope(q, cos, sin)
    lane = lax.broadcasted_iota(I32, q.shape, 1)
    even = (lane & NA_HEAD_DIM) == 0
    qre_ref[...] = jnp.where(even, qr, 0.0).astype(BF16)
    qro_ref[...] = jnp.where(even, 0.0, qr).astype(BF16)
    qpe_ref[...] = jnp.where(even, q, 0.0).astype(BF16)
    qpo_ref[...] = jnp.where(even, 0.0, q).astype(BF16)

    glu_ref[...] = _gelu_tanh(_dot(h, w_ref[:, LG_OFF:LG_OFF + LRU_WIDTH])).astype(BF16)
    ga_ref[...] = _sigmoid(_dot(h, w_ref[:, GA_OFF:GA_OFF + D_MODEL])).astype(BF16)
    gb_ref[...] = _sigmoid(_dot(h, w_ref[:, GB_OFF:GB_OFF + D_MODEL])).astype(BF16)


def _inproj_call(x2, mod3, g_mix, w_in_bf, cos_t, sin_t, seq):
    t = x2.shape[0]
    tm = PROJ_TM
    per_b = seq // tm
    row = lambda i: (i, 0)
    wide = lambda n, dt: jax.ShapeDtypeStruct((t, n), dt)
    return pl.pallas_call(
        _inproj_body,
        grid=(t // tm,),
        in_specs=[pl.BlockSpec((tm, D_MODEL), row),
                  pl.BlockSpec((1, 1, 6 * D_MODEL), lambda i: (i // per_b, 0, 0)),
                  pl.BlockSpec((1, D_MODEL), lambda i: (0, 0)),
                  pl.BlockSpec((D_MODEL, PROJ_COLS), lambda i: (0, 0), pipeline_mode=pl.Buffered(1)),
                  pl.BlockSpec((tm, NA_WIDTH), lambda i: (i % per_b, 0)),
                  pl.BlockSpec((tm, NA_WIDTH), lambda i: (i % per_b, 0))],
        out_specs=[pl.BlockSpec((tm, NA_WIDTH), row)] * 6
                  + [pl.BlockSpec((tm, LRU_WIDTH), row)] * 4,
        out_shape=[wide(NA_WIDTH, BF16)] * 6
                  + [wide(LRU_WIDTH, F32), wide(LRU_WIDTH, BF16), wide(D_MODEL, BF16), wide(D_MODEL, BF16)],
        compiler_params=_cparams(("parallel",)),
        name="inproj",
    )(x2, mod3, g_mix, w_in_bf, cos_t, sin_t)


def _ctxproj_body(x_ref, mod_ref, g_ref, w_ref, k_ref, v_ref, lx_ref):
    sh = mod_ref[:, 0:D_MODEL]
    sc = mod_ref[:, D_MODEL:2 * D_MODEL]
    h = _ada_norm(x_ref[...], g_ref[...], sc, sh).astype(BF16)
    k_ref[...] = _dot(h, w_ref[:, K_OFF:K_OFF + NA_WIDTH]).astype(BF16)
    v_ref[...] = _dot(h, w_ref[:, V_OFF:V_OFF + NA_WIDTH]).astype(BF16)
    lx_ref[...] = _dot(h, w_ref[:, LX_OFF:LX_OFF + LRU_WIDTH])


def _ctxproj_call(c2, mod_c, g_mix, w_ctx_bf):
    t = c2.shape[0]
    tm = PROJ_TM
    row = lambda i: (i, 0)
    return pl.pallas_call(
        _ctxproj_body,
        grid=(t // tm,),
        in_specs=[pl.BlockSpec((tm, D_MODEL), row),
                  pl.BlockSpec((1, 6 * D_MODEL), lambda i: (0, 0)),
                  pl.BlockSpec((1, D_MODEL), lambda i: (0, 0)),
                  pl.BlockSpec((D_MODEL, CTX_COLS), lambda i: (0, 0))],
        out_specs=[pl.BlockSpec((tm, NA_WIDTH), row), pl.BlockSpec((tm, NA_WIDTH), row),
                   pl.BlockSpec((tm, LRU_WIDTH), row)],
        out_shape=[jax.ShapeDtypeStruct((t, NA_WIDTH), BF16), jax.ShapeDtypeStruct((t, NA_WIDTH), BF16),
                   jax.ShapeDtypeStruct((t, LRU_WIDTH), F32)],
        compiler_params=_cparams(("parallel",)),
        name="ctxproj",
    )(c2, mod_c, g_mix, w_ctx_bf)


N_DR = 2 * NA_WIN_ROWS - 1
N_DC = 2 * NA_WIN_COLS - 1


def _rpbcol_body(rpb_ref, o_ref):
    n = GRID_W * GRID_W
    flat = lax.broadcasted_iota(I32, (32, n), 1)
    qc = flat >> 6
    kc = flat & (GRID_W - 1)
    dc = jnp.clip(kc - qc, 1 - NA_WIN_COLS, NA_WIN_COLS - 1) + (NA_WIN_COLS - 1)
    d_iota = lax.broadcasted_iota(I32, (32, n), 0)
    onehot = jnp.where(dc == d_iota, 1.0, 0.0).astype(BF16)
    r = rpb_ref[...]
    r1 = r.astype(BF16)
    rem = r - r1.astype(F32)
    r2 = rem.astype(BF16)
    r3 = (rem - r2.astype(F32)).astype(BF16)
    val = _dot(r1, onehot) + (_dot(r2, onehot) + _dot(r3, onehot))
    qc1 = qc[0:1, :]
    kc1 = kc[0:1, :]
    c_start = jnp.clip(qc1 - NA_WIN_COLS // 2, 0, GRID_W - NA_WIN_COLS)
    band = (kc1 >= c_start) & (kc1 < c_start + NA_WIN_COLS)
    o_ref[...] = jnp.where(band, val, NEG_INF)


def _rpbcol_call(rpb):
    rows = NA_HEADS * N_DR
    r2 = jnp.pad(rpb.reshape(rows, N_DC), ((0, 0), (0, 32 - N_DC)))
    n = GRID_W * GRID_W
    return pl.pallas_call(
        _rpbcol_body,
        in_specs=[pl.BlockSpec((rows, 32), lambda: (0, 0))],
        out_specs=pl.BlockSpec((rows, n), lambda: (0, 0)),
        out_shape=jax.ShapeDtypeStruct((rows, n), F32),
        name="rpbcol",
    )(r2)


def _bias_tables(rpbcol):
    t = rpbcol.reshape(NA_HEADS, N_DR, GRID_W, GRID_W)
    neg = jnp.full((NA_HEADS, GRID_W, GRID_W), NEG_INF, F32)
    n_kj = K_ROW_BLOCKS * Q_ROWS
    classes = []
    for lo_fn, dr_off in ((lambda ri: 0, 7), (lambda ri: ri, 3), (lambda ri: 4, -1)):
        rows = []
        for ri in range(Q_ROWS):
            lo = lo_fn(ri)
            blocks = []
            for kj in range(n_kj):
                inside = lo <= kj < lo + NA_WIN_ROWS
                blocks.append(t[:, kj - ri + dr_off] if inside else neg)
            rows.append(jnp.concatenate(blocks, axis=2))
        classes.append(jnp.concatenate(rows, axis=1))
    return jnp.stack(classes, axis=0)


def _attn_body(qre_ref, qro_ref, qpe_ref, qpo_ref, k0_ref, k1_ref, k2_ref, v0_ref, v1_ref, v2_ref,
               kc_ref, vc_ref, bias_ref, o_ref):
    lane = lax.broadcasted_iota(I32, (ATT_TQ, LANES), 1)
    lane1 = lax.broadcasted_iota(I32, (1, LANES), 1)
    head_lanes = [jnp.where(lane1 < NA_HEAD_DIM, 1.0, 0.0).astype(BF16),
                  jnp.where(lane1 < NA_HEAD_DIM, 0.0, 1.0).astype(BF16)]
    k_refs = (k0_ref, k1_ref, k2_ref)
    v_refs = (v0_ref, v1_ref, v2_ref)
    for bi, p in [(bi, p) for bi in range(ATT_BATCH) for p in range(NA_HEADS // 2)]:
        sl = slice(p * LANES, (p + 1) * LANES)
        k_lat = jnp.concatenate([r[bi, :, sl] for r in k_refs], axis=0)
        kc = kc_ref[bi, :, sl]
        v_all = jnp.concatenate([r[bi, :, sl] for r in v_refs] + [vc_ref[bi, :, sl]], axis=0)
        outs = []
        for hh, (qr_ref, qp_ref) in enumerate(((qre_ref, qpe_ref), (qro_ref, qpo_ref))):
            h = 2 * p + hh
            s_lat = _dot_nt(qr_ref[bi, :, sl], k_lat) + bias_ref[0, h]
            s_ctx = _dot_nt(qp_ref[bi, :, sl], kc)
            tiles = ([s_lat[:, j * ATT_TQ:(j + 1) * ATT_TQ] for j in range(K_ROW_BLOCKS)]
                     + [s_ctx[:, j * ATT_TQ:(j + 1) * ATT_TQ] for j in range(s_ctx.shape[1] // ATT_TQ)])
            m = functools.reduce(jnp.maximum, tiles).max(axis=-1, keepdims=True)
            prob = jnp.exp(jnp.concatenate([s_lat, s_ctx], axis=1) - m).astype(BF16)
            mine = head_lanes[hh]
            acc = _dot(prob, v_all * mine + (1.0 - mine).astype(BF16))
            outs.append(acc / pltpu.roll(acc, NA_HEAD_DIM, 1))
        o_ref[bi, :, sl] = jnp.where(lane < NA_HEAD_DIM, outs[0], outs[1]).astype(BF16)


ATT_BATCH = 4


def _attn_call(qre, qro, qpe, qpo, k, v, kc, vc, bias, bsz, seq, n_ctx):
    n_grp = seq // ATT_TQ
    max_kb = n_grp - K_ROW_BLOCKS
    assert bsz % ATT_BATCH == 0
    by_batch = lambda a, n: a.reshape(bsz, n, NA_WIDTH)

    def qmap(g, b):
        return (b, g, 0)

    def kmap(j):
        return lambda g, b: (b, jnp.clip(g - 1, 0, max_kb) + j, 0)

    def cls(g, b):
        return (jnp.where(g == 0, 0, jnp.where(g == n_grp - 1, 2, 1)), 0, 0, 0)

    qspec = pl.BlockSpec((ATT_BATCH, ATT_TQ, NA_WIDTH), qmap)
    cspec = pl.BlockSpec((ATT_BATCH, n_ctx, NA_WIDTH), lambda g, b: (b, 0, 0))
    k3, v3 = by_batch(k, seq), by_batch(v, seq)
    return pl.pallas_call(
        _attn_body,
        grid=(n_grp, bsz // ATT_BATCH),
        in_specs=[qspec] * 4
                 + [pl.BlockSpec((ATT_BATCH, ATT_TQ, NA_WIDTH), kmap(j)) for j in range(K_ROW_BLOCKS)] * 2
                 + [cspec, cspec, pl.BlockSpec((1, NA_HEADS, ATT_TQ, ATT_TK), cls)],
        out_specs=qspec,
        out_shape=jax.ShapeDtypeStruct((bsz, seq, NA_WIDTH), BF16),
        compiler_params=_cparams(("arbitrary", "arbitrary")),
        name="attn",
    )(by_batch(qre, seq), by_batch(qro, seq), by_batch(qpe, seq), by_batch(qpo, seq), k3, k3, k3, v3, v3, v3,
      by_batch(kc, n_ctx), by_batch(vc, n_ctx), bias).reshape(bsz * seq, NA_WIDTH)


def _shift_down(v, row):
    return jnp.where(row >= 1, pltpu.roll(v, 1, 0), 0.0)


def _shift_up(v, row):
    return jnp.where(row < SUBLANES - 1, pltpu.roll(v, SUBLANES - 1, 0), 0.0)


def _conv4(x, w, b):
    n = x.shape[0]
    s = SUBLANES
    row = lax.broadcasted_iota(I32, (s, LANES), 0)
    last = _shift_down(x[n - s:n], row)
    last2 = _shift_down(x[n - 2 * s:n - s], row)
    first = _shift_up(x[0:s], row)
    xm1 = jnp.concatenate([last, x[0:n - s]], axis=0)
    xm2 = jnp.concatenate([last2, last, x[0:n - 2 * s]], axis=0)
    xp1 = jnp.concatenate([x[s:n], first], axis=0)
    return (w[0:1, :] * xm2 + w[1:2, :] * xm1 + w[2:3, :] * x + w[3:4, :] * xp1) + b


def _softplus(z):
    return jnp.maximum(z, 0.0) + jnp.log1p(jnp.exp(-jnp.abs(z)))


def _gates(xc, wa, wx, ba, bx, lam, a_ref, u_ref):
    n = xc.shape[0]
    xb = xc.astype(BF16)
    tr = jnp.tanh(_dot(xb, (0.5 * wa).astype(BF16)) + 0.5 * ba)
    ti = jnp.tanh(_dot(xb, (0.5 * wx).astype(BF16)) + 0.5 * bx)
    half_c = (0.5 * LRU_C) * _softplus(-lam)
    neg_log_a = half_c * tr + half_c
    a = jnp.exp(-neg_log_a)
    a_ref[0:n, :] = a
    s2 = jnp.tanh(neg_log_a) * (a * a + 1.0)
    root = jnp.where(s2 > 0.0, s2 * lax.rsqrt(s2), 0.0)
    xh = 0.5 * xc
    u_ref[0:n, :] = root * (xh * ti + xh)


def _scan4(a, u, h, p):
    a01 = a[1] * a[0]
    u01 = a[1] * u[0] + u[1]
    a23 = a[3] * a[2]
    u23 = a[3] * u[2] + u[3]
    a012 = a[2] * a01
    u012 = a[2] * u01 + u[2]
    a0123 = a23 * a01
    u0123 = a23 * u01 + u23
    hs = [a[0] * h + u[0], a01 * h + u01, a012 * h + u012, a0123 * h + u0123]
    ps = [a[0] * p, a01 * p, a012 * p, a0123 * p]
    return hs, ps


SCAN_STEPS = 4


def _scan_local(af_ref, uf_ref, ab_ref, ub_ref, n_vreg):
    s = SUBLANES
    zero = jnp.zeros((s, LANES), F32)
    one = jnp.ones((s, LANES), F32)
    span = SCAN_STEPS * s

    def body(q, carry):
        hf, pf, hb, pb = carry
        base = pl.multiple_of(q * span, span)
        rows = [pl.ds(base + i * s, s) for i in range(SCAN_STEPS)]
        hs, ps = _scan4([af_ref[r, :] for r in rows], [uf_ref[r, :] for r in rows], hf, pf)
        for r, h, p in zip(rows, hs, ps):
            uf_ref[r, :] = h
            af_ref[r, :] = p
        hf, pf = hs[-1], ps[-1]
        base = pl.multiple_of((n_vreg - SCAN_STEPS) * s - q * span, span)
        rows = [pl.ds(base + (SCAN_STEPS - 1 - i) * s, s) for i in range(SCAN_STEPS)]
        hs, ps = _scan4([ab_ref[r, :] for r in rows], [ub_ref[r, :] for r in rows], hb, pb)
        for r, h, p in zip(rows, hs, ps):
            ub_ref[r, :] = h
            ab_ref[r, :] = p
        return hf, pf, hs[-1], ps[-1]

    return lax.fori_loop(0, n_vreg // SCAN_STEPS, body, (zero, one, zero, one), unroll=2)


def _link_states(hf, pf, hb, pb, h0f, h0b):
    s = SUBLANES
    row = lax.broadcasted_iota(I32, (s, LANES), 0)
    a, u = pf, hf
    for k in (1, 2, 4):
        keep = row >= k
        u = u + a * jnp.where(keep, pltpu.roll(u, k, 0), 0.0)
        a = a * jnp.where(keep, pltpu.roll(a, k, 0), 1.0)
    end_f = u + a * h0f
    in_f = jnp.where(row >= 1, pltpu.roll(end_f, 1, 0), h0f)
    a, u = pb, hb
    for k in (1, 2, 4):
        keep = row < s - k
        u = u + a * jnp.where(keep, pltpu.roll(u, s - k, 0), 0.0)
        a = a * jnp.where(keep, pltpu.roll(a, s - k, 0), 1.0)
    end_b = u + a * h0b
    in_b = jnp.where(row < s - 1, pltpu.roll(end_b, s - 1, 0), h0b)
    return in_f, in_b, end_f[s - 1:s, :], end_b[0:1, :]


PITCH_PAD = 4


def _to_split(x_ref, lanes, pad_ref, dst_ref, n):
    s = SUBLANES
    n_j = n // s
    pitch = n_j + PITCH_PAD
    for q in range(s):
        pad_ref[pl.ds(q * pitch, n_j), :] = x_ref[0, pl.ds(q * n_j, n_j), lanes]

    def body(j, c):
        dst_ref[pl.ds(pl.multiple_of(j * s, s), s), :] = pad_ref[pl.ds(j, s, stride=pitch), :]
        return c

    lax.fori_loop(0, n_j, body, 0, unroll=8)


LRU_BLOCKS_PER_STEP = 4


def _lru_body(lx_ref, lxc_ref, cw_ref, cb_ref, wa_ref, wx_ref, ba_ref, bx_ref, lam_ref,
              o_ref, af_ref, uf_ref, ab_ref, ub_ref, pad_ref):
    n = lx_ref.shape[1]
    n_c = lxc_ref.shape[1]
    s = SUBLANES
    n_j = n // s
    pitch = n_j + PITCH_PAD
    zero = jnp.zeros((1, LANES), F32)

    for blk in range(LRU_BLOCKS_PER_STEP):
        lanes = slice(blk * LRU_BLOCK, (blk + 1) * LRU_BLOCK)
        conv_w = cw_ref[:, lanes]
        conv_b = cb_ref[:, lanes]
        gate = [(wa_ref[d, blk], wx_ref[d, blk], ba_ref[d:d + 1, lanes], bx_ref[d:d + 1, lanes],
                 lam_ref[d:d + 1, lanes]) for d in range(2)]

        _to_split(lxc_ref, lanes, pad_ref, uf_ref, n_c)
        xc = _conv4(uf_ref[0:n_c, :], conv_w, conv_b)
        _gates(xc, *gate[0], af_ref, uf_ref)
        _gates(xc, *gate[1], ab_ref, ub_ref)
        ends = _scan_local(af_ref, uf_ref, ab_ref, ub_ref, n_c // s)
        _, _, cf, cb = _link_states(*ends, zero, zero)

        _to_split(lx_ref, lanes, pad_ref, uf_ref, n)
        xl = _conv4(uf_ref[...], conv_w, conv_b)
        _gates(xl, *gate[0], af_ref, uf_ref)
        _gates(xl, *gate[1], ab_ref, ub_ref)
        ends = _scan_local(af_ref, uf_ref, ab_ref, ub_ref, n_j)
        in_f, in_b, _, _ = _link_states(*ends, cf, cb)

        def finish(j, c):
            rows = pl.ds(pl.multiple_of(j * s, s), s)
            h = (uf_ref[rows, :] + af_ref[rows, :] * in_f) + (ub_ref[rows, :] + ab_ref[rows, :] * in_b)
            pad_ref[pl.ds(j, s, stride=pitch), :] = h
            return c

        lax.fori_loop(0, n_j, finish, 0, unroll=8)
        for q in range(s):
            o_ref[0, pl.ds(q * n_j, n_j), lanes] = pad_ref[pl.ds(q * pitch, n_j), :].astype(o_ref.dtype)


def _lru_call(lx3, lxc3, conv_w, conv_b, wa, wx, ba, bx, lam):
    bsz, seq, _ = lx3.shape
    n_ctx = lxc3.shape[1]
    per = LRU_BLOCKS_PER_STEP
    width = per * LRU_BLOCK
    col = lambda b, n: (b, 0, n)
    par = lambda b, n: (0, n)
    wspec = pl.BlockSpec((2, per, LRU_BLOCK, LRU_BLOCK), lambda b, n: (0, n, 0, 0))
    return pl.pallas_call(
        _lru_body,
        grid=(bsz, LRU_BLOCKS // per),
        in_specs=[pl.BlockSpec((1, seq, width), col),
                  pl.BlockSpec((1, n_ctx, width), col),
                  pl.BlockSpec((LRU_CONV, width), par),
                  pl.BlockSpec((1, width), par),
                  wspec, wspec,
                  pl.BlockSpec((2, width), par),
                  pl.BlockSpec((2, width), par),
                  pl.BlockSpec((2, width), par)],
        out_specs=pl.BlockSpec((1, seq, width), col),
        out_shape=jax.ShapeDtypeStruct((bsz, seq, LRU_WIDTH), BF16),
        scratch_shapes=[pltpu.VMEM((seq, LRU_BLOCK), F32)] * 4
                       + [pltpu.VMEM((seq + SUBLANES * PITCH_PAD, LRU_BLOCK), F32)],
        compiler_params=_cparams(("parallel", "arbitrary")),
        name="lru",
    )(lx3, lxc3, conv_w, conv_b, wa, wx, ba, bx, lam)


TOKEN_TILE = D_MODEL // LANES


def _store_token_tiles(ref, row0, x):
    m = x.shape[0]
    for c in range(TOKEN_TILE):
        ref[pl.ds(row0 + c, m, stride=TOKEN_TILE), :] = x[:, c * LANES:(c + 1) * LANES]


def _load_token_tiles(ref, row0, m):
    return jnp.concatenate([ref[pl.ds(row0 + c, m, stride=TOKEN_TILE), :] for c in range(TOKEN_TILE)], axis=1)


def _merge_body(x_ref, oa_ref, hs_ref, glu_ref, ga_ref, gb_ref, mod_ref, g_ref, wua_ref, wul_ref, wo_ref,
                wr_ref, br_ref, x1_ref, h2_ref, lt_ref):
    ga1 = mod_ref[0, :, 2 * D_MODEL:3 * D_MODEL]
    sh2 = mod_ref[0, :, 3 * D_MODEL:4 * D_MODEL]
    sc2 = mod_ref[0, :, 4 * D_MODEL:5 * D_MODEL]
    o_lru = hs_ref[...] * glu_ref[...]
    y = (ga_ref[...].astype(F32) * _dot(oa_ref[...], wua_ref[...])
         + gb_ref[...].astype(F32) * _dot(o_lru, wul_ref[...]))
    x1 = x_ref[...] + ga1 * _dot(y.astype(BF16), wo_ref[...])
    x1_ref[...] = x1
    h2 = _ada_norm(x1, g_ref[...], sc2, sh2)
    lt_ref[...] = _dot3_nt(wr_ref[...], h2) + br_ref[:, 0:1]
    _store_token_tiles(h2_ref, 0, h2)


def _merge_call(x2, o_att, hs, glu, ga, gb, mod3, g_ffn, wua, wul, wo, wr_t, br, seq):
    t = x2.shape[0]
    tm = MERGE_TM
    per_b = seq // tm
    row = lambda i: (i, 0)
    full = lambda i: (0, 0)
    resident = lambda shape: pl.BlockSpec(shape, full, pipeline_mode=pl.Buffered(1))
    return pl.pallas_call(
        _merge_body,
        grid=(t // tm,),
        in_specs=[pl.BlockSpec((tm, D_MODEL), row),
                  pl.BlockSpec((tm, NA_WIDTH), row),
                  pl.BlockSpec((tm, LRU_WIDTH), row),
                  pl.BlockSpec((tm, LRU_WIDTH), row),
                  pl.BlockSpec((tm, D_MODEL), row),
                  pl.BlockSpec((tm, D_MODEL), row),
                  pl.BlockSpec((1, 1, 6 * D_MODEL), lambda i: (i // per_b, 0, 0)),
                  pl.BlockSpec((1, D_MODEL), full),
                  resident((NA_WIDTH, D_MODEL)),
                  resident((LRU_WIDTH, D_MODEL)),
                  resident((D_MODEL, D_MODEL)),
                  pl.BlockSpec((ROUTE_ROWS, D_MODEL), full),
                  pl.BlockSpec((ROUTE_ROWS, LANES), full)],
        out_specs=[pl.BlockSpec((tm, D_MODEL), row),
                   pl.BlockSpec((tm * TOKEN_TILE, LANES), row),
                   pl.BlockSpec((ROUTE_ROWS, tm), lambda i: (0, i))],
        out_shape=[jax.ShapeDtypeStruct((t, D_MODEL), F32),
                   jax.ShapeDtypeStruct((t * TOKEN_TILE, LANES), F32),
                   jax.ShapeDtypeStruct((ROUTE_ROWS, t), F32)],
        compiler_params=_cparams(("parallel",)),
        name="merge",
    )(x2, o_att, hs, glu, ga, gb, mod3, g_ffn, wua, wul, wo, wr_t, br)


def _route_body(lt_ref, eid_ref, gate_ref, rank_ref, cnt_ref, carry_ref):
    step = pl.program_id(0)

    @pl.when(step == 0)
    def _():
        carry_ref[...] = jnp.zeros_like(carry_ref)

    tb = lt_ref.shape[1]
    lg = [lt_ref[r:r + 1, :] for r in range(N_GROUPS)]
    best = lg[0]
    gidx = jnp.zeros((1, tb), I32)
    for r in range(1, N_GROUPS):
        better = lg[r] > best
        gidx = jnp.where(better, r, gidx)
        best = jnp.maximum(best, lg[r])
    den = jnp.exp(lg[0] - best)
    for r in range(1, N_GROUPS):
        den = den + jnp.exp(lg[r] - best)
    p_top = 1.0 / den

    ev = []
    for j in range(EXPERTS_PER_GROUP):
        sel = lt_ref[N_GROUPS + j:N_GROUPS + j + 1, :]
        for g in range(1, N_GROUPS):
            row = N_GROUPS + g * EXPERTS_PER_GROUP + j
            sel = jnp.where(gidx == g, lt_ref[row:row + 1, :], sel)
        ev.append(sel)
    v0 = ev[0]
    i0 = jnp.zeros((1, tb), I32)
    for j in range(1, EXPERTS_PER_GROUP):
        better = ev[j] > v0
        i0 = jnp.where(better, j, i0)
        v0 = jnp.maximum(v0, ev[j])
    v1 = jnp.full((1, tb), -jnp.inf, F32)
    i1 = jnp.zeros((1, tb), I32)
    for j in range(EXPERTS_PER_GROUP):
        better = (ev[j] > v1) & (i0 != j)
        i1 = jnp.where(better, j, i1)
        v1 = jnp.where(better, ev[j], v1)
    e1 = jnp.exp(v1 - v0)
    inv = 1.0 / (1.0 + e1)
    eid0 = gidx * EXPERTS_PER_GROUP + i0
    eid1 = gidx * EXPERTS_PER_GROUP + i1
    eid_ref[0:1, :] = eid0
    eid_ref[1:2, :] = eid1
    gate_ref[...] = jnp.zeros_like(gate_ref)
    gate_ref[0:1, :] = p_top * inv
    gate_ref[1:2, :] = p_top * (e1 * inv)

    sub = 256
    e_iota = lax.broadcasted_iota(I32, (N_EXPERTS, sub), 0)
    tri = jnp.where(lax.broadcasted_iota(I32, (sub, sub), 0) <= lax.broadcasted_iota(I32, (sub, sub), 1),
                    1.0, 0.0).astype(BF16)
    carry = carry_ref[...]
    for c in range(tb // sub):
        sl = slice(c * sub, (c + 1) * sub)
        m0 = eid0[:, sl] == e_iota
        m1 = eid1[:, sl] == e_iota
        oh = jnp.where(m0 | m1, 1.0, 0.0)
        incl = _dot(oh.astype(BF16), tri)
        excl = incl - oh + carry[:, 0:1]
        rank_ref[0:1, sl] = jnp.sum(jnp.where(m0, excl, 0.0), axis=0, keepdims=True).astype(I32)
        rank_ref[1:2, sl] = jnp.sum(jnp.where(m1, excl, 0.0), axis=0, keepdims=True).astype(I32)
        carry = carry + incl[:, sub - 1:sub]
    carry_ref[...] = carry
    cnt_ref[...] = carry


def _route_call(logits_t):
    t = logits_t.shape[1]
    tb = ROUTE_TB
    col = lambda i: (0, i)
    return pl.pallas_call(
        _route_body,
        grid=(t // tb,),
        in_specs=[pl.BlockSpec((ROUTE_ROWS, tb), col)],
        out_specs=[pl.BlockSpec((TOP_K, tb), col), pl.BlockSpec((SUBLANES, tb), col),
                   pl.BlockSpec((TOP_K, tb), col), pl.BlockSpec((N_EXPERTS, LANES), lambda i: (0, 0))],
        out_shape=[jax.ShapeDtypeStruct((TOP_K, t), I32), jax.ShapeDtypeStruct((SUBLANES, t), F32),
                   jax.ShapeDtypeStruct((TOP_K, t), I32), jax.ShapeDtypeStruct((N_EXPERTS, LANES), F32)],
        scratch_shapes=[pltpu.VMEM((N_EXPERTS, LANES), F32)],
        compiler_params=_cparams(("arbitrary",)),
        name="route",
    )(logits_t)


def _dest_body(cnt_ref, eid_ref, rank_ref, dest_ref, blk_ref):
    cnt = cnt_ref[...].astype(I32)
    padded = ((cnt + (MOE_BLK - 1)) >> MOE_BLK_LOG2) << MOE_BLK_LOG2
    e_iota = lax.broadcasted_iota(I32, (N_EXPERTS, LANES), 0)
    p_end = jnp.zeros((N_EXPERTS, LANES), I32)
    for e in range(N_EXPERTS):
        tot = jnp.sum(jnp.where(e_iota <= e, padded, 0), axis=0, keepdims=True)
        p_end = jnp.where(e_iota == e, tot, p_end)
    p_start = p_end - padded
    tb = eid_ref.shape[1]
    ps = jnp.concatenate([p_start] * (tb // LANES), axis=1)
    e_wide = lax.broadcasted_iota(I32, (N_EXPERTS, tb), 0)
    for k in range(TOP_K):
        start = jnp.sum(jnp.where(eid_ref[k:k + 1, :] == e_wide, ps, 0), axis=0, keepdims=True)
        dest_ref[k:k + 1, :] = start + rank_ref[k:k + 1, :]
    nb = blk_ref.shape[1]
    pe = jnp.concatenate([p_end] * (nb // LANES), axis=1)
    first_row = lax.broadcasted_iota(I32, (N_EXPERTS, nb), 1) * MOE_BLK
    n_before = jnp.sum(jnp.where(pe <= first_row, 1, 0), axis=0, keepdims=True)
    blk = jnp.minimum(n_before, N_EXPERTS - 1)
    blk_ref[...] = jnp.broadcast_to(blk, blk_ref.shape)
    blk_ref[1:2, :] = jnp.broadcast_to(p_end[N_EXPERTS - 1:N_EXPERTS, 0:1] >> MOE_BLK_LOG2, (1, nb))
    on_diag = e_iota == lax.broadcasted_iota(I32, (N_EXPERTS, LANES), 1)
    blk_ref[2:3, 0:LANES] = jnp.sum(jnp.where(on_diag, p_end, 0), axis=0, keepdims=True)
    blk_ref[3:4, 0:LANES] = jnp.sum(jnp.where(on_diag, padded, 0), axis=0, keepdims=True)


def _dest_call(cnt, eid, rank, nb_pad):
    t = eid.shape[1]
    tb = ROUTE_TB
    col = lambda i: (0, i)
    return pl.pallas_call(
        _dest_body,
        grid=(t // tb,),
        in_specs=[pl.BlockSpec((N_EXPERTS, LANES), lambda i: (0, 0)),
                  pl.BlockSpec((TOP_K, tb), col), pl.BlockSpec((TOP_K, tb), col)],
        out_specs=[pl.BlockSpec((TOP_K, tb), col), pl.BlockSpec((SUBLANES, nb_pad), lambda i: (0, 0))],
        out_shape=[jax.ShapeDtypeStruct((TOP_K, t), I32), jax.ShapeDtypeStruct((SUBLANES, nb_pad), I32)],
        compiler_params=_cparams(("arbitrary",)),
        name="dest",
    )(cnt, eid, rank)


def _token_tile(ref, t):
    return ref.at[pl.ds(pl.multiple_of(t * TOKEN_TILE, TOKEN_TILE), TOKEN_TILE)]


def _dispatch_body(dest_ref, pend_ref, plen_ref, h_ref, xs_ref, zero_ref, sem, zsem):
    tm = h_ref.shape[0] // TOKEN_TILE
    blk_rows = MOE_BLK * TOKEN_TILE
    n_tok = pl.num_programs(0) * tm
    base = pl.program_id(0) * tm

    @pl.when(pl.program_id(0) == 0)
    def _():
        zero_ref[...] = jnp.zeros_like(zero_ref)

        def zero_block(start):
            rows = pl.ds(pl.multiple_of(start * TOKEN_TILE, blk_rows), blk_rows)
            return pltpu.make_async_copy(zero_ref, xs_ref.at[rows], zsem)

        def fill(e, c):
            @pl.when(plen_ref[e] > 0)
            def _():
                zero_block(pend_ref[e] - MOE_BLK).start()
            return c

        def drain(e, c):
            @pl.when(plen_ref[e] > 0)
            def _():
                zero_block(pend_ref[e] - MOE_BLK).wait()
            return c

        lax.fori_loop(0, N_EXPERTS, fill, 0)
        n_used = pend_ref[N_EXPERTS - 1] >> MOE_BLK_LOG2
        n_blk = xs_ref.shape[0] // blk_rows
        lax.fori_loop(n_used, n_blk, lambda j, c: (zero_block(j * MOE_BLK).start(), c)[1], 0)
        lax.fori_loop(0, N_EXPERTS, drain, 0)
        lax.fori_loop(n_used, n_blk, lambda j, c: (zero_block(j * MOE_BLK).wait(), c)[1], 0)

    def issue(r, c):
        for k in range(TOP_K):
            d = dest_ref[k * n_tok + base + r]
            pltpu.make_async_copy(_token_tile(h_ref, r), _token_tile(xs_ref, d), sem).start(priority=k)
        return c

    lax.fori_loop(0, tm, issue, 0, unroll=8)
    for k in range(TOP_K):
        pltpu.make_async_copy(h_ref, xs_ref.at[pl.ds(0, tm * TOKEN_TILE)], sem).wait()


def _dispatch_call(dest_flat, p_end, p_len, h2t, n_slots):
    rows = DISPATCH_TM * TOKEN_TILE
    return pl.pallas_call(
        _dispatch_body,
        grid_spec=pltpu.PrefetchScalarGridSpec(
            num_scalar_prefetch=3,
            grid=(h2t.shape[0] // rows,),
            in_specs=[pl.BlockSpec((rows, LANES), lambda i, d, pe, pn: (i, 0))],
            out_specs=pl.BlockSpec(memory_space=pl.ANY),
            scratch_shapes=[pltpu.VMEM((MOE_BLK * TOKEN_TILE, LANES), F32),
                            pltpu.SemaphoreType.DMA(()), pltpu.SemaphoreType.DMA(())]),
        out_shape=jax.ShapeDtypeStruct((n_slots * TOKEN_TILE, LANES), F32),
        compiler_params=_cparams(("arbitrary",)),
        name="dispatch",
    )(dest_flat, p_end, p_len, h2t)


EXPERT_BLKS_PER_STEP = 4


def _experts_body(blk_ref, used_ref, xs_ref, w1_hbm, w3_hbm, w2_hbm, y_ref,
                  f1_ref, f3_ref, f2_ref, b1_ref, b3_ref, b2_ref, loaded_ref, sem):
    i = pl.program_id(0)
    per = EXPERT_BLKS_PER_STEP
    used = used_ref[0]

    def weight_copies(e):
        return [pltpu.make_async_copy(w_hbm.at[e], f_ref, sem)
                for w_hbm, f_ref in ((w1_hbm, f1_ref), (w3_hbm, f3_ref), (w2_hbm, f2_ref))]

    @pl.when(i == 0)
    def _():
        loaded_ref[0] = -1
        for cp in weight_copies(blk_ref[0]):
            cp.start()

    for half in range(per):
        blk = per * i + half
        row0 = half * MOE_BLK * TOKEN_TILE
        e = blk_ref[blk]

        @pl.when((blk < used) & (e != loaded_ref[0]))
        def _():
            for cp in weight_copies(e):
                cp.wait()
            b1_ref[...] = f1_ref[...].astype(BF16)
            b3_ref[...] = f3_ref[...].astype(BF16)
            b2_ref[...] = f2_ref[...].astype(BF16)
            loaded_ref[0] = e
            nxt = lax.while_loop(lambda j: (j < used) & (blk_ref[jnp.minimum(j, blk_ref.shape[0] - 1)] == e),
                                 lambda j: j + 1, blk + 1)

            @pl.when(nxt < used)
            def _():
                for cp in weight_copies(blk_ref[jnp.minimum(nxt, blk_ref.shape[0] - 1)]):
                    cp.start()

        @pl.when(blk < used)
        def _():
            x = _load_token_tiles(xs_ref, row0, MOE_BLK).astype(BF16)
            g = _dot(x, b1_ref[...])
            u = _dot(x, b3_ref[...])
            mid = (g * _sigmoid(g)) * u
            _store_token_tiles(y_ref, row0, _dot(mid.astype(BF16), b2_ref[...]))

        @pl.when(blk >= used)
        def _():
            y_ref[pl.ds(row0, MOE_BLK * TOKEN_TILE), :] = jnp.zeros((MOE_BLK * TOKEN_TILE, LANES), F32)


def _experts_call(blk_e, n_used, xs, w1, w3, w2):
    per = EXPERT_BLKS_PER_STEP
    rows = per * MOE_BLK * TOKEN_TILE
    nb = xs.shape[0] // (MOE_BLK * TOKEN_TILE)
    assert nb % per == 0 and blk_e.shape[0] == nb
    hbm = pl.BlockSpec(memory_space=pl.ANY)
    return pl.pallas_call(
        _experts_body,
        grid_spec=pltpu.PrefetchScalarGridSpec(
            num_scalar_prefetch=2,
            grid=(nb // per,),
            in_specs=[pl.BlockSpec((rows, LANES), lambda i, blk, used: (i, 0)), hbm, hbm, hbm],
            out_specs=pl.BlockSpec((rows, LANES), lambda i, blk, used: (i, 0)),
            scratch_shapes=[pltpu.VMEM((D_MODEL, D_EXPERT), F32), pltpu.VMEM((D_MODEL, D_EXPERT), F32),
                            pltpu.VMEM((D_EXPERT, D_MODEL), F32),
                            pltpu.VMEM((D_MODEL, D_EXPERT), BF16), pltpu.VMEM((D_MODEL, D_EXPERT), BF16),
                            pltpu.VMEM((D_EXPERT, D_MODEL), BF16),
                            pltpu.SMEM((1,), I32), pltpu.SemaphoreType.DMA(())]),
        out_shape=jax.ShapeDtypeStruct(xs.shape, F32),
        compiler_params=_cparams(("arbitrary",)),
        name="experts",
    )(blk_e, n_used, xs, w1, w3, w2)


def _combine_body(dest_ref, x1_ref, gate_ref, mod_ref, gf_ref, y_ref, o_ref, buf_ref, sem):
    tm = x1_ref.shape[0]
    step = pl.program_id(0)
    n_step = pl.num_programs(0)
    n_tok = n_step * tm
    slot = step % 2
    region = tm * TOKEN_TILE

    def region_row0(buf, k):
        return pl.multiple_of((buf * TOP_K + k) * region, region)

    def start_gather(for_step):
        def issue(r, c):
            for k in range(TOP_K):
                d = dest_ref[k * n_tok + for_step * tm + r]
                dst = buf_ref.at[pl.ds(pl.multiple_of(region_row0(for_step % 2, k) + r * TOKEN_TILE, TOKEN_TILE),
                                       TOKEN_TILE)]
                pltpu.make_async_copy(_token_tile(y_ref, d), dst, sem.at[for_step % 2]).start(priority=k)
            return c

        lax.fori_loop(0, tm, issue, 0, unroll=8)

    @pl.when(step == 0)
    def _():
        start_gather(step)

    @pl.when(step + 1 < n_step)
    def _():
        start_gather(step + 1)

    eye = jnp.where(lax.broadcasted_iota(I32, (tm, tm), 0) == lax.broadcasted_iota(I32, (tm, tm), 1),
                    1.0, 0.0).astype(BF16)
    g = gate_ref[...]
    g1 = g.astype(BF16)
    rem = g - g1.astype(F32)
    g2 = rem.astype(BF16)
    g3 = (rem - g2.astype(F32)).astype(BF16)
    gt = _dot_nt(eye, g1) + (_dot_nt(eye, g2) + _dot_nt(eye, g3))

    for k in range(TOP_K):
        pltpu.make_async_copy(y_ref.at[pl.ds(0, region)], buf_ref.at[pl.ds(region_row0(slot, k), region)],
                              sem.at[slot]).wait()

    ga2 = mod_ref[0, :, 5 * D_MODEL:6 * D_MODEL]
    moe = (gt[:, 0:1] * _load_token_tiles(buf_ref, region_row0(slot, 0), tm)
           + gt[:, 1:2] * _load_token_tiles(buf_ref, region_row0(slot, 1), tm))
    x2 = x1_ref[...] + ga2 * moe
    ms = jnp.mean(x2 * x2, axis=-1, keepdims=True)
    o_ref[...] = x2 * lax.rsqrt(ms + EPS) * gf_ref[...]


def _combine_call(dest_flat, x1, gate, mod3, g_final, y, seq):
    t = x1.shape[0]
    tm = COMBINE_TM
    per_b = seq // tm
    return pl.pallas_call(
        _combine_body,
        grid_spec=pltpu.PrefetchScalarGridSpec(
            num_scalar_prefetch=1,
            grid=(t // tm,),
            in_specs=[pl.BlockSpec((tm, D_MODEL), lambda i, d: (i, 0)),
                      pl.BlockSpec((SUBLANES, tm), lambda i, d: (0, i)),
                      pl.BlockSpec((1, 1, 6 * D_MODEL), lambda i, d: (i // per_b, 0, 0)),
                      pl.BlockSpec((1, D_MODEL), lambda i, d: (0, 0)),
                      pl.BlockSpec(memory_space=pl.ANY)],
            out_specs=pl.BlockSpec((tm, D_MODEL), lambda i, d: (i, 0)),
            scratch_shapes=[pltpu.VMEM((2 * TOP_K * tm * TOKEN_TILE, LANES), F32),
                            pltpu.SemaphoreType.DMA((2,))]),
        out_shape=jax.ShapeDtypeStruct((t, D_MODEL), F32),
        compiler_params=_cparams(("arbitrary",)),
        name="combine",
    )(dest_flat, x1, gate, mod3, g_final, y)


def _rope_tables(seq):
    half = NA_HEAD_DIM // 2
    nf = half // 2
    inv_freq = ROPE_THETA ** (-jnp.arange(nf, dtype=F32) / nf)
    t = jnp.arange(seq)
    row_pos = (t // GRID_W).astype(F32)
    col_pos = (t % GRID_W).astype(F32)
    ang_r = row_pos[:, None] * inv_freq
    ang_c = col_pos[:, None] * inv_freq
    cos = jnp.concatenate([jnp.cos(ang_r), jnp.cos(ang_r), jnp.cos(ang_c), jnp.cos(ang_c)], axis=-1)
    sin = jnp.concatenate([-jnp.sin(ang_r), jnp.sin(ang_r), -jnp.sin(ang_c), jnp.sin(ang_c)], axis=-1)
    return jnp.tile(cos, (1, NA_HEADS)), jnp.tile(sin, (1, NA_HEADS))


def _layer(x, c, ctx, c_ctx, w_mod, b_mod, g_mix, g_ffn, w_in, rpb, conv_w, conv_b, lru_wa, lru_ba,
           lru_wx, lru_bx, lru_lambda, w_up_attn, w_up_lru, w_out, wg, bg, we, be, w1, w3, w2, g_final):
    bsz, seq, d = x.shape
    n_ctx = ctx.shape[1]
    t = bsz * seq
    assert d == D_MODEL and seq % ATT_TQ == 0 and seq // ATT_TQ > K_ROW_BLOCKS and n_ctx % ATT_TQ == 0
    assert bsz + 1 <= MOD_ROWS and seq % PROJ_TM == 0 and seq % MERGE_TM == 0
    assert t % ROUTE_TB == 0 and t % COMBINE_TM == 0 and t % DISPATCH_TM == 0
    assert (bsz * n_ctx) % PROJ_TM == 0 and n_ctx <= seq
    assert seq % (SUBLANES * SUBLANES) == 0 and n_ctx % (SUBLANES * SUBLANES) == 0
    assert (seq // SUBLANES) % SCAN_STEPS == 0 and (n_ctx // SUBLANES) % SCAN_STEPS == 0

    cc = jnp.concatenate([c, c_ctx[None, :], jnp.zeros((MOD_ROWS - bsz - 1, d), F32)], axis=0)
    mod = _mod_call(cc, w_mod, b_mod)
    mod3 = mod[:bsz].reshape(bsz, 1, 6 * d)
    mod_c = mod[bsz:bsz + 1]

    x2 = x.reshape(t, d)
    g_mix2 = g_mix.reshape(1, d)
    w_in_bf = w_in.astype(BF16)
    kc, vc, lxc = _ctxproj_call(ctx.reshape(bsz * n_ctx, d), mod_c, g_mix2, w_in_bf[:, :CTX_COLS])
    cos_t, sin_t = _rope_tables(seq)
    qre, qro, qpe, qpo, k, v, lx, glu, ga, gb = _inproj_call(x2, mod3, g_mix2, w_in_bf, cos_t, sin_t, seq)

    bias = _bias_tables(_rpbcol_call(rpb))
    o_att = _attn_call(qre, qro, qpe, qpo, k, v, kc, vc, bias, bsz, seq, n_ctx)

    hs = _lru_call(lx.reshape(bsz, seq, LRU_WIDTH), lxc.reshape(bsz, n_ctx, LRU_WIDTH),
                   conv_w, conv_b.reshape(1, LRU_WIDTH), lru_wa, lru_wx, lru_ba, lru_bx, lru_lambda)

    wr_t = jnp.concatenate([wg.T, we.T, jnp.zeros((ROUTE_ROWS - N_GROUPS - N_EXPERTS, d), F32)], axis=0)
    br = jnp.concatenate([bg, be, jnp.zeros((ROUTE_ROWS - N_GROUPS - N_EXPERTS,), F32)])
    br = jnp.broadcast_to(br[:, None], (ROUTE_ROWS, LANES))
    x1, h2, logits_t = _merge_call(x2, o_att, hs.reshape(t, LRU_WIDTH), glu, ga, gb, mod3,
                                   g_ffn.reshape(1, d), w_up_attn.astype(BF16), w_up_lru.astype(BF16),
                                   w_out.astype(BF16), wr_t, br, seq)

    eid, gate, rank, cnt = _route_call(logits_t)
    n_blk = -(-(t * TOP_K + N_EXPERTS * (MOE_BLK - 1)) // MOE_BLK)
    n_blk = -(-n_blk // EXPERT_BLKS_PER_STEP) * EXPERT_BLKS_PER_STEP
    nb_pad = -(-n_blk // LANES) * LANES
    dest, blk = _dest_call(cnt, eid, rank, nb_pad)
    dest_flat = dest.reshape(TOP_K * t)
    xs = _dispatch_call(dest_flat, blk[2, :N_EXPERTS], blk[3, :N_EXPERTS], h2, n_blk * MOE_BLK)
    y = _experts_call(blk[0, :n_blk], blk[1, :1], xs, w1, w3, w2)
    return _combine_call(dest_flat, x1, gate, mod3, g_final.reshape(1, d), y, seq).reshape(bsz, seq, d)


def kernel(x, c, ctx, c_ctx, w_mod, b_mod, g_mix, g_ffn, w_in, rpb, conv_w, conv_b, lru_wa, lru_ba, lru_wx,
           lru_bx, lru_lambda, w_up_attn, w_up_lru, w_out, router_group_w, router_group_b, router_expert_w,
           router_expert_b, expert_w_gate, expert_w_up, expert_w_down, g_final):
    assert w_mod.shape[0] == 1, "single-layer block"
    return _layer(x, c, ctx, c_ctx, w_mod[0], b_mod[0], g_mix[0], g_ffn[0], w_in[0], rpb[0], conv_w[0],
                  conv_b[0], lru_wa[0], lru_ba[0], lru_wx[0], lru_bx[0], lru_lambda[0], w_up_attn[0],
                  w_up_lru[0], w_out[0], router_group_w[0], router_group_b[0], router_expert_w[0],
                  router_expert_b[0], expert_w_gate[0], expert_w_up[0], expert_w_down[0], g_final)
```

```python
import functools

import numpy as np
import jax
import jax.numpy as jnp
from jax import lax
from jax.experimental import pallas as pl
from jax.experimental.pallas import tpu as pltpu

F32 = jnp.float32
BF16 = jnp.bfloat16
I32 = jnp.int32
U32 = jnp.uint32

D_MODEL = 1024
GRID_W = 64
EPS = 1e-6
NEG_INF = -1e30

NA_HEADS = 8
NA_HEAD_DIM = 64
NA_WIDTH = NA_HEADS * NA_HEAD_DIM
NA_WIN_ROWS = 8
NA_WIN_COLS = 16
ROPE_THETA = 10000.0

LRU_WIDTH = D_MODEL
LRU_BLOCKS = 8
LRU_BLOCK = LRU_WIDTH // LRU_BLOCKS
LRU_CONV = 4
LRU_C = 8.0

N_GROUPS = 4
EXPERTS_PER_GROUP = 8
N_EXPERTS = N_GROUPS * EXPERTS_PER_GROUP
TOP_K = 2
D_EXPERT = 512

K_OFF = 0
V_OFF = K_OFF + NA_WIDTH
LX_OFF = V_OFF + NA_WIDTH
CTX_COLS = LX_OFF + LRU_WIDTH
Q_OFF = CTX_COLS
LG_OFF = Q_OFF + NA_WIDTH
GA_OFF = LG_OFF + LRU_WIDTH
GB_OFF = GA_OFF + D_MODEL
PROJ_COLS = GB_OFF + D_MODEL

LANES = 128
SUBLANES = 8

Q_ROWS = 4
K_ROW_BLOCKS = 3
ATT_TQ = Q_ROWS * GRID_W
ATT_TK = K_ROW_BLOCKS * ATT_TQ

MOE_BLK_LOG2 = 8
MOE_BLK = 1 << MOE_BLK_LOG2
MOD_ROWS = 24
ROUTE_ROWS = 64

PROJ_TM = 512
MERGE_TM = 512
ROUTE_TB = 2048
DISPATCH_TM = 2048
COMBINE_TM = 512

VMEM_LIMIT = 56 * 1024 * 1024


def _cparams(sem, vmem=VMEM_LIMIT):
    return pltpu.CompilerParams(dimension_semantics=sem, vmem_limit_bytes=vmem)


def _dot(a, b):
    return jnp.dot(a, b, preferred_element_type=F32)


def _dot_nt(a, b):
    return lax.dot_general(a, b, (((1,), (1,)), ((), ())), preferred_element_type=F32)


def _split2(a):
    hi = a.astype(BF16)
    lo = (a - hi.astype(F32)).astype(BF16)
    return hi, lo


def _dot3(a, b):
    ah, al = _split2(a)
    bh, bl = _split2(b)
    return _dot(ah, bh) + (_dot(ah, bl) + _dot(al, bh))


def _dot3_nt(a, b):
    ah, al = _split2(a)
    bh, bl = _split2(b)
    return _dot_nt(ah, bh) + (_dot_nt(ah, bl) + _dot_nt(al, bh))


def _sigmoid(x):
    return 1.0 / (1.0 + jnp.exp(-x))


def _ada_norm(x, g, sc, sh):
    ms = jnp.mean(x * x, axis=-1, keepdims=True)
    return (x * lax.rsqrt(ms + EPS) * g) * (1.0 + sc) + sh


def _mod_body(cc_ref, w_ref, b_ref, o_ref):
    cc = cc_ref[...]
    o_ref[...] = _dot3(cc * _sigmoid(cc), w_ref[...]) + b_ref[...]


def _mod_call(cc, w_mod, b_mod):
    n = w_mod.shape[1]
    bn = 1024
    return pl.pallas_call(
        _mod_body,
        grid=(n // bn,),
        in_specs=[pl.BlockSpec((MOD_ROWS, D_MODEL), lambda j: (0, 0)),
                  pl.BlockSpec((D_MODEL, bn), lambda j: (0, j)),
                  pl.BlockSpec((1, bn), lambda j: (0, j))],
        out_specs=pl.BlockSpec((MOD_ROWS, bn), lambda j: (0, j)),
        out_shape=jax.ShapeDtypeStruct((MOD_ROWS, n), F32),
        compiler_params=_cparams(("arbitrary",)),
        name="mod",
    )(cc, w_mod, b_mod.reshape(1, n))


def _rope(t, cos, sin):
    lane = lax.broadcasted_iota(I32, (t.shape[0], LANES), 1)
    first = (lane & 16) == 0
    parts = []
    for c in range(t.shape[1] // LANES):
        tc = t[:, c * LANES:(c + 1) * LANES]
        parts.append(jnp.where(first, pltpu.roll(tc, LANES - 16, 1), pltpu.roll(tc, 16, 1)))
    partner = jnp.concatenate(parts, axis=1)
    return t * cos + partner * sin


def _gelu_tanh(x):
    return 0.5 * x * (1.0 + jnp.tanh(0.7978845608028654 * (x + 0.044715 * (x * x * x))))


def _inproj_body(x_ref, mod_ref, g_ref, w_ref, cos_ref, sin_ref,
                 qr_ref, qp_ref, k_ref, v_ref, lx_ref, glu_ref, ga_ref, gb_ref):
    sh = mod_ref[0, :, 0:D_MODEL]
    sc = mod_ref[0, :, D_MODEL:2 * D_MODEL]
    h = _ada_norm(x_ref[...], g_ref[...], sc, sh).astype(BF16)
    cos = cos_ref[...]
    sin = sin_ref[...]
    scale = NA_HEAD_DIM ** -0.5

    k_ref[...] = _rope(_dot(h, w_ref[:, K_OFF:K_OFF + NA_WIDTH]), cos, sin).astype(BF16)
    v_ref[...] = _dot(h, w_ref[:, V_OFF:V_OFF + NA_WIDTH]).astype(BF16)
    lx_ref[...] = _dot(h, w_ref[:, LX_OFF:LX_OFF + LRU_WIDTH])

    q = _dot(h, w_ref[:, Q_OFF:Q_OFF + NA_WIDTH]) * scale
    qr_ref[...] = _rope(q, cos, sin).astype(BF16)
    qp_ref[...] = q.astype(BF16)

    glu_ref[...] = _gelu_tanh(_dot(h, w_ref[:, LG_OFF:LG_OFF + LRU_WIDTH])).astype(BF16)
    ga_ref[...] = _sigmoid(_dot(h, w_ref[:, GA_OFF:GA_OFF + D_MODEL])).astype(BF16)
    gb_ref[...] = _sigmoid(_dot(h, w_ref[:, GB_OFF:GB_OFF + D_MODEL])).astype(BF16)


def _inproj_call(x2, mod3, g_mix, w_in_bf, cos_t, sin_t, seq):
    t = x2.shape[0]
    tm = PROJ_TM
    per_b = seq // tm
    row = lambda i: (i, 0)
    wide = lambda n, dt: jax.ShapeDtypeStruct((t, n), dt)
    return pl.pallas_call(
        _inproj_body,
        grid=(t // tm,),
        in_specs=[pl.BlockSpec((tm, D_MODEL), row),
                  pl.BlockSpec((1, 1, 6 * D_MODEL), lambda i: (i // per_b, 0, 0)),
                  pl.BlockSpec((1, D_MODEL), lambda i: (0, 0)),
                  pl.BlockSpec((D_MODEL, PROJ_COLS), lambda i: (0, 0), pipeline_mode=pl.Buffered(1)),
                  pl.BlockSpec((tm, NA_WIDTH), lambda i: (i % per_b, 0)),
                  pl.BlockSpec((tm, NA_WIDTH), lambda i: (i % per_b, 0))],
        out_specs=[pl.BlockSpec((tm, NA_WIDTH), row)] * 4
                  + [pl.BlockSpec((tm, LRU_WIDTH), row)] * 4,
        out_shape=[wide(NA_WIDTH, BF16)] * 4
                  + [wide(LRU_WIDTH, F32), wide(LRU_WIDTH, BF16), wide(D_MODEL, BF16), wide(D_MODEL, BF16)],
        compiler_params=_cparams(("parallel",)),
        name="inproj",
    )(x2, mod3, g_mix, w_in_bf, cos_t, sin_t)


def _ctxproj_body(x_ref, mod_ref, g_ref, w_ref, k_ref, v_ref, lx_ref):
    sh = mod_ref[:, 0:D_MODEL]
    sc = mod_ref[:, D_MODEL:2 * D_MODEL]
    h = _ada_norm(x_ref[...], g_ref[...], sc, sh).astype(BF16)
    k_ref[...] = _dot(h, w_ref[:, K_OFF:K_OFF + NA_WIDTH]).astype(BF16)
    v_ref[...] = _dot(h, w_ref[:, V_OFF:V_OFF + NA_WIDTH]).astype(BF16)
    lx_ref[...] = _dot(h, w_ref[:, LX_OFF:LX_OFF + LRU_WIDTH])


def _ctxproj_call(c2, mod_c, g_mix, w_ctx_bf):
    t = c2.shape[0]
    tm = PROJ_TM
    row = lambda i: (i, 0)
    return pl.pallas_call(
        _ctxproj_body,
        grid=(t // tm,),
        in_specs=[pl.BlockSpec((tm, D_MODEL), row),
                  pl.BlockSpec((1, 6 * D_MODEL), lambda i: (0, 0)),
                  pl.BlockSpec((1, D_MODEL), lambda i: (0, 0)),
                  pl.BlockSpec((D_MODEL, CTX_COLS), lambda i: (0, 0))],
        out_specs=[pl.BlockSpec((tm, NA_WIDTH), row), pl.BlockSpec((tm, NA_WIDTH), row),
                   pl.BlockSpec((tm, LRU_WIDTH), row)],
        out_shape=[jax.ShapeDtypeStruct((t, NA_WIDTH), BF16), jax.ShapeDtypeStruct((t, NA_WIDTH), BF16),
                   jax.ShapeDtypeStruct((t, LRU_WIDTH), F32)],
        compiler_params=_cparams(("parallel",)),
        name="ctxproj",
    )(c2, mod_c, g_mix, w_ctx_bf)


N_DR = 2 * NA_WIN_ROWS - 1
N_DC = 2 * NA_WIN_COLS - 1


def _rpbcol_body(rpb_ref, o_ref):
    n = GRID_W * GRID_W
    flat = lax.broadcasted_iota(I32, (32, n), 1)
    qc = flat >> 6
    kc = flat & (GRID_W - 1)
    dc = jnp.clip(kc - qc, 1 - NA_WIN_COLS, NA_WIN_COLS - 1) + (NA_WIN_COLS - 1)
    d_iota = lax.broadcasted_iota(I32, (32, n), 0)
    onehot = jnp.where(dc == d_iota, 1.0, 0.0).astype(BF16)
    r = rpb_ref[...]
    r1 = r.astype(BF16)
    rem = r - r1.astype(F32)
    r2 = rem.astype(BF16)
    r3 = (rem - r2.astype(F32)).astype(BF16)
    val = _dot(r1, onehot) + (_dot(r2, onehot) + _dot(r3, onehot))
    qc1 = qc[0:1, :]
    kc1 = kc[0:1, :]
    c_start = jnp.clip(qc1 - NA_WIN_COLS // 2, 0, GRID_W - NA_WIN_COLS)
    band = (kc1 >= c_start) & (kc1 < c_start + NA_WIN_COLS)
    o_ref[...] = jnp.where(band, val, NEG_INF)


def _rpbcol_call(rpb):
    rows = NA_HEADS * N_DR
    r2 = jnp.pad(rpb.reshape(rows, N_DC), ((0, 0), (0, 32 - N_DC)))
    n = GRID_W * GRID_W
    return pl.pallas_call(
        _rpbcol_body,
        in_specs=[pl.BlockSpec((rows, 32), lambda: (0, 0))],
        out_specs=pl.BlockSpec((rows, n), lambda: (0, 0)),
        out_shape=jax.ShapeDtypeStruct((rows, n), F32),
        name="rpbcol",
    )(r2)


def _bias_tables(rpbcol):
    t = rpbcol.reshape(NA_HEADS, N_DR, GRID_W, GRID_W)
    neg = jnp.full((NA_HEADS, GRID_W, GRID_W), NEG_INF, F32)
    n_kj = K_ROW_BLOCKS * Q_ROWS
    classes = []
    for lo_fn, dr_off in ((lambda ri: 0, 7), (lambda ri: ri, 3), (lambda ri: 4, -1)):
        rows = []
        for ri in range(Q_ROWS):
            lo = lo_fn(ri)
            blocks = []
            for kj in range(n_kj):
                inside = lo <= kj < lo + NA_WIN_ROWS
                blocks.append(t[:, kj - ri + dr_off] if inside else neg)
            rows.append(jnp.concatenate(blocks, axis=2))
        classes.append(jnp.concatenate(rows, axis=1))
    return jnp.stack(classes, axis=0)


def _attn_body(qr_ref, qp_ref, k0_ref, k1_ref, k2_ref, v0_ref, v1_ref, v2_ref,
               kc_ref, vc_ref, bias_ref, o_ref):
    lane = lax.broadcasted_iota(I32, (ATT_TQ, LANES), 1)
    lane1 = lax.broadcasted_iota(I32, (1, LANES), 1)
    head_lanes = [jnp.where(lane1 < NA_HEAD_DIM, 1.0, 0.0).astype(BF16),
                  jnp.where(lane1 < NA_HEAD_DIM, 0.0, 1.0).astype(BF16)]
    k_refs = (k0_ref, k1_ref, k2_ref)
    v_refs = (v0_ref, v1_ref, v2_ref)
    for bi, p in [(bi, p) for bi in range(ATT_BATCH) for p in range(NA_HEADS // 2)]:
        sl = slice(p * LANES, (p + 1) * LANES)
        k_lat = jnp.concatenate([r[bi, :, sl] for r in k_refs], axis=0)
        kc = kc_ref[bi, :, sl]
        v_all = jnp.concatenate([r[bi, :, sl] for r in v_refs] + [vc_ref[bi, :, sl]], axis=0)
        q_rot = qr_ref[bi, :, sl]
        q_plain = qp_ref[bi, :, sl]
        outs = []
        for hh in range(2):
            h = 2 * p + hh
            mine = head_lanes[hh]
            s_lat = _dot_nt(q_rot, k_lat * mine) + bias_ref[0, h]
            s_ctx = _dot_nt(q_plain, kc * mine)
            tiles = ([s_lat[:, j * ATT_TQ:(j + 1) * ATT_TQ] for j in range(K_ROW_BLOCKS)]
                     + [s_ctx[:, j * ATT_TQ:(j + 1) * ATT_TQ] for j in range(s_ctx.shape[1] // ATT_TQ)])
            m = functools.reduce(jnp.maximum, tiles).max(axis=-1, keepdims=True)
            prob = jnp.exp(jnp.concatenate([s_lat, s_ctx], axis=1) - m).astype(BF16)
            acc = _dot(prob, v_all * mine + (1.0 - mine).astype(BF16))
            outs.append(acc / pltpu.roll(acc, NA_HEAD_DIM, 1))
        o_ref[bi, :, sl] = jnp.where(lane < NA_HEAD_DIM, outs[0], outs[1]).astype(BF16)


ATT_BATCH = 4


def _attn_call(qr, qp, k, v, kc, vc, bias, bsz, seq, n_ctx):
    n_grp = seq // ATT_TQ
    max_kb = n_grp - K_ROW_BLOCKS
    assert bsz % ATT_BATCH == 0
    by_batch = lambda a, n: a.reshape(bsz, n, NA_WIDTH)

    def qmap(g, b):
        return (b, g, 0)

    def kmap(j):
        return lambda g, b: (b, jnp.clip(g - 1, 0, max_kb) + j, 0)

    def cls(g, b):
        return (jnp.where(g == 0, 0, jnp.where(g == n_grp - 1, 2, 1)), 0, 0, 0)

    qspec = pl.BlockSpec((ATT_BATCH, ATT_TQ, NA_WIDTH), qmap)
    cspec = pl.BlockSpec((ATT_BATCH, n_ctx, NA_WIDTH), lambda g, b: (b, 0, 0))
    k3, v3 = by_batch(k, seq), by_batch(v, seq)
    return pl.pallas_call(
        _attn_body,
        grid=(n_grp, bsz // ATT_BATCH),
        in_specs=[qspec] * 2
                 + [pl.BlockSpec((ATT_BATCH, ATT_TQ, NA_WIDTH), kmap(j)) for j in range(K_ROW_BLOCKS)] * 2
                 + [cspec, cspec, pl.BlockSpec((1, NA_HEADS, ATT_TQ, ATT_TK), cls)],
        out_specs=qspec,
        out_shape=jax.ShapeDtypeStruct((bsz, seq, NA_WIDTH), BF16),
        compiler_params=_cparams(("arbitrary", "arbitrary")),
        name="attn",
    )(by_batch(qr, seq), by_batch(qp, seq), k3, k3, k3, v3, v3, v3,
      by_batch(kc, n_ctx), by_batch(vc, n_ctx), bias).reshape(bsz * seq, NA_WIDTH)


def _shift_down(v, row):
    return jnp.where(row >= 1, pltpu.roll(v, 1, 0), 0.0)


def _shift_up(v, row):
    return jnp.where(row < SUBLANES - 1, pltpu.roll(v, SUBLANES - 1, 0), 0.0)


def _conv4(x, w, b):
    n = x.shape[0]
    s = SUBLANES
    row = lax.broadcasted_iota(I32, (s, LANES), 0)
    last = _shift_down(x[n - s:n], row)
    last2 = _shift_down(x[n - 2 * s:n - s], row)
    first = _shift_up(x[0:s], row)
    xm1 = jnp.concatenate([last, x[0:n - s]], axis=0)
    xm2 = jnp.concatenate([last2, last, x[0:n - 2 * s]], axis=0)
    xp1 = jnp.concatenate([x[s:n], first], axis=0)
    return (w[0:1, :] * xm2 + w[1:2, :] * xm1 + w[2:3, :] * x + w[3:4, :] * xp1) + b


def _softplus(z):
    return jnp.maximum(z, 0.0) + jnp.log1p(jnp.exp(-jnp.abs(z)))


def _gates(xc, wa, wx, ba, bx, lam, a_ref, u_ref):
    n = xc.shape[0]
    xb = xc.astype(BF16)
    tr = jnp.tanh(_dot(xb, (0.5 * wa).astype(BF16)) + 0.5 * ba)
    ti = jnp.tanh(_dot(xb, (0.5 * wx).astype(BF16)) + 0.5 * bx)
    half_c = (0.5 * LRU_C) * _softplus(-lam)
    neg_log_a = half_c * tr + half_c
    a = jnp.exp(-neg_log_a)
    a_ref[0:n, :] = a
    s2 = jnp.tanh(neg_log_a) * (a * a + 1.0)
    root = jnp.where(s2 > 0.0, s2 * lax.rsqrt(s2), 0.0)
    xh = 0.5 * xc
    u_ref[0:n, :] = root * (xh * ti + xh)


def _scan4(a, u, h, p):
    a01 = a[1] * a[0]
    u01 = a[1] * u[0] + u[1]
    a23 = a[3] * a[2]
    u23 = a[3] * u[2] + u[3]
    a012 = a[2] * a01
    u012 = a[2] * u01 + u[2]
    a0123 = a23 * a01
    u0123 = a23 * u01 + u23
    hs = [a[0] * h + u[0], a01 * h + u01, a012 * h + u012, a0123 * h + u0123]
    ps = [a[0] * p, a01 * p, a012 * p, a0123 * p]
    return hs, ps


SCAN_STEPS = 4


def _scan_local(af_ref, uf_ref, ab_ref, ub_ref, n_vreg):
    s = SUBLANES
    zero = jnp.zeros((s, LANES), F32)
    one = jnp.ones((s, LANES), F32)
    span = SCAN_STEPS * s

    def body(q, carry):
        hf, pf, hb, pb = carry
        base = pl.multiple_of(q * span, span)
        rows = [pl.ds(base + i * s, s) for i in range(SCAN_STEPS)]
        hs, ps = _scan4([af_ref[r, :] for r in rows], [uf_ref[r, :] for r in rows], hf, pf)
        for r, h, p in zip(rows, hs, ps):
            uf_ref[r, :] = h
            af_ref[r, :] = p
        hf, pf = hs[-1], ps[-1]
        base = pl.multiple_of((n_vreg - SCAN_STEPS) * s - q * span, span)
        rows = [pl.ds(base + (SCAN_STEPS - 1 - i) * s, s) for i in range(SCAN_STEPS)]
        hs, ps = _scan4([ab_ref[r, :] for r in rows], [ub_ref[r, :] for r in rows], hb, pb)
        for r, h, p in zip(rows, hs, ps):
            ub_ref[r, :] = h
            ab_ref[r, :] = p
        return hf, pf, hs[-1], ps[-1]

    return lax.fori_loop(0, n_vreg // SCAN_STEPS, body, (zero, one, zero, one), unroll=2)


def _link_states(hf, pf, hb, pb, h0f, h0b):
    s = SUBLANES
    row = lax.broadcasted_iota(I32, (s, LANES), 0)
    a, u = pf, hf
    for k in (1, 2, 4):
        keep = row >= k
        u = u + a * jnp.where(keep, pltpu.roll(u, k, 0), 0.0)
        a = a * jnp.where(keep, pltpu.roll(a, k, 0), 1.0)
    end_f = u + a * h0f
    in_f = jnp.where(row >= 1, pltpu.roll(end_f, 1, 0), h0f)
    a, u = pb, hb
    for k in (1, 2, 4):
        keep = row < s - k
        u = u + a * jnp.where(keep, pltpu.roll(u, s - k, 0), 0.0)
        a = a * jnp.where(keep, pltpu.roll(a, s - k, 0), 1.0)
    end_b = u + a * h0b
    in_b = jnp.where(row < s - 1, pltpu.roll(end_b, s - 1, 0), h0b)
    return in_f, in_b, end_f[s - 1:s, :], end_b[0:1, :]


PITCH_PAD = 4


def _to_split(x_ref, lanes, pad_ref, dst_ref, n):
    s = SUBLANES
    n_j = n // s
    pitch = n_j + PITCH_PAD
    for q in range(s):
        pad_ref[pl.ds(q * pitch, n_j), :] = x_ref[0, pl.ds(q * n_j, n_j), lanes]

    def body(j, c):
        dst_ref[pl.ds(pl.multiple_of(j * s, s), s), :] = pad_ref[pl.ds(j, s, stride=pitch), :]
        return c

    lax.fori_loop(0, n_j, body, 0, unroll=8)


LRU_BLOCKS_PER_STEP = 4


def _lru_body(lx_ref, lxc_ref, cw_ref, cb_ref, wa_ref, wx_ref, ba_ref, bx_ref, lam_ref,
              o_ref, af_ref, uf_ref, ab_ref, ub_ref, pad_ref):
    n = lx_ref.shape[1]
    n_c = lxc_ref.shape[1]
    s = SUBLANES
    n_j = n // s
    pitch = n_j + PITCH_PAD
    zero = jnp.zeros((1, LANES), F32)

    for blk in range(LRU_BLOCKS_PER_STEP):
        lanes = slice(blk * LRU_BLOCK, (blk + 1) * LRU_BLOCK)
        conv_w = cw_ref[:, lanes]
        conv_b = cb_ref[:, lanes]
        gate = [(wa_ref[d, blk], wx_ref[d, blk], ba_ref[d:d + 1, lanes], bx_ref[d:d + 1, lanes],
                 lam_ref[d:d + 1, lanes]) for d in range(2)]

        _to_split(lxc_ref, lanes, pad_ref, uf_ref, n_c)
        xc = _conv4(uf_ref[0:n_c, :], conv_w, conv_b)
        _gates(xc, *gate[0], af_ref, uf_ref)
        _gates(xc, *gate[1], ab_ref, ub_ref)
        ends = _scan_local(af_ref, uf_ref, ab_ref, ub_ref, n_c // s)
        _, _, cf, cb = _link_states(*ends, zero, zero)

        _to_split(lx_ref, lanes, pad_ref, uf_ref, n)
        xl = _conv4(uf_ref[...], conv_w, conv_b)
        _gates(xl, *gate[0], af_ref, uf_ref)
        _gates(xl, *gate[1], ab_ref, ub_ref)
        ends = _scan_local(af_ref, uf_ref, ab_ref, ub_ref, n_j)
        in_f, in_b, _, _ = _link_states(*ends, cf, cb)

        def finish(j, c):
            rows = pl.ds(pl.multiple_of(j * s, s), s)
            h = (uf_ref[rows, :] + af_ref[rows, :] * in_f) + (ub_ref[rows, :] + ab_ref[rows, :] * in_b)
            pad_ref[pl.ds(j, s, stride=pitch), :] = h
            return c

        lax.fori_loop(0, n_j, finish, 0, unroll=8)
        for q in range(s):
            o_ref[0, pl.ds(q * n_j, n_j), lanes] = pad_ref[pl.ds(q * pitch, n_j), :].astype(o_ref.dtype)


def _lru_call(lx3, lxc3, conv_w, conv_b, wa, wx, ba, bx, lam):
    bsz, seq, _ = lx3.shape
    n_ctx = lxc3.shape[1]
    per = LRU_BLOCKS_PER_STEP
    width = per * LRU_BLOCK
    col = lambda b, n: (b, 0, n)
    par = lambda b, n: (0, n)
    wspec = pl.BlockSpec((2, per, LRU_BLOCK, LRU_BLOCK), lambda b, n: (0, n, 0, 0))
    return pl.pallas_call(
        _lru_body,
        grid=(bsz, LRU_BLOCKS // per),
        in_specs=[pl.BlockSpec((1, seq, width), col),
                  pl.BlockSpec((1, n_ctx, width), col),
                  pl.BlockSpec((LRU_CONV, width), par),
                  pl.BlockSpec((1, width), par),
                  wspec, wspec,
                  pl.BlockSpec((2, width), par),
                  pl.BlockSpec((2, width), par),
                  pl.BlockSpec((2, width), par)],
        out_specs=pl.BlockSpec((1, seq, width), col),
        out_shape=jax.ShapeDtypeStruct((bsz, seq, LRU_WIDTH), BF16),
        scratch_shapes=[pltpu.VMEM((seq, LRU_BLOCK), F32)] * 4
                       + [pltpu.VMEM((seq + SUBLANES * PITCH_PAD, LRU_BLOCK), F32)],
        compiler_params=_cparams(("parallel", "arbitrary")),
        name="lru",
    )(lx3, lxc3, conv_w, conv_b, wa, wx, ba, bx, lam)


TOKEN_TILE = D_MODEL // LANES


def _store_token_tiles(ref, row0, x):
    m = x.shape[0]
    for c in range(TOKEN_TILE):
        ref[pl.ds(row0 + c, m, stride=TOKEN_TILE), :] = x[:, c * LANES:(c + 1) * LANES]


def _load_token_tiles(ref, row0, m):
    return jnp.concatenate([ref[pl.ds(row0 + c, m, stride=TOKEN_TILE), :] for c in range(TOKEN_TILE)], axis=1)


def _merge_body(x_ref, oa_ref, hs_ref, glu_ref, ga_ref, gb_ref, mod_ref, g_ref, wua_ref, wul_ref, wo_ref,
                wr_ref, br_ref, x1_ref, h2_ref, lt_ref):
    ga1 = mod_ref[0, :, 2 * D_MODEL:3 * D_MODEL]
    sh2 = mod_ref[0, :, 3 * D_MODEL:4 * D_MODEL]
    sc2 = mod_ref[0, :, 4 * D_MODEL:5 * D_MODEL]
    o_lru = hs_ref[...] * glu_ref[...]
    y = (ga_ref[...].astype(F32) * _dot(oa_ref[...], wua_ref[...])
         + gb_ref[...].astype(F32) * _dot(o_lru, wul_ref[...]))
    x1 = x_ref[...] + ga1 * _dot(y.astype(BF16), wo_ref[...])
    x1_ref[...] = x1
    h2 = _ada_norm(x1, g_ref[...], sc2, sh2)
    lt_ref[...] = _dot3_nt(wr_ref[...], h2) + br_ref[:, 0:1]
    _store_token_tiles(h2_ref, 0, h2)


def _merge_call(x2, o_att, hs, glu, ga, gb, mod3, g_ffn, wua, wul, wo, wr_t, br, seq):
    t = x2.shape[0]
    tm = MERGE_TM
    per_b = seq // tm
    row = lambda i: (i, 0)
    full = lambda i: (0, 0)
    resident = lambda shape: pl.BlockSpec(shape, full, pipeline_mode=pl.Buffered(1))
    return pl.pallas_call(
        _merge_body,
        grid=(t // tm,),
        in_specs=[pl.BlockSpec((tm, D_MODEL), row),
                  pl.BlockSpec((tm, NA_WIDTH), row),
                  pl.BlockSpec((tm, LRU_WIDTH), row),
                  pl.BlockSpec((tm, LRU_WIDTH), row),
                  pl.BlockSpec((tm, D_MODEL), row),
                  pl.BlockSpec((tm, D_MODEL), row),
                  pl.BlockSpec((1, 1, 6 * D_MODEL), lambda i: (i // per_b, 0, 0)),
                  pl.BlockSpec((1, D_MODEL), full),
                  resident((NA_WIDTH, D_MODEL)),
                  resident((LRU_WIDTH, D_MODEL)),
                  resident((D_MODEL, D_MODEL)),
                  pl.BlockSpec((ROUTE_ROWS, D_MODEL), full),
                  pl.BlockSpec((ROUTE_ROWS, LANES), full)],
        out_specs=[pl.BlockSpec((tm, D_MODEL), row),
                   pl.BlockSpec((tm * TOKEN_TILE, LANES), row),
                   pl.BlockSpec((ROUTE_ROWS, tm), lambda i: (0, i))],
        out_shape=[jax.ShapeDtypeStruct((t, D_MODEL), F32),
                   jax.ShapeDtypeStruct((t * TOKEN_TILE, LANES), F32),
                   jax.ShapeDtypeStruct((ROUTE_ROWS, t), F32)],
        compiler_params=_cparams(("parallel",)),
        name="merge",
    )(x2, o_att, hs, glu, ga, gb, mod3, g_ffn, wua, wul, wo, wr_t, br)


def _route_body(lt_ref, eid_ref, gate_ref, rank_ref, cnt_ref, carry_ref):
    step = pl.program_id(0)

    @pl.when(step == 0)
    def _():
        carry_ref[...] = jnp.zeros_like(carry_ref)

    tb = lt_ref.shape[1]
    lg = [lt_ref[r:r + 1, :] for r in range(N_GROUPS)]
    best = lg[0]
    gidx = jnp.zeros((1, tb), I32)
    for r in range(1, N_GROUPS):
        better = lg[r] > best
        gidx = jnp.where(better, r, gidx)
        best = jnp.maximum(best, lg[r])
    den = jnp.exp(lg[0] - best)
    for r in range(1, N_GROUPS):
        den = den + jnp.exp(lg[r] - best)
    p_top = 1.0 / den

    ev = []
    for j in range(EXPERTS_PER_GROUP):
        sel = lt_ref[N_GROUPS + j:N_GROUPS + j + 1, :]
        for g in range(1, N_GROUPS):
            row = N_GROUPS + g * EXPERTS_PER_GROUP + j
            sel = jnp.where(gidx == g, lt_ref[row:row + 1, :], sel)
        ev.append(sel)
    v0 = ev[0]
    i0 = jnp.zeros((1, tb), I32)
    for j in range(1, EXPERTS_PER_GROUP):
        better = ev[j] > v0
        i0 = jnp.where(better, j, i0)
        v0 = jnp.maximum(v0, ev[j])
    v1 = jnp.full((1, tb), -jnp.inf, F32)
    i1 = jnp.zeros((1, tb), I32)
    for j in range(EXPERTS_PER_GROUP):
        better = (ev[j] > v1) & (i0 != j)
        i1 = jnp.where(better, j, i1)
        v1 = jnp.where(better, ev[j], v1)
    e1 = jnp.exp(v1 - v0)
    inv = 1.0 / (1.0 + e1)
    eid0 = gidx * EXPERTS_PER_GROUP + i0
    eid1 = gidx * EXPERTS_PER_GROUP + i1
    eid_ref[0:1, :] = eid0
    eid_ref[1:2, :] = eid1
    gate_ref[...] = jnp.zeros_like(gate_ref)
    gate_ref[0:1, :] = p_top * inv
    gate_ref[1:2, :] = p_top * (e1 * inv)

    sub = 256
    e_iota = lax.broadcasted_iota(I32, (N_EXPERTS, sub), 0)
    tri = jnp.where(lax.broadcasted_iota(I32, (sub, sub), 0) <= lax.broadcasted_iota(I32, (sub, sub), 1),
                    1.0, 0.0).astype(BF16)
    carry = carry_ref[...]
    for c in range(tb // sub):
        sl = slice(c * sub, (c + 1) * sub)
        m0 = eid0[:, sl] == e_iota
        m1 = eid1[:, sl] == e_iota
        oh = jnp.where(m0 | m1, 1.0, 0.0)
        incl = _dot(oh.astype(BF16), tri)
        excl = incl - oh + carry[:, 0:1]
        rank_ref[0:1, sl] = jnp.sum(jnp.where(m0, excl, 0.0), axis=0, keepdims=True).astype(I32)
        rank_ref[1:2, sl] = jnp.sum(jnp.where(m1, excl, 0.0), axis=0, keepdims=True).astype(I32)
        carry = carry + incl[:, sub - 1:sub]
    carry_ref[...] = carry
    cnt_ref[...] = carry


def _route_call(logits_t):
    t = logits_t.shape[1]
    tb = ROUTE_TB
    col = lambda i: (0, i)
    return pl.pallas_call(
        _route_body,
        grid=(t // tb,),
        in_specs=[pl.BlockSpec((ROUTE_ROWS, tb), col)],
        out_specs=[pl.BlockSpec((TOP_K, tb), col), pl.BlockSpec((SUBLANES, tb), col),
                   pl.BlockSpec((TOP_K, tb), col), pl.BlockSpec((N_EXPERTS, LANES), lambda i: (0, 0))],
        out_shape=[jax.ShapeDtypeStruct((TOP_K, t), I32), jax.ShapeDtypeStruct((SUBLANES, t), F32),
                   jax.ShapeDtypeStruct((TOP_K, t), I32), jax.ShapeDtypeStruct((N_EXPERTS, LANES), F32)],
        scratch_shapes=[pltpu.VMEM((N_EXPERTS, LANES), F32)],
        compiler_params=_cparams(("arbitrary",)),
        name="route",
    )(logits_t)


def _dest_body(cnt_ref, eid_ref, rank_ref, dest_ref, blk_ref):
    cnt = cnt_ref[...].astype(I32)
    padded = ((cnt + (MOE_BLK - 1)) >> MOE_BLK_LOG2) << MOE_BLK_LOG2
    e_iota = lax.broadcasted_iota(I32, (N_EXPERTS, LANES), 0)
    p_end = jnp.zeros((N_EXPERTS, LANES), I32)
    for e in range(N_EXPERTS):
        tot = jnp.sum(jnp.where(e_iota <= e, padded, 0), axis=0, keepdims=True)
        p_end = jnp.where(e_iota == e, tot, p_end)
    p_start = p_end - padded
    tb = eid_ref.shape[1]
    ps = jnp.concatenate([p_start] * (tb // LANES), axis=1)
    e_wide = lax.broadcasted_iota(I32, (N_EXPERTS, tb), 0)
    for k in range(TOP_K):
        start = jnp.sum(jnp.where(eid_ref[k:k + 1, :] == e_wide, ps, 0), axis=0, keepdims=True)
        dest_ref[k:k + 1, :] = start + rank_ref[k:k + 1, :]
    nb = blk_ref.shape[1]
    pe = jnp.concatenate([p_end] * (nb // LANES), axis=1)
    first_row = lax.broadcasted_iota(I32, (N_EXPERTS, nb), 1) * MOE_BLK
    n_before = jnp.sum(jnp.where(pe <= first_row, 1, 0), axis=0, keepdims=True)
    blk = jnp.minimum(n_before, N_EXPERTS - 1)
    blk_ref[...] = jnp.broadcast_to(blk, blk_ref.shape)
    blk_ref[1:2, :] = jnp.broadcast_to(p_end[N_EXPERTS - 1:N_EXPERTS, 0:1] >> MOE_BLK_LOG2, (1, nb))
    on_diag = e_iota == lax.broadcasted_iota(I32, (N_EXPERTS, LANES), 1)
    blk_ref[2:3, 0:LANES] = jnp.sum(jnp.where(on_diag, p_end, 0), axis=0, keepdims=True)
    blk_ref[3:4, 0:LANES] = jnp.sum(jnp.where(on_diag, padded, 0), axis=0, keepdims=True)


def _dest_call(cnt, eid, rank, nb_pad):
    t = eid.shape[1]
    tb = ROUTE_TB
    col = lambda i: (0, i)
    return pl.pallas_call(
        _dest_body,
        grid=(t // tb,),
        in_specs=[pl.BlockSpec((N_EXPERTS, LANES), lambda i: (0, 0)),
                  pl.BlockSpec((TOP_K, tb), col), pl.BlockSpec((TOP_K, tb), col)],
        out_specs=[pl.BlockSpec((TOP_K, tb), col), pl.BlockSpec((SUBLANES, nb_pad), lambda i: (0, 0))],
        out_shape=[jax.ShapeDtypeStruct((TOP_K, t), I32), jax.ShapeDtypeStruct((SUBLANES, nb_pad), I32)],
        compiler_params=_cparams(("arbitrary",)),
        name="dest",
    )(cnt, eid, rank)


def _token_tile(ref, t):
    return ref.at[pl.ds(pl.multiple_of(t * TOKEN_TILE, TOKEN_TILE), TOKEN_TILE)]


def _dispatch_body(dest_ref, pend_ref, plen_ref, h_ref, xs_ref, zero_ref, sem, zsem):
    tm = h_ref.shape[0] // TOKEN_TILE
    blk_rows = MOE_BLK * TOKEN_TILE
    n_tok = pl.num_programs(0) * tm
    base = pl.program_id(0) * tm

    @pl.when(pl.program_id(0) == 0)
    def _():
        zero_ref[...] = jnp.zeros_like(zero_ref)

        def zero_block(start):
            rows = pl.ds(pl.multiple_of(start * TOKEN_TILE, blk_rows), blk_rows)
            return pltpu.make_async_copy(zero_ref, xs_ref.at[rows], zsem)

        def fill(e, c):
            @pl.when(plen_ref[e] > 0)
            def _():
                zero_block(pend_ref[e] - MOE_BLK).start()
            return c

        def drain(e, c):
            @pl.when(plen_ref[e] > 0)
            def _():
                zero_block(pend_ref[e] - MOE_BLK).wait()
            return c

        lax.fori_loop(0, N_EXPERTS, fill, 0)
        n_used = pend_ref[N_EXPERTS - 1] >> MOE_BLK_LOG2
        n_blk = xs_ref.shape[0] // blk_rows
        lax.fori_loop(n_used, n_blk, lambda j, c: (zero_block(j * MOE_BLK).start(), c)[1], 0)
        lax.fori_loop(0, N_EXPERTS, drain, 0)
        lax.fori_loop(n_used, n_blk, lambda j, c: (zero_block(j * MOE_BLK).wait(), c)[1], 0)

    def issue(r, c):
        for k in range(TOP_K):
            d = dest_ref[k * n_tok + base + r]
            pltpu.make_async_copy(_token_tile(h_ref, r), _token_tile(xs_ref, d), sem).start(priority=k)
        return c

    lax.fori_loop(0, tm, issue, 0, unroll=8)
    for k in range(TOP_K):
        pltpu.make_async_copy(h_ref, xs_ref.at[pl.ds(0, tm * TOKEN_TILE)], sem).wait()


def _dispatch_call(dest_flat, p_end, p_len, h2t, n_slots):
    rows = DISPATCH_TM * TOKEN_TILE
    return pl.pallas_call(
        _dispatch_body,
        grid_spec=pltpu.PrefetchScalarGridSpec(
            num_scalar_prefetch=3,
            grid=(h2t.shape[0] // rows,),
            in_specs=[pl.BlockSpec((rows, LANES), lambda i, d, pe, pn: (i, 0))],
            out_specs=pl.BlockSpec(memory_space=pl.ANY),
            scratch_shapes=[pltpu.VMEM((MOE_BLK * TOKEN_TILE, LANES), F32),
                            pltpu.SemaphoreType.DMA(()), pltpu.SemaphoreType.DMA(())]),
        out_shape=jax.ShapeDtypeStruct((n_slots * TOKEN_TILE, LANES), F32),
        compiler_params=_cparams(("arbitrary",)),
        name="dispatch",
    )(dest_flat, p_end, p_len, h2t)


EXPERT_BLKS_PER_STEP = 4


def _experts_body(blk_ref, used_ref, xs_ref, w1_hbm, w3_hbm, w2_hbm, y_ref,
                  f1_ref, f3_ref, f2_ref, b1_ref, b3_ref, b2_ref, loaded_ref, sem):
    i = pl.program_id(0)
    per = EXPERT_BLKS_PER_STEP
    used = used_ref[0]

    def weight_copies(e):
        return [pltpu.make_async_copy(w_hbm.at[e], f_ref, sem)
                for w_hbm, f_ref in ((w1_hbm, f1_ref), (w3_hbm, f3_ref), (w2_hbm, f2_ref))]

    @pl.when(i == 0)
    def _():
        loaded_ref[0] = -1
        for cp in weight_copies(blk_ref[0]):
            cp.start()

    for half in range(per):
        blk = per * i + half
        row0 = half * MOE_BLK * TOKEN_TILE
        e = blk_ref[blk]

        @pl.when((blk < used) & (e != loaded_ref[0]))
        def _():
            for cp in weight_copies(e):
                cp.wait()
            b1_ref[...] = f1_ref[...].astype(BF16)
            b3_ref[...] = f3_ref[...].astype(BF16)
            b2_ref[...] = f2_ref[...].astype(BF16)
            loaded_ref[0] = e
            nxt = lax.while_loop(lambda j: (j < used) & (blk_ref[jnp.minimum(j, blk_ref.shape[0] - 1)] == e),
                                 lambda j: j + 1, blk + 1)

            @pl.when(nxt < used)
            def _():
                for cp in weight_copies(blk_ref[jnp.minimum(nxt, blk_ref.shape[0] - 1)]):
                    cp.start()

        @pl.when(blk < used)
        def _():
            x = _load_token_tiles(xs_ref, row0, MOE_BLK).astype(BF16)
            g = _dot(x, b1_ref[...])
            u = _dot(x, b3_ref[...])
            mid = (g * _sigmoid(g)) * u
            _store_token_tiles(y_ref, row0, _dot(mid.astype(BF16), b2_ref[...]))

        @pl.when(blk >= used)
        def _():
            y_ref[pl.ds(row0, MOE_BLK * TOKEN_TILE), :] = jnp.zeros((MOE_BLK * TOKEN_TILE, LANES), F32)


def _experts_call(blk_e, n_used, xs, w1, w3, w2):
    per = EXPERT_BLKS_PER_STEP
    rows = per * MOE_BLK * TOKEN_TILE
    nb = xs.shape[0] // (MOE_BLK * TOKEN_TILE)
    assert nb % per == 0 and blk_e.shape[0] == nb
    hbm = pl.BlockSpec(memory_space=pl.ANY)
    return pl.pallas_call(
        _experts_body,
        grid_spec=pltpu.PrefetchScalarGridSpec(
            num_scalar_prefetch=2,
            grid=(nb // per,),
            in_specs=[pl.BlockSpec((rows, LANES), lambda i, blk, used: (i, 0)), hbm, hbm, hbm],
            out_specs=pl.BlockSpec((rows, LANES), lambda i, blk, used: (i, 0)),
            scratch_shapes=[pltpu.VMEM((D_MODEL, D_EXPERT), F32), pltpu.VMEM((D_MODEL, D_EXPERT), F32),
                            pltpu.VMEM((D_EXPERT, D_MODEL), F32),
                            pltpu.VMEM((D_MODEL, D_EXPERT), BF16), pltpu.VMEM((D_MODEL, D_EXPERT), BF16),
                            pltpu.VMEM((D_EXPERT, D_MODEL), BF16),
                            pltpu.SMEM((1,), I32), pltpu.SemaphoreType.DMA(())]),
        out_shape=jax.ShapeDtypeStruct(xs.shape, F32),
        compiler_params=_cparams(("arbitrary",)),
        name="experts",
    )(blk_e, n_used, xs, w1, w3, w2)


def _combine_body(dest_ref, x1_ref, gate_ref, mod_ref, gf_ref, y_ref, o_ref, buf_ref, sem):
    tm = x1_ref.shape[0]
    step = pl.program_id(0)
    n_step = pl.num_programs(0)
    n_tok = n_step * tm
    slot = step % 2
    region = tm * TOKEN_TILE

    def region_row0(buf, k):
        return pl.multiple_of((buf * TOP_K + k) * region, region)

    def start_gather(for_step):
        def issue(r, c):
            for k in range(TOP_K):
                d = dest_ref[k * n_tok + for_step * tm + r]
                dst = buf_ref.at[pl.ds(pl.multiple_of(region_row0(for_step % 2, k) + r * TOKEN_TILE, TOKEN_TILE),
                                       TOKEN_TILE)]
                pltpu.make_async_copy(_token_tile(y_ref, d), dst, sem.at[for_step % 2]).start(priority=k)
            return c

        lax.fori_loop(0, tm, issue, 0, unroll=8)

    @pl.when(step == 0)
    def _():
        start_gather(step)

    @pl.when(step + 1 < n_step)
    def _():
        start_gather(step + 1)

    eye = jnp.where(lax.broadcasted_iota(I32, (tm, tm), 0) == lax.broadcasted_iota(I32, (tm, tm), 1),
                    1.0, 0.0).astype(BF16)
    g = gate_ref[...]
    g1 = g.astype(BF16)
    rem = g - g1.astype(F32)
    g2 = rem.astype(BF16)
    g3 = (rem - g2.astype(F32)).astype(BF16)
    gt = _dot_nt(eye, g1) + (_dot_nt(eye, g2) + _dot_nt(eye, g3))

    for k in range(TOP_K):
        pltpu.make_async_copy(y_ref.at[pl.ds(0, region)], buf_ref.at[pl.ds(region_row0(slot, k), region)],
                              sem.at[slot]).wait()

    ga2 = mod_ref[0, :, 5 * D_MODEL:6 * D_MODEL]
    moe = (gt[:, 0:1] * _load_token_tiles(buf_ref, region_row0(slot, 0), tm)
           + gt[:, 1:2] * _load_token_tiles(buf_ref, region_row0(slot, 1), tm))
    x2 = x1_ref[...] + ga2 * moe
    ms = jnp.mean(x2 * x2, axis=-1, keepdims=True)
    o_ref[...] = x2 * lax.rsqrt(ms + EPS) * gf_ref[...]


def _combine_call(dest_flat, x1, gate, mod3, g_final, y, seq):
    t = x1.shape[0]
    tm = COMBINE_TM
    per_b = seq // tm
    return pl.pallas_call(
        _combine_body,
        grid_spec=pltpu.PrefetchScalarGridSpec(
            num_scalar_prefetch=1,
            grid=(t // tm,),
            in_specs=[pl.BlockSpec((tm, D_MODEL), lambda i, d: (i, 0)),
                      pl.BlockSpec((SUBLANES, tm), lambda i, d: (0, i)),
                      pl.BlockSpec((1, 1, 6 * D_MODEL), lambda i, d: (i // per_b, 0, 0)),
                      pl.BlockSpec((1, D_MODEL), lambda i, d: (0, 0)),
                      pl.BlockSpec(memory_space=pl.ANY)],
            out_specs=pl.BlockSpec((tm, D_MODEL), lambda i, d: (i, 0)),
            scratch_shapes=[pltpu.VMEM((2 * TOP_K * tm * TOKEN_TILE, LANES), F32),
                            pltpu.SemaphoreType.DMA((2,))]),
        out_shape=jax.ShapeDtypeStruct((t, D_MODEL), F32),
        compiler_params=_cparams(("arbitrary",)),
        name="combine",
    )(dest_flat, x1, gate, mod3, g_final, y)


def _rope_tables(seq):
    half = NA_HEAD_DIM // 2
    nf = half // 2
    inv_freq = ROPE_THETA ** (-jnp.arange(nf, dtype=F32) / nf)
    t = jnp.arange(seq)
    row_pos = (t // GRID_W).astype(F32)
    col_pos = (t % GRID_W).astype(F32)
    ang_r = row_pos[:, None] * inv_freq
    ang_c = col_pos[:, None] * inv_freq
    cos = jnp.concatenate([jnp.cos(ang_r), jnp.cos(ang_r), jnp.cos(ang_c), jnp.cos(ang_c)], axis=-1)
    sin = jnp.concatenate([-jnp.sin(ang_r), jnp.sin(ang_r), -jnp.sin(ang_c), jnp.sin(ang_c)], axis=-1)
    return jnp.tile(cos, (1, NA_HEADS)), jnp.tile(sin, (1, NA_HEADS))


def _layer(x, c, ctx, c_ctx, w_mod, b_mod, g_mix, g_ffn, w_in, rpb, conv_w, conv_b, lru_wa, lru_ba,
           lru_wx, lru_bx, lru_lambda, w_up_attn, w_up_lru, w_out, wg, bg, we, be, w1, w3, w2, g_final):
    bsz, seq, d = x.shape
    n_ctx = ctx.shape[1]
    t = bsz * seq
    assert d == D_MODEL and seq % ATT_TQ == 0 and seq // ATT_TQ > K_ROW_BLOCKS and n_ctx % ATT_TQ == 0
    assert bsz + 1 <= MOD_ROWS and seq % PROJ_TM == 0 and seq % MERGE_TM == 0
    assert t % ROUTE_TB == 0 and t % COMBINE_TM == 0 and t % DISPATCH_TM == 0
    assert (bsz * n_ctx) % PROJ_TM == 0 and n_ctx <= seq
    assert seq % (SUBLANES * SUBLANES) == 0 and n_ctx % (SUBLANES * SUBLANES) == 0
    assert (seq // SUBLANES) % SCAN_STEPS == 0 and (n_ctx // SUBLANES) % SCAN_STEPS == 0

    cc = jnp.concatenate([c, c_ctx[None, :], jnp.zeros((MOD_ROWS - bsz - 1, d), F32)], axis=0)
    mod = _mod_call(cc, w_mod, b_mod)
    mod3 = mod[:bsz].reshape(bsz, 1, 6 * d)
    mod_c = mod[bsz:bsz + 1]

    x2 = x.reshape(t, d)
    g_mix2 = g_mix.reshape(1, d)
    w_in_bf = w_in.astype(BF16)
    kc, vc, lxc = _ctxproj_call(ctx.reshape(bsz * n_ctx, d), mod_c, g_mix2, w_in_bf[:, :CTX_COLS])
    cos_t, sin_t = _rope_tables(seq)
    q_rot, q_plain, k, v, lx, glu, ga, gb = _inproj_call(x2, mod3, g_mix2, w_in_bf, cos_t, sin_t, seq)

    bias = _bias_tables(_rpbcol_call(rpb))
    o_att = _attn_call(q_rot, q_plain, k, v, kc, vc, bias, bsz, seq, n_ctx)

    hs = _lru_call(lx.reshape(bsz, seq, LRU_WIDTH), lxc.reshape(bsz, n_ctx, LRU_WIDTH),
                   conv_w, conv_b.reshape(1, LRU_WIDTH), lru_wa, lru_wx, lru_ba, lru_bx, lru_lambda)

    wr_t = jnp.concatenate([wg.T, we.T, jnp.zeros((ROUTE_ROWS - N_GROUPS - N_EXPERTS, d), F32)], axis=0)
    br = jnp.concatenate([bg, be, jnp.zeros((ROUTE_ROWS - N_GROUPS - N_EXPERTS,), F32)])
    br = jnp.broadcast_to(br[:, None], (ROUTE_ROWS, LANES))
    x1, h2, logits_t = _merge_call(x2, o_att, hs.reshape(t, LRU_WIDTH), glu, ga, gb, mod3,
                                   g_ffn.reshape(1, d), w_up_attn.astype(BF16), w_up_lru.astype(BF16),
                                   w_out.astype(BF16), wr_t, br, seq)

    eid, gate, rank, cnt = _route_call(logits_t)
    n_blk = -(-(t * TOP_K + N_EXPERTS * (MOE_BLK - 1)) // MOE_BLK)
    n_blk = -(-n_blk // EXPERT_BLKS_PER_STEP) * EXPERT_BLKS_PER_STEP
    nb_pad = -(-n_blk // LANES) * LANES
    dest, blk = _dest_call(cnt, eid, rank, nb_pad)
    dest_flat = dest.reshape(TOP_K * t)
    xs = _dispatch_call(dest_flat, blk[2, :N_EXPERTS], blk[3, :N_EXPERTS], h2, n_blk * MOE_BLK)
    y = _experts_call(blk[0, :n_blk], blk[1, :1], xs, w1, w3, w2)
    return _combine_call(dest_flat, x1, gate, mod3, g_final.reshape(1, d), y, seq).reshape(bsz, seq, d)


def kernel(x, c, ctx, c_ctx, w_mod, b_mod, g_mix, g_ffn, w_in, rpb, conv_w, conv_b, lru_wa, lru_ba, lru_wx,
           lru_bx, lru_lambda, w_up_attn, w_up_lru, w_out, router_group_w, router_group_b, router_expert_w,
           router_expert_b, expert_w_gate, expert_w_up, expert_w_down, g_final):
    assert w_mod.shape[0] == 1, "single-layer block"
    return _layer(x, c, ctx, c_ctx, w_mod[0], b_mod[0], g_mix[0], g_ffn[0], w_in[0], rpb[0], conv_w[0],
                  conv_b[0], lru_wa[0], lru_ba[0], lru_wx[0], lru_bx[0], lru_lambda[0], w_up_attn[0],
                  w_up_lru[0], w_out[0], router_group_w[0], router_group_b[0], router_expert_w[0],
                  router_expert_b[0], expert_w_gate[0], expert_w_up[0], expert_w_down[0], g_final)
```

```python
import functools

import jax
import jax.numpy as jnp
from jax import lax
from jax.experimental import pallas as pl
from jax.experimental.pallas import tpu as pltpu

F32 = jnp.float32
BF16 = jnp.bfloat16
I32 = jnp.int32

D_MODEL = 1024
GRID_W = 64
EPS = 1e-6
NEG_INF = -1e30

NA_HEADS = 8
NA_HEAD_DIM = 64
NA_WIDTH = NA_HEADS * NA_HEAD_DIM
NA_WIN_ROWS = 8
NA_WIN_COLS = 16
ROPE_THETA = 10000.0

LRU_WIDTH = D_MODEL
LRU_BLOCKS = 8
LRU_BLOCK = LRU_WIDTH // LRU_BLOCKS
LRU_CONV = 4
LRU_C = 8.0

N_GROUPS = 4
EXPERTS_PER_GROUP = 8
N_EXPERTS = N_GROUPS * EXPERTS_PER_GROUP
TOP_K = 2
D_EXPERT = 512

K_OFF = 0
V_OFF = K_OFF + NA_WIDTH
LX_OFF = V_OFF + NA_WIDTH
CTX_COLS = LX_OFF + LRU_WIDTH
Q_OFF = CTX_COLS
LG_OFF = Q_OFF + NA_WIDTH
GA_OFF = LG_OFF + LRU_WIDTH
GB_OFF = GA_OFF + D_MODEL
PROJ_COLS = GB_OFF + D_MODEL

LANES = 128
SUBLANES = 8

Q_ROWS = 4
K_ROW_BLOCKS = 3
ATT_TQ = Q_ROWS * GRID_W
ATT_TK = K_ROW_BLOCKS * ATT_TQ

MOE_BLK_LOG2 = 8
MOE_BLK = 1 << MOE_BLK_LOG2
MOD_ROWS = 24
ROUTE_ROWS = 64

PROJ_TM = 512
MERGE_TM = 512
ROUTE_TB = 2048
DISPATCH_TM = 2048
COMBINE_TM = 512

VMEM_LIMIT = 56 * 1024 * 1024


def _cparams(sem, vmem=VMEM_LIMIT):
    return pltpu.CompilerParams(dimension_semantics=sem, vmem_limit_bytes=vmem)


def _dot(a, b):
    return jnp.dot(a, b, preferred_element_type=F32)


def _dot_nt(a, b):
    return lax.dot_general(a, b, (((1,), (1,)), ((), ())), preferred_element_type=F32)


def _split2(a):
    hi = a.astype(BF16)
    lo = (a - hi.astype(F32)).astype(BF16)
    return hi, lo


def _dot3(a, b):
    ah, al = _split2(a)
    bh, bl = _split2(b)
    return _dot(ah, bh) + (_dot(ah, bl) + _dot(al, bh))


def _dot3_nt(a, b):
    ah, al = _split2(a)
    bh, bl = _split2(b)
    return _dot_nt(ah, bh) + (_dot_nt(ah, bl) + _dot_nt(al, bh))


def _sigmoid(x):
    return 1.0 / (1.0 + jnp.exp(-x))


def _ada_norm(x, g, sc, sh):
    ms = jnp.mean(x * x, axis=-1, keepdims=True)
    return (x * lax.rsqrt(ms + EPS) * g) * (1.0 + sc) + sh


def _mod_body(cc_ref, w_ref, b_ref, o_ref):
    cc = cc_ref[...]
    o_ref[...] = _dot3(cc * _sigmoid(cc), w_ref[...]) + b_ref[...]


def _mod_call(cc, w_mod, b_mod):
    n = w_mod.shape[1]
    bn = 1024
    return pl.pallas_call(
        _mod_body,
        grid=(n // bn,),
        in_specs=[pl.BlockSpec((MOD_ROWS, D_MODEL), lambda j: (0, 0)),
                  pl.BlockSpec((D_MODEL, bn), lambda j: (0, j)),
                  pl.BlockSpec((1, bn), lambda j: (0, j))],
        out_specs=pl.BlockSpec((MOD_ROWS, bn), lambda j: (0, j)),
        out_shape=jax.ShapeDtypeStruct((MOD_ROWS, n), F32),
        compiler_params=_cparams(("arbitrary",)),
        name="mod",
    )(cc, w_mod, b_mod.reshape(1, n))


def _rope(t, cos, sin):
    lane = lax.broadcasted_iota(I32, (t.shape[0], LANES), 1)
    first = (lane & 16) == 0
    parts = []
    for c in range(t.shape[1] // LANES):
        tc = t[:, c * LANES:(c + 1) * LANES]
        parts.append(jnp.where(first, pltpu.roll(tc, LANES - 16, 1), pltpu.roll(tc, 16, 1)))
    partner = jnp.concatenate(parts, axis=1)
    return t * cos + partner * sin


def _gelu_tanh(x):
    return 0.5 * x * (1.0 + jnp.tanh(0.7978845608028654 * (x + 0.044715 * (x * x * x))))


def _inproj_body(x_ref, mod_ref, g_ref, w_ref, cos_ref, sin_ref,
                 qre_ref, qro_ref, qpe_ref, qpo_ref, k_ref, v_ref, lx_ref, glu_ref, ga_ref, gb_ref):
    sh = mod_ref[0, :, 0:D_MODEL]
    sc = mod_ref[0, :, D_MODEL:2 * D_MODEL]
    h = _ada_norm(x_ref[...], g_ref[...], sc, sh).astype(BF16)
    cos = cos_ref[...]
    sin = sin_ref[...]
    scale = NA_HEAD_DIM ** -0.5

    k_ref[...] = _rope(_dot(h, w_ref[:, K_OFF:K_OFF + NA_WIDTH]), cos, sin).astype(BF16)
    v_ref[...] = _dot(h, w_ref[:, V_OFF:V_OFF + NA_WIDTH]).astype(BF16)
    lx_ref[...] = _dot(h, w_ref[:, LX_OFF:LX_OFF + LRU_WIDTH])

    q = _dot(h, w_ref[:, Q_OFF:Q_OFF + NA_WIDTH]) * scale
    qr = _rope(q, cos, sin)
    lane = lax.broadcasted_iota(I32, q.shape, 1)
    even = (lane & NA_HEAD_DIM) == 0
    qre_ref[...] = jnp.where(even, qr, 0.0).astype(BF16)
    qro_ref[...] = jnp.where(even, 0.0, qr).astype(BF16)
    qpe_ref[...] = jnp.where(even, q, 0.0).astype(BF16)
    qpo_ref[...] = jnp.where(even, 0.0, q).astype(BF16)

    glu_ref[...] = _gelu_tanh(_dot(h, w_ref[:, LG_OFF:LG_OFF + LRU_WIDTH])).astype(BF16)
    ga_ref[...] = _sigmoid(_dot(h, w_ref[:, GA_OFF:GA_OFF + D_MODEL])).astype(BF16)
    gb_ref[...] = _sigmoid(_dot(h, w_ref[:, GB_OFF:GB_OFF + D_MODEL])).astype(BF16)


def _inproj_call(x2, mod3, g_mix, w_in_bf, cos_t, sin_t, seq):
    t = x2.shape[0]
    tm = PROJ_TM
    per_b = seq // tm
    row = lambda i: (i, 0)
    wide = lambda n, dt: jax.ShapeDtypeStruct((t, n), dt)
    return pl.pallas_call(
        _inproj_body,
        grid=(t // tm,),
        in_specs=[pl.BlockSpec((tm, D_MODEL), row),
                  pl.BlockSpec((1, 1, 6 * D_MODEL), lambda i: (i // per_b, 0, 0)),
                  pl.BlockSpec((1, D_MODEL), lambda i: (0, 0)),
                  pl.BlockSpec((D_MODEL, PROJ_COLS), lambda i: (0, 0), pipeline_mode=pl.Buffered(1)),
                  pl.BlockSpec((tm, NA_WIDTH), lambda i: (i % per_b, 0)),
                  pl.BlockSpec((tm, NA_WIDTH), lambda i: (i % per_b, 0))],
        out_specs=[pl.BlockSpec((tm, NA_WIDTH), row)] * 6
                  + [pl.BlockSpec((tm, LRU_WIDTH), row)] * 4,
        out_shape=[wide(NA_WIDTH, BF16)] * 6
                  + [wide(LRU_WIDTH, F32), wide(LRU_WIDTH, BF16), wide(D_MODEL, BF16), wide(D_MODEL, BF16)],
        compiler_params=_cparams(("parallel",)),
        name="inproj",
    )(x2, mod3, g_mix, w_in_bf, cos_t, sin_t)


def _ctxproj_body(x_ref, mod_ref, g_ref, w_ref, k_ref, v_ref, lx_ref):
    sh = mod_ref[:, 0:D_MODEL]
    sc = mod_ref[:, D_MODEL:2 * D_MODEL]
    h = _ada_norm(x_ref[...], g_ref[...], sc, sh).astype(BF16)
    k_ref[...] = _dot(h, w_ref[:, K_OFF:K_OFF + NA_WIDTH]).astype(BF16)
    v_ref[...] = _dot(h, w_ref[:, V_OFF:V_OFF + NA_WIDTH]).astype(BF16)
    lx_ref[...] = _dot(h, w_ref[:, LX_OFF:LX_OFF + LRU_WIDTH])


def _ctxproj_call(c2, mod_c, g_mix, w_ctx_bf):
    t = c2.shape[0]
    tm = PROJ_TM
    row = lambda i: (i, 0)
    return pl.pallas_call(
        _ctxproj_body,
        grid=(t // tm,),
        in_specs=[pl.BlockSpec((tm, D_MODEL), row),
                  pl.BlockSpec((1, 6 * D_MODEL), lambda i: (0, 0)),
                  pl.BlockSpec((1, D_MODEL), lambda i: (0, 0)),
                  pl.BlockSpec((D_MODEL, CTX_COLS), lambda i: (0, 0))],
        out_specs=[pl.BlockSpec((tm, NA_WIDTH), row), pl.BlockSpec((tm, NA_WIDTH), row),
                   pl.BlockSpec((tm, LRU_WIDTH), row)],
        out_shape=[jax.ShapeDtypeStruct((t, NA_WIDTH), BF16), jax.ShapeDtypeStruct((t, NA_WIDTH), BF16),
                   jax.ShapeDtypeStruct((t, LRU_WIDTH), F32)],
        compiler_params=_cparams(("parallel",)),
        name="ctxproj",
    )(c2, mod_c, g_mix, w_ctx_bf)


N_DR = 2 * NA_WIN_ROWS - 1
N_DC = 2 * NA_WIN_COLS - 1


def _rpbcol_body(rpb_ref, o_ref):
    n = GRID_W * GRID_W
    flat = lax.broadcasted_iota(I32, (32, n), 1)
    qc = flat >> 6
    kc = flat & (GRID_W - 1)
    dc = jnp.clip(kc - qc, 1 - NA_WIN_COLS, NA_WIN_COLS - 1) + (NA_WIN_COLS - 1)
    d_iota = lax.broadcasted_iota(I32, (32, n), 0)
    onehot = jnp.where(dc == d_iota, 1.0, 0.0).astype(BF16)
    r = rpb_ref[...]
    r1 = r.astype(BF16)
    rem = r - r1.astype(F32)
    r2 = rem.astype(BF16)
    r3 = (rem - r2.astype(F32)).astype(BF16)
    val = _dot(r1, onehot) + (_dot(r2, onehot) + _dot(r3, onehot))
    qc1 = qc[0:1, :]
    kc1 = kc[0:1, :]
    c_start = jnp.clip(qc1 - NA_WIN_COLS // 2, 0, GRID_W - NA_WIN_COLS)
    band = (kc1 >= c_start) & (kc1 < c_start + NA_WIN_COLS)
    o_ref[...] = jnp.where(band, val, NEG_INF)


def _rpbcol_call(rpb):
    rows = NA_HEADS * N_DR
    r2 = jnp.pad(rpb.reshape(rows, N_DC), ((0, 0), (0, 32 - N_DC)))
    n = GRID_W * GRID_W
    return pl.pallas_call(
        _rpbcol_body,
        in_specs=[pl.BlockSpec((rows, 32), lambda: (0, 0))],
        out_specs=pl.BlockSpec((rows, n), lambda: (0, 0)),
        out_shape=jax.ShapeDtypeStruct((rows, n), F32),
        name="rpbcol",
    )(r2)


def _bias_tables(rpbcol):
    t = rpbcol.reshape(NA_HEADS, N_DR, GRID_W, GRID_W)
    neg = jnp.full((NA_HEADS, GRID_W, GRID_W), NEG_INF, F32)
    n_kj = K_ROW_BLOCKS * Q_ROWS
    classes = []
    for lo_fn, dr_off in ((lambda ri: 0, 7), (lambda ri: ri, 3), (lambda ri: 4, -1)):
        rows = []
        for ri in range(Q_ROWS):
            lo = lo_fn(ri)
            blocks = []
            for kj in range(n_kj):
                inside = lo <= kj < lo + NA_WIN_ROWS
                blocks.append(t[:, kj - ri + dr_off] if inside else neg)
            rows.append(jnp.concatenate(blocks, axis=2))
        classes.append(jnp.concatenate(rows, axis=1))
    return jnp.stack(classes, axis=0)


def _attn_body(qre_ref, qro_ref, qpe_ref, qpo_ref, k0_ref, k1_ref, k2_ref, v0_ref, v1_ref, v2_ref,
               kc_ref, vc_ref, bias_ref, o_ref):
    lane = lax.broadcasted_iota(I32, (ATT_TQ, LANES), 1)
    lane1 = lax.broadcasted_iota(I32, (1, LANES), 1)
    head_lanes = [jnp.where(lane1 < NA_HEAD_DIM, 1.0, 0.0).astype(BF16),
                  jnp.where(lane1 < NA_HEAD_DIM, 0.0, 1.0).astype(BF16)]
    k_refs = (k0_ref, k1_ref, k2_ref)
    v_refs = (v0_ref, v1_ref, v2_ref)
    for bi, p in [(bi, p) for bi in range(ATT_BATCH) for p in range(NA_HEADS // 2)]:
        sl = slice(p * LANES, (p + 1) * LANES)
        k_lat = jnp.concatenate([r[bi, :, sl] for r in k_refs], axis=0)
        kc = kc_ref[bi, :, sl]
        v_all = jnp.concatenate([r[bi, :, sl] for r in v_refs] + [vc_ref[bi, :, sl]], axis=0)
        outs = []
        for hh, (qr_ref, qp_ref) in enumerate(((qre_ref, qpe_ref), (qro_ref, qpo_ref))):
            h = 2 * p + hh
            s_lat = _dot_nt(qr_ref[bi, :, sl], k_lat) + bias_ref[0, h]
            s_ctx = _dot_nt(qp_ref[bi, :, sl], kc)
            tiles = ([s_lat[:, j * ATT_TQ:(j + 1) * ATT_TQ] for j in range(K_ROW_BLOCKS)]
                     + [s_ctx[:, j * ATT_TQ:(j + 1) * ATT_TQ] for j in range(s_ctx.shape[1] // ATT_TQ)])
            m = functools.reduce(jnp.maximum, tiles).max(axis=-1, keepdims=True)
            prob = jnp.exp(jnp.concatenate([s_lat, s_ctx], axis=1) - m).astype(BF16)
            mine = head_lanes[hh]
            acc = _dot(prob, v_all * mine + (1.0 - mine).astype(BF16))
            outs.append(acc / pltpu.roll(acc, NA_HEAD_DIM, 1))
        o_ref[bi, :, sl] = jnp.where(lane < NA_HEAD_DIM, outs[0], outs[1]).astype(BF16)


ATT_BATCH = 4


def _attn_call(qre, qro, qpe, qpo, k, v, kc, vc, bias, bsz, seq, n_ctx):
    n_grp = seq // ATT_TQ
    max_kb = n_grp - K_ROW_BLOCKS
    assert bsz % ATT_BATCH == 0
    by_batch = lambda a, n: a.reshape(bsz, n, NA_WIDTH)

    def qmap(g, b):
        return (b, g, 0)

    def kmap(j):
        return lambda g, b: (b, jnp.clip(g - 1, 0, max_kb) + j, 0)

    def cls(g, b):
        return (jnp.where(g == 0, 0, jnp.where(g == n_grp - 1, 2, 1)), 0, 0, 0)

    qspec = pl.BlockSpec((ATT_BATCH, ATT_TQ, NA_WIDTH), qmap)
    cspec = pl.BlockSpec((ATT_BATCH, n_ctx, NA_WIDTH), lambda g, b: (b, 0, 0))
    k3, v3 = by_batch(k, seq), by_batch(v, seq)
    return pl.pallas_call(
        _attn_body,
        grid=(n_grp, bsz // ATT_BATCH),
        in_specs=[qspec] * 4
                 + [pl.BlockSpec((ATT_BATCH, ATT_TQ, NA_WIDTH), kmap(j)) for j in range(K_ROW_BLOCKS)] * 2
                 + [cspec, cspec, pl.BlockSpec((1, NA_HEADS, ATT_TQ, ATT_TK), cls)],
        out_specs=qspec,
        out_shape=jax.ShapeDtypeStruct((bsz, seq, NA_WIDTH), BF16),
        compiler_params=_cparams(("arbitrary", "arbitrary")),
        name="attn",
    )(by_batch(qre, seq), by_batch(qro, seq), by_batch(qpe, seq), by_batch(qpo, seq), k3, k3, k3, v3, v3, v3,
      by_batch(kc, n_ctx), by_batch(vc, n_ctx), bias).reshape(bsz * seq, NA_WIDTH)


def _shift_down(v, row):
    return jnp.where(row >= 1, pltpu.roll(v, 1, 0), 0.0)


def _shift_up(v, row):
    return jnp.where(row < SUBLANES - 1, pltpu.roll(v, SUBLANES - 1, 0), 0.0)


def _conv4(x, w, b):
    n = x.shape[0]
    s = SUBLANES
    row = lax.broadcasted_iota(I32, (s, LANES), 0)
    last = _shift_down(x[n - s:n], row)
    last2 = _shift_down(x[n - 2 * s:n - s], row)
    first = _shift_up(x[0:s], row)
    xm1 = jnp.concatenate([last, x[0:n - s]], axis=0)
    xm2 = jnp.concatenate([last2, last, x[0:n - 2 * s]], axis=0)
    xp1 = jnp.concatenate([x[s:n], first], axis=0)
    return (w[0:1, :] * xm2 + w[1:2, :] * xm1 + w[2:3, :] * x + w[3:4, :] * xp1) + b


def _softplus(z):
    return jnp.maximum(z, 0.0) + jnp.log1p(jnp.exp(-jnp.abs(z)))


def _gates(xb, xh, wa, wx, ba, bx, lam, a_ref, u_ref):
    n = xb.shape[0]
    tr = jnp.tanh(_dot(xb, (0.5 * wa).astype(BF16)) + 0.5 * ba)
    ti = jnp.tanh(_dot(xb, (0.5 * wx).astype(BF16)) + 0.5 * bx)
    half_c = (0.5 * LRU_C) * _softplus(-lam)
    neg_log_a = half_c * tr + half_c
    a = jnp.exp(-neg_log_a)
    a_ref[0:n, :] = a
    s2 = jnp.tanh(neg_log_a) * (a * a + 1.0)
    root = jnp.where(s2 > 0.0, s2 * lax.rsqrt(s2), 0.0)
    u_ref[0:n, :] = root * (xh * ti + xh)


def _scan4(a, u, h, p):
    a01 = a[1] * a[0]
    u01 = a[1] * u[0] + u[1]
    a23 = a[3] * a[2]
    u23 = a[3] * u[2] + u[3]
    a012 = a[2] * a01
    u012 = a[2] * u01 + u[2]
    a0123 = a23 * a01
    u0123 = a23 * u01 + u23
    hs = [a[0] * h + u[0], a01 * h + u01, a012 * h + u012, a0123 * h + u0123]
    ps = [a[0] * p, a01 * p, a012 * p, a0123 * p]
    return hs, ps


SCAN_STEPS = 4


def _scan_local(af_ref, uf_ref, ab_ref, ub_ref, n_vreg):
    s = SUBLANES
    zero = jnp.zeros((s, LANES), F32)
    one = jnp.ones((s, LANES), F32)
    span = SCAN_STEPS * s

    def body(q, carry):
        hf, pf, hb, pb = carry
        base = pl.multiple_of(q * span, span)
        rows = [pl.ds(base + i * s, s) for i in range(SCAN_STEPS)]
        hs, ps = _scan4([af_ref[r, :] for r in rows], [uf_ref[r, :] for r in rows], hf, pf)
        for r, h, p in zip(rows, hs, ps):
            uf_ref[r, :] = h
            af_ref[r, :] = p
        hf, pf = hs[-1], ps[-1]
        base = pl.multiple_of((n_vreg - SCAN_STEPS) * s - q * span, span)
        rows = [pl.ds(base + (SCAN_STEPS - 1 - i) * s, s) for i in range(SCAN_STEPS)]
        hs, ps = _scan4([ab_ref[r, :] for r in rows], [ub_ref[r, :] for r in rows], hb, pb)
        for r, h, p in zip(rows, hs, ps):
            ub_ref[r, :] = h
            ab_ref[r, :] = p
        return hf, pf, hs[-1], ps[-1]

    return lax.fori_loop(0, n_vreg // SCAN_STEPS, body, (zero, one, zero, one), unroll=2)


def _link_states(hf, pf, hb, pb, h0f, h0b):
    s = SUBLANES
    row = lax.broadcasted_iota(I32, (s, LANES), 0)
    a, u = pf, hf
    for k in (1, 2, 4):
        keep = row >= k
        u = u + a * jnp.where(keep, pltpu.roll(u, k, 0), 0.0)
        a = a * jnp.where(keep, pltpu.roll(a, k, 0), 1.0)
    end_f = u + a * h0f
    in_f = jnp.where(row >= 1, pltpu.roll(end_f, 1, 0), h0f)
    a, u = pb, hb
    for k in (1, 2, 4):
        keep = row < s - k
        u = u + a * jnp.where(keep, pltpu.roll(u, s - k, 0), 0.0)
        a = a * jnp.where(keep, pltpu.roll(a, s - k, 0), 1.0)
    end_b = u + a * h0b
    in_b = jnp.where(row < s - 1, pltpu.roll(end_b, s - 1, 0), h0b)
    return in_f, in_b, end_f[s - 1:s, :], end_b[0:1, :]


PITCH_PAD = 4


def _to_split(x_ref, lanes, pad_ref, dst_ref, n):
    s = SUBLANES
    n_j = n // s
    pitch = n_j + PITCH_PAD
    for q in range(s):
        pad_ref[pl.ds(q * pitch, n_j), :] = x_ref[0, pl.ds(q * n_j, n_j), lanes]

    def body(j, c):
        dst_ref[pl.ds(pl.multiple_of(j * s, s), s), :] = pad_ref[pl.ds(j, s, stride=pitch), :]
        return c

    lax.fori_loop(0, n_j, body, 0, unroll=8)


LRU_BLOCKS_PER_STEP = 4


def _lru_body(lx_ref, lxc_ref, cw_ref, cb_ref, wa_ref, wx_ref, ba_ref, bx_ref, lam_ref,
              o_ref, af_ref, uf_ref, ab_ref, ub_ref, pad_ref):
    n = lx_ref.shape[1]
    n_c = lxc_ref.shape[1]
    s = SUBLANES
    n_j = n // s
    pitch = n_j + PITCH_PAD
    zero = jnp.zeros((1, LANES), F32)

    for blk in range(LRU_BLOCKS_PER_STEP):
        lanes = slice(blk * LRU_BLOCK, (blk + 1) * LRU_BLOCK)
        conv_w = cw_ref[:, lanes]
        conv_b = cb_ref[:, lanes]
        gate = [(wa_ref[d, blk], wx_ref[d, blk], ba_ref[d:d + 1, lanes], bx_ref[d:d + 1, lanes],
                 lam_ref[d:d + 1, lanes]) for d in range(2)]

        _to_split(lxc_ref, lanes, pad_ref, uf_ref, n_c)
        xc = _conv4(uf_ref[0:n_c, :], conv_w, conv_b)
        xb, xh = xc.astype(BF16), 0.5 * xc
        _gates(xb, xh, *gate[0], af_ref, uf_ref)
        _gates(xb, xh, *gate[1], ab_ref, ub_ref)
        ends = _scan_local(af_ref, uf_ref, ab_ref, ub_ref, n_c // s)
        _, _, cf, cb = _link_states(*ends, zero, zero)

        _to_split(lx_ref, lanes, pad_ref, uf_ref, n)
        xl = _conv4(uf_ref[...], conv_w, conv_b)
        xb, xh = xl.astype(BF16), 0.5 * xl
        _gates(xb, xh, *gate[0], af_ref, uf_ref)
        _gates(xb, xh, *gate[1], ab_ref, ub_ref)
        ends = _scan_local(af_ref, uf_ref, ab_ref, ub_ref, n_j)
        in_f, in_b, _, _ = _link_states(*ends, cf, cb)

        def finish(j, c):
            rows = pl.ds(pl.multiple_of(j * s, s), s)
            h = (uf_ref[rows, :] + af_ref[rows, :] * in_f) + (ub_ref[rows, :] + ab_ref[rows, :] * in_b)
            pad_ref[pl.ds(j, s, stride=pitch), :] = h
            return c

        lax.fori_loop(0, n_j, finish, 0, unroll=8)
        for q in range(s):
            o_ref[0, pl.ds(q * n_j, n_j), lanes] = pad_ref[pl.ds(q * pitch, n_j), :].astype(o_ref.dtype)


def _lru_call(lx3, lxc3, conv_w, conv_b, wa, wx, ba, bx, lam):
    bsz, seq, _ = lx3.shape
    n_ctx = lxc3.shape[1]
    per = LRU_BLOCKS_PER_STEP
    width = per * LRU_BLOCK
    col = lambda b, n: (b, 0, n)
    par = lambda b, n: (0, n)
    wspec = pl.BlockSpec((2, per, LRU_BLOCK, LRU_BLOCK), lambda b, n: (0, n, 0, 0))
    return pl.pallas_call(
        _lru_body,
        grid=(bsz, LRU_BLOCKS // per),
        in_specs=[pl.BlockSpec((1, seq, width), col),
                  pl.BlockSpec((1, n_ctx, width), col),
                  pl.BlockSpec((LRU_CONV, width), par),
                  pl.BlockSpec((1, width), par),
                  wspec, wspec,
                  pl.BlockSpec((2, width), par),
                  pl.BlockSpec((2, width), par),
                  pl.BlockSpec((2, width), par)],
        out_specs=pl.BlockSpec((1, seq, width), col),
        out_shape=jax.ShapeDtypeStruct((bsz, seq, LRU_WIDTH), BF16),
        scratch_shapes=[pltpu.VMEM((seq, LRU_BLOCK), F32)] * 4
                       + [pltpu.VMEM((seq + SUBLANES * PITCH_PAD, LRU_BLOCK), F32)],
        compiler_params=_cparams(("parallel", "arbitrary")),
        name="lru",
    )(lx3, lxc3, conv_w, conv_b, wa, wx, ba, bx, lam)


TOKEN_TILE = D_MODEL // LANES


def _store_token_tiles(ref, row0, x):
    m = x.shape[0]
    for c in range(TOKEN_TILE):
        ref[pl.ds(row0 + c, m, stride=TOKEN_TILE), :] = x[:, c * LANES:(c + 1) * LANES]


def _load_token_tiles(ref, row0, m):
    return jnp.concatenate([ref[pl.ds(row0 + c, m, stride=TOKEN_TILE), :] for c in range(TOKEN_TILE)], axis=1)


def _merge_body(x_ref, oa_ref, hs_ref, glu_ref, ga_ref, gb_ref, mod_ref, g_ref, wua_ref, wul_ref, wo_ref,
                wr_ref, br_ref, x1_ref, h2_ref, lt_ref):
    ga1 = mod_ref[0, :, 2 * D_MODEL:3 * D_MODEL]
    sh2 = mod_ref[0, :, 3 * D_MODEL:4 * D_MODEL]
    sc2 = mod_ref[0, :, 4 * D_MODEL:5 * D_MODEL]
    o_lru = hs_ref[...] * glu_ref[...]
    y = (ga_ref[...].astype(F32) * _dot(oa_ref[...], wua_ref[...])
         + gb_ref[...].astype(F32) * _dot(o_lru, wul_ref[...]))
    x1 = x_ref[...] + ga1 * _dot(y.astype(BF16), wo_ref[...])
    x1_ref[...] = x1
    h2 = _ada_norm(x1, g_ref[...], sc2, sh2)
    lt_ref[...] = _dot3_nt(wr_ref[...], h2) + br_ref[:, 0:1]
    _store_token_tiles(h2_ref, 0, h2)


def _merge_call(x2, o_att, hs, glu, ga, gb, mod3, g_ffn, wua, wul, wo, wr_t, br, seq):
    t = x2.shape[0]
    tm = MERGE_TM
    per_b = seq // tm
    row = lambda i: (i, 0)
    full = lambda i: (0, 0)
    resident = lambda shape: pl.BlockSpec(shape, full, pipeline_mode=pl.Buffered(1))
    return pl.pallas_call(
        _merge_body,
        grid=(t // tm,),
        in_specs=[pl.BlockSpec((tm, D_MODEL), row),
                  pl.BlockSpec((tm, NA_WIDTH), row),
                  pl.BlockSpec((tm, LRU_WIDTH), row),
                  pl.BlockSpec((tm, LRU_WIDTH), row),
                  pl.BlockSpec((tm, D_MODEL), row),
                  pl.BlockSpec((tm, D_MODEL), row),
                  pl.BlockSpec((1, 1, 6 * D_MODEL), lambda i: (i // per_b, 0, 0)),
                  pl.BlockSpec((1, D_MODEL), full),
                  resident((NA_WIDTH, D_MODEL)),
                  resident((LRU_WIDTH, D_MODEL)),
                  resident((D_MODEL, D_MODEL)),
                  pl.BlockSpec((ROUTE_ROWS, D_MODEL), full),
                  pl.BlockSpec((ROUTE_ROWS, LANES), full)],
        out_specs=[pl.BlockSpec((tm, D_MODEL), row),
                   pl.BlockSpec((tm * TOKEN_TILE, LANES), row),
                   pl.BlockSpec((ROUTE_ROWS, tm), lambda i: (0, i))],
        out_shape=[jax.ShapeDtypeStruct((t, D_MODEL), F32),
                   jax.ShapeDtypeStruct((t * TOKEN_TILE, LANES), F32),
                   jax.ShapeDtypeStruct((ROUTE_ROWS, t), F32)],
        compiler_params=_cparams(("parallel",)),
        name="merge",
    )(x2, o_att, hs, glu, ga, gb, mod3, g_ffn, wua, wul, wo, wr_t, br)


def _route_body(lt_ref, eid_ref, gate_ref, rank_ref, cnt_ref, carry_ref):
    step = pl.program_id(0)

    @pl.when(step == 0)
    def _():
        carry_ref[...] = jnp.zeros_like(carry_ref)

    tb = lt_ref.shape[1]
    lg = [lt_ref[r:r + 1, :] for r in range(N_GROUPS)]
    best = lg[0]
    gidx = jnp.zeros((1, tb), I32)
    for r in range(1, N_GROUPS):
        better = lg[r] > best
        gidx = jnp.where(better, r, gidx)
        best = jnp.maximum(best, lg[r])
    den = jnp.exp(lg[0] - best)
    for r in range(1, N_GROUPS):
        den = den + jnp.exp(lg[r] - best)
    p_top = 1.0 / den

    ev = []
    for j in range(EXPERTS_PER_GROUP):
        sel = lt_ref[N_GROUPS + j:N_GROUPS + j + 1, :]
        for g in range(1, N_GROUPS):
            row = N_GROUPS + g * EXPERTS_PER_GROUP + j
            sel = jnp.where(gidx == g, lt_ref[row:row + 1, :], sel)
        ev.append(sel)
    v0 = ev[0]
    i0 = jnp.zeros((1, tb), I32)
    for j in range(1, EXPERTS_PER_GROUP):
        better = ev[j] > v0
        i0 = jnp.where(better, j, i0)
        v0 = jnp.maximum(v0, ev[j])
    v1 = jnp.full((1, tb), -jnp.inf, F32)
    i1 = jnp.zeros((1, tb), I32)
    for j in range(EXPERTS_PER_GROUP):
        better = (ev[j] > v1) & (i0 != j)
        i1 = jnp.where(better, j, i1)
        v1 = jnp.where(better, ev[j], v1)
    e1 = jnp.exp(v1 - v0)
    inv = 1.0 / (1.0 + e1)
    eid0 = gidx * EXPERTS_PER_GROUP + i0
    eid1 = gidx * EXPERTS_PER_GROUP + i1
    eid_ref[0:1, :] = eid0
    eid_ref[1:2, :] = eid1
    gate_ref[...] = jnp.zeros_like(gate_ref)
    gate_ref[0:1, :] = p_top * inv
    gate_ref[1:2, :] = p_top * (e1 * inv)

    sub = 256
    e_iota = lax.broadcasted_iota(I32, (N_EXPERTS, sub), 0)
    tri = jnp.where(lax.broadcasted_iota(I32, (sub, sub), 0) <= lax.broadcasted_iota(I32, (sub, sub), 1),
                    1.0, 0.0).astype(BF16)
    carry = carry_ref[...]
    for c in range(tb // sub):
        sl = slice(c * sub, (c + 1) * sub)
        m0 = eid0[:, sl] == e_iota
        m1 = eid1[:, sl] == e_iota
        oh = jnp.where(m0 | m1, 1.0, 0.0)
        incl = _dot(oh.astype(BF16), tri)
        excl = incl - oh + carry[:, 0:1]
        rank_ref[0:1, sl] = jnp.sum(jnp.where(m0, excl, 0.0), axis=0, keepdims=True).astype(I32)
        rank_ref[1:2, sl] = jnp.sum(jnp.where(m1, excl, 0.0), axis=0, keepdims=True).astype(I32)
        carry = carry + incl[:, sub - 1:sub]
    carry_ref[...] = carry
    cnt_ref[...] = carry


def _route_call(logits_t):
    t = logits_t.shape[1]
    tb = ROUTE_TB
    col = lambda i: (0, i)
    return pl.pallas_call(
        _route_body,
        grid=(t // tb,),
        in_specs=[pl.BlockSpec((ROUTE_ROWS, tb), col)],
        out_specs=[pl.BlockSpec((TOP_K, tb), col), pl.BlockSpec((SUBLANES, tb), col),
                   pl.BlockSpec((TOP_K, tb), col), pl.BlockSpec((N_EXPERTS, LANES), lambda i: (0, 0))],
        out_shape=[jax.ShapeDtypeStruct((TOP_K, t), I32), jax.ShapeDtypeStruct((SUBLANES, t), F32),
                   jax.ShapeDtypeStruct((TOP_K, t), I32), jax.ShapeDtypeStruct((N_EXPERTS, LANES), F32)],
        scratch_shapes=[pltpu.VMEM((N_EXPERTS, LANES), F32)],
        compiler_params=_cparams(("arbitrary",)),
        name="route",
    )(logits_t)


def _dest_body(cnt_ref, eid_ref, rank_ref, dest_ref, blk_ref):
    cnt = cnt_ref[...].astype(I32)
    padded = ((cnt + (MOE_BLK - 1)) >> MOE_BLK_LOG2) << MOE_BLK_LOG2
    e_iota = lax.broadcasted_iota(I32, (N_EXPERTS, LANES), 0)
    p_end = jnp.zeros((N_EXPERTS, LANES), I32)
    for e in range(N_EXPERTS):
        tot = jnp.sum(jnp.where(e_iota <= e, padded, 0), axis=0, keepdims=True)
        p_end = jnp.where(e_iota == e, tot, p_end)
    p_start = p_end - padded
    tb = eid_ref.shape[1]
    ps = jnp.concatenate([p_start] * (tb // LANES), axis=1)
    e_wide = lax.broadcasted_iota(I32, (N_EXPERTS, tb), 0)
    for k in range(TOP_K):
        start = jnp.sum(jnp.where(eid_ref[k:k + 1, :] == e_wide, ps, 0), axis=0, keepdims=True)
        dest_ref[k:k + 1, :] = start + rank_ref[k:k + 1, :]
    nb = blk_ref.shape[1]
    pe = jnp.concatenate([p_end] * (nb // LANES), axis=1)
    first_row = lax.broadcasted_iota(I32, (N_EXPERTS, nb), 1) * MOE_BLK
    n_before = jnp.sum(jnp.where(pe <= first_row, 1, 0), axis=0, keepdims=True)
    blk = jnp.minimum(n_before, N_EXPERTS - 1)
    blk_ref[...] = jnp.broadcast_to(blk, blk_ref.shape)
    blk_ref[1:2, :] = jnp.broadcast_to(p_end[N_EXPERTS - 1:N_EXPERTS, 0:1] >> MOE_BLK_LOG2, (1, nb))
    on_diag = e_iota == lax.broadcasted_iota(I32, (N_EXPERTS, LANES), 1)
    blk_ref[2:3, 0:LANES] = jnp.sum(jnp.where(on_diag, p_end, 0), axis=0, keepdims=True)
    blk_ref[3:4, 0:LANES] = jnp.sum(jnp.where(on_diag, padded, 0), axis=0, keepdims=True)


def _dest_call(cnt, eid, rank, nb_pad):
    t = eid.shape[1]
    tb = ROUTE_TB
    col = lambda i: (0, i)
    return pl.pallas_call(
        _dest_body,
        grid=(t // tb,),
        in_specs=[pl.BlockSpec((N_EXPERTS, LANES), lambda i: (0, 0)),
                  pl.BlockSpec((TOP_K, tb), col), pl.BlockSpec((TOP_K, tb), col)],
        out_specs=[pl.BlockSpec((TOP_K, tb), col), pl.BlockSpec((SUBLANES, nb_pad), lambda i: (0, 0))],
        out_shape=[jax.ShapeDtypeStruct((TOP_K, t), I32), jax.ShapeDtypeStruct((SUBLANES, nb_pad), I32)],
        compiler_params=_cparams(("arbitrary",)),
        name="dest",
    )(cnt, eid, rank)


def _token_tile(ref, t):
    return ref.at[pl.ds(pl.multiple_of(t * TOKEN_TILE, TOKEN_TILE), TOKEN_TILE)]


def _dispatch_body(dest_ref, pend_ref, plen_ref, h_ref, xs_ref, zero_ref, sem, zsem):
    tm = h_ref.shape[0] // TOKEN_TILE
    blk_rows = MOE_BLK * TOKEN_TILE
    n_tok = pl.num_programs(0) * tm
    base = pl.program_id(0) * tm

    @pl.when(pl.program_id(0) == 0)
    def _():
        zero_ref[...] = jnp.zeros_like(zero_ref)

        def zero_block(start):
            rows = pl.ds(pl.multiple_of(start * TOKEN_TILE, blk_rows), blk_rows)
            return pltpu.make_async_copy(zero_ref, xs_ref.at[rows], zsem)

        def fill(e, c):
            @pl.when(plen_ref[e] > 0)
            def _():
                zero_block(pend_ref[e] - MOE_BLK).start()
            return c

        def drain(e, c):
            @pl.when(plen_ref[e] > 0)
            def _():
                zero_block(pend_ref[e] - MOE_BLK).wait()
            return c

        lax.fori_loop(0, N_EXPERTS, fill, 0)
        n_used = pend_ref[N_EXPERTS - 1] >> MOE_BLK_LOG2
        n_blk = xs_ref.shape[0] // blk_rows
        lax.fori_loop(n_used, n_blk, lambda j, c: (zero_block(j * MOE_BLK).start(), c)[1], 0)
        lax.fori_loop(0, N_EXPERTS, drain, 0)
        lax.fori_loop(n_used, n_blk, lambda j, c: (zero_block(j * MOE_BLK).wait(), c)[1], 0)

    def issue(r, c):
        for k in range(TOP_K):
            d = dest_ref[k * n_tok + base + r]
            pltpu.make_async_copy(_token_tile(h_ref, r), _token_tile(xs_ref, d), sem).start(priority=k)
        return c

    lax.fori_loop(0, tm, issue, 0, unroll=8)
    for k in range(TOP_K):
        pltpu.make_async_copy(h_ref, xs_ref.at[pl.ds(0, tm * TOKEN_TILE)], sem).wait()


def _dispatch_call(dest_flat, p_end, p_len, h2t, n_slots):
    rows = DISPATCH_TM * TOKEN_TILE
    return pl.pallas_call(
        _dispatch_body,
        grid_spec=pltpu.PrefetchScalarGridSpec(
            num_scalar_prefetch=3,
            grid=(h2t.shape[0] // rows,),
            in_specs=[pl.BlockSpec((rows, LANES), lambda i, d, pe, pn: (i, 0))],
            out_specs=pl.BlockSpec(memory_space=pl.ANY),
            scratch_shapes=[pltpu.VMEM((MOE_BLK * TOKEN_TILE, LANES), F32),
                            pltpu.SemaphoreType.DMA(()), pltpu.SemaphoreType.DMA(())]),
        out_shape=jax.ShapeDtypeStruct((n_slots * TOKEN_TILE, LANES), F32),
        compiler_params=_cparams(("arbitrary",)),
        name="dispatch",
    )(dest_flat, p_end, p_len, h2t)


EXPERT_BLKS_PER_STEP = 4


def _experts_body(blk_ref, used_ref, xs_ref, w1_hbm, w3_hbm, w2_hbm, y_ref,
                  f1_ref, f3_ref, f2_ref, b1_ref, b3_ref, b2_ref, loaded_ref, sem):
    i = pl.program_id(0)
    per = EXPERT_BLKS_PER_STEP
    used = used_ref[0]

    def weight_copies(e):
        return [pltpu.make_async_copy(w_hbm.at[e], f_ref, sem)
                for w_hbm, f_ref in ((w1_hbm, f1_ref), (w3_hbm, f3_ref), (w2_hbm, f2_ref))]

    @pl.when(i == 0)
    def _():
        loaded_ref[0] = -1
        for cp in weight_copies(blk_ref[0]):
            cp.start()

    for half in range(per):
        blk = per * i + half
        row0 = half * MOE_BLK * TOKEN_TILE
        e = blk_ref[blk]

        @pl.when((blk < used) & (e != loaded_ref[0]))
        def _():
            for cp in weight_copies(e):
                cp.wait()
            b1_ref[...] = f1_ref[...].astype(BF16)
            b3_ref[...] = f3_ref[...].astype(BF16)
            b2_ref[...] = f2_ref[...].astype(BF16)
            loaded_ref[0] = e
            nxt = lax.while_loop(lambda j: (j < used) & (blk_ref[jnp.minimum(j, blk_ref.shape[0] - 1)] == e),
                                 lambda j: j + 1, blk + 1)

            @pl.when(nxt < used)
            def _():
                for cp in weight_copies(blk_ref[jnp.minimum(nxt, blk_ref.shape[0] - 1)]):
                    cp.start()

        @pl.when(blk < used)
        def _():
            x = _load_token_tiles(xs_ref, row0, MOE_BLK).astype(BF16)
            g = _dot(x, b1_ref[...])
            u = _dot(x, b3_ref[...])
            mid = (g * _sigmoid(g)) * u
            _store_token_tiles(y_ref, row0, _dot(mid.astype(BF16), b2_ref[...]))

        @pl.when(blk >= used)
        def _():
            y_ref[pl.ds(row0, MOE_BLK * TOKEN_TILE), :] = jnp.zeros((MOE_BLK * TOKEN_TILE, LANES), F32)


def _experts_call(blk_e, n_used, xs, w1, w3, w2):
    per = EXPERT_BLKS_PER_STEP
    rows = per * MOE_BLK * TOKEN_TILE
    nb = xs.shape[0] // (MOE_BLK * TOKEN_TILE)
    assert nb % per == 0 and blk_e.shape[0] == nb
    hbm = pl.BlockSpec(memory_space=pl.ANY)
    return pl.pallas_call(
        _experts_body,
        grid_spec=pltpu.PrefetchScalarGridSpec(
            num_scalar_prefetch=2,
            grid=(nb // per,),
            in_specs=[pl.BlockSpec((rows, LANES), lambda i, blk, used: (i, 0)), hbm, hbm, hbm],
            out_specs=pl.BlockSpec((rows, LANES), lambda i, blk, used: (i, 0)),
            scratch_shapes=[pltpu.VMEM((D_MODEL, D_EXPERT), F32), pltpu.VMEM((D_MODEL, D_EXPERT), F32),
                            pltpu.VMEM((D_EXPERT, D_MODEL), F32),
                            pltpu.VMEM((D_MODEL, D_EXPERT), BF16), pltpu.VMEM((D_MODEL, D_EXPERT), BF16),
                            pltpu.VMEM((D_EXPERT, D_MODEL), BF16),
                            pltpu.SMEM((1,), I32), pltpu.SemaphoreType.DMA(())]),
        out_shape=jax.ShapeDtypeStruct(xs.shape, F32),
        compiler_params=_cparams(("arbitrary",)),
        name="experts",
    )(blk_e, n_used, xs, w1, w3, w2)


def _combine_body(dest_ref, x1_ref, gate_ref, mod_ref, gf_ref, y_ref, o_ref, buf_ref, sem):
    tm = x1_ref.shape[0]
    step = pl.program_id(0)
    n_step = pl.num_programs(0)
    n_tok = n_step * tm
    slot = step % 2
    region = tm * TOKEN_TILE

    def region_row0(buf, k):
        return pl.multiple_of((buf * TOP_K + k) * region, region)

    def start_gather(for_step):
        def issue(r, c):
            for k in range(TOP_K):
                d = dest_ref[k * n_tok + for_step * tm + r]
                dst = buf_ref.at[pl.ds(pl.multiple_of(region_row0(for_step % 2, k) + r * TOKEN_TILE, TOKEN_TILE),
                                       TOKEN_TILE)]
                pltpu.make_async_copy(_token_tile(y_ref, d), dst, sem.at[for_step % 2]).start(priority=k)
            return c

        lax.fori_loop(0, tm, issue, 0, unroll=8)

    @pl.when(step == 0)
    def _():
        start_gather(step)

    @pl.when(step + 1 < n_step)
    def _():
        start_gather(step + 1)

    eye = jnp.where(lax.broadcasted_iota(I32, (tm, tm), 0) == lax.broadcasted_iota(I32, (tm, tm), 1),
                    1.0, 0.0).astype(BF16)
    g = gate_ref[...]
    g1 = g.astype(BF16)
    rem = g - g1.astype(F32)
    g2 = rem.astype(BF16)
    g3 = (rem - g2.astype(F32)).astype(BF16)
    gt = _dot_nt(eye, g1) + (_dot_nt(eye, g2) + _dot_nt(eye, g3))

    for k in range(TOP_K):
        pltpu.make_async_copy(y_ref.at[pl.ds(0, region)], buf_ref.at[pl.ds(region_row0(slot, k), region)],
                              sem.at[slot]).wait()

    ga2 = mod_ref[0, :, 5 * D_MODEL:6 * D_MODEL]
    moe = (gt[:, 0:1] * _load_token_tiles(buf_ref, region_row0(slot, 0), tm)
           + gt[:, 1:2] * _load_token_tiles(buf_ref, region_row0(slot, 1), tm))
    x2 = x1_ref[...] + ga2 * moe
    ms = jnp.mean(x2 * x2, axis=-1, keepdims=True)
    o_ref[...] = x2 * lax.rsqrt(ms + EPS) * gf_ref[...]


def _combine_call(dest_flat, x1, gate, mod3, g_final, y, seq):
    t = x1.shape[0]
    tm = COMBINE_TM
    per_b = seq // tm
    return pl.pallas_call(
        _combine_body,
        grid_spec=pltpu.PrefetchScalarGridSpec(
            num_scalar_prefetch=1,
            grid=(t // tm,),
            in_specs=[pl.BlockSpec((tm, D_MODEL), lambda i, d: (i, 0)),
                      pl.BlockSpec((SUBLANES, tm), lambda i, d: (0, i)),
                      pl.BlockSpec((1, 1, 6 * D_MODEL), lambda i, d: (i // per_b, 0, 0)),
                      pl.BlockSpec((1, D_MODEL), lambda i, d: (0, 0)),
                      pl.BlockSpec(memory_space=pl.ANY)],
            out_specs=pl.BlockSpec((tm, D_MODEL), lambda i, d: (i, 0)),
            scratch_shapes=[pltpu.VMEM((2 * TOP_K * tm * TOKEN_TILE, LANES), F32),
                            pltpu.SemaphoreType.DMA((2,))]),
        out_shape=jax.ShapeDtypeStruct((t, D_MODEL), F32),
        compiler_params=_cparams(("arbitrary",)),
        name="combine",
    )(dest_flat, x1, gate, mod3, g_final, y)


def _rope_tables(seq):
    half = NA_HEAD_DIM // 2
    nf = half // 2
    inv_freq = ROPE_THETA ** (-jnp.arange(nf, dtype=F32) / nf)
    t = jnp.arange(seq)
    row_pos = (t // GRID_W).astype(F32)
    col_pos = (t % GRID_W).astype(F32)
    ang_r = row_pos[:, None] * inv_freq
    ang_c = col_pos[:, None] * inv_freq
    cos = jnp.concatenate([jnp.cos(ang_r), jnp.cos(ang_r), jnp.cos(ang_c), jnp.cos(ang_c)], axis=-1)
    sin = jnp.concatenate([-jnp.sin(ang_r), jnp.sin(ang_r), -jnp.sin(ang_c), jnp.sin(ang_c)], axis=-1)
    return jnp.tile(cos, (1, NA_HEADS)), jnp.tile(sin, (1, NA_HEADS))


def _layer(x, c, ctx, c_ctx, w_mod, b_mod, g_mix, g_ffn, w_in, rpb, conv_w, conv_b, lru_wa, lru_ba,
           lru_wx, lru_bx, lru_lambda, w_up_attn, w_up_lru, w_out, wg, bg, we, be, w1, w3, w2, g_final):
    bsz, seq, d = x.shape
    n_ctx = ctx.shape[1]
    t = bsz * seq
    assert d == D_MODEL and seq % ATT_TQ == 0 and seq // ATT_TQ > K_ROW_BLOCKS and n_ctx % ATT_TQ == 0
    assert bsz + 1 <= MOD_ROWS and seq % PROJ_TM == 0 and seq % MERGE_TM == 0
    assert t % ROUTE_TB == 0 and t % COMBINE_TM == 0 and t % DISPATCH_TM == 0
    assert (bsz * n_ctx) % PROJ_TM == 0 and n_ctx <= seq
    assert seq % (SUBLANES * SUBLANES) == 0 and n_ctx % (SUBLANES * SUBLANES) == 0
    assert (seq // SUBLANES) % SCAN_STEPS == 0 and (n_ctx // SUBLANES) % SCAN_STEPS == 0

    cc = jnp.concatenate([c, c_ctx[None, :], jnp.zeros((MOD_ROWS - bsz - 1, d), F32)], axis=0)
    mod = _mod_call(cc, w_mod, b_mod)
    mod3 = mod[:bsz].reshape(bsz, 1, 6 * d)
    mod_c = mod[bsz:bsz + 1]

    x2 = x.reshape(t, d)
    g_mix2 = g_mix.reshape(1, d)
    w_in_bf = w_in.astype(BF16)
    kc, vc, lxc = _ctxproj_call(ctx.reshape(bsz * n_ctx, d), mod_c, g_mix2, w_in_bf[:, :CTX_COLS])
    cos_t, sin_t = _rope_tables(seq)
    qre, qro, qpe, qpo, k, v, lx, glu, ga, gb = _inproj_call(x2, mod3, g_mix2, w_in_bf, cos_t, sin_t, seq)

    bias = _bias_tables(_rpbcol_call(rpb))
    o_att = _attn_call(qre, qro, qpe, qpo, k, v, kc, vc, bias, bsz, seq, n_ctx)

    hs = _lru_call(lx.reshape(bsz, seq, LRU_WIDTH), lxc.reshape(bsz, n_ctx, LRU_WIDTH),
                   conv_w, conv_b.reshape(1, LRU_WIDTH), lru_wa, lru_wx, lru_ba, lru_bx, lru_lambda)

    wr_t = jnp.concatenate([wg.T, we.T, jnp.zeros((ROUTE_ROWS - N_GROUPS - N_EXPERTS, d), F32)], axis=0)
    br = jnp.concatenate([bg, be, jnp.zeros((ROUTE_ROWS - N_GROUPS - N_EXPERTS,), F32)])
    br = jnp.broadcast_to(br[:, None], (ROUTE_ROWS, LANES))
    x1, h2, logits_t = _merge_call(x2, o_att, hs.reshape(t, LRU_WIDTH), glu, ga, gb, mod3,
                                   g_ffn.reshape(1, d), w_up_attn.astype(BF16), w_up_lru.astype(BF16),
                                   w_out.astype(BF16), wr_t, br, seq)

    eid, gate, rank, cnt = _route_call(logits_t)
    n_blk = -(-(t * TOP_K + N_EXPERTS * (MOE_BLK - 1)) // MOE_BLK)
    n_blk = -(-n_blk // EXPERT_BLKS_PER_STEP) * EXPERT_BLKS_PER_STEP
    nb_pad = -(-n_blk // LANES) * LANES
    dest, blk = _dest_call(cnt, eid, rank, nb_pad)
    dest_flat = dest.reshape(TOP_K * t)
    xs = _dispatch_call(dest_flat, blk[2, :N_EXPERTS], blk[3, :N_EXPERTS], h2, n_blk * MOE_BLK)
    y = _experts_call(blk[0, :n_blk], blk[1, :1], xs, w1, w3, w2)
    return _combine_call(dest_flat, x1, gate, mod3, g_final.reshape(1, d), y, seq).reshape(bsz, seq, d)


def kernel(x, c, ctx, c_ctx, w_mod, b_mod, g_mix, g_ffn, w_in, rpb, conv_w, conv_b, lru_wa, lru_ba, lru_wx,
           lru_bx, lru_lambda, w_up_attn, w_up_lru, w_out, router_group_w, router_group_b, router_expert_w,
           router_expert_b, expert_w_gate, expert_w_up, expert_w_down, g_final):
    assert w_mod.shape[0] == 1, "single-layer block"
    return _layer(x, c, ctx, c_ctx, w_mod[0], b_mod[0], g_mix[0], g_ffn[0], w_in[0], rpb[0], conv_w[0],
                  conv_b[0], lru_wa[0], lru_ba[0], lru_wx[0], lru_bx[0], lru_lambda[0], w_up_attn[0],
                  w_up_lru[0], w_out[0], router_group_w[0], router_group_b[0], router_expert_w[0],
                  router_expert_b[0], expert_w_gate[0], expert_w_up[0], expert_w_down[0], g_final)
```

```python
import functools

import jax
import jax.numpy as jnp
from jax import lax
from jax.experimental import pallas as pl
from jax.experimental.pallas import tpu as pltpu

F32 = jnp.float32
BF16 = jnp.bfloat16
I32 = jnp.int32

D_MODEL = 1024
GRID_W = 64
EPS = 1e-6
NEG_INF = -1e30

NA_HEADS = 8
NA_HEAD_DIM = 64
NA_WIDTH = NA_HEADS * NA_HEAD_DIM
NA_WIN_ROWS = 8
NA_WIN_COLS = 16
ROPE_THETA = 10000.0

LRU_WIDTH = D_MODEL
LRU_BLOCKS = 8
LRU_BLOCK = LRU_WIDTH // LRU_BLOCKS
LRU_CONV = 4
LRU_C = 8.0

N_GROUPS = 4
EXPERTS_PER_GROUP = 8
N_EXPERTS = N_GROUPS * EXPERTS_PER_GROUP
TOP_K = 2
D_EXPERT = 512

K_OFF = 0
V_OFF = K_OFF + NA_WIDTH
LX_OFF = V_OFF + NA_WIDTH
CTX_COLS = LX_OFF + LRU_WIDTH
Q_OFF = CTX_COLS
LG_OFF = Q_OFF + NA_WIDTH
GA_OFF = LG_OFF + LRU_WIDTH
GB_OFF = GA_OFF + D_MODEL
PROJ_COLS = GB_OFF + D_MODEL

LANES = 128
SUBLANES = 8

Q_ROWS = 4
K_ROW_BLOCKS = 3
ATT_TQ = Q_ROWS * GRID_W
ATT_TK = K_ROW_BLOCKS * ATT_TQ

MOE_BLK_LOG2 = 9
MOE_BLK = 1 << MOE_BLK_LOG2
MOD_ROWS = 24
ROUTE_ROWS = 64

PROJ_TM = 512
MERGE_TM = 512
ROUTE_TB = 2048
DISPATCH_TM = 2048
COMBINE_TM = 512

VMEM_LIMIT = 56 * 1024 * 1024


def _cparams(sem, vmem=VMEM_LIMIT):
    return pltpu.CompilerParams(dimension_semantics=sem, vmem_limit_bytes=vmem)


def _dot(a, b):
    return jnp.dot(a, b, preferred_element_type=F32)


def _dot_nt(a, b):
    return lax.dot_general(a, b, (((1,), (1,)), ((), ())), preferred_element_type=F32)


def _split2(a):
    hi = a.astype(BF16)
    lo = (a - hi.astype(F32)).astype(BF16)
    return hi, lo


def _dot3(a, b):
    ah, al = _split2(a)
    bh, bl = _split2(b)
    return _dot(ah, bh) + (_dot(ah, bl) + _dot(al, bh))


def _dot3_nt(a, b):
    ah, al = _split2(a)
    bh, bl = _split2(b)
    return _dot_nt(ah, bh) + (_dot_nt(ah, bl) + _dot_nt(al, bh))


def _sigmoid(x):
    return 1.0 / (1.0 + jnp.exp(-x))


def _ada_norm(x, g, sc, sh):
    ms = jnp.mean(x * x, axis=-1, keepdims=True)
    return (x * lax.rsqrt(ms + EPS) * g) * (1.0 + sc) + sh


def _mod_body(cc_ref, w_ref, b_ref, o_ref):
    cc = cc_ref[...]
    o_ref[...] = _dot3(cc * _sigmoid(cc), w_ref[...]) + b_ref[...]


def _mod_call(cc, w_mod, b_mod):
    n = w_mod.shape[1]
    bn = 1024
    return pl.pallas_call(
        _mod_body,
        grid=(n // bn,),
        in_specs=[pl.BlockSpec((MOD_ROWS, D_MODEL), lambda j: (0, 0)),
                  pl.BlockSpec((D_MODEL, bn), lambda j: (0, j)),
                  pl.BlockSpec((1, bn), lambda j: (0, j))],
        out_specs=pl.BlockSpec((MOD_ROWS, bn), lambda j: (0, j)),
        out_shape=jax.ShapeDtypeStruct((MOD_ROWS, n), F32),
        compiler_params=_cparams(("arbitrary",)),
        name="mod",
    )(cc, w_mod, b_mod.reshape(1, n))


def _rope(t, cos, sin):
    lane = lax.broadcasted_iota(I32, (t.shape[0], LANES), 1)
    first = (lane & 16) == 0
    parts = []
    for c in range(t.shape[1] // LANES):
        tc = t[:, c * LANES:(c + 1) * LANES]
        parts.append(jnp.where(first, pltpu.roll(tc, LANES - 16, 1), pltpu.roll(tc, 16, 1)))
    partner = jnp.concatenate(parts, axis=1)
    return t * cos + partner * sin


def _gelu_tanh(x):
    return 0.5 * x * (1.0 + jnp.tanh(0.7978845608028654 * (x + 0.044715 * (x * x * x))))


def _inproj_body(x_ref, mod_ref, g_ref, w_ref, cos_ref, sin_ref,
                 qre_ref, qro_ref, qpe_ref, qpo_ref, k_ref, v_ref, lx_ref, glu_ref, ga_ref, gb_ref):
    sh = mod_ref[0, :, 0:D_MODEL]
    sc = mod_ref[0, :, D_MODEL:2 * D_MODEL]
    h = _ada_norm(x_ref[...], g_ref[...], sc, sh).astype(BF16)
    cos = cos_ref[...]
    sin = sin_ref[...]
    scale = NA_HEAD_DIM ** -0.5

    k_ref[...] = _rope(_dot(h, w_ref[:, K_OFF:K_OFF + NA_WIDTH]), cos, sin).astype(BF16)
    v_ref[...] = _dot(h, w_ref[:, V_OFF:V_OFF + NA_WIDTH]).astype(BF16)
    lx_ref[...] = _dot(h, w_ref[:, LX_OFF:LX_OFF + LRU_WIDTH])

    q = _dot(h, w_ref[:, Q_OFF:Q_OFF + NA_WIDTH]) * scale
    qr = _rope(q, cos, sin)
    lane = lax.broadcasted_iota(I32, q.shape, 1)
    even = (lane & NA_HEAD_DIM) == 0
    qre_ref[...] = jnp.where(even, qr, 0.0).astype(BF16)
    qro_ref[...] = jnp.where(even, 0.0, qr).astype(BF16)
    qpe_ref[...] = jnp.where(even, q, 0.0).astype(BF16)
    qpo_ref[...] = jnp.where(even, 0.0, q).astype(BF16)

    glu_ref[...] = _gelu_tanh(_dot(h, w_ref[:, LG_OFF:LG_OFF + LRU_WIDTH])).astype(BF16)
    ga_ref[...] = _sigmoid(_dot(h, w_ref[:, GA_OFF:GA_OFF + D_MODEL])).astype(BF16)
    gb_ref[...] = _sigmoid(_dot(h, w_ref[:, GB_OFF:GB_OFF + D_MODEL])).astype(BF16)


def _inproj_call(x2, mod3, g_mix, w_in_bf, cos_t, sin_t, seq):
    t = x2.shape[0]
    tm = PROJ_TM
    per_b = seq // tm
    row = lambda i: (i, 0)
    wide = lambda n, dt: jax.ShapeDtypeStruct((t, n), dt)
    return pl.pallas_call(
        _inproj_body,
        grid=(t // tm,),
        in_specs=[pl.BlockSpec((tm, D_MODEL), row),
                  pl.BlockSpec((1, 1, 6 * D_MODEL), lambda i: (i // per_b, 0, 0)),
                  pl.BlockSpec((1, D_MODEL), lambda i: (0, 0)),
                  pl.BlockSpec((D_MODEL, PROJ_COLS), lambda i: (0, 0), pipeline_mode=pl.Buffered(1)),
                  pl.BlockSpec((tm, NA_WIDTH), lambda i: (i % per_b, 0)),
                  pl.BlockSpec((tm, NA_WIDTH), lambda i: (i % per_b, 0))],
        out_specs=[pl.BlockSpec((tm, NA_WIDTH), row)] * 6
                  + [pl.BlockSpec((tm, LRU_WIDTH), row)] * 4,
        out_shape=[wide(NA_WIDTH, BF16)] * 6
                  + [wide(LRU_WIDTH, F32), wide(LRU_WIDTH, BF16), wide(D_MODEL, BF16), wide(D_MODEL, BF16)],
        compiler_params=_cparams(("parallel",)),
        name="inproj",
    )(x2, mod3, g_mix, w_in_bf, cos_t, sin_t)


def _ctxproj_body(x_ref, mod_ref, g_ref, w_ref, k_ref, v_ref, lx_ref):
    sh = mod_ref[:, 0:D_MODEL]
    sc = mod_ref[:, D_MODEL:2 * D_MODEL]
    h = _ada_norm(x_ref[...], g_ref[...], sc, sh).astype(BF16)
    k_ref[...] = _dot(h, w_ref[:, K_OFF:K_OFF + NA_WIDTH]).astype(BF16)
    v_ref[...] = _dot(h, w_ref[:, V_OFF:V_OFF + NA_WIDTH]).astype(BF16)
    lx_ref[...] = _dot(h, w_ref[:, LX_OFF:LX_OFF + LRU_WIDTH])


def _ctxproj_call(c2, mod_c, g_mix, w_ctx_bf):
    t = c2.shape[0]
    tm = PROJ_TM
    row = lambda i: (i, 0)
    return pl.pallas_call(
        _ctxproj_body,
        grid=(t // tm,),
        in_specs=[pl.BlockSpec((tm, D_MODEL), row),
                  pl.BlockSpec((1, 6 * D_MODEL), lambda i: (0, 0)),
                  pl.BlockSpec((1, D_MODEL), lambda i: (0, 0)),
                  pl.BlockSpec((D_MODEL, CTX_COLS), lambda i: (0, 0))],
        out_specs=[pl.BlockSpec((tm, NA_WIDTH), row), pl.BlockSpec((tm, NA_WIDTH), row),
                   pl.BlockSpec((tm, LRU_WIDTH), row)],
        out_shape=[jax.ShapeDtypeStruct((t, NA_WIDTH), BF16), jax.ShapeDtypeStruct((t, NA_WIDTH), BF16),
                   jax.ShapeDtypeStruct((t, LRU_WIDTH), F32)],
        compiler_params=_cparams(("parallel",)),
        name="ctxproj",
    )(c2, mod_c, g_mix, w_ctx_bf)


N_DR = 2 * NA_WIN_ROWS - 1
N_DC = 2 * NA_WIN_COLS - 1


def _rpbcol_body(rpb_ref, o_ref):
    n = GRID_W * GRID_W
    flat = lax.broadcasted_iota(I32, (32, n), 1)
    qc = flat >> 6
    kc = flat & (GRID_W - 1)
    dc = jnp.clip(kc - qc, 1 - NA_WIN_COLS, NA_WIN_COLS - 1) + (NA_WIN_COLS - 1)
    d_iota = lax.broadcasted_iota(I32, (32, n), 0)
    onehot = jnp.where(dc == d_iota, 1.0, 0.0).astype(BF16)
    r = rpb_ref[...]
    r1 = r.astype(BF16)
    rem = r - r1.astype(F32)
    r2 = rem.astype(BF16)
    r3 = (rem - r2.astype(F32)).astype(BF16)
    val = _dot(r1, onehot) + (_dot(r2, onehot) + _dot(r3, onehot))
    qc1 = qc[0:1, :]
    kc1 = kc[0:1, :]
    c_start = jnp.clip(qc1 - NA_WIN_COLS // 2, 0, GRID_W - NA_WIN_COLS)
    band = (kc1 >= c_start) & (kc1 < c_start + NA_WIN_COLS)
    o_ref[...] = jnp.where(band, val, NEG_INF)


def _rpbcol_call(rpb):
    rows = NA_HEADS * N_DR
    r2 = jnp.pad(rpb.reshape(rows, N_DC), ((0, 0), (0, 32 - N_DC)))
    n = GRID_W * GRID_W
    return pl.pallas_call(
        _rpbcol_body,
        in_specs=[pl.BlockSpec((rows, 32), lambda: (0, 0))],
        out_specs=pl.BlockSpec((rows, n), lambda: (0, 0)),
        out_shape=jax.ShapeDtypeStruct((rows, n), F32),
        name="rpbcol",
    )(r2)


def _bias_tables(rpbcol):
    t = rpbcol.reshape(NA_HEADS, N_DR, GRID_W, GRID_W)
    neg = jnp.full((NA_HEADS, GRID_W, GRID_W), NEG_INF, F32)
    n_kj = K_ROW_BLOCKS * Q_ROWS
    classes = []
    for lo_fn, dr_off in ((lambda ri: 0, 7), (lambda ri: ri, 3), (lambda ri: 4, -1)):
        rows = []
        for ri in range(Q_ROWS):
            lo = lo_fn(ri)
            blocks = []
            for kj in range(n_kj):
                inside = lo <= kj < lo + NA_WIN_ROWS
                blocks.append(t[:, kj - ri + dr_off] if inside else neg)
            rows.append(jnp.concatenate(blocks, axis=2))
        classes.append(jnp.concatenate(rows, axis=1))
    return jnp.stack(classes, axis=0)


def _attn_body(qre_ref, qro_ref, qpe_ref, qpo_ref, k0_ref, k1_ref, k2_ref, v0_ref, v1_ref, v2_ref,
               kc_ref, vc_ref, bias_ref, o_ref):
    lane = lax.broadcasted_iota(I32, (ATT_TQ, LANES), 1)
    lane1 = lax.broadcasted_iota(I32, (1, LANES), 1)
    head_lanes = [jnp.where(lane1 < NA_HEAD_DIM, 1.0, 0.0).astype(BF16),
                  jnp.where(lane1 < NA_HEAD_DIM, 0.0, 1.0).astype(BF16)]
    k_refs = (k0_ref, k1_ref, k2_ref)
    v_refs = (v0_ref, v1_ref, v2_ref)
    for bi, p in [(bi, p) for bi in range(ATT_BATCH) for p in range(NA_HEADS // 2)]:
        sl = slice(p * LANES, (p + 1) * LANES)
        k_lat = jnp.concatenate([r[bi, :, sl] for r in k_refs], axis=0)
        kc = kc_ref[bi, :, sl]
        v_all = jnp.concatenate([r[bi, :, sl] for r in v_refs] + [vc_ref[bi, :, sl]], axis=0)
        outs = []
        for hh, (qr_ref, qp_ref) in enumerate(((qre_ref, qpe_ref), (qro_ref, qpo_ref))):
            h = 2 * p + hh
            s_lat = _dot_nt(qr_ref[bi, :, sl], k_lat) + bias_ref[0, h]
            s_ctx = _dot_nt(qp_ref[bi, :, sl], kc)
            tiles = ([s_lat[:, j * ATT_TQ:(j + 1) * ATT_TQ] for j in range(K_ROW_BLOCKS)]
                     + [s_ctx[:, j * ATT_TQ:(j + 1) * ATT_TQ] for j in range(s_ctx.shape[1] // ATT_TQ)])
            m = functools.reduce(jnp.maximum, tiles).max(axis=-1, keepdims=True)
            prob = jnp.exp(jnp.concatenate([s_lat, s_ctx], axis=1) - m).astype(BF16)
            mine = head_lanes[hh]
            acc = _dot(prob, v_all * mine + (1.0 - mine).astype(BF16))
            outs.append(acc / pltpu.roll(acc, NA_HEAD_DIM, 1))
        o_ref[bi, :, sl] = jnp.where(lane < NA_HEAD_DIM, outs[0], outs[1]).astype(BF16)


ATT_BATCH = 4


def _attn_call(qre, qro, qpe, qpo, k, v, kc, vc, bias, bsz, seq, n_ctx):
    n_grp = seq // ATT_TQ
    max_kb = n_grp - K_ROW_BLOCKS
    assert bsz % ATT_BATCH == 0
    by_batch = lambda a, n: a.reshape(bsz, n, NA_WIDTH)

    def qmap(g, b):
        return (b, g, 0)

    def kmap(j):
        return lambda g, b: (b, jnp.clip(g - 1, 0, max_kb) + j, 0)

    def cls(g, b):
        return (jnp.where(g == 0, 0, jnp.where(g == n_grp - 1, 2, 1)), 0, 0, 0)

    qspec = pl.BlockSpec((ATT_BATCH, ATT_TQ, NA_WIDTH), qmap)
    cspec = pl.BlockSpec((ATT_BATCH, n_ctx, NA_WIDTH), lambda g, b: (b, 0, 0))
    k3, v3 = by_batch(k, seq), by_batch(v, seq)
    return pl.pallas_call(
        _attn_body,
        grid=(n_grp, bsz // ATT_BATCH),
        in_specs=[qspec] * 4
                 + [pl.BlockSpec((ATT_BATCH, ATT_TQ, NA_WIDTH), kmap(j)) for j in range(K_ROW_BLOCKS)] * 2
                 + [cspec, cspec, pl.BlockSpec((1, NA_HEADS, ATT_TQ, ATT_TK), cls)],
        out_specs=qspec,
        out_shape=jax.ShapeDtypeStruct((bsz, seq, NA_WIDTH), BF16),
        compiler_params=_cparams(("arbitrary", "arbitrary")),
        name="attn",
    )(by_batch(qre, seq), by_batch(qro, seq), by_batch(qpe, seq), by_batch(qpo, seq), k3, k3, k3, v3, v3, v3,
      by_batch(kc, n_ctx), by_batch(vc, n_ctx), bias).reshape(bsz * seq, NA_WIDTH)


def _shift_down(v, row):
    return jnp.where(row >= 1, pltpu.roll(v, 1, 0), 0.0)


def _shift_up(v, row):
    return jnp.where(row < SUBLANES - 1, pltpu.roll(v, SUBLANES - 1, 0), 0.0)


def _conv4(x, w, b):
    n = x.shape[0]
    s = SUBLANES
    row = lax.broadcasted_iota(I32, (s, LANES), 0)
    last = _shift_down(x[n - s:n], row)
    last2 = _shift_down(x[n - 2 * s:n - s], row)
    first = _shift_up(x[0:s], row)
    xm1 = jnp.concatenate([last, x[0:n - s]], axis=0)
    xm2 = jnp.concatenate([last2, last, x[0:n - 2 * s]], axis=0)
    xp1 = jnp.concatenate([x[s:n], first], axis=0)
    return (w[0:1, :] * xm2 + w[1:2, :] * xm1 + w[2:3, :] * x + w[3:4, :] * xp1) + b


def _softplus(z):
    return jnp.maximum(z, 0.0) + jnp.log1p(jnp.exp(-jnp.abs(z)))


def _gates(xb, xh, wa, wx, ba, bx, lam, a_ref, u_ref):
    n = xb.shape[0]
    tr = jnp.tanh(_dot(xb, (0.5 * wa).astype(BF16)) + 0.5 * ba)
    ti = jnp.tanh(_dot(xb, (0.5 * wx).astype(BF16)) + 0.5 * bx)
    half_c = (0.5 * LRU_C) * _softplus(-lam)
    neg_log_a = half_c * tr + half_c
    a = jnp.exp(-neg_log_a)
    a_ref[0:n, :] = a
    s2 = jnp.tanh(neg_log_a) * (a * a + 1.0)
    root = jnp.where(s2 > 0.0, s2 * lax.rsqrt(s2), 0.0)
    u_ref[0:n, :] = root * (xh * ti + xh)


def _scan4(a, u, h, p):
    a01 = a[1] * a[0]
    u01 = a[1] * u[0] + u[1]
    a23 = a[3] * a[2]
    u23 = a[3] * u[2] + u[3]
    a012 = a[2] * a01
    u012 = a[2] * u01 + u[2]
    a0123 = a23 * a01
    u0123 = a23 * u01 + u23
    hs = [a[0] * h + u[0], a01 * h + u01, a012 * h + u012, a0123 * h + u0123]
    ps = [a[0] * p, a01 * p, a012 * p, a0123 * p]
    return hs, ps


SCAN_STEPS = 4


def _scan_local(af_ref, uf_ref, ab_ref, ub_ref, n_vreg):
    s = SUBLANES
    zero = jnp.zeros((s, LANES), F32)
    one = jnp.ones((s, LANES), F32)
    span = SCAN_STEPS * s

    def body(q, carry):
        hf, pf, hb, pb = carry
        base = pl.multiple_of(q * span, span)
        rows = [pl.ds(base + i * s, s) for i in range(SCAN_STEPS)]
        hs, ps = _scan4([af_ref[r, :] for r in rows], [uf_ref[r, :] for r in rows], hf, pf)
        for r, h, p in zip(rows, hs, ps):
            uf_ref[r, :] = h
            af_ref[r, :] = p
        hf, pf = hs[-1], ps[-1]
        base = pl.multiple_of((n_vreg - SCAN_STEPS) * s - q * span, span)
        rows = [pl.ds(base + (SCAN_STEPS - 1 - i) * s, s) for i in range(SCAN_STEPS)]
        hs, ps = _scan4([ab_ref[r, :] for r in rows], [ub_ref[r, :] for r in rows], hb, pb)
        for r, h, p in zip(rows, hs, ps):
            ub_ref[r, :] = h
            ab_ref[r, :] = p
        return hf, pf, hs[-1], ps[-1]

    return lax.fori_loop(0, n_vreg // SCAN_STEPS, body, (zero, one, zero, one), unroll=2)


def _link_states(hf, pf, hb, pb, h0f, h0b):
    s = SUBLANES
    row = lax.broadcasted_iota(I32, (s, LANES), 0)
    a, u = pf, hf
    for k in (1, 2, 4):
        keep = row >= k
        u = u + a * jnp.where(keep, pltpu.roll(u, k, 0), 0.0)
        a = a * jnp.where(keep, pltpu.roll(a, k, 0), 1.0)
    end_f = u + a * h0f
    in_f = jnp.where(row >= 1, pltpu.roll(end_f, 1, 0), h0f)
    a, u = pb, hb
    for k in (1, 2, 4):
        keep = row < s - k
        u = u + a * jnp.where(keep, pltpu.roll(u, s - k, 0), 0.0)
        a = a * jnp.where(keep, pltpu.roll(a, s - k, 0), 1.0)
    end_b = u + a * h0b
    in_b = jnp.where(row < s - 1, pltpu.roll(end_b, s - 1, 0), h0b)
    return in_f, in_b, end_f[s - 1:s, :], end_b[0:1, :]


PITCH_PAD = 4


def _to_split(x_ref, lanes, pad_ref, dst_ref, n):
    s = SUBLANES
    n_j = n // s
    pitch = n_j + PITCH_PAD
    for q in range(s):
        pad_ref[pl.ds(q * pitch, n_j), :] = x_ref[0, pl.ds(q * n_j, n_j), lanes]

    def body(j, c):
        dst_ref[pl.ds(pl.multiple_of(j * s, s), s), :] = pad_ref[pl.ds(j, s, stride=pitch), :]
        return c

    lax.fori_loop(0, n_j, body, 0, unroll=8)


LRU_BLOCKS_PER_STEP = 4


def _lru_body(lx_ref, lxc_ref, cw_ref, cb_ref, wa_ref, wx_ref, ba_ref, bx_ref, lam_ref,
              o_ref, af_ref, uf_ref, ab_ref, ub_ref, pad_ref):
    n = lx_ref.shape[1]
    n_c = lxc_ref.shape[1]
    s = SUBLANES
    n_j = n // s
    pitch = n_j + PITCH_PAD
    zero = jnp.zeros((1, LANES), F32)

    for blk in range(LRU_BLOCKS_PER_STEP):
        lanes = slice(blk * LRU_BLOCK, (blk + 1) * LRU_BLOCK)
        conv_w = cw_ref[:, lanes]
        conv_b = cb_ref[:, lanes]
        gate = [(wa_ref[d, blk], wx_ref[d, blk], ba_ref[d:d + 1, lanes], bx_ref[d:d + 1, lanes],
                 lam_ref[d:d + 1, lanes]) for d in range(2)]

        _to_split(lxc_ref, lanes, pad_ref, uf_ref, n_c)
        xc = _conv4(uf_ref[0:n_c, :], conv_w, conv_b)
        xb, xh = xc.astype(BF16), 0.5 * xc
        _gates(xb, xh, *gate[0], af_ref, uf_ref)
        _gates(xb, xh, *gate[1], ab_ref, ub_ref)
        ends = _scan_local(af_ref, uf_ref, ab_ref, ub_ref, n_c // s)
        _, _, cf, cb = _link_states(*ends, zero, zero)

        _to_split(lx_ref, lanes, pad_ref, uf_ref, n)
        xl = _conv4(uf_ref[...], conv_w, conv_b)
        xb, xh = xl.astype(BF16), 0.5 * xl
        _gates(xb, xh, *gate[0], af_ref, uf_ref)
        _gates(xb, xh, *gate[1], ab_ref, ub_ref)
        ends = _scan_local(af_ref, uf_ref, ab_ref, ub_ref, n_j)
        in_f, in_b, _, _ = _link_states(*ends, cf, cb)

        def finish(j, c):
            rows = pl.ds(pl.multiple_of(j * s, s), s)
            h = (uf_ref[rows, :] + af_ref[rows, :] * in_f) + (ub_ref[rows, :] + ab_ref[rows, :] * in_b)
            pad_ref[pl.ds(j, s, stride=pitch), :] = h
            return c

        lax.fori_loop(0, n_j, finish, 0, unroll=8)
        for q in range(s):
            o_ref[0, pl.ds(q * n_j, n_j), lanes] = pad_ref[pl.ds(q * pitch, n_j), :].astype(o_ref.dtype)


def _lru_call(lx3, lxc3, conv_w, conv_b, wa, wx, ba, bx, lam):
    bsz, seq, _ = lx3.shape
    n_ctx = lxc3.shape[1]
    per = LRU_BLOCKS_PER_STEP
    width = per * LRU_BLOCK
    col = lambda b, n: (b, 0, n)
    par = lambda b, n: (0, n)
    wspec = pl.BlockSpec((2, per, LRU_BLOCK, LRU_BLOCK), lambda b, n: (0, n, 0, 0))
    return pl.pallas_call(
        _lru_body,
        grid=(bsz, LRU_BLOCKS // per),
        in_specs=[pl.BlockSpec((1, seq, width), col),
                  pl.BlockSpec((1, n_ctx, width), col),
                  pl.BlockSpec((LRU_CONV, width), par),
                  pl.BlockSpec((1, width), par),
                  wspec, wspec,
                  pl.BlockSpec((2, width), par),
                  pl.BlockSpec((2, width), par),
                  pl.BlockSpec((2, width), par)],
        out_specs=pl.BlockSpec((1, seq, width), col),
        out_shape=jax.ShapeDtypeStruct((bsz, seq, LRU_WIDTH), BF16),
        scratch_shapes=[pltpu.VMEM((seq, LRU_BLOCK), F32)] * 4
                       + [pltpu.VMEM((seq + SUBLANES * PITCH_PAD, LRU_BLOCK), F32)],
        compiler_params=_cparams(("parallel", "arbitrary")),
        name="lru",
    )(lx3, lxc3, conv_w, conv_b, wa, wx, ba, bx, lam)


TOKEN_TILE = D_MODEL // LANES


def _store_token_tiles(ref, row0, x):
    m = x.shape[0]
    for c in range(TOKEN_TILE):
        ref[pl.ds(row0 + c, m, stride=TOKEN_TILE), :] = x[:, c * LANES:(c + 1) * LANES]


def _load_token_tiles(ref, row0, m):
    return jnp.concatenate([ref[pl.ds(row0 + c, m, stride=TOKEN_TILE), :] for c in range(TOKEN_TILE)], axis=1)


def _merge_body(x_ref, oa_ref, hs_ref, glu_ref, ga_ref, gb_ref, mod_ref, g_ref, wua_ref, wul_ref, wo_ref,
                wr_ref, br_ref, x1_ref, h2_ref, lt_ref):
    ga1 = mod_ref[0, :, 2 * D_MODEL:3 * D_MODEL]
    sh2 = mod_ref[0, :, 3 * D_MODEL:4 * D_MODEL]
    sc2 = mod_ref[0, :, 4 * D_MODEL:5 * D_MODEL]
    o_lru = hs_ref[...] * glu_ref[...]
    y = (ga_ref[...].astype(F32) * _dot(oa_ref[...], wua_ref[...])
         + gb_ref[...].astype(F32) * _dot(o_lru, wul_ref[...]))
    x1 = x_ref[...] + ga1 * _dot(y.astype(BF16), wo_ref[...])
    x1_ref[...] = x1
    h2 = _ada_norm(x1, g_ref[...], sc2, sh2)
    lt_ref[...] = _dot3_nt(wr_ref[...], h2) + br_ref[:, 0:1]
    _store_token_tiles(h2_ref, 0, h2)


def _merge_call(x2, o_att, hs, glu, ga, gb, mod3, g_ffn, wua, wul, wo, wr_t, br, seq):
    t = x2.shape[0]
    tm = MERGE_TM
    per_b = seq // tm
    row = lambda i: (i, 0)
    full = lambda i: (0, 0)
    resident = lambda shape: pl.BlockSpec(shape, full, pipeline_mode=pl.Buffered(1))
    return pl.pallas_call(
        _merge_body,
        grid=(t // tm,),
        in_specs=[pl.BlockSpec((tm, D_MODEL), row),
                  pl.BlockSpec((tm, NA_WIDTH), row),
                  pl.BlockSpec((tm, LRU_WIDTH), row),
                  pl.BlockSpec((tm, LRU_WIDTH), row),
                  pl.BlockSpec((tm, D_MODEL), row),
                  pl.BlockSpec((tm, D_MODEL), row),
                  pl.BlockSpec((1, 1, 6 * D_MODEL), lambda i: (i // per_b, 0, 0)),
                  pl.BlockSpec((1, D_MODEL), full),
                  resident((NA_WIDTH, D_MODEL)),
                  resident((LRU_WIDTH, D_MODEL)),
                  resident((D_MODEL, D_MODEL)),
                  pl.BlockSpec((ROUTE_ROWS, D_MODEL), full),
                  pl.BlockSpec((ROUTE_ROWS, LANES), full)],
        out_specs=[pl.BlockSpec((tm, D_MODEL), row),
                   pl.BlockSpec((tm * TOKEN_TILE, LANES), row),
                   pl.BlockSpec((ROUTE_ROWS, tm), lambda i: (0, i))],
        out_shape=[jax.ShapeDtypeStruct((t, D_MODEL), F32),
                   jax.ShapeDtypeStruct((t * TOKEN_TILE, LANES), F32),
                   jax.ShapeDtypeStruct((ROUTE_ROWS, t), F32)],
        compiler_params=_cparams(("parallel",)),
        name="merge",
    )(x2, o_att, hs, glu, ga, gb, mod3, g_ffn, wua, wul, wo, wr_t, br)


def _route_body(lt_ref, eid_ref, gate_ref, rank_ref, cnt_ref, carry_ref):
    step = pl.program_id(0)

    @pl.when(step == 0)
    def _():
        carry_ref[...] = jnp.zeros_like(carry_ref)

    tb = lt_ref.shape[1]
    lg = [lt_ref[r:r + 1, :] for r in range(N_GROUPS)]
    best = lg[0]
    gidx = jnp.zeros((1, tb), I32)
    for r in range(1, N_GROUPS):
        better = lg[r] > best
        gidx = jnp.where(better, r, gidx)
        best = jnp.maximum(best, lg[r])
    den = jnp.exp(lg[0] - best)
    for r in range(1, N_GROUPS):
        den = den + jnp.exp(lg[r] - best)
    p_top = 1.0 / den

    ev = []
    for j in range(EXPERTS_PER_GROUP):
        sel = lt_ref[N_GROUPS + j:N_GROUPS + j + 1, :]
        for g in range(1, N_GROUPS):
            row = N_GROUPS + g * EXPERTS_PER_GROUP + j
            sel = jnp.where(gidx == g, lt_ref[row:row + 1, :], sel)
        ev.append(sel)
    v0 = ev[0]
    i0 = jnp.zeros((1, tb), I32)
    for j in range(1, EXPERTS_PER_GROUP):
        better = ev[j] > v0
        i0 = jnp.where(better, j, i0)
        v0 = jnp.maximum(v0, ev[j])
    v1 = jnp.full((1, tb), -jnp.inf, F32)
    i1 = jnp.zeros((1, tb), I32)
    for j in range(EXPERTS_PER_GROUP):
        better = (ev[j] > v1) & (i0 != j)
        i1 = jnp.where(better, j, i1)
        v1 = jnp.where(better, ev[j], v1)
    e1 = jnp.exp(v1 - v0)
    inv = 1.0 / (1.0 + e1)
    eid0 = gidx * EXPERTS_PER_GROUP + i0
    eid1 = gidx * EXPERTS_PER_GROUP + i1
    eid_ref[0:1, :] = eid0
    eid_ref[1:2, :] = eid1
    gate_ref[...] = jnp.zeros_like(gate_ref)
    gate_ref[0:1, :] = p_top * inv
    gate_ref[1:2, :] = p_top * (e1 * inv)

    sub = 256
    e_iota = lax.broadcasted_iota(I32, (N_EXPERTS, sub), 0)
    tri = jnp.where(lax.broadcasted_iota(I32, (sub, sub), 0) <= lax.broadcasted_iota(I32, (sub, sub), 1),
                    1.0, 0.0).astype(BF16)
    carry = carry_ref[...]
    for c in range(tb // sub):
        sl = slice(c * sub, (c + 1) * sub)
        m0 = eid0[:, sl] == e_iota
        m1 = eid1[:, sl] == e_iota
        oh = jnp.where(m0 | m1, 1.0, 0.0)
        incl = _dot(oh.astype(BF16), tri)
        excl = incl - oh + carry[:, 0:1]
        rank_ref[0:1, sl] = jnp.sum(jnp.where(m0, excl, 0.0), axis=0, keepdims=True).astype(I32)
        rank_ref[1:2, sl] = jnp.sum(jnp.where(m1, excl, 0.0), axis=0, keepdims=True).astype(I32)
        carry = carry + incl[:, sub - 1:sub]
    carry_ref[...] = carry
    cnt_ref[...] = carry


def _route_call(logits_t):
    t = logits_t.shape[1]
    tb = ROUTE_TB
    col = lambda i: (0, i)
    return pl.pallas_call(
        _route_body,
        grid=(t // tb,),
        in_specs=[pl.BlockSpec((ROUTE_ROWS, tb), col)],
        out_specs=[pl.BlockSpec((TOP_K, tb), col), pl.BlockSpec((SUBLANES, tb), col),
                   pl.BlockSpec((TOP_K, tb), col), pl.BlockSpec((N_EXPERTS, LANES), lambda i: (0, 0))],
        out_shape=[jax.ShapeDtypeStruct((TOP_K, t), I32), jax.ShapeDtypeStruct((SUBLANES, t), F32),
                   jax.ShapeDtypeStruct((TOP_K, t), I32), jax.ShapeDtypeStruct((N_EXPERTS, LANES), F32)],
        scratch_shapes=[pltpu.VMEM((N_EXPERTS, LANES), F32)],
        compiler_params=_cparams(("arbitrary",)),
        name="route",
    )(logits_t)


def _dest_body(cnt_ref, eid_ref, rank_ref, dest_ref, blk_ref):
    cnt = cnt_ref[...].astype(I32)
    padded = ((cnt + (MOE_BLK - 1)) >> MOE_BLK_LOG2) << MOE_BLK_LOG2
    e_iota = lax.broadcasted_iota(I32, (N_EXPERTS, LANES), 0)
    p_end = jnp.zeros((N_EXPERTS, LANES), I32)
    for e in range(N_EXPERTS):
        tot = jnp.sum(jnp.where(e_iota <= e, padded, 0), axis=0, keepdims=True)
        p_end = jnp.where(e_iota == e, tot, p_end)
    p_start = p_end - padded
    tb = eid_ref.shape[1]
    ps = jnp.concatenate([p_start] * (tb // LANES), axis=1)
    e_wide = lax.broadcasted_iota(I32, (N_EXPERTS, tb), 0)
    for k in range(TOP_K):
        start = jnp.sum(jnp.where(eid_ref[k:k + 1, :] == e_wide, ps, 0), axis=0, keepdims=True)
        dest_ref[k:k + 1, :] = start + rank_ref[k:k + 1, :]
    nb = blk_ref.shape[1]
    pe = jnp.concatenate([p_end] * (nb // LANES), axis=1)
    first_row = lax.broadcasted_iota(I32, (N_EXPERTS, nb), 1) * MOE_BLK
    n_before = jnp.sum(jnp.where(pe <= first_row, 1, 0), axis=0, keepdims=True)
    blk = jnp.minimum(n_before, N_EXPERTS - 1)
    blk_ref[...] = jnp.broadcast_to(blk, blk_ref.shape)
    blk_ref[1:2, :] = jnp.broadcast_to(p_end[N_EXPERTS - 1:N_EXPERTS, 0:1] >> MOE_BLK_LOG2, (1, nb))
    on_diag = e_iota == lax.broadcasted_iota(I32, (N_EXPERTS, LANES), 1)
    blk_ref[2:3, 0:LANES] = jnp.sum(jnp.where(on_diag, p_end, 0), axis=0, keepdims=True)
    blk_ref[3:4, 0:LANES] = jnp.sum(jnp.where(on_diag, padded, 0), axis=0, keepdims=True)


def _dest_call(cnt, eid, rank, nb_pad):
    t = eid.shape[1]
    tb = ROUTE_TB
    col = lambda i: (0, i)
    return pl.pallas_call(
        _dest_body,
        grid=(t // tb,),
        in_specs=[pl.BlockSpec((N_EXPERTS, LANES), lambda i: (0, 0)),
                  pl.BlockSpec((TOP_K, tb), col), pl.BlockSpec((TOP_K, tb), col)],
        out_specs=[pl.BlockSpec((TOP_K, tb), col), pl.BlockSpec((SUBLANES, nb_pad), lambda i: (0, 0))],
        out_shape=[jax.ShapeDtypeStruct((TOP_K, t), I32), jax.ShapeDtypeStruct((SUBLANES, nb_pad), I32)],
        compiler_params=_cparams(("arbitrary",)),
        name="dest",
    )(cnt, eid, rank)


def _token_tile(ref, t):
    return ref.at[pl.ds(pl.multiple_of(t * TOKEN_TILE, TOKEN_TILE), TOKEN_TILE)]


def _dispatch_body(dest_ref, pend_ref, plen_ref, h_ref, xs_ref, zero_ref, sem, zsem):
    tm = h_ref.shape[0] // TOKEN_TILE
    blk_rows = MOE_BLK * TOKEN_TILE
    n_tok = pl.num_programs(0) * tm
    base = pl.program_id(0) * tm

    @pl.when(pl.program_id(0) == 0)
    def _():
        zero_ref[...] = jnp.zeros_like(zero_ref)

        def zero_block(start):
            rows = pl.ds(pl.multiple_of(start * TOKEN_TILE, blk_rows), blk_rows)
            return pltpu.make_async_copy(zero_ref, xs_ref.at[rows], zsem)

        def fill(e, c):
            @pl.when(plen_ref[e] > 0)
            def _():
                zero_block(pend_ref[e] - MOE_BLK).start()
            return c

        def drain(e, c):
            @pl.when(plen_ref[e] > 0)
            def _():
                zero_block(pend_ref[e] - MOE_BLK).wait()
            return c

        lax.fori_loop(0, N_EXPERTS, fill, 0)
        n_used = pend_ref[N_EXPERTS - 1] >> MOE_BLK_LOG2
        n_blk = xs_ref.shape[0] // blk_rows
        lax.fori_loop(n_used, n_blk, lambda j, c: (zero_block(j * MOE_BLK).start(), c)[1], 0)
        lax.fori_loop(0, N_EXPERTS, drain, 0)
        lax.fori_loop(n_used, n_blk, lambda j, c: (zero_block(j * MOE_BLK).wait(), c)[1], 0)

    def issue(r, c):
        for k in range(TOP_K):
            d = dest_ref[k * n_tok + base + r]
            pltpu.make_async_copy(_token_tile(h_ref, r), _token_tile(xs_ref, d), sem).start(priority=k)
        return c

    lax.fori_loop(0, tm, issue, 0, unroll=8)
    for k in range(TOP_K):
        pltpu.make_async_copy(h_ref, xs_ref.at[pl.ds(0, tm * TOKEN_TILE)], sem).wait()


def _dispatch_call(dest_flat, p_end, p_len, h2t, n_slots):
    rows = DISPATCH_TM * TOKEN_TILE
    return pl.pallas_call(
        _dispatch_body,
        grid_spec=pltpu.PrefetchScalarGridSpec(
            num_scalar_prefetch=3,
            grid=(h2t.shape[0] // rows,),
            in_specs=[pl.BlockSpec((rows, LANES), lambda i, d, pe, pn: (i, 0))],
            out_specs=pl.BlockSpec(memory_space=pl.ANY),
            scratch_shapes=[pltpu.VMEM((MOE_BLK * TOKEN_TILE, LANES), F32),
                            pltpu.SemaphoreType.DMA(()), pltpu.SemaphoreType.DMA(())]),
        out_shape=jax.ShapeDtypeStruct((n_slots * TOKEN_TILE, LANES), F32),
        compiler_params=_cparams(("arbitrary",)),
        name="dispatch",
    )(dest_flat, p_end, p_len, h2t)


EXPERT_BLKS_PER_STEP = 2


def _experts_body(blk_ref, used_ref, xs_ref, w1_hbm, w3_hbm, w2_hbm, y_ref,
                  f1_ref, f3_ref, f2_ref, b1_ref, b3_ref, b2_ref, loaded_ref, sem):
    i = pl.program_id(0)
    per = EXPERT_BLKS_PER_STEP
    used = used_ref[0]

    def weight_copies(e):
        return [pltpu.make_async_copy(w_hbm.at[e], f_ref, sem)
                for w_hbm, f_ref in ((w1_hbm, f1_ref), (w3_hbm, f3_ref), (w2_hbm, f2_ref))]

    @pl.when(i == 0)
    def _():
        loaded_ref[0] = -1
        for cp in weight_copies(blk_ref[0]):
            cp.start()

    for half in range(per):
        blk = per * i + half
        row0 = half * MOE_BLK * TOKEN_TILE
        e = blk_ref[blk]

        @pl.when((blk < used) & (e != loaded_ref[0]))
        def _():
            for cp in weight_copies(e):
                cp.wait()
            b1_ref[...] = f1_ref[...].astype(BF16)
            b3_ref[...] = f3_ref[...].astype(BF16)
            b2_ref[...] = f2_ref[...].astype(BF16)
            loaded_ref[0] = e
            nxt = lax.while_loop(lambda j: (j < used) & (blk_ref[jnp.minimum(j, blk_ref.shape[0] - 1)] == e),
                                 lambda j: j + 1, blk + 1)

            @pl.when(nxt < used)
            def _():
                for cp in weight_copies(blk_ref[jnp.minimum(nxt, blk_ref.shape[0] - 1)]):
                    cp.start()

        @pl.when(blk < used)
        def _():
            x = _load_token_tiles(xs_ref, row0, MOE_BLK).astype(BF16)
            g = _dot(x, b1_ref[...])
            u = _dot(x, b3_ref[...])
            mid = (g * _sigmoid(g)) * u
            _store_token_tiles(y_ref, row0, _dot(mid.astype(BF16), b2_ref[...]))

        @pl.when(blk >= used)
        def _():
            y_ref[pl.ds(row0, MOE_BLK * TOKEN_TILE), :] = jnp.zeros((MOE_BLK * TOKEN_TILE, LANES), F32)


def _experts_call(blk_e, n_used, xs, w1, w3, w2):
    per = EXPERT_BLKS_PER_STEP
    rows = per * MOE_BLK * TOKEN_TILE
    nb = xs.shape[0] // (MOE_BLK * TOKEN_TILE)
    assert nb % per == 0 and blk_e.shape[0] == nb
    hbm = pl.BlockSpec(memory_space=pl.ANY)
    return pl.pallas_call(
        _experts_body,
        grid_spec=pltpu.PrefetchScalarGridSpec(
            num_scalar_prefetch=2,
            grid=(nb // per,),
            in_specs=[pl.BlockSpec((rows, LANES), lambda i, blk, used: (i, 0)), hbm, hbm, hbm],
            out_specs=pl.BlockSpec((rows, LANES), lambda i, blk, used: (i, 0)),
            scratch_shapes=[pltpu.VMEM((D_MODEL, D_EXPERT), F32), pltpu.VMEM((D_MODEL, D_EXPERT), F32),
                            pltpu.VMEM((D_EXPERT, D_MODEL), F32),
                            pltpu.VMEM((D_MODEL, D_EXPERT), BF16), pltpu.VMEM((D_MODEL, D_EXPERT), BF16),
                            pltpu.VMEM((D_EXPERT, D_MODEL), BF16),
                            pltpu.SMEM((1,), I32), pltpu.SemaphoreType.DMA(())]),
        out_shape=jax.ShapeDtypeStruct(xs.shape, F32),
        compiler_params=_cparams(("arbitrary",)),
        name="experts",
    )(blk_e, n_used, xs, w1, w3, w2)


def _combine_body(dest_ref, x1_ref, gate_ref, mod_ref, gf_ref, y_ref, o_ref, buf_ref, sem):
    tm = x1_ref.shape[0]
    step = pl.program_id(0)
    n_step = pl.num_programs(0)
    n_tok = n_step * tm
    slot = step % 2
    region = tm * TOKEN_TILE

    def region_row0(buf, k):
        return pl.multiple_of((buf * TOP_K + k) * region, region)

    def start_gather(for_step):
        def issue(r, c):
            for k in range(TOP_K):
                d = dest_ref[k * n_tok + for_step * tm + r]
                dst = buf_ref.at[pl.ds(pl.multiple_of(region_row0(for_step % 2, k) + r * TOKEN_TILE, TOKEN_TILE),
                                       TOKEN_TILE)]
                pltpu.make_async_copy(_token_tile(y_ref, d), dst, sem.at[for_step % 2]).start(priority=k)
            return c

        lax.fori_loop(0, tm, issue, 0, unroll=8)

    @pl.when(step == 0)
    def _():
        start_gather(step)

    @pl.when(step + 1 < n_step)
    def _():
        start_gather(step + 1)

    eye = jnp.where(lax.broadcasted_iota(I32, (tm, tm), 0) == lax.broadcasted_iota(I32, (tm, tm), 1),
                    1.0, 0.0).astype(BF16)
    g = gate_ref[...]
    g1 = g.astype(BF16)
    rem = g - g1.astype(F32)
    g2 = rem.astype(BF16)
    g3 = (rem - g2.astype(F32)).astype(BF16)
    gt = _dot_nt(eye, g1) + (_dot_nt(eye, g2) + _dot_nt(eye, g3))

    for k in range(TOP_K):
        pltpu.make_async_copy(y_ref.at[pl.ds(0, region)], buf_ref.at[pl.ds(region_row0(slot, k), region)],
                              sem.at[slot]).wait()

    ga2 = mod_ref[0, :, 5 * D_MODEL:6 * D_MODEL]
    moe = (gt[:, 0:1] * _load_token_tiles(buf_ref, region_row0(slot, 0), tm)
           + gt[:, 1:2] * _load_token_tiles(buf_ref, region_row0(slot, 1), tm))
    x2 = x1_ref[...] + ga2 * moe
    ms = jnp.mean(x2 * x2, axis=-1, keepdims=True)
    o_ref[...] = x2 * lax.rsqrt(ms + EPS) * gf_ref[...]


def _combine_call(dest_flat, x1, gate, mod3, g_final, y, seq):
    t = x1.shape[0]
    tm = COMBINE_TM
    per_b = seq // tm
    return pl.pallas_call(
        _combine_body,
        grid_spec=pltpu.PrefetchScalarGridSpec(
            num_scalar_prefetch=1,
            grid=(t // tm,),
            in_specs=[pl.BlockSpec((tm, D_MODEL), lambda i, d: (i, 0)),
                      pl.BlockSpec((SUBLANES, tm), lambda i, d: (0, i)),
                      pl.BlockSpec((1, 1, 6 * D_MODEL), lambda i, d: (i // per_b, 0, 0)),
                      pl.BlockSpec((1, D_MODEL), lambda i, d: (0, 0)),
                      pl.BlockSpec(memory_space=pl.ANY)],
            out_specs=pl.BlockSpec((tm, D_MODEL), lambda i, d: (i, 0)),
            scratch_shapes=[pltpu.VMEM((2 * TOP_K * tm * TOKEN_TILE, LANES), F32),
                            pltpu.SemaphoreType.DMA((2,))]),
        out_shape=jax.ShapeDtypeStruct((t, D_MODEL), F32),
        compiler_params=_cparams(("arbitrary",)),
        name="combine",
    )(dest_flat, x1, gate, mod3, g_final, y)


def _rope_tables(seq):
    half = NA_HEAD_DIM // 2
    nf = half // 2
    inv_freq = ROPE_THETA ** (-jnp.arange(nf, dtype=F32) / nf)
    t = jnp.arange(seq)
    row_pos = (t // GRID_W).astype(F32)
    col_pos = (t % GRID_W).astype(F32)
    ang_r = row_pos[:, None] * inv_freq
    ang_c = col_pos[:, None] * inv_freq
    cos = jnp.concatenate([jnp.cos(ang_r), jnp.cos(ang_r), jnp.cos(ang_c), jnp.cos(ang_c)], axis=-1)
    sin = jnp.concatenate([-jnp.sin(ang_r), jnp.sin(ang_r), -jnp.sin(ang_c), jnp.sin(ang_c)], axis=-1)
    return jnp.tile(cos, (1, NA_HEADS)), jnp.tile(sin, (1, NA_HEADS))


def _layer(x, c, ctx, c_ctx, w_mod, b_mod, g_mix, g_ffn, w_in, rpb, conv_w, conv_b, lru_wa, lru_ba,
           lru_wx, lru_bx, lru_lambda, w_up_attn, w_up_lru, w_out, wg, bg, we, be, w1, w3, w2, g_final):
    bsz, seq, d = x.shape
    n_ctx = ctx.shape[1]
    t = bsz * seq
    assert d == D_MODEL and seq % ATT_TQ == 0 and seq // ATT_TQ > K_ROW_BLOCKS and n_ctx % ATT_TQ == 0
    assert bsz + 1 <= MOD_ROWS and seq % PROJ_TM == 0 and seq % MERGE_TM == 0
    assert t % ROUTE_TB == 0 and t % COMBINE_TM == 0 and t % DISPATCH_TM == 0
    assert (bsz * n_ctx) % PROJ_TM == 0 and n_ctx <= seq
    assert seq % (SUBLANES * SUBLANES) == 0 and n_ctx % (SUBLANES * SUBLANES) == 0
    assert (seq // SUBLANES) % SCAN_STEPS == 0 and (n_ctx // SUBLANES) % SCAN_STEPS == 0

    cc = jnp.concatenate([c, c_ctx[None, :], jnp.zeros((MOD_ROWS - bsz - 1, d), F32)], axis=0)
    mod = _mod_call(cc, w_mod, b_mod)
    mod3 = mod[:bsz].reshape(bsz, 1, 6 * d)
    mod_c = mod[bsz:bsz + 1]

    x2 = x.reshape(t, d)
    g_mix2 = g_mix.reshape(1, d)
    w_in_bf = w_in.astype(BF16)
    kc, vc, lxc = _ctxproj_call(ctx.reshape(bsz * n_ctx, d), mod_c, g_mix2, w_in_bf[:, :CTX_COLS])
    cos_t, sin_t = _rope_tables(seq)
    qre, qro, qpe, qpo, k, v, lx, glu, ga, gb = _inproj_call(x2, mod3, g_mix2, w_in_bf, cos_t, sin_t, seq)

    bias = _bias_tables(_rpbcol_call(rpb))
    o_att = _attn_call(qre, qro, qpe, qpo, k, v, kc, vc, bias, bsz, seq, n_ctx)

    hs = _lru_call(lx.reshape(bsz, seq, LRU_WIDTH), lxc.reshape(bsz, n_ctx, LRU_WIDTH),
                   conv_w, conv_b.reshape(1, LRU_WIDTH), lru_wa, lru_wx, lru_ba, lru_bx, lru_lambda)

    wr_t = jnp.concatenate([wg.T, we.T, jnp.zeros((ROUTE_ROWS - N_GROUPS - N_EXPERTS, d), F32)], axis=0)
    br = jnp.concatenate([bg, be, jnp.zeros((ROUTE_ROWS - N_GROUPS - N_EXPERTS,), F32)])
    br = jnp.broadcast_to(br[:, None], (ROUTE_ROWS, LANES))
    x1, h2, logits_t = _merge_call(x2, o_att, hs.reshape(t, LRU_WIDTH), glu, ga, gb, mod3,
                                   g_ffn.reshape(1, d), w_up_attn.astype(BF16), w_up_lru.astype(BF16),
                                   w_out.astype(BF16), wr_t, br, seq)

    eid, gate, rank, cnt = _route_call(logits_t)
    n_blk = -(-(t * TOP_K + N_EXPERTS * (MOE_BLK - 1)) // MOE_BLK)
    n_blk = -(-n_blk // EXPERT_BLKS_PER_STEP) * EXPERT_BLKS_PER_STEP
    nb_pad = -(-n_blk // LANES) * LANES
    dest, blk = _dest_call(cnt, eid, rank, nb_pad)
    dest_flat = dest.reshape(TOP_K * t)
    xs = _dispatch_call(dest_flat, blk[2, :N_EXPERTS], blk[3, :N_EXPERTS], h2, n_blk * MOE_BLK)
    y = _experts_call(blk[0, :n_blk], blk[1, :1], xs, w1, w3, w2)
    return _combine_call(dest_flat, x1, gate, mod3, g_final.reshape(1, d), y, seq).reshape(bsz, seq, d)


def kernel(x, c, ctx, c_ctx, w_mod, b_mod, g_mix, g_ffn, w_in, rpb, conv_w, conv_b, lru_wa, lru_ba, lru_wx,
           lru_bx, lru_lambda, w_up_attn, w_up_lru, w_out, router_group_w, router_group_b, router_expert_w,
           router_expert_b, expert_w_gate, expert_w_up, expert_w_down, g_final):
    assert w_mod.shape[0] == 1, "single-layer block"
    return _layer(x, c, ctx, c_ctx, w_mod[0], b_mod[0], g_mix[0], g_ffn[0], w_in[0], rpb[0], conv_w[0],
                  conv_b[0], lru_wa[0], lru_ba[0], lru_wx[0], lru_bx[0], lru_lambda[0], w_up_attn[0],
                  w_up_lru[0], w_out[0], router_group_w[0], router_group_b[0], router_expert_w[0],
                  router_expert_b[0], expert_w_gate[0], expert_w_up[0], expert_w_down[0], g_final)
```

```python
import functools

import jax
import jax.numpy as jnp
from jax import lax
from jax.experimental import pallas as pl
from jax.experimental.pallas import tpu as pltpu

F32 = jnp.float32
BF16 = jnp.bfloat16
I32 = jnp.int32

D_MODEL = 1024
GRID_W = 64
EPS = 1e-6
NEG_INF = -1e30

NA_HEADS = 8
NA_HEAD_DIM = 64
NA_WIDTH = NA_HEADS * NA_HEAD_DIM
NA_WIN_ROWS = 8
NA_WIN_COLS = 16
ROPE_THETA = 10000.0

LRU_WIDTH = D_MODEL
LRU_BLOCKS = 8
LRU_BLOCK = LRU_WIDTH // LRU_BLOCKS
LRU_CONV = 4
LRU_C = 8.0

N_GROUPS = 4
EXPERTS_PER_GROUP = 8
N_EXPERTS = N_GROUPS * EXPERTS_PER_GROUP
TOP_K = 2
D_EXPERT = 512

K_OFF = 0
V_OFF = K_OFF + NA_WIDTH
LX_OFF = V_OFF + NA_WIDTH
CTX_COLS = LX_OFF + LRU_WIDTH
Q_OFF = CTX_COLS
LG_OFF = Q_OFF + NA_WIDTH
GA_OFF = LG_OFF + LRU_WIDTH
GB_OFF = GA_OFF + D_MODEL
PROJ_COLS = GB_OFF + D_MODEL

LANES = 128
SUBLANES = 8

Q_ROWS = 4
K_ROW_BLOCKS = 3
ATT_TQ = Q_ROWS * GRID_W
ATT_TK = K_ROW_BLOCKS * ATT_TQ

MOE_BLK_LOG2 = 9
MOE_BLK = 1 << MOE_BLK_LOG2
MOD_ROWS = 24
ROUTE_ROWS = 64

PROJ_TM = 512
MERGE_TM = 512
ROUTE_TB = 2048
DISPATCH_TM = 2048
COMBINE_TM = 512

VMEM_LIMIT = 56 * 1024 * 1024


def _cparams(sem, vmem=VMEM_LIMIT):
    return pltpu.CompilerParams(dimension_semantics=sem, vmem_limit_bytes=vmem)


def _dot(a, b):
    return jnp.dot(a, b, preferred_element_type=F32)


def _dot_nt(a, b):
    return lax.dot_general(a, b, (((1,), (1,)), ((), ())), preferred_element_type=F32)


def _split2(a):
    hi = a.astype(BF16)
    lo = (a - hi.astype(F32)).astype(BF16)
    return hi, lo


def _dot3(a, b):
    ah, al = _split2(a)
    bh, bl = _split2(b)
    return _dot(ah, bh) + (_dot(ah, bl) + _dot(al, bh))


def _dot3_nt(a, b):
    ah, al = _split2(a)
    bh, bl = _split2(b)
    return _dot_nt(ah, bh) + (_dot_nt(ah, bl) + _dot_nt(al, bh))


def _sigmoid(x):
    return 1.0 / (1.0 + jnp.exp(-x))


def _ada_norm(x, g, sc, sh):
    ms = jnp.mean(x * x, axis=-1, keepdims=True)
    return (x * lax.rsqrt(ms + EPS) * g) * (1.0 + sc) + sh


def _mod_body(cc_ref, w_ref, b_ref, o_ref):
    cc = cc_ref[...]
    o_ref[...] = _dot3(cc * _sigmoid(cc), w_ref[...]) + b_ref[...]


def _mod_call(cc, w_mod, b_mod):
    n = w_mod.shape[1]
    bn = 1024
    return pl.pallas_call(
        _mod_body,
        grid=(n // bn,),
        in_specs=[pl.BlockSpec((MOD_ROWS, D_MODEL), lambda j: (0, 0)),
                  pl.BlockSpec((D_MODEL, bn), lambda j: (0, j)),
                  pl.BlockSpec((1, bn), lambda j: (0, j))],
        out_specs=pl.BlockSpec((MOD_ROWS, bn), lambda j: (0, j)),
        out_shape=jax.ShapeDtypeStruct((MOD_ROWS, n), F32),
        compiler_params=_cparams(("arbitrary",)),
        name="mod",
    )(cc, w_mod, b_mod.reshape(1, n))


def _rope(t, cos, sin):
    lane = lax.broadcasted_iota(I32, (t.shape[0], LANES), 1)
    first = (lane & 16) == 0
    parts = []
    for c in range(t.shape[1] // LANES):
        tc = t[:, c * LANES:(c + 1) * LANES]
        parts.append(jnp.where(first, pltpu.roll(tc, LANES - 16, 1), pltpu.roll(tc, 16, 1)))
    partner = jnp.concatenate(parts, axis=1)
    return t * cos + partner * sin


def _gelu_tanh(x):
    return 0.5 * x * (1.0 + jnp.tanh(0.7978845608028654 * (x + 0.044715 * (x * x * x))))


def _inproj_body(x_ref, mod_ref, g_ref, w_ref, cos_ref, sin_ref,
                 qre_ref, qro_ref, qpe_ref, qpo_ref, k_ref, v_ref, lx_ref, glu_ref, ga_ref, gb_ref):
    sh = mod_ref[0, :, 0:D_MODEL]
    sc = mod_ref[0, :, D_MODEL:2 * D_MODEL]
    h = _ada_norm(x_ref[...], g_ref[...], sc, sh).astype(BF16)
    cos = cos_ref[...]
    sin = sin_ref[...]
    scale = NA_HEAD_DIM ** -0.5

    k_ref[...] = _rope(_dot(h, w_ref[:, K_OFF:K_OFF + NA_WIDTH]), cos, sin).astype(BF16)
    v_ref[...] = _dot(h, w_ref[:, V_OFF:V_OFF + NA_WIDTH]).astype(BF16)
    lx_ref[...] = _dot(h, w_ref[:, LX_OFF:LX_OFF + LRU_WIDTH])

    q = _dot(h, w_ref[:, Q_OFF:Q_OFF + NA_WIDTH]) * scale
    qr = _rope(q, cos, sin)
    lane = lax.broadcasted_iota(I32, q.shape, 1)
    even = (lane & NA_HEAD_DIM) == 0
    qre_ref[...] = jnp.where(even, qr, 0.0).astype(BF16)
    qro_ref[...] = jnp.where(even, 0.0, qr).astype(BF16)
    qpe_ref[...] = jnp.where(even, q, 0.0).astype(BF16)
    qpo_ref[...] = jnp.where(even, 0.0, q).astype(BF16)

    glu_ref[...] = _gelu_tanh(_dot(h, w_ref[:, LG_OFF:LG_OFF + LRU_WIDTH])).astype(BF16)
    ga_ref[...] = _sigmoid(_dot(h, w_ref[:, GA_OFF:GA_OFF + D_MODEL])).astype(BF16)
    gb_ref[...] = _sigmoid(_dot(h, w_ref[:, GB_OFF:GB_OFF + D_MODEL])).astype(BF16)


def _inproj_call(x2, mod3, g_mix, w_in_bf, cos_t, sin_t, seq):
    t = x2.shape[0]
    tm = PROJ_TM
    per_b = seq // tm
    row = lambda i: (i, 0)
    wide = lambda n, dt: jax.ShapeDtypeStruct((t, n), dt)
    return pl.pallas_call(
        _inproj_body,
        grid=(t // tm,),
        in_specs=[pl.BlockSpec((tm, D_MODEL), row),
                  pl.BlockSpec((1, 1, 6 * D_MODEL), lambda i: (i // per_b, 0, 0)),
                  pl.BlockSpec((1, D_MODEL), lambda i: (0, 0)),
                  pl.BlockSpec((D_MODEL, PROJ_COLS), lambda i: (0, 0), pipeline_mode=pl.Buffered(1)),
                  pl.BlockSpec((tm, NA_WIDTH), lambda i: (i % per_b, 0)),
                  pl.BlockSpec((tm, NA_WIDTH), lambda i: (i % per_b, 0))],
        out_specs=[pl.BlockSpec((tm, NA_WIDTH), row)] * 6
                  + [pl.BlockSpec((tm, LRU_WIDTH), row)] * 4,
        out_shape=[wide(NA_WIDTH, BF16)] * 6
                  + [wide(LRU_WIDTH, F32), wide(LRU_WIDTH, BF16), wide(D_MODEL, BF16), wide(D_MODEL, BF16)],
        compiler_params=_cparams(("parallel",)),
        name="inproj",
    )(x2, mod3, g_mix, w_in_bf, cos_t, sin_t)


def _ctxproj_body(x_ref, mod_ref, g_ref, w_ref, k_ref, v_ref, lx_ref):
    sh = mod_ref[:, 0:D_MODEL]
    sc = mod_ref[:, D_MODEL:2 * D_MODEL]
    h = _ada_norm(x_ref[...], g_ref[...], sc, sh).astype(BF16)
    k_ref[...] = _dot(h, w_ref[:, K_OFF:K_OFF + NA_WIDTH]).astype(BF16)
    v_ref[...] = _dot(h, w_ref[:, V_OFF:V_OFF + NA_WIDTH]).astype(BF16)
    lx_ref[...] = _dot(h, w_ref[:, LX_OFF:LX_OFF + LRU_WIDTH])


def _ctxproj_call(c2, mod_c, g_mix, w_ctx_bf):
    t = c2.shape[0]
    tm = PROJ_TM
    row = lambda i: (i, 0)
    return pl.pallas_call(
        _ctxproj_body,
        grid=(t // tm,),
        in_specs=[pl.BlockSpec((tm, D_MODEL), row),
                  pl.BlockSpec((1, 6 * D_MODEL), lambda i: (0, 0)),
                  pl.BlockSpec((1, D_MODEL), lambda i: (0, 0)),
                  pl.BlockSpec((D_MODEL, CTX_COLS), lambda i: (0, 0))],
        out_specs=[pl.BlockSpec((tm, NA_WIDTH), row), pl.BlockSpec((tm, NA_WIDTH), row),
                   pl.BlockSpec((tm, LRU_WIDTH), row)],
        out_shape=[jax.ShapeDtypeStruct((t, NA_WIDTH), BF16), jax.ShapeDtypeStruct((t, NA_WIDTH), BF16),
                   jax.ShapeDtypeStruct((t, LRU_WIDTH), F32)],
        compiler_params=_cparams(("parallel",)),
        name="ctxproj",
    )(c2, mod_c, g_mix, w_ctx_bf)


N_DR = 2 * NA_WIN_ROWS - 1
N_DC = 2 * NA_WIN_COLS - 1


def _rpbcol_body(rpb_ref, o_ref):
    n = GRID_W * GRID_W
    flat = lax.broadcasted_iota(I32, (32, n), 1)
    qc = flat >> 6
    kc = flat & (GRID_W - 1)
    dc = jnp.clip(kc - qc, 1 - NA_WIN_COLS, NA_WIN_COLS - 1) + (NA_WIN_COLS - 1)
    d_iota = lax.broadcasted_iota(I32, (32, n), 0)
    onehot = jnp.where(dc == d_iota, 1.0, 0.0).astype(BF16)
    r = rpb_ref[...]
    r1 = r.astype(BF16)
    rem = r - r1.astype(F32)
    r2 = rem.astype(BF16)
    r3 = (rem - r2.astype(F32)).astype(BF16)
    val = _dot(r1, onehot) + (_dot(r2, onehot) + _dot(r3, onehot))
    qc1 = qc[0:1, :]
    kc1 = kc[0:1, :]
    c_start = jnp.clip(qc1 - NA_WIN_COLS // 2, 0, GRID_W - NA_WIN_COLS)
    band = (kc1 >= c_start) & (kc1 < c_start + NA_WIN_COLS)
    o_ref[...] = jnp.where(band, val, NEG_INF)


def _rpbcol_call(rpb):
    rows = NA_HEADS * N_DR
    r2 = jnp.pad(rpb.reshape(rows, N_DC), ((0, 0), (0, 32 - N_DC)))
    n = GRID_W * GRID_W
    return pl.pallas_call(
        _rpbcol_body,
        in_specs=[pl.BlockSpec((rows, 32), lambda: (0, 0))],
        out_specs=pl.BlockSpec((rows, n), lambda: (0, 0)),
        out_shape=jax.ShapeDtypeStruct((rows, n), F32),
        name="rpbcol",
    )(r2)


def _bias_tables(rpbcol):
    t = rpbcol.reshape(NA_HEADS, N_DR, GRID_W, GRID_W)
    neg = jnp.full((NA_HEADS, GRID_W, GRID_W), NEG_INF, F32)
    n_kj = K_ROW_BLOCKS * Q_ROWS
    classes = []
    for lo_fn, dr_off in ((lambda ri: 0, 7), (lambda ri: ri, 3), (lambda ri: 4, -1)):
        rows = []
        for ri in range(Q_ROWS):
            lo = lo_fn(ri)
            blocks = []
            for kj in range(n_kj):
                inside = lo <= kj < lo + NA_WIN_ROWS
                blocks.append(t[:, kj - ri + dr_off] if inside else neg)
            rows.append(jnp.concatenate(blocks, axis=2))
        classes.append(jnp.concatenate(rows, axis=1))
    return jnp.stack(classes, axis=0)


def _attn_body(qre_ref, qro_ref, qpe_ref, qpo_ref, k0_ref, k1_ref, k2_ref, v0_ref, v1_ref, v2_ref,
               kc_ref, vc_ref, bias_ref, o_ref):
    lane = lax.broadcasted_iota(I32, (ATT_TQ, LANES), 1)
    lane1 = lax.broadcasted_iota(I32, (1, LANES), 1)
    head_lanes = [jnp.where(lane1 < NA_HEAD_DIM, 1.0, 0.0).astype(BF16),
                  jnp.where(lane1 < NA_HEAD_DIM, 0.0, 1.0).astype(BF16)]
    k_refs = (k0_ref, k1_ref, k2_ref)
    v_refs = (v0_ref, v1_ref, v2_ref)
    for bi, p in [(bi, p) for bi in range(ATT_BATCH) for p in range(NA_HEADS // 2)]:
        sl = slice(p * LANES, (p + 1) * LANES)
        k_lat = jnp.concatenate([r[bi, :, sl] for r in k_refs], axis=0)
        kc = kc_ref[bi, :, sl]
        v_all = jnp.concatenate([r[bi, :, sl] for r in v_refs] + [vc_ref[bi, :, sl]], axis=0)
        outs = []
        for hh, (qr_ref, qp_ref) in enumerate(((qre_ref, qpe_ref), (qro_ref, qpo_ref))):
            h = 2 * p + hh
            s_lat = _dot_nt(qr_ref[bi, :, sl], k_lat) + bias_ref[0, h]
            s_ctx = _dot_nt(qp_ref[bi, :, sl], kc)
            tiles = ([s_lat[:, j * ATT_TQ:(j + 1) * ATT_TQ] for j in range(K_ROW_BLOCKS)]
                     + [s_ctx[:, j * ATT_TQ:(j + 1) * ATT_TQ] for j in range(s_ctx.shape[1] // ATT_TQ)])
            m = functools.reduce(jnp.maximum, tiles).max(axis=-1, keepdims=True)
            prob = jnp.exp(jnp.concatenate([s_lat, s_ctx], axis=1) - m).astype(BF16)
            mine = head_lanes[hh]
            acc = _dot(prob, v_all * mine + (1.0 - mine).astype(BF16))
            outs.append(acc / pltpu.roll(acc, NA_HEAD_DIM, 1))
        o_ref[bi, :, sl] = jnp.where(lane < NA_HEAD_DIM, outs[0], outs[1]).astype(BF16)


ATT_BATCH = 4


def _attn_call(qre, qro, qpe, qpo, k, v, kc, vc, bias, bsz, seq, n_ctx):
    n_grp = seq // ATT_TQ
    max_kb = n_grp - K_ROW_BLOCKS
    assert bsz % ATT_BATCH == 0
    by_batch = lambda a, n: a.reshape(bsz, n, NA_WIDTH)

    def qmap(g, b):
        return (b, g, 0)

    def kmap(j):
        return lambda g, b: (b, jnp.clip(g - 1, 0, max_kb) + j, 0)

    def cls(g, b):
        return (jnp.where(g == 0, 0, jnp.where(g == n_grp - 1, 2, 1)), 0, 0, 0)

    qspec = pl.BlockSpec((ATT_BATCH, ATT_TQ, NA_WIDTH), qmap)
    cspec = pl.BlockSpec((ATT_BATCH, n_ctx, NA_WIDTH), lambda g, b: (b, 0, 0))
    k3, v3 = by_batch(k, seq), by_batch(v, seq)
    return pl.pallas_call(
        _attn_body,
        grid=(n_grp, bsz // ATT_BATCH),
        in_specs=[qspec] * 4
                 + [pl.BlockSpec((ATT_BATCH, ATT_TQ, NA_WIDTH), kmap(j)) for j in range(K_ROW_BLOCKS)] * 2
                 + [cspec, cspec, pl.BlockSpec((1, NA_HEADS, ATT_TQ, ATT_TK), cls)],
        out_specs=qspec,
        out_shape=jax.ShapeDtypeStruct((bsz, seq, NA_WIDTH), BF16),
        compiler_params=_cparams(("arbitrary", "arbitrary")),
        name="attn",
    )(by_batch(qre, seq), by_batch(qro, seq), by_batch(qpe, seq), by_batch(qpo, seq), k3, k3, k3, v3, v3, v3,
      by_batch(kc, n_ctx), by_batch(vc, n_ctx), bias).reshape(bsz * seq, NA_WIDTH)


def _shift_down(v, row):
    return jnp.where(row >= 1, pltpu.roll(v, 1, 0), 0.0)


def _shift_up(v, row):
    return jnp.where(row < SUBLANES - 1, pltpu.roll(v, SUBLANES - 1, 0), 0.0)


def _conv4(x, w, b):
    n = x.shape[0]
    s = SUBLANES
    row = lax.broadcasted_iota(I32, (s, LANES), 0)
    last = _shift_down(x[n - s:n], row)
    last2 = _shift_down(x[n - 2 * s:n - s], row)
    first = _shift_up(x[0:s], row)
    xm1 = jnp.concatenate([last, x[0:n - s]], axis=0)
    xm2 = jnp.concatenate([last2, last, x[0:n - 2 * s]], axis=0)
    xp1 = jnp.concatenate([x[s:n], first], axis=0)
    return (w[0:1, :] * xm2 + w[1:2, :] * xm1 + w[2:3, :] * x + w[3:4, :] * xp1) + b


def _softplus(z):
    return jnp.maximum(z, 0.0) + jnp.log1p(jnp.exp(-jnp.abs(z)))


def _gates(xb, xh, wa, wx, ba, bx, lam, a_ref, u_ref):
    n = xb.shape[0]
    tr = jnp.tanh(_dot(xb, (0.5 * wa).astype(BF16)) + 0.5 * ba)
    ti = jnp.tanh(_dot(xb, (0.5 * wx).astype(BF16)) + 0.5 * bx)
    half_c = (0.5 * LRU_C) * _softplus(-lam)
    neg_log_a = half_c * tr + half_c
    a = jnp.exp(-neg_log_a)
    a_ref[0:n, :] = a
    s2 = jnp.tanh(neg_log_a) * (a * a + 1.0)
    root = jnp.where(s2 > 0.0, s2 * lax.rsqrt(s2), 0.0)
    u_ref[0:n, :] = root * (xh * ti + xh)


def _scan4(a, u, h, p):
    a01 = a[1] * a[0]
    u01 = a[1] * u[0] + u[1]
    a23 = a[3] * a[2]
    u23 = a[3] * u[2] + u[3]
    a012 = a[2] * a01
    u012 = a[2] * u01 + u[2]
    a0123 = a23 * a01
    u0123 = a23 * u01 + u23
    hs = [a[0] * h + u[0], a01 * h + u01, a012 * h + u012, a0123 * h + u0123]
    ps = [a[0] * p, a01 * p, a012 * p, a0123 * p]
    return hs, ps


SCAN_STEPS = 4


def _scan_local(af_ref, uf_ref, ab_ref, ub_ref, n_vreg):
    s = SUBLANES
    zero = jnp.zeros((s, LANES), F32)
    one = jnp.ones((s, LANES), F32)
    span = SCAN_STEPS * s

    def body(q, carry):
        hf, pf, hb, pb = carry
        base = pl.multiple_of(q * span, span)
        rows = [pl.ds(base + i * s, s) for i in range(SCAN_STEPS)]
        hs, ps = _scan4([af_ref[r, :] for r in rows], [uf_ref[r, :] for r in rows], hf, pf)
        for r, h, p in zip(rows, hs, ps):
            uf_ref[r, :] = h
            af_ref[r, :] = p
        hf, pf = hs[-1], ps[-1]
        base = pl.multiple_of((n_vreg - SCAN_STEPS) * s - q * span, span)
        rows = [pl.ds(base + (SCAN_STEPS - 1 - i) * s, s) for i in range(SCAN_STEPS)]
        hs, ps = _scan4([ab_ref[r, :] for r in rows], [ub_ref[r, :] for r in rows], hb, pb)
        for r, h, p in zip(rows, hs, ps):
            ub_ref[r, :] = h
            ab_ref[r, :] = p
        return hf, pf, hs[-1], ps[-1]

    return lax.fori_loop(0, n_vreg // SCAN_STEPS, body, (zero, one, zero, one), unroll=2)


def _link_states(hf, pf, hb, pb, h0f, h0b):
    s = SUBLANES
    row = lax.broadcasted_iota(I32, (s, LANES), 0)
    a, u = pf, hf
    for k in (1, 2, 4):
        keep = row >= k
        u = u + a * jnp.where(keep, pltpu.roll(u, k, 0), 0.0)
        a = a * jnp.where(keep, pltpu.roll(a, k, 0), 1.0)
    end_f = u + a * h0f
    in_f = jnp.where(row >= 1, pltpu.roll(end_f, 1, 0), h0f)
    a, u = pb, hb
    for k in (1, 2, 4):
        keep = row < s - k
        u = u + a * jnp.where(keep, pltpu.roll(u, s - k, 0), 0.0)
        a = a * jnp.where(keep, pltpu.roll(a, s - k, 0), 1.0)
    end_b = u + a * h0b
    in_b = jnp.where(row < s - 1, pltpu.roll(end_b, s - 1, 0), h0b)
    return in_f, in_b, end_f[s - 1:s, :], end_b[0:1, :]


PITCH_PAD = 4


def _to_split(x_ref, lanes, pad_ref, dst_ref, n):
    s = SUBLANES
    n_j = n // s
    pitch = n_j + PITCH_PAD
    for q in range(s):
        pad_ref[pl.ds(q * pitch, n_j), :] = x_ref[0, pl.ds(q * n_j, n_j), lanes]

    def body(j, c):
        dst_ref[pl.ds(pl.multiple_of(j * s, s), s), :] = pad_ref[pl.ds(j, s, stride=pitch), :]
        return c

    lax.fori_loop(0, n_j, body, 0, unroll=8)


LRU_BLOCKS_PER_STEP = 4


def _lru_body(lx_ref, lxc_ref, cw_ref, cb_ref, wa_ref, wx_ref, ba_ref, bx_ref, lam_ref,
              o_ref, af_ref, uf_ref, ab_ref, ub_ref, pad_ref):
    n = lx_ref.shape[1]
    n_c = lxc_ref.shape[1]
    s = SUBLANES
    n_j = n // s
    pitch = n_j + PITCH_PAD
    zero = jnp.zeros((1, LANES), F32)

    for blk in range(LRU_BLOCKS_PER_STEP):
        lanes = slice(blk * LRU_BLOCK, (blk + 1) * LRU_BLOCK)
        conv_w = cw_ref[:, lanes]
        conv_b = cb_ref[:, lanes]
        gate = [(wa_ref[d, blk], wx_ref[d, blk], ba_ref[d:d + 1, lanes], bx_ref[d:d + 1, lanes],
                 lam_ref[d:d + 1, lanes]) for d in range(2)]

        _to_split(lxc_ref, lanes, pad_ref, uf_ref, n_c)
        xc = _conv4(uf_ref[0:n_c, :], conv_w, conv_b)
        xb, xh = xc.astype(BF16), 0.5 * xc
        _gates(xb, xh, *gate[0], af_ref, uf_ref)
        _gates(xb, xh, *gate[1], ab_ref, ub_ref)
        ends = _scan_local(af_ref, uf_ref, ab_ref, ub_ref, n_c // s)
        _, _, cf, cb = _link_states(*ends, zero, zero)

        _to_split(lx_ref, lanes, pad_ref, uf_ref, n)
        xl = _conv4(uf_ref[...], conv_w, conv_b)
        xb, xh = xl.astype(BF16), 0.5 * xl
        _gates(xb, xh, *gate[0], af_ref, uf_ref)
        _gates(xb, xh, *gate[1], ab_ref, ub_ref)
        ends = _scan_local(af_ref, uf_ref, ab_ref, ub_ref, n_j)
        in_f, in_b, _, _ = _link_states(*ends, cf, cb)

        def finish(j, c):
            rows = pl.ds(pl.multiple_of(j * s, s), s)
            h = (uf_ref[rows, :] + af_ref[rows, :] * in_f) + (ub_ref[rows, :] + ab_ref[rows, :] * in_b)
            pad_ref[pl.ds(j, s, stride=pitch), :] = h
            return c

        lax.fori_loop(0, n_j, finish, 0, unroll=8)
        for q in range(s):
            o_ref[0, pl.ds(q * n_j, n_j), lanes] = pad_ref[pl.ds(q * pitch, n_j), :].astype(o_ref.dtype)


def _lru_call(lx3, lxc3, conv_w, conv_b, wa, wx, ba, bx, lam):
    bsz, seq, _ = lx3.shape
    n_ctx = lxc3.shape[1]
    per = LRU_BLOCKS_PER_STEP
    width = per * LRU_BLOCK
    col = lambda b, n: (b, 0, n)
    par = lambda b, n: (0, n)
    wspec = pl.BlockSpec((2, per, LRU_BLOCK, LRU_BLOCK), lambda b, n: (0, n, 0, 0))
    return pl.pallas_call(
        _lru_body,
        grid=(bsz, LRU_BLOCKS // per),
        in_specs=[pl.BlockSpec((1, seq, width), col),
                  pl.BlockSpec((1, n_ctx, width), col),
                  pl.BlockSpec((LRU_CONV, width), par),
                  pl.BlockSpec((1, width), par),
                  wspec, wspec,
                  pl.BlockSpec((2, width), par),
                  pl.BlockSpec((2, width), par),
                  pl.BlockSpec((2, width), par)],
        out_specs=pl.BlockSpec((1, seq, width), col),
        out_shape=jax.ShapeDtypeStruct((bsz, seq, LRU_WIDTH), BF16),
        scratch_shapes=[pltpu.VMEM((seq, LRU_BLOCK), F32)] * 4
                       + [pltpu.VMEM((seq + SUBLANES * PITCH_PAD, LRU_BLOCK), F32)],
        compiler_params=_cparams(("parallel", "arbitrary")),
        name="lru",
    )(lx3, lxc3, conv_w, conv_b, wa, wx, ba, bx, lam)


TOKEN_TILE = D_MODEL // LANES


def _store_token_tiles(ref, row0, x):
    m = x.shape[0]
    for c in range(TOKEN_TILE):
        ref[pl.ds(row0 + c, m, stride=TOKEN_TILE), :] = x[:, c * LANES:(c + 1) * LANES]


def _load_token_tiles(ref, row0, m):
    return jnp.concatenate([ref[pl.ds(row0 + c, m, stride=TOKEN_TILE), :] for c in range(TOKEN_TILE)], axis=1)


def _merge_body(x_ref, oa_ref, hs_ref, glu_ref, ga_ref, gb_ref, mod_ref, g_ref, wua_ref, wul_ref, wo_ref,
                wr_ref, br_ref, x1_ref, h2_ref, lt_ref):
    ga1 = mod_ref[0, :, 2 * D_MODEL:3 * D_MODEL]
    sh2 = mod_ref[0, :, 3 * D_MODEL:4 * D_MODEL]
    sc2 = mod_ref[0, :, 4 * D_MODEL:5 * D_MODEL]
    o_lru = hs_ref[...] * glu_ref[...]
    y = (ga_ref[...].astype(F32) * _dot(oa_ref[...], wua_ref[...])
         + gb_ref[...].astype(F32) * _dot(o_lru, wul_ref[...]))
    x1 = x_ref[...] + ga1 * _dot(y.astype(BF16), wo_ref[...])
    x1_ref[...] = x1
    h2 = _ada_norm(x1, g_ref[...], sc2, sh2)
    lt_ref[...] = _dot3_nt(wr_ref[...], h2) + br_ref[:, 0:1]
    _store_token_tiles(h2_ref, 0, h2)


def _merge_call(x2, o_att, hs, glu, ga, gb, mod3, g_ffn, wua, wul, wo, wr_t, br, seq):
    t = x2.shape[0]
    tm = MERGE_TM
    per_b = seq // tm
    row = lambda i: (i, 0)
    full = lambda i: (0, 0)
    resident = lambda shape: pl.BlockSpec(shape, full, pipeline_mode=pl.Buffered(1))
    return pl.pallas_call(
        _merge_body,
        grid=(t // tm,),
        in_specs=[pl.BlockSpec((tm, D_MODEL), row),
                  pl.BlockSpec((tm, NA_WIDTH), row),
                  pl.BlockSpec((tm, LRU_WIDTH), row),
                  pl.BlockSpec((tm, LRU_WIDTH), row),
                  pl.BlockSpec((tm, D_MODEL), row),
                  pl.BlockSpec((tm, D_MODEL), row),
                  pl.BlockSpec((1, 1, 6 * D_MODEL), lambda i: (i // per_b, 0, 0)),
                  pl.BlockSpec((1, D_MODEL), full),
                  resident((NA_WIDTH, D_MODEL)),
                  resident((LRU_WIDTH, D_MODEL)),
                  resident((D_MODEL, D_MODEL)),
                  pl.BlockSpec((ROUTE_ROWS, D_MODEL), full),
                  pl.BlockSpec((ROUTE_ROWS, LANES), full)],
        out_specs=[pl.BlockSpec((tm, D_MODEL), row),
                   pl.BlockSpec((tm * TOKEN_TILE, LANES), row),
                   pl.BlockSpec((ROUTE_ROWS, tm), lambda i: (0, i))],
        out_shape=[jax.ShapeDtypeStruct((t, D_MODEL), F32),
                   jax.ShapeDtypeStruct((t * TOKEN_TILE, LANES), F32),
                   jax.ShapeDtypeStruct((ROUTE_ROWS, t), F32)],
        compiler_params=_cparams(("parallel",)),
        name="merge",
    )(x2, o_att, hs, glu, ga, gb, mod3, g_ffn, wua, wul, wo, wr_t, br)


def _route_body(lt_ref, eid_ref, gate_ref, rank_ref, cnt_ref, carry_ref):
    step = pl.program_id(0)

    @pl.when(step == 0)
    def _():
        carry_ref[...] = jnp.zeros_like(carry_ref)

    tb = lt_ref.shape[1]
    lg = [lt_ref[r:r + 1, :] for r in range(N_GROUPS)]
    best = lg[0]
    gidx = jnp.zeros((1, tb), I32)
    for r in range(1, N_GROUPS):
        better = lg[r] > best
        gidx = jnp.where(better, r, gidx)
        best = jnp.maximum(best, lg[r])
    den = jnp.exp(lg[0] - best)
    for r in range(1, N_GROUPS):
        den = den + jnp.exp(lg[r] - best)
    p_top = 1.0 / den

    ev = []
    for j in range(EXPERTS_PER_GROUP):
        sel = lt_ref[N_GROUPS + j:N_GROUPS + j + 1, :]
        for g in range(1, N_GROUPS):
            row = N_GROUPS + g * EXPERTS_PER_GROUP + j
            sel = jnp.where(gidx == g, lt_ref[row:row + 1, :], sel)
        ev.append(sel)
    v0 = ev[0]
    i0 = jnp.zeros((1, tb), I32)
    for j in range(1, EXPERTS_PER_GROUP):
        better = ev[j] > v0
        i0 = jnp.where(better, j, i0)
        v0 = jnp.maximum(v0, ev[j])
    v1 = jnp.full((1, tb), -jnp.inf, F32)
    i1 = jnp.zeros((1, tb), I32)
    for j in range(EXPERTS_PER_GROUP):
        better = (ev[j] > v1) & (i0 != j)
        i1 = jnp.where(better, j, i1)
        v1 = jnp.where(better, ev[j], v1)
    e1 = jnp.exp(v1 - v0)
    inv = 1.0 / (1.0 + e1)
    eid0 = gidx * EXPERTS_PER_GROUP + i0
    eid1 = gidx * EXPERTS_PER_GROUP + i1
    eid_ref[0:1, :] = eid0
    eid_ref[1:2, :] = eid1
    gate_ref[...] = jnp.zeros_like(gate_ref)
    gate_ref[0:1, :] = p_top * inv
    gate_ref[1:2, :] = p_top * (e1 * inv)

    sub = 256
    e_iota = lax.broadcasted_iota(I32, (N_EXPERTS, sub), 0)
    tri = jnp.where(lax.broadcasted_iota(I32, (sub, sub), 0) <= lax.broadcasted_iota(I32, (sub, sub), 1),
                    1.0, 0.0).astype(BF16)
    carry = carry_ref[...]
    for c in range(tb // sub):
        sl = slice(c * sub, (c + 1) * sub)
        m0 = eid0[:, sl] == e_iota
        m1 = eid1[:, sl] == e_iota
        oh = jnp.where(m0 | m1, 1.0, 0.0)
        incl = _dot(oh.astype(BF16), tri)
        excl = incl - oh + carry[:, 0:1]
        rank_ref[0:1, sl] = jnp.sum(jnp.where(m0, excl, 0.0), axis=0, keepdims=True).astype(I32)
        rank_ref[1:2, sl] = jnp.sum(jnp.where(m1, excl, 0.0), axis=0, keepdims=True).astype(I32)
        carry = carry + incl[:, sub - 1:sub]
    carry_ref[...] = carry
    cnt_ref[...] = carry


def _route_call(logits_t):
    t = logits_t.shape[1]
    tb = ROUTE_TB
    col = lambda i: (0, i)
    return pl.pallas_call(
        _route_body,
        grid=(t // tb,),
        in_specs=[pl.BlockSpec((ROUTE_ROWS, tb), col)],
        out_specs=[pl.BlockSpec((TOP_K, tb), col), pl.BlockSpec((SUBLANES, tb), col),
                   pl.BlockSpec((TOP_K, tb), col), pl.BlockSpec((N_EXPERTS, LANES), lambda i: (0, 0))],
        out_shape=[jax.ShapeDtypeStruct((TOP_K, t), I32), jax.ShapeDtypeStruct((SUBLANES, t), F32),
                   jax.ShapeDtypeStruct((TOP_K, t), I32), jax.ShapeDtypeStruct((N_EXPERTS, LANES), F32)],
        scratch_shapes=[pltpu.VMEM((N_EXPERTS, LANES), F32)],
        compiler_params=_cparams(("arbitrary",)),
        name="route",
    )(logits_t)


def _dest_body(cnt_ref, eid_ref, rank_ref, dest_ref, blk_ref):
    cnt = cnt_ref[...].astype(I32)
    padded = ((cnt + (MOE_BLK - 1)) >> MOE_BLK_LOG2) << MOE_BLK_LOG2
    e_iota = lax.broadcasted_iota(I32, (N_EXPERTS, LANES), 0)
    p_end = jnp.zeros((N_EXPERTS, LANES), I32)
    for e in range(N_EXPERTS):
        tot = jnp.sum(jnp.where(e_iota <= e, padded, 0), axis=0, keepdims=True)
        p_end = jnp.where(e_iota == e, tot, p_end)
    p_start = p_end - padded
    tb = eid_ref.shape[1]
    ps = jnp.concatenate([p_start] * (tb // LANES), axis=1)
    e_wide = lax.broadcasted_iota(I32, (N_EXPERTS, tb), 0)
    for k in range(TOP_K):
        start = jnp.sum(jnp.where(eid_ref[k:k + 1, :] == e_wide, ps, 0), axis=0, keepdims=True)
        dest_ref[k:k + 1, :] = start + rank_ref[k:k + 1, :]
    nb = blk_ref.shape[1]
    pe = jnp.concatenate([p_end] * (nb // LANES), axis=1)
    first_row = lax.broadcasted_iota(I32, (N_EXPERTS, nb), 1) * MOE_BLK
    n_before = jnp.sum(jnp.where(pe <= first_row, 1, 0), axis=0, keepdims=True)
    blk = jnp.minimum(n_before, N_EXPERTS - 1)
    blk_ref[...] = jnp.broadcast_to(blk, blk_ref.shape)
    blk_ref[1:2, :] = jnp.broadcast_to(p_end[N_EXPERTS - 1:N_EXPERTS, 0:1] >> MOE_BLK_LOG2, (1, nb))
    on_diag = e_iota == lax.broadcasted_iota(I32, (N_EXPERTS, LANES), 1)
    blk_ref[2:3, 0:LANES] = jnp.sum(jnp.where(on_diag, p_end, 0), axis=0, keepdims=True)
    blk_ref[3:4, 0:LANES] = jnp.sum(jnp.where(on_diag, padded, 0), axis=0, keepdims=True)


def _dest_call(cnt, eid, rank, nb_pad):
    t = eid.shape[1]
    tb = ROUTE_TB
    col = lambda i: (0, i)
    return pl.pallas_call(
        _dest_body,
        grid=(t // tb,),
        in_specs=[pl.BlockSpec((N_EXPERTS, LANES), lambda i: (0, 0)),
                  pl.BlockSpec((TOP_K, tb), col), pl.BlockSpec((TOP_K, tb), col)],
        out_specs=[pl.BlockSpec((TOP_K, tb), col), pl.BlockSpec((SUBLANES, nb_pad), lambda i: (0, 0))],
        out_shape=[jax.ShapeDtypeStruct((TOP_K, t), I32), jax.ShapeDtypeStruct((SUBLANES, nb_pad), I32)],
        compiler_params=_cparams(("arbitrary",)),
        name="dest",
    )(cnt, eid, rank)


def _token_tile(ref, t):
    return ref.at[pl.ds(pl.multiple_of(t * TOKEN_TILE, TOKEN_TILE), TOKEN_TILE)]


def _dispatch_body(dest_ref, pend_ref, plen_ref, h_ref, xs_ref, zero_ref, sem, zsem):
    tm = h_ref.shape[0] // TOKEN_TILE
    blk_rows = MOE_BLK * TOKEN_TILE
    n_tok = pl.num_programs(0) * tm
    base = pl.program_id(0) * tm

    @pl.when(pl.program_id(0) == 0)
    def _():
        zero_ref[...] = jnp.zeros_like(zero_ref)

        def zero_block(start):
            rows = pl.ds(pl.multiple_of(start * TOKEN_TILE, blk_rows), blk_rows)
            return pltpu.make_async_copy(zero_ref, xs_ref.at[rows], zsem)

        def fill(e, c):
            @pl.when(plen_ref[e] > 0)
            def _():
                zero_block(pend_ref[e] - MOE_BLK).start()
            return c

        def drain(e, c):
            @pl.when(plen_ref[e] > 0)
            def _():
                zero_block(pend_ref[e] - MOE_BLK).wait()
            return c

        lax.fori_loop(0, N_EXPERTS, fill, 0)
        n_used = pend_ref[N_EXPERTS - 1] >> MOE_BLK_LOG2
        n_blk = xs_ref.shape[0] // blk_rows
        lax.fori_loop(n_used, n_blk, lambda j, c: (zero_block(j * MOE_BLK).start(), c)[1], 0)
        lax.fori_loop(0, N_EXPERTS, drain, 0)
        lax.fori_loop(n_used, n_blk, lambda j, c: (zero_block(j * MOE_BLK).wait(), c)[1], 0)

    def issue(r, c):
        for k in range(TOP_K):
            d = dest_ref[k * n_tok + base + r]
            pltpu.make_async_copy(_token_tile(h_ref, r), _token_tile(xs_ref, d), sem).start(priority=k)
        return c

    lax.fori_loop(0, tm, issue, 0, unroll=8)
    for k in range(TOP_K):
        pltpu.make_async_copy(h_ref, xs_ref.at[pl.ds(0, tm * TOKEN_TILE)], sem).wait()


def _dispatch_call(dest_flat, p_end, p_len, h2t, n_slots):
    rows = DISPATCH_TM * TOKEN_TILE
    return pl.pallas_call(
        _dispatch_body,
        grid_spec=pltpu.PrefetchScalarGridSpec(
            num_scalar_prefetch=3,
            grid=(h2t.shape[0] // rows,),
            in_specs=[pl.BlockSpec((rows, LANES), lambda i, d, pe, pn: (i, 0))],
            out_specs=pl.BlockSpec(memory_space=pl.ANY),
            scratch_shapes=[pltpu.VMEM((MOE_BLK * TOKEN_TILE, LANES), F32),
                            pltpu.SemaphoreType.DMA(()), pltpu.SemaphoreType.DMA(())]),
        out_shape=jax.ShapeDtypeStruct((n_slots * TOKEN_TILE, LANES), F32),
        compiler_params=_cparams(("arbitrary",)),
        name="dispatch",
    )(dest_flat, p_end, p_len, h2t)


EXPERT_BLKS_PER_STEP = 4


def _experts_body(blk_ref, used_ref, xs_ref, w1_hbm, w3_hbm, w2_hbm, y_ref,
                  f1_ref, f3_ref, f2_ref, b1_ref, b3_ref, b2_ref, loaded_ref, sem):
    i = pl.program_id(0)
    per = EXPERT_BLKS_PER_STEP
    used = used_ref[0]

    def weight_copies(e):
        return [pltpu.make_async_copy(w_hbm.at[e], f_ref, sem)
                for w_hbm, f_ref in ((w1_hbm, f1_ref), (w3_hbm, f3_ref), (w2_hbm, f2_ref))]

    @pl.when(i == 0)
    def _():
        loaded_ref[0] = -1
        for cp in weight_copies(blk_ref[0]):
            cp.start()

    for half in range(per):
        blk = per * i + half
        row0 = half * MOE_BLK * TOKEN_TILE
        e = blk_ref[blk]

        @pl.when((blk < used) & (e != loaded_ref[0]))
        def _():
            for cp in weight_copies(e):
                cp.wait()
            b1_ref[...] = f1_ref[...].astype(BF16)
            b3_ref[...] = f3_ref[...].astype(BF16)
            b2_ref[...] = f2_ref[...].astype(BF16)
            loaded_ref[0] = e
            nxt = lax.while_loop(lambda j: (j < used) & (blk_ref[jnp.minimum(j, blk_ref.shape[0] - 1)] == e),
                                 lambda j: j + 1, blk + 1)

            @pl.when(nxt < used)
            def _():
                for cp in weight_copies(blk_ref[jnp.minimum(nxt, blk_ref.shape[0] - 1)]):
                    cp.start()

        @pl.when(blk < used)
        def _():
            x = _load_token_tiles(xs_ref, row0, MOE_BLK).astype(BF16)
            g = _dot(x, b1_ref[...])
            u = _dot(x, b3_ref[...])
            mid = (g * _sigmoid(g)) * u
            _store_token_tiles(y_ref, row0, _dot(mid.astype(BF16), b2_ref[...]))

        @pl.when(blk >= used)
        def _():
            y_ref[pl.ds(row0, MOE_BLK * TOKEN_TILE), :] = jnp.zeros((MOE_BLK * TOKEN_TILE, LANES), F32)


def _experts_call(blk_e, n_used, xs, w1, w3, w2):
    per = EXPERT_BLKS_PER_STEP
    rows = per * MOE_BLK * TOKEN_TILE
    nb = xs.shape[0] // (MOE_BLK * TOKEN_TILE)
    assert nb % per == 0 and blk_e.shape[0] == nb
    hbm = pl.BlockSpec(memory_space=pl.ANY)
    return pl.pallas_call(
        _experts_body,
        grid_spec=pltpu.PrefetchScalarGridSpec(
            num_scalar_prefetch=2,
            grid=(nb // per,),
            in_specs=[pl.BlockSpec((rows, LANES), lambda i, blk, used: (i, 0)), hbm, hbm, hbm],
            out_specs=pl.BlockSpec((rows, LANES), lambda i, blk, used: (i, 0)),
            scratch_shapes=[pltpu.VMEM((D_MODEL, D_EXPERT), F32), pltpu.VMEM((D_MODEL, D_EXPERT), F32),
                            pltpu.VMEM((D_EXPERT, D_MODEL), F32),
                            pltpu.VMEM((D_MODEL, D_EXPERT), BF16), pltpu.VMEM((D_MODEL, D_EXPERT), BF16),
                            pltpu.VMEM((D_EXPERT, D_MODEL), BF16),
                            pltpu.SMEM((1,), I32), pltpu.SemaphoreType.DMA(())]),
        out_shape=jax.ShapeDtypeStruct(xs.shape, F32),
        compiler_params=_cparams(("arbitrary",)),
        name="experts",
    )(blk_e, n_used, xs, w1, w3, w2)


def _combine_body(dest_ref, x1_ref, gate_ref, mod_ref, gf_ref, y_ref, o_ref, buf_ref, sem):
    tm = x1_ref.shape[0]
    step = pl.program_id(0)
    n_step = pl.num_programs(0)
    n_tok = n_step * tm
    slot = step % 2
    region = tm * TOKEN_TILE

    def region_row0(buf, k):
        return pl.multiple_of((buf * TOP_K + k) * region, region)

    def start_gather(for_step):
        def issue(r, c):
            for k in range(TOP_K):
                d = dest_ref[k * n_tok + for_step * tm + r]
                dst = buf_ref.at[pl.ds(pl.multiple_of(region_row0(for_step % 2, k) + r * TOKEN_TILE, TOKEN_TILE),
                                       TOKEN_TILE)]
                pltpu.make_async_copy(_token_tile(y_ref, d), dst, sem.at[for_step % 2]).start(priority=k)
            return c

        lax.fori_loop(0, tm, issue, 0, unroll=8)

    @pl.when(step == 0)
    def _():
        start_gather(step)

    @pl.when(step + 1 < n_step)
    def _():
        start_gather(step + 1)

    eye = jnp.where(lax.broadcasted_iota(I32, (tm, tm), 0) == lax.broadcasted_iota(I32, (tm, tm), 1),
                    1.0, 0.0).astype(BF16)
    g = gate_ref[...]
    g1 = g.astype(BF16)
    rem = g - g1.astype(F32)
    g2 = rem.astype(BF16)
    g3 = (rem - g2.astype(F32)).astype(BF16)
    gt = _dot_nt(eye, g1) + (_dot_nt(eye, g2) + _dot_nt(eye, g3))

    for k in range(TOP_K):
        pltpu.make_async_copy(y_ref.at[pl.ds(0, region)], buf_ref.at[pl.ds(region_row0(slot, k), region)],
                              sem.at[slot]).wait()

    ga2 = mod_ref[0, :, 5 * D_MODEL:6 * D_MODEL]
    moe = (gt[:, 0:1] * _load_token_tiles(buf_ref, region_row0(slot, 0), tm)
           + gt[:, 1:2] * _load_token_tiles(buf_ref, region_row0(slot, 1), tm))
    x2 = x1_ref[...] + ga2 * moe
    ms = jnp.mean(x2 * x2, axis=-1, keepdims=True)
    o_ref[...] = x2 * lax.rsqrt(ms + EPS) * gf_ref[...]


def _combine_call(dest_flat, x1, gate, mod3, g_final, y, seq):
    t = x1.shape[0]
    tm = COMBINE_TM
    per_b = seq // tm
    return pl.pallas_call(
        _combine_body,
        grid_spec=pltpu.PrefetchScalarGridSpec(
            num_scalar_prefetch=1,
            grid=(t // tm,),
            in_specs=[pl.BlockSpec((tm, D_MODEL), lambda i, d: (i, 0)),
                      pl.BlockSpec((SUBLANES, tm), lambda i, d: (0, i)),
                      pl.BlockSpec((1, 1, 6 * D_MODEL), lambda i, d: (i // per_b, 0, 0)),
                      pl.BlockSpec((1, D_MODEL), lambda i, d: (0, 0)),
                      pl.BlockSpec(memory_space=pl.ANY)],
            out_specs=pl.BlockSpec((tm, D_MODEL), lambda i, d: (i, 0)),
            scratch_shapes=[pltpu.VMEM((2 * TOP_K * tm * TOKEN_TILE, LANES), F32),
                            pltpu.SemaphoreType.DMA((2,))]),
        out_shape=jax.ShapeDtypeStruct((t, D_MODEL), F32),
        compiler_params=_cparams(("arbitrary",)),
        name="combine",
    )(dest_flat, x1, gate, mod3, g_final, y)


def _rope_tables(seq):
    half = NA_HEAD_DIM // 2
    nf = half // 2
    inv_freq = ROPE_THETA ** (-jnp.arange(nf, dtype=F32) / nf)
    t = jnp.arange(seq)
    row_pos = (t // GRID_W).astype(F32)
    col_pos = (t % GRID_W).astype(F32)
    ang_r = row_pos[:, None] * inv_freq
    ang_c = col_pos[:, None] * inv_freq
    cos = jnp.concatenate([jnp.cos(ang_r), jnp.cos(ang_r), jnp.cos(ang_c), jnp.cos(ang_c)], axis=-1)
    sin = jnp.concatenate([-jnp.sin(ang_r), jnp.sin(ang_r), -jnp.sin(ang_c), jnp.sin(ang_c)], axis=-1)
    return jnp.tile(cos, (1, NA_HEADS)), jnp.tile(sin, (1, NA_HEADS))


def _layer(x, c, ctx, c_ctx, w_mod, b_mod, g_mix, g_ffn, w_in, rpb, conv_w, conv_b, lru_wa, lru_ba,
           lru_wx, lru_bx, lru_lambda, w_up_attn, w_up_lru, w_out, wg, bg, we, be, w1, w3, w2, g_final):
    bsz, seq, d = x.shape
    n_ctx = ctx.shape[1]
    t = bsz * seq
    assert d == D_MODEL and seq % ATT_TQ == 0 and seq // ATT_TQ > K_ROW_BLOCKS and n_ctx % ATT_TQ == 0
    assert bsz + 1 <= MOD_ROWS and seq % PROJ_TM == 0 and seq % MERGE_TM == 0
    assert t % ROUTE_TB == 0 and t % COMBINE_TM == 0 and t % DISPATCH_TM == 0
    assert (bsz * n_ctx) % PROJ_TM == 0 and n_ctx <= seq
    assert seq % (SUBLANES * SUBLANES) == 0 and n_ctx % (SUBLANES * SUBLANES) == 0
    assert (seq // SUBLANES) % SCAN_STEPS == 0 and (n_ctx // SUBLANES) % SCAN_STEPS == 0

    cc = jnp.concatenate([c, c_ctx[None, :], jnp.zeros((MOD_ROWS - bsz - 1, d), F32)], axis=0)
    mod = _mod_call(cc, w_mod, b_mod)
    mod3 = mod[:bsz].reshape(bsz, 1, 6 * d)
    mod_c = mod[bsz:bsz + 1]

    x2 = x.reshape(t, d)
    g_mix2 = g_mix.reshape(1, d)
    w_in_bf = w_in.astype(BF16)
    kc, vc, lxc = _ctxproj_call(ctx.reshape(bsz * n_ctx, d), mod_c, g_mix2, w_in_bf[:, :CTX_COLS])
    cos_t, sin_t = _rope_tables(seq)
    qre, qro, qpe, qpo, k, v, lx, glu, ga, gb = _inproj_call(x2, mod3, g_mix2, w_in_bf, cos_t, sin_t, seq)

    bias = _bias_tables(_rpbcol_call(rpb))
    o_att = _attn_call(qre, qro, qpe, qpo, k, v, kc, vc, bias, bsz, seq, n_ctx)

    hs = _lru_call(lx.reshape(bsz, seq, LRU_WIDTH), lxc.reshape(bsz, n_ctx, LRU_WIDTH),
                   conv_w, conv_b.reshape(1, LRU_WIDTH), lru_wa, lru_wx, lru_ba, lru_bx, lru_lambda)

    wr_t = jnp.concatenate([wg.T, we.T, jnp.zeros((ROUTE_ROWS - N_GROUPS - N_EXPERTS, d), F32)], axis=0)
    br = jnp.concatenate([bg, be, jnp.zeros((ROUTE_ROWS - N_GROUPS - N_EXPERTS,), F32)])
    br = jnp.broadcast_to(br[:, None], (ROUTE_ROWS, LANES))
    x1, h2, logits_t = _merge_call(x2, o_att, hs.reshape(t, LRU_WIDTH), glu, ga, gb, mod3,
                                   g_ffn.reshape(1, d), w_up_attn.astype(BF16), w_up_lru.astype(BF16),
                                   w_out.astype(BF16), wr_t, br, seq)

    eid, gate, rank, cnt = _route_call(logits_t)
    n_blk = -(-(t * TOP_K + N_EXPERTS * (MOE_BLK - 1)) // MOE_BLK)
    n_blk = -(-n_blk // EXPERT_BLKS_PER_STEP) * EXPERT_BLKS_PER_STEP
    nb_pad = -(-n_blk // LANES) * LANES
    dest, blk = _dest_call(cnt, eid, rank, nb_pad)
    dest_flat = dest.reshape(TOP_K * t)
    xs = _dispatch_call(dest_flat, blk[2, :N_EXPERTS], blk[3, :N_EXPERTS], h2, n_blk * MOE_BLK)
    y = _experts_call(blk[0, :n_blk], blk[1, :1], xs, w1, w3, w2)
    return _combine_call(dest_flat, x1, gate, mod3, g_final.reshape(1, d), y, seq).reshape(bsz, seq, d)


def kernel(x, c, ctx, c_ctx, w_mod, b_mod, g_mix, g_ffn, w_in, rpb, conv_w, conv_b, lru_wa, lru_ba, lru_wx,
           lru_bx, lru_lambda, w_up_attn, w_up_lru, w_out, router_group_w, router_group_b, router_expert_w,
           router_expert_b, expert_w_gate, expert_w_up, expert_w_down, g_final):
    assert w_mod.shape[0] == 1, "single-layer block"
    return _layer(x, c, ctx, c_ctx, w_mod[0], b_mod[0], g_mix[0], g_ffn[0], w_in[0], rpb[0], conv_w[0],
                  conv_b[0], lru_wa[0], lru_ba[0], lru_wx[0], lru_bx[0], lru_lambda[0], w_up_attn[0],
                  w_up_lru[0], w_out[0], router_group_w[0], router_group_b[0], router_expert_w[0],
                  router_expert_b[0], expert_w_gate[0], expert_w_up[0], expert_w_down[0], g_final)
```
